```python
import math
import jax, jax.numpy as jnp
from jax import lax
import numpy as np

D_MODEL = 1024
BATCH = 8
SEQ = 2048
DEPTH = 1

N_MEM = 256
H_RNN = 8
RNN_KDIM = 128
RNN_VDIM = 128
RNN_WIDTH = H_RNN * RNN_KDIM
RNN_VWIDTH = H_RNN * RNN_VDIM
RNN_CHUNK = 64
H_ATT = 16
ATT_HD = 64
G_KV = 4
HG = H_ATT // G_KV
ATT_WIDTH = H_ATT * ATT_HD
KV_WIDTH = G_KV * ATT_HD
CMP_LEN = 32
CMP_STRIDE = 16
CMP_HIDDEN = 128
SLC_LEN = 64
SLC_TOPK = 8
WINDOW = 512
Q_BLOCK = 64
FORCE_BONUS = 1.0e4
H_X = 4
X_HD = 128
D_FF = -(-8 * D_MODEL // (3 * 256)) * 256
SPLIT_SIZES = (RNN_WIDTH, RNN_WIDTH, RNN_VWIDTH, RNN_VWIDTH,
               ATT_WIDTH, 6 * KV_WIDTH, 3 * H_ATT, D_MODEL, D_MODEL)
IN_WIDTH = sum(SPLIT_SIZES)

kernel_name = "hybrid_hgrn2_nsa_gated_block"


def rms_norm(x, g, eps=1e-6):
    xf = x.astype(jnp.float32)
    y = xf * lax.rsqrt(jnp.mean(xf * xf, axis=-1, keepdims=True) + eps)
    return (y * g.astype(jnp.float32)).astype(x.dtype)


def masked_softmax(s, mask):
    s = jnp.where(mask, s.astype(jnp.float32), -jnp.inf)
    m = jnp.max(s, axis=-1, keepdims=True)
    m = jnp.where(jnp.isfinite(m), m, 0.0)
    p = jnp.exp(s - m)
    d = jnp.sum(p, axis=-1, keepdims=True)
    return p / jnp.where(d > 0, d, 1.0)


def alibi_slopes(n):
    return 2.0 ** (-8.0 * jnp.arange(1, n + 1, dtype=jnp.float32) / n)


def hgrn2(q, f_logit, inp, og, lb, g_norm):
    B, T, _ = q.shape
    C = RNN_CHUNK
    N = T // C
    f32 = jnp.float32
    lbf = lb.astype(f32)
    f = lbf + (1.0 - lbf) * jax.nn.sigmoid(f_logit.astype(f32))
    k = 1.0 - f
    logf = jnp.log(f)
    qa = jax.nn.silu(q.astype(f32))

    def chunks(t, d):
        return t.reshape(B, N, C, H_RNN, d).transpose(0, 3, 1, 2, 4)

    qc, kc, lc = chunks(qa, RNN_KDIM), chunks(k, RNN_KDIM), chunks(logf, RNN_KDIM)
    vc = chunks(inp.astype(f32), RNN_VDIM)
    b = jnp.cumsum(lc, axis=3)
    b_last = b[:, :, :, -1:, :]
    q_dec = qc * jnp.exp(b)
    k_dec = kc * jnp.exp(-b)
    k_end = kc * jnp.exp(b_last - b)
    causal = jnp.tril(jnp.ones((C, C), dtype=bool))
    A = jnp.where(causal, jnp.einsum('bhnck,bhnsk->bhncs', q_dec, k_dec), 0.0)
    o_intra = jnp.einsum('bhncs,bhnsv->bhncv', A, vc)
    U = jnp.einsum('bhnsk,bhnsv->nbhkv', k_end, vc)
    decay = jnp.exp(b_last[:, :, :, 0, :]).transpose(2, 0, 1, 3)[..., None]

    def step(S, xs):
        dec, u = xs
        return dec * S + u, S

    S0 = jnp.zeros((B, H_RNN, RNN_KDIM, RNN_VDIM), f32)
    _, S_prev = lax.scan(step, S0, (decay, U))
    o_inter = jnp.einsum('bhnck,nbhkv->bhncv', q_dec, S_prev)
    o = (o_intra + o_inter).transpose(0, 2, 3, 1, 4)
    o = rms_norm(o, g_norm) * jax.nn.silu(og.astype(f32).reshape(B, N, C, H_RNN, RNN_VDIM))
    return o.reshape(B, T, RNN_VWIDTH).astype(q.dtype)


def nsa(q, kv, gate_logits, pe_ck, w_ck1, w_ck2, pe_cv, w_cv1, w_cv2):
    B, T, _ = q.shape
    n_sel = min(SLC_TOPK, T // SLC_LEN)
    pos = jnp.arange(T)
    slopes = alibi_slopes(H_ATT).reshape(1, G_KV, HG, 1, 1)
    qh = q.reshape(B, T, G_KV, HG, ATT_HD).transpose(0, 2, 3, 1, 4) * (ATT_HD ** -0.5)
    k_c, v_c, k_s, v_s, k_w, v_w = [
        t.reshape(B, T, G_KV, ATT_HD).transpose(0, 2, 1, 3) for t in jnp.split(kv, 6, axis=-1)]

    Nc = (T - CMP_LEN) // CMP_STRIDE + 1
    starts = CMP_STRIDE * jnp.arange(Nc)
    blk_idx = starts[:, None] + jnp.arange(CMP_LEN)[None, :]

    def compress(t, pe, w1, w2):
        blocks = (t[:, :, blk_idx] + pe).reshape(B, G_KV, Nc, CMP_LEN * ATT_HD)
        return jax.nn.silu(blocks @ w1) @ w2

    kc = compress(k_c, pe_ck, w_ck1, w_ck2)
    vc = compress(v_c, pe_cv, w_cv1, w_cv2)
    centre = starts + (CMP_LEN - 1) / 2.0
    ends = starts + CMP_LEN - 1
    s_c = jnp.einsum('bghtd,bgnd->bghtn', qh, kc) - slopes * (pos[:, None] - centre[None, :])
    p_c = masked_softmax(s_c, ends[None, :] <= pos[:, None])
    o_cmp = jnp.einsum('bghtn,bgnd->bghtd', p_c.astype(vc.dtype), vc)

    Ns = T // SLC_LEN
    s_start = SLC_LEN * jnp.arange(Ns)
    overlap = ((starts[:, None] + CMP_LEN > s_start[None, :]) &
               (starts[:, None] < s_start[None, :] + SLC_LEN)).astype(jnp.float32)
    imp = jnp.einsum('bgtn,ns->bgts', jnp.sum(p_c, axis=2), overlap)
    cur = pos // SLC_LEN
    jb = jnp.arange(Ns)
    valid = s_start[None, :] <= pos[:, None]
    forced = (jb[None, :] == 0) | (jb[None, :] == cur[:, None]) | (jb[None, :] == cur[:, None] - 1)
    score = jnp.where(valid, imp + jnp.where(forced, FORCE_BONUS, 0.0), -jnp.inf)
    _, sel = lax.top_k(score, n_sel)

    nQ = T // Q_BLOCK
    q_blocks = qh.reshape(B, G_KV, HG, nQ, Q_BLOCK, ATT_HD).transpose(3, 0, 1, 2, 4, 5)
    blk_ids = jnp.arange(nQ)

    Kb = k_s.reshape(B, G_KV, Ns, SLC_LEN, ATT_HD)
    Vb = v_s.reshape(B, G_KV, Ns, SLC_LEN, ATT_HD)
    sel_blocks = sel.reshape(B, G_KV, nQ, Q_BLOCK, n_sel).transpose(2, 0, 1, 3, 4)
    bi = jnp.arange(B)[:, None, None, None]
    gi = jnp.arange(G_KV)[None, :, None, None]
    n_keys = n_sel * SLC_LEN

    def sel_attend(args):
        qb, ib, c = args
        kg = Kb[bi, gi, ib].reshape(B, G_KV, Q_BLOCK, n_keys, ATT_HD)
        vg = Vb[bi, gi, ib].reshape(B, G_KV, Q_BLOCK, n_keys, ATT_HD)
        k_pos = (ib[..., None] * SLC_LEN + jnp.arange(SLC_LEN)).reshape(B, G_KV, Q_BLOCK, n_keys)
        t_q = c * Q_BLOCK + jnp.arange(Q_BLOCK)
        dist = (t_q[None, None, :, None] - k_pos)[:, :, None]
        s = jnp.einsum('bghqd,bgqkd->bghqk', qb, kg) - slopes * dist
        p = masked_softmax(s, dist >= 0)
        return jnp.einsum('bghqk,bgqkd->bghqd', p.astype(vg.dtype), vg)

    o_slc = lax.map(sel_attend, (q_blocks, sel_blocks, blk_ids))

    KW = WINDOW + Q_BLOCK
    kw_pad = jnp.pad(k_w, ((0, 0), (0, 0), (WINDOW, 0), (0, 0)))
    vw_pad = jnp.pad(v_w, ((0, 0), (0, 0), (WINDOW, 0), (0, 0)))

    def win_attend(args):
        qb, c = args
        kb = lax.dynamic_slice_in_dim(kw_pad, c * Q_BLOCK, KW, axis=2)
        vb = lax.dynamic_slice_in_dim(vw_pad, c * Q_BLOCK, KW, axis=2)
        t_q = c * Q_BLOCK + jnp.arange(Q_BLOCK)
        k_pos = c * Q_BLOCK - WINDOW + jnp.arange(KW)
        dist = t_q[:, None] - k_pos[None, :]
        mask = (dist >= 0) & (dist < WINDOW) & (k_pos[None, :] >= 0)
        s = jnp.einsum('bghqd,bgkd->bghqk', qb, kb) - slopes * dist
        p = masked_softmax(s, mask)
        return jnp.einsum('bghqk,bgkd->bghqd', p.astype(vb.dtype), vb)

    o_win = lax.map(win_attend, (q_blocks, blk_ids))

    def unblock(o):
        return o.transpose(1, 2, 3, 0, 4, 5).reshape(B, G_KV, HG, T, ATT_HD)

    g = jax.nn.sigmoid(gate_logits.astype(jnp.float32)).reshape(B, T, G_KV, HG, 3)
    g = g.transpose(0, 2, 3, 1, 4).astype(q.dtype)
    o = (g[..., 0:1] * o_cmp + g[..., 1:2] * unblock(o_slc) + g[..., 2:3] * unblock(o_win))
    return o.transpose(0, 3, 1, 2, 4).reshape(B, T, ATT_WIDTH)


def cross_attention(a, memn, w_xq, w_xkv, w_xo):
    B, T, _ = a.shape
    M = memn.shape[1]
    q = (a @ w_xq).reshape(B, T, H_X, X_HD) * (X_HD ** -0.5)
    k, v = jnp.split(memn @ w_xkv, 2, axis=-1)
    k = k.reshape(B, M, H_X, X_HD)
    v = v.reshape(B, M, H_X, X_HD)
    p = jax.nn.softmax(jnp.einsum('bthd,bmhd->bhtm', q, k).astype(jnp.float32), axis=-1)
    o = jnp.einsum('bhtm,bmhd->bthd', p.astype(v.dtype), v).reshape(B, T, H_X * X_HD)
    return o @ w_xo


def setup_inputs(seed: int = 0) -> dict:
    key = jax.random.key(seed)
    ks = jax.random.split(key, 24)
    L = DEPTH
    f32 = jnp.float32

    def w(k, shape, fan_in):
        return jax.random.normal(k, shape, f32) * fan_in ** -0.5

    def gain(k, shape):
        return 1.0 + 0.02 * jax.random.normal(k, shape, f32)

    flat = CMP_LEN * ATT_HD
    return {
        "x": jax.random.normal(ks[0], (BATCH, SEQ, D_MODEL), f32),
        "mem": jax.random.normal(ks[1], (BATCH, N_MEM, D_MODEL), f32),
        "g_mix": gain(ks[2], (L, D_MODEL)),
        "w_in": w(ks[3], (L, D_MODEL, IN_WIDTH), D_MODEL),
        "lower_bounds": 0.1 * jax.random.normal(ks[4], (L + 1, RNN_WIDTH), f32),
        "g_rnn_out": gain(ks[5], (L, RNN_VDIM)),
        "pe_ck": 0.1 * jax.random.normal(ks[6], (L, CMP_LEN, ATT_HD), f32),
        "w_ck1": w(ks[7], (L, flat, CMP_HIDDEN), flat),
        "w_ck2": w(ks[8], (L, CMP_HIDDEN, ATT_HD), CMP_HIDDEN),
        "pe_cv": 0.1 * jax.random.normal(ks[9], (L, CMP_LEN, ATT_HD), f32),
        "w_cv1": w(ks[10], (L, flat, CMP_HIDDEN), flat),
        "w_cv2": w(ks[11], (L, CMP_HIDDEN, ATT_HD), CMP_HIDDEN),
        "w_proj_rnn": w(ks[12], (L, RNN_VWIDTH, D_MODEL), RNN_VWIDTH),
        "w_proj_att": w(ks[13], (L, ATT_WIDTH, D_MODEL), ATT_WIDTH),
        "w_out": w(ks[14], (L, D_MODEL, D_MODEL), D_MODEL),
        "g_xattn": gain(ks[15], (L, D_MODEL)),
        "g_mem": gain(ks[16], (L, D_MODEL)),
        "w_xq": w(ks[17], (L, D_MODEL, H_X * X_HD), D_MODEL),
        "w_xkv": w(ks[18], (L, D_MODEL, 2 * H_X * X_HD), D_MODEL),
        "w_xo": w(ks[19], (L, H_X * X_HD, D_MODEL), H_X * X_HD),
        "g_ffn": gain(ks[20], (L, D_MODEL)),
        "w_gate_up": w(ks[21], (L, D_MODEL, 2 * D_FF), D_MODEL),
        "w_down": w(ks[22], (L, D_FF, D_MODEL), D_FF),
        "g_final": gain(ks[23], (D_MODEL,)),
    }


def reference(x, mem, g_mix, w_in, lower_bounds, g_rnn_out, pe_ck, w_ck1, w_ck2,
              pe_cv, w_cv1, w_cv2, w_proj_rnn, w_proj_att, w_out, g_xattn, g_mem,
              w_xq, w_xkv, w_xo, g_ffn, w_gate_up, w_down, g_final):
    split_points = np.cumsum(SPLIT_SIZES)[:-1].tolist()
    lbs = jnp.cumsum(jax.nn.softmax(lower_bounds.astype(jnp.float32), axis=0), axis=0)
    h = x
    for l in range(DEPTH):
        a = rms_norm(h, g_mix[l])
        q_r, f_r, i_r, og_r, q_a, kv_a, gate_a, mg_r, mg_a = jnp.split(a @ w_in[l], split_points, axis=-1)
        y_r = hgrn2(q_r, f_r, i_r, og_r, lbs[l], g_rnn_out[l])
        y_a = nsa(q_a, kv_a, gate_a, pe_ck[l], w_ck1[l], w_ck2[l], pe_cv[l], w_cv1[l], w_cv2[l])
        merged = (jax.nn.sigmoid(mg_r) * (y_r @ w_proj_rnn[l]) +
                  jax.nn.sigmoid(mg_a) * (y_a @ w_proj_att[l]))
        h = h + merged @ w_out[l]
        h = h + cross_attention(rms_norm(h, g_xattn[l]), rms_norm(mem, g_mem[l]),
                                w_xq[l], w_xkv[l], w_xo[l])
        gt, up = jnp.split(rms_norm(h, g_ffn[l]) @ w_gate_up[l], 2, axis=-1)
        h = h + (jax.nn.silu(gt) * up) @ w_down[l]
    return rms_norm(h, g_final)
```

```python
import functools

import jax
import jax.numpy as jnp
import numpy as np
from jax import lax
from jax.experimental import pallas as pl
from jax.experimental.pallas import tpu as pltpu

F32 = jnp.float32
BF16 = jnp.bfloat16

D_MODEL = 1024
N_MEM = 256
H_RNN = 8
RNN_DIM = 128
RNN_CHUNK = 64
H_ATT = 16
ATT_HD = 64
G_KV = 4
HG = H_ATT // G_KV
CMP_LEN = 32
CMP_STRIDE = 16
CMP_HIDDEN = 128
SLC_LEN = 64
SLC_TOPK = 8
WINDOW = 512
FORCE_BONUS = 1.0e4
H_X = 4
X_HD = 128
D_FF = 2816
EPS = 1e-6

LANE = 128
VMEM_LIMIT = 56 * 1024 * 1024
TOK_TILE = 512
ATT_TILE = 128
CMP_TILE = 512
N_CMP_PAD = 128
NEG_BIG = -1.0e30

QT_ROWS = H_ATT * ATT_HD
VT_ROWS = 2 * G_KV * ATT_HD
GATE_ROWS = 16
RNN_COLS = 4 * H_RNN * RNN_DIM
KV_COLS = 4 * G_KV * ATT_HD
MG_COLS = 2 * D_MODEL
N_COLS = RNN_COLS + KV_COLS + MG_COLS


def _cparams(sem):
    return pltpu.CompilerParams(dimension_semantics=sem, vmem_limit_bytes=VMEM_LIMIT)


def _rms(xf, g):
    return xf * lax.rsqrt(jnp.mean(xf * xf, axis=-1, keepdims=True) + EPS) * g


def _sigmoid(x):
    return 1.0 / (1.0 + jnp.exp(-x))


def _silu(x):
    return x * _sigmoid(x)


def _dot(a, b):
    return jnp.dot(a, b, preferred_element_type=F32)


def _dot_nt(a, b):
    return lax.dot_general(a, b, (((1,), (1,)), ((), ())), preferred_element_type=F32)


def _dot_tn(a, b):
    return lax.dot_general(a, b, (((0,), (0,)), ((), ())), preferred_element_type=F32)


def _in_proj_kernel(x_ref, g_ref, w_ref, o_ref, *, sub):
    a = _rms(x_ref[...], g_ref[...]).astype(BF16)
    tn = o_ref.shape[1]
    for n in range(tn // sub):
        o_ref[:, n * sub:(n + 1) * sub] = _dot(a, w_ref[:, n * sub:(n + 1) * sub]).astype(o_ref.dtype)


def _in_proj(x2, g, w):
    m, d = x2.shape
    n = w.shape[1]
    tn = n // 2
    return pl.pallas_call(
        functools.partial(_in_proj_kernel, sub=512),
        out_shape=jax.ShapeDtypeStruct((m, n), BF16),
        grid=(2, m // TOK_TILE),
        in_specs=[pl.BlockSpec((TOK_TILE, d), lambda j, i: (i, 0)),
                  pl.BlockSpec((1, d), lambda j, i: (0, 0)),
                  pl.BlockSpec((d, tn), lambda j, i: (0, j))],
        out_specs=pl.BlockSpec((TOK_TILE, tn), lambda j, i: (i, j)),
        compiler_params=_cparams(("arbitrary", "arbitrary")),
        name="in_proj",
    )(x2, g, w)


def _in_proj_t_kernel(x_ref, g_ref, w_ref, qv_ref, gt_ref):
    a = _rms(x_ref[...], g_ref[...]).astype(BF16)
    r = _dot_nt(w_ref[...], a)
    nqv = qv_ref.shape[1]
    for c in range(qv_ref.shape[0]):
        qv_ref[c] = r[:nqv, c * LANE:(c + 1) * LANE].astype(qv_ref.dtype)
        gt_ref[c] = r[nqv:, c * LANE:(c + 1) * LANE]


def _in_proj_t(x, g, wt):
    b, t, d = x.shape
    rows = wt.shape[0]
    nqv = QT_ROWS + VT_ROWS
    ngt = rows - nqv
    nc = TOK_TILE // LANE
    return pl.pallas_call(
        _in_proj_t_kernel,
        out_shape=(jax.ShapeDtypeStruct((b, t // LANE, nqv, LANE), BF16),
                   jax.ShapeDtypeStruct((b, t // LANE, ngt, LANE), F32)),
        grid=(b, t // TOK_TILE),
        in_specs=[pl.BlockSpec((None, TOK_TILE, d), lambda i, j: (i, j, 0)),
                  pl.BlockSpec((1, d), lambda i, j: (0, 0)),
                  pl.BlockSpec((rows, d), lambda i, j: (0, 0))],
        out_specs=(pl.BlockSpec((None, nc, nqv, LANE), lambda i, j: (i, j, 0, 0)),
                   pl.BlockSpec((None, nc, ngt, LANE), lambda i, j: (i, j, 0, 0))),
        compiler_params=_cparams(("arbitrary", "arbitrary")),
        name="in_proj_t",
    )(x, g, wt)


def _hgrn_kernel(r_ref, lb_ref, gn_ref, o_ref):
    c = RNN_CHUNK
    kd = RNN_DIM
    n_chunks = r_ref.shape[0] // c
    lb = lb_ref[...]
    gn = gn_ref[...]
    row = lax.broadcasted_iota(jnp.int32, (c, c), 0)
    col = lax.broadcasted_iota(jnp.int32, (c, c), 1)
    causal = row >= col
    tril = causal.astype(F32)

    def body(n, s_t):
        rows = pl.ds(pl.multiple_of(n * c, c), c)
        q = r_ref[rows, 0:kd].astype(F32)
        fl = r_ref[rows, kd:2 * kd].astype(F32)
        v = r_ref[rows, 2 * kd:3 * kd]
        og = r_ref[rows, 3 * kd:4 * kd].astype(F32)
        f = lb + (1.0 - lb) * _sigmoid(fl)
        k = 1.0 - f
        logf = jnp.log(f)
        bcum = jnp.dot(tril, logf, preferred_element_type=F32,
                       precision=lax.Precision.HIGHEST)
        b_last = bcum[c - 1:c, :]
        q_dec = (_silu(q) * jnp.exp(bcum)).astype(BF16)
        k_dec = (k * jnp.exp(-bcum)).astype(BF16)
        k_end = (k * jnp.exp(b_last - bcum)).astype(BF16)
        dec = jnp.exp(b_last)
        a = jnp.where(causal, _dot_nt(q_dec, k_dec), 0.0)
        o = _dot(a.astype(BF16), v) + _dot_nt(q_dec, s_t.astype(BF16))
        s_new = s_t * dec + _dot_tn(v, k_end)
        y = _rms(o, gn) * _silu(og)
        o_ref[rows, :] = y.astype(o_ref.dtype)
        return s_new

    lax.fori_loop(0, n_chunks, body, jnp.zeros((kd, kd), F32))


def _hgrn(proj3, lb, gn):
    b, t, _ = proj3.shape
    return pl.pallas_call(
        _hgrn_kernel,
        out_shape=jax.ShapeDtypeStruct((b, t, H_RNN * RNN_DIM), BF16),
        grid=(b, H_RNN),
        in_specs=[pl.BlockSpec((None, t, 4 * RNN_DIM), lambda i, h: (i, 0, h)),
                  pl.BlockSpec((1, RNN_DIM), lambda i, h: (0, h)),
                  pl.BlockSpec((1, RNN_DIM), lambda i, h: (0, 0))],
        out_specs=pl.BlockSpec((None, t, RNN_DIM), lambda i, h: (i, 0, h)),
        compiler_params=_cparams(("arbitrary", "arbitrary")),
        name="hgrn",
    )(proj3, lb, gn)


def _compress_kernel(kv_ref, pek_ref, pev_ref, wk1_ref, wv1_ref, wk2_ref, wv2t_ref,
                     kc_ref, vct_ref, xs_ref, xk_ref, xv_ref):
    t = kv_ref.shape[0]
    hd = ATT_HD
    xs_ref[0:t, :] = kv_ref[...].astype(F32)
    xs_ref[t:, :] = jnp.zeros((xs_ref.shape[0] - t, xs_ref.shape[1]), F32)
    for l in range(CMP_LEN):
        blk = xs_ref[pl.ds(l, N_CMP_PAD, stride=CMP_STRIDE), :]
        xk_ref[:, l * hd:(l + 1) * hd] = (blk[:, 0:hd] + pek_ref[l:l + 1, :]).astype(BF16)
        xv_ref[:, l * hd:(l + 1) * hd] = (blk[:, hd:2 * hd] + pev_ref[l:l + 1, :]).astype(BF16)
    hk = _silu(_dot(xk_ref[...], wk1_ref[...])).astype(BF16)
    hv = _silu(_dot(xv_ref[...], wv1_ref[...])).astype(BF16)
    kc_ref[...] = _dot(hk, wk2_ref[...])
    vct_ref[...] = _dot_nt(wv2t_ref[...], hv)


def _compress(proj3, pek, pev, wk1, wv1, wk2, wv2t):
    b, t, _ = proj3.shape
    kv_unit0 = RNN_COLS // LANE
    flat = CMP_LEN * ATT_HD
    full = lambda shape: pl.BlockSpec(shape, lambda i, g: (0,) * len(shape))
    return pl.pallas_call(
        _compress_kernel,
        out_shape=(jax.ShapeDtypeStruct((b, G_KV, N_CMP_PAD, ATT_HD), F32),
                   jax.ShapeDtypeStruct((b, G_KV, ATT_HD, N_CMP_PAD), F32)),
        grid=(b, G_KV),
        in_specs=[pl.BlockSpec((None, t, LANE), lambda i, g: (i, 0, kv_unit0 + 2 * g)),
                  full((CMP_LEN, ATT_HD)), full((CMP_LEN, ATT_HD)),
                  full((flat, CMP_HIDDEN)), full((flat, CMP_HIDDEN)),
                  full((CMP_HIDDEN, ATT_HD)), full((ATT_HD, CMP_HIDDEN))],
        out_specs=(pl.BlockSpec((None, None, N_CMP_PAD, ATT_HD), lambda i, g: (i, g, 0, 0)),
                   pl.BlockSpec((None, None, ATT_HD, N_CMP_PAD), lambda i, g: (i, g, 0, 0))),
        scratch_shapes=[pltpu.VMEM((t + CMP_STRIDE, LANE), F32),
                        pltpu.VMEM((N_CMP_PAD, flat), BF16),
                        pltpu.VMEM((N_CMP_PAD, flat), BF16)],
        compiler_params=_cparams(("arbitrary", "arbitrary")),
        name="compress",
    )(proj3, pek, pev, wk1, wv1, wk2, wv2t)


def _cmp_select_kernel(qt_ref, kc_ref, vct_ref, slope_ref, ov_ref, ocmp_ref, sel_ref, *, n_cmp, n_sel):
    nch = qt_ref.shape[0]
    tt = nch * LANE
    t0 = pl.program_id(2) * tt
    qt = jnp.concatenate([qt_ref[c] for c in range(nch)], axis=1)
    kc = kc_ref[...].astype(BF16)
    vct = vct_ref[...].astype(BF16)
    n_i = lax.broadcasted_iota(jnp.int32, (N_CMP_PAD, tt), 0)
    t_i = lax.broadcasted_iota(jnp.int32, (N_CMP_PAD, tt), 1) + t0
    visible = (CMP_STRIDE * n_i + (CMP_LEN - 1) <= t_i) & (n_i < n_cmp)
    dist = t_i.astype(F32) - (n_i.astype(F32) * float(CMP_STRIDE) + (CMP_LEN - 1) / 2.0)
    psum = jnp.zeros((N_CMP_PAD, tt), F32)
    outs = []
    for h in range(HG):
        s = _dot(kc, qt[h * ATT_HD:(h + 1) * ATT_HD, :]) - slope_ref[h] * dist
        s = jnp.where(visible, s, -jnp.inf)
        m = jnp.max(s, axis=0, keepdims=True)
        m = jnp.where(m == -jnp.inf, 0.0, m)
        p = jnp.exp(s - m)
        d = jnp.sum(p, axis=0, keepdims=True)
        p = p / jnp.where(d > 0, d, 1.0)
        psum = psum + p
        outs.append(_dot(vct, p.astype(BF16)))
    ocmp = jnp.concatenate(outs, axis=0)
    for c in range(nch):
        ocmp_ref[c] = ocmp[:, c * LANE:(c + 1) * LANE]
    imp = jnp.dot(ov_ref[...], psum, preferred_element_type=F32,
                  precision=lax.Precision.HIGHEST)
    ns = imp.shape[0]
    j_i = lax.broadcasted_iota(jnp.int32, (ns, tt), 0)
    tq = lax.broadcasted_iota(jnp.int32, (ns, tt), 1) + t0
    cur = tq // SLC_LEN
    valid = j_i * SLC_LEN <= tq
    forced = (j_i == 0) | (j_i == cur) | (j_i == cur - 1)
    score = jnp.where(valid, imp + jnp.where(forced, FORCE_BONUS, 0.0), -jnp.inf)
    rank = jnp.zeros((ns, tt), jnp.int32)
    for jp in range(ns):
        sj = score[jp:jp + 1, :]
        ahead = (sj > score) | ((sj == score) & (jp < j_i))
        rank = rank + ahead.astype(jnp.int32)
    sel = (rank < n_sel).astype(F32)
    for c in range(nch):
        sel_ref[c] = sel[:, c * LANE:(c + 1) * LANE]


def _cmp_select(qv, kc, vct, slopes, overlap_t, t):
    b = qv.shape[0]
    ns = t // SLC_LEN
    nch = CMP_TILE // LANE
    n_cmp = (t - CMP_LEN) // CMP_STRIDE + 1
    n_sel = min(SLC_TOPK, ns)
    rows = HG * ATT_HD
    return pl.pallas_call(
        functools.partial(_cmp_select_kernel, n_cmp=n_cmp, n_sel=n_sel),
        out_shape=(jax.ShapeDtypeStruct((b, t // LANE, H_ATT * ATT_HD, LANE), F32),
                   jax.ShapeDtypeStruct((b, G_KV, t // LANE, ns, LANE), F32)),
        grid=(b, G_KV, t // CMP_TILE),
        in_specs=[pl.BlockSpec((None, nch, rows, LANE), lambda i, g, j: (i, j, g, 0)),
                  pl.BlockSpec((None, None, N_CMP_PAD, ATT_HD), lambda i, g, j: (i, g, 0, 0)),
                  pl.BlockSpec((None, None, ATT_HD, N_CMP_PAD), lambda i, g, j: (i, g, 0, 0)),
                  pl.BlockSpec((None, HG, 1, 1), lambda i, g, j: (g, 0, 0, 0)),
                  pl.BlockSpec((ns, N_CMP_PAD), lambda i, g, j: (0, 0))],
        out_specs=(pl.BlockSpec((None, nch, rows, LANE), lambda i, g, j: (i, j, g, 0)),
                   pl.BlockSpec((None, None, nch, ns, LANE), lambda i, g, j: (i, g, j, 0, 0))),
        compiler_params=_cparams(("arbitrary", "arbitrary", "arbitrary")),
        name="cmp_select",
    )(qv, kc, vct, slopes, overlap_t)


def _sparse_kernel(qt_ref, k_ref, vs_ref, vw_ref, gt_ref, ocmp_ref, sel_ref, slope_ref, o_ref):
    tq = ATT_TILE
    kc = ATT_TILE
    hd = ATT_HD
    nl = HG * tq
    qi = pl.program_id(2)
    qt = qt_ref[...]
    q_all = jnp.concatenate([qt[h * hd:(h + 1) * hd, :] for h in range(HG)], axis=1)
    zero = jnp.zeros_like(q_all)
    q_sel = jnp.concatenate([q_all, zero], axis=0)
    q_win = jnp.concatenate([zero, q_all], axis=0)
    slope = jnp.concatenate([jnp.broadcast_to(slope_ref[h], (1, tq)) for h in range(HG)], axis=1)
    k_i = lax.broadcasted_iota(jnp.int32, (kc, nl), 0)
    t_i = lax.broadcasted_iota(jnp.int32, (kc, nl), 1) % tq
    d0 = t_i - k_i
    blocks_per_chunk = kc // SLC_LEN

    def step(ch, carry, k_op, v_ref, masker):
        m, l, acc = carry
        rows = pl.ds(pl.multiple_of(ch * kc, kc), kc)
        dist = d0 + (qi - ch) * kc
        s = _dot(k_ref[rows, :], k_op) - slope * dist.astype(F32)
        s = jnp.where(masker(ch, dist), s, NEG_BIG)
        m_new = jnp.maximum(m, jnp.max(s, axis=0, keepdims=True))
        alpha = jnp.exp(m - m_new)
        p = jnp.exp(s - m_new)
        l_new = l * alpha + jnp.sum(p, axis=0, keepdims=True)
        acc_new = acc * alpha + _dot(v_ref[ch], p.astype(BF16))
        return m_new, l_new, acc_new

    def sel_mask(ch, dist):
        parts = []
        for jb in range(blocks_per_chunk):
            r = sel_ref[pl.ds(ch * blocks_per_chunk + jb, 1), :]
            row = jnp.concatenate([r] * HG, axis=1)
            parts.append(jnp.broadcast_to(row, (SLC_LEN, nl)))
        return (jnp.concatenate(parts, axis=0) > 0.5) & (dist >= 0)

    def win_mask(ch, dist):
        return (dist >= 0) & (dist < WINDOW)

    init = (jnp.full((1, nl), NEG_BIG, F32), jnp.zeros((1, nl), F32), jnp.zeros((hd, nl), F32))
    ms, ls, accs = lax.fori_loop(
        0, qi + 1, lambda ch, c: step(ch, c, q_sel, vs_ref, sel_mask), init)
    first_w = jnp.maximum(qi - WINDOW // kc, 0)
    mw, lw, accw = lax.fori_loop(
        first_w, qi + 1, lambda ch, c: step(ch, c, q_win, vw_ref, win_mask), init)
    del ms, mw
    o_slc = accs / ls
    o_win = accw / lw
    gates = _sigmoid(gt_ref[...])
    oc = ocmp_ref[...]
    outs = []
    for h in range(HG):
        lanes = slice(h * tq, (h + 1) * tq)
        y = (gates[h:h + 1, :] * oc[h * hd:(h + 1) * hd, :]
             + gates[HG + h:HG + h + 1, :] * o_slc[:, lanes]
             + gates[2 * HG + h:2 * HG + h + 1, :] * o_win[:, lanes])
        outs.append(y.T)
    o_ref[...] = jnp.concatenate(outs, axis=1).astype(o_ref.dtype)


def _sparse(qv, gt, proj3, ocmp, sel, slopes, t):
    b = qv.shape[0]
    nchunk = t // LANE
    ns = t // SLC_LEN
    rows = HG * ATT_HD
    ksw_unit0 = RNN_COLS // LANE + 1
    vs_blk0 = QT_ROWS // ATT_HD
    vw_blk0 = vs_blk0 + G_KV
    return pl.pallas_call(
        _sparse_kernel,
        out_shape=jax.ShapeDtypeStruct((b, t, H_ATT * ATT_HD), BF16),
        grid=(b, G_KV, t // ATT_TILE),
        in_specs=[pl.BlockSpec((None, None, rows, LANE), lambda i, g, j: (i, j, g, 0)),
                  pl.BlockSpec((None, t, LANE), lambda i, g, j: (i, 0, ksw_unit0 + 2 * g)),
                  pl.BlockSpec((None, nchunk, ATT_HD, LANE), lambda i, g, j: (i, 0, vs_blk0 + g, 0)),
                  pl.BlockSpec((None, nchunk, ATT_HD, LANE), lambda i, g, j: (i, 0, vw_blk0 + g, 0)),
                  pl.BlockSpec((None, None, GATE_ROWS, LANE), lambda i, g, j: (i, j, g, 0)),
                  pl.BlockSpec((None, None, rows, LANE), lambda i, g, j: (i, j, g, 0)),
                  pl.BlockSpec((None, None, None, ns, LANE), lambda i, g, j: (i, g, j, 0, 0)),
                  pl.BlockSpec((None, HG, 1, 1), lambda i, g, j: (g, 0, 0, 0))],
        out_specs=pl.BlockSpec((None, ATT_TILE, rows), lambda i, g, j: (i, j, g)),
        compiler_params=_cparams(("arbitrary", "arbitrary", "arbitrary")),
        name="sparse",
    )(qv, proj3, qv, qv, gt, ocmp, sel, slopes)


def _merge_kernel(x_ref, yr_ref, ya_ref, mgr_ref, mga_ref, wr_ref, wa_ref, wo_ref, o_ref):
    pr = _dot(yr_ref[...], wr_ref[...])
    pa = _dot(ya_ref[...], wa_ref[...])
    merged = _sigmoid(mgr_ref[...].astype(F32)) * pr + _sigmoid(mga_ref[...].astype(F32)) * pa
    o_ref[...] = x_ref[...] + _dot(merged.astype(BF16), wo_ref[...])


def _merge(x2, yr, ya, proj, wr, wa, wo):
    m, d = x2.shape
    mg0 = (RNN_COLS + KV_COLS) // d
    tile = lambda col: pl.BlockSpec((TOK_TILE, d), lambda i: (i, col))
    wfull = pl.BlockSpec((d, d), lambda i: (0, 0))
    return pl.pallas_call(
        _merge_kernel,
        out_shape=jax.ShapeDtypeStruct((m, d), F32),
        grid=(m // TOK_TILE,),
        in_specs=[tile(0), tile(0), tile(0), tile(mg0), tile(mg0 + 1), wfull, wfull, wfull],
        out_specs=tile(0),
        compiler_params=_cparams(("arbitrary",)),
        name="merge",
    )(x2, yr, ya, proj, proj, wr, wa, wo)


def _mem_kv_kernel(mem_ref, g_ref, wkt_ref, wv_ref, kt_ref, v_ref):
    a = _rms(mem_ref[...], g_ref[...]).astype(BF16)
    kt_ref[...] = _dot_nt(wkt_ref[...], a).astype(kt_ref.dtype)
    v_ref[...] = _dot(a, wv_ref[...]).astype(v_ref.dtype)


def _mem_kv(mem, g, wkt, wv):
    b, nm, d = mem.shape
    hw = H_X * X_HD
    return pl.pallas_call(
        _mem_kv_kernel,
        out_shape=(jax.ShapeDtypeStruct((b, hw, nm), BF16),
                   jax.ShapeDtypeStruct((b, nm, hw), BF16)),
        grid=(b,),
        in_specs=[pl.BlockSpec((None, nm, d), lambda i: (i, 0, 0)),
                  pl.BlockSpec((1, d), lambda i: (0, 0)),
                  pl.BlockSpec((hw, d), lambda i: (0, 0)),
                  pl.BlockSpec((d, hw), lambda i: (0, 0))],
        out_specs=(pl.BlockSpec((None, hw, nm), lambda i: (i, 0, 0)),
                   pl.BlockSpec((None, nm, hw), lambda i: (i, 0, 0))),
        compiler_params=_cparams(("arbitrary",)),
        name="mem_kv",
    )(mem, g, wkt, wv)


def _xattn_kernel(h_ref, g_ref, wq_ref, kt_ref, v_ref, wo_ref, o_ref):
    h = h_ref[...]
    a = _rms(h, g_ref[...]).astype(BF16)
    q = (_dot(a, wq_ref[...]) * (X_HD ** -0.5)).astype(BF16)
    outs = []
    for hh in range(H_X):
        cols = slice(hh * X_HD, (hh + 1) * X_HD)
        s = _dot(q[:, cols], kt_ref[cols, :])
        m = jnp.max(s, axis=-1, keepdims=True)
        p = jnp.exp(s - m)
        p = p / jnp.sum(p, axis=-1, keepdims=True)
        outs.append(_dot(p.astype(BF16), v_ref[:, cols]))
    o = jnp.concatenate(outs, axis=1).astype(BF16)
    o_ref[...] = h + _dot(o, wo_ref[...])


def _xattn(h3, g, wq, kt, v, wo):
    b, t, d = h3.shape
    hw = H_X * X_HD
    nm = v.shape[1]
    return pl.pallas_call(
        _xattn_kernel,
        out_shape=jax.ShapeDtypeStruct((b, t, d), F32),
        grid=(b, t // TOK_TILE),
        in_specs=[pl.BlockSpec((None, TOK_TILE, d), lambda i, j: (i, j, 0)),
                  pl.BlockSpec((1, d), lambda i, j: (0, 0)),
                  pl.BlockSpec((d, hw), lambda i, j: (0, 0)),
                  pl.BlockSpec((None, hw, nm), lambda i, j: (i, 0, 0)),
                  pl.BlockSpec((None, nm, hw), lambda i, j: (i, 0, 0)),
                  pl.BlockSpec((hw, d), lambda i, j: (0, 0))],
        out_specs=pl.BlockSpec((None, TOK_TILE, d), lambda i, j: (i, j, 0)),
        compiler_params=_cparams(("arbitrary", "arbitrary")),
        name="xattn",
    )(h3, g, wq, kt, v, wo)


def _ffn_kernel(h_ref, g_ref, wg_ref, wu_ref, wd_ref, gf_ref, o_ref, *, fc):
    h = h_ref[...]
    a = _rms(h, g_ref[...]).astype(BF16)
    acc = h
    for f in range(wg_ref.shape[1] // fc):
        cols = slice(f * fc, (f + 1) * fc)
        mid = _silu(_dot(a, wg_ref[:, cols])) * _dot(a, wu_ref[:, cols])
        acc = acc + _dot(mid.astype(BF16), wd_ref[cols, :])
    o_ref[...] = _rms(acc, gf_ref[...])


def _ffn(h2, g, wg, wu, wd, gf):
    m, d = h2.shape
    ff = wg.shape[1]
    const = lambda shape: pl.BlockSpec(shape, lambda i: (0, 0))
    return pl.pallas_call(
        functools.partial(_ffn_kernel, fc=ff // 2),
        out_shape=jax.ShapeDtypeStruct((m, d), F32),
        grid=(m // TOK_TILE,),
        in_specs=[pl.BlockSpec((TOK_TILE, d), lambda i: (i, 0)),
                  const((1, d)), const((d, ff)), const((d, ff)), const((ff, d)), const((1, d))],
        out_specs=pl.BlockSpec((TOK_TILE, d), lambda i: (i, 0)),
        compiler_params=_cparams(("arbitrary",)),
        name="ffn",
    )(h2, g, wg, wu, wd, gf)


def _split_w_in(w):
    sizes = (1024, 1024, 1024, 1024, H_ATT * ATT_HD, 6 * G_KV * ATT_HD, 3 * H_ATT, D_MODEL, D_MODEL)
    parts = []
    off = 0
    for s in sizes:
        parts.append(w[:, off:off + s])
        off += s
    return parts


def _layout_w_in(w):
    d = w.shape[0]
    q_r, f_r, i_r, og_r, q_a, kv, gate, mg_r, mg_a = _split_w_in(w)
    rnn = jnp.stack([q_r, f_r, i_r, og_r], axis=1).reshape(d, 4, H_RNN, RNN_DIM)
    rnn = rnn.transpose(0, 2, 1, 3).reshape(d, RNN_COLS)
    k_c, v_c, k_s, v_s, k_w, v_w = jnp.split(kv, 6, axis=1)
    kv4 = jnp.stack([k_c, v_c, k_s, k_w], axis=1).reshape(d, 4, G_KV, ATT_HD)
    kv4 = kv4.transpose(0, 2, 1, 3).reshape(d, KV_COLS)
    w_n = jnp.concatenate([rnn, kv4, mg_r, mg_a], axis=1).astype(BF16)
    gate_g = gate.reshape(d, G_KV, HG, 3).transpose(0, 1, 3, 2).reshape(d, G_KV, 3 * HG)
    gate_g = jnp.pad(gate_g, ((0, 0), (0, 0), (0, GATE_ROWS - 3 * HG))).reshape(d, G_KV * GATE_ROWS)
    w_t = jnp.concatenate([q_a * (ATT_HD ** -0.5), v_s, v_w, gate_g], axis=1).T.astype(BF16)
    return w_n, w_t


def _overlap_t(t):
    nc = (t - CMP_LEN) // CMP_STRIDE + 1
    ns = t // SLC_LEN
    starts = CMP_STRIDE * np.arange(N_CMP_PAD)
    s_start = SLC_LEN * np.arange(ns)
    ov = ((starts[None, :] + CMP_LEN > s_start[:, None]) & (starts[None, :] < s_start[:, None] + SLC_LEN)
          & (np.arange(N_CMP_PAD)[None, :] < nc))
    return jnp.asarray(ov.astype(np.float32))


def kernel(x, mem, g_mix, w_in, lower_bounds, g_rnn_out, pe_ck, w_ck1, w_ck2, pe_cv, w_cv1, w_cv2,
           w_proj_rnn, w_proj_att, w_out, g_xattn, g_mem, w_xq, w_xkv, w_xo, g_ffn, w_gate_up,
           w_down, g_final):
    b, t, d = x.shape
    depth = g_mix.shape[0]
    assert depth == 1, "the final RMSNorm is fused into the layer's FFN kernel"
    lbs = jnp.cumsum(jax.nn.softmax(lower_bounds.astype(F32), axis=0), axis=0)
    slopes = (2.0 ** (-8.0 * jnp.arange(1, H_ATT + 1, dtype=F32) / H_ATT)).reshape(G_KV, HG, 1, 1)
    overlap_t = _overlap_t(t)
    h = x
    for l in range(depth):
        w_n, w_t = _layout_w_in(w_in[l])
        x2 = h.reshape(b * t, d)
        proj = _in_proj(x2, g_mix[l][None, :], w_n)
        proj3 = proj.reshape(b, t, N_COLS)
        qv, gt = _in_proj_t(h, g_mix[l][None, :], w_t)
        y_r = _hgrn(proj3, lbs[l][None, :], g_rnn_out[l][None, :])
        kc, vct = _compress(proj3, pe_ck[l], pe_cv[l], w_ck1[l].astype(BF16), w_cv1[l].astype(BF16),
                            w_ck2[l].astype(BF16), w_cv2[l].T.astype(BF16))
        ocmp, sel = _cmp_select(qv, kc, vct, slopes, overlap_t, t)
        y_a = _sparse(qv, gt, proj3, ocmp, sel, slopes, t)
        h1 = _merge(x2, y_r.reshape(b * t, d), y_a.reshape(b * t, d), proj,
                    w_proj_rnn[l].astype(BF16), w_proj_att[l].astype(BF16), w_out[l].astype(BF16))
        w_xk, w_xv = jnp.split(w_xkv[l], 2, axis=1)
        kt, v = _mem_kv(mem, g_mem[l][None, :], w_xk.T.astype(BF16), w_xv.astype(BF16))
        h2 = _xattn(h1.reshape(b, t, d), g_xattn[l][None, :], w_xq[l].astype(BF16), kt, v,
                    w_xo[l].astype(BF16))
        w_g, w_u = jnp.split(w_gate_up[l], 2, axis=1)
        h = _ffn(h2.reshape(b * t, d), g_ffn[l][None, :], w_g.astype(BF16), w_u.astype(BF16),
                 w_down[l].astype(BF16), g_final[None, :]).reshape(b, t, d)
    return h
```

```python
import functools

import jax
import jax.numpy as jnp
import numpy as np
from jax import lax
from jax.experimental import pallas as pl
from jax.experimental.pallas import tpu as pltpu

F32 = jnp.float32
BF16 = jnp.bfloat16

D_MODEL = 1024
N_MEM = 256
H_RNN = 8
RNN_DIM = 128
RNN_CHUNK = 64
H_ATT = 16
ATT_HD = 64
G_KV = 4
HG = H_ATT // G_KV
CMP_LEN = 32
CMP_STRIDE = 16
CMP_HIDDEN = 128
SLC_LEN = 64
SLC_TOPK = 8
WINDOW = 512
FORCE_BONUS = 1.0e4
H_X = 4
X_HD = 128
D_FF = 2816
EPS = 1e-6

LANE = 128
VMEM_LIMIT = 56 * 1024 * 1024
TOK_TILE = 512
ATT_TILE = 128
CMP_TILE = 512
N_CMP_PAD = 128
HGRN_GROUP = 4
ONES_ROWS = 16
NEG_BIG = -1.0e30

QT_ROWS = H_ATT * ATT_HD
VT_ROWS = 2 * G_KV * ATT_HD
GATE_ROWS = 16
RNN_COLS = 4 * H_RNN * RNN_DIM
KV_COLS = 4 * G_KV * ATT_HD
MG_COLS = 2 * D_MODEL
N_COLS = RNN_COLS + KV_COLS + MG_COLS


def _cparams(sem):
    return pltpu.CompilerParams(dimension_semantics=sem, vmem_limit_bytes=VMEM_LIMIT)


def _rms(xf, g):
    return xf * lax.rsqrt(jnp.mean(xf * xf, axis=-1, keepdims=True) + EPS) * g


def _sigmoid(x):
    return 1.0 / (1.0 + jnp.exp(-x))


def _silu(x):
    return x * _sigmoid(x)


def _dot(a, b):
    return jnp.dot(a, b, preferred_element_type=F32)


def _dot_nt(a, b):
    return lax.dot_general(a, b, (((1,), (1,)), ((), ())), preferred_element_type=F32)


def _dot_tn(a, b):
    return lax.dot_general(a, b, (((0,), (0,)), ((), ())), preferred_element_type=F32)


def _in_proj_kernel(x_ref, g_ref, w_ref, o_ref, *, sub):
    a = _rms(x_ref[...], g_ref[...]).astype(BF16)
    tn = o_ref.shape[1]
    for n in range(tn // sub):
        o_ref[:, n * sub:(n + 1) * sub] = _dot(a, w_ref[:, n * sub:(n + 1) * sub]).astype(o_ref.dtype)


def _in_proj(x2, g, w):
    m, d = x2.shape
    n = w.shape[1]
    tn = n // 2
    return pl.pallas_call(
        functools.partial(_in_proj_kernel, sub=512),
        out_shape=jax.ShapeDtypeStruct((m, n), BF16),
        grid=(2, m // TOK_TILE),
        in_specs=[pl.BlockSpec((TOK_TILE, d), lambda j, i: (i, 0)),
                  pl.BlockSpec((1, d), lambda j, i: (0, 0)),
                  pl.BlockSpec((d, tn), lambda j, i: (0, j))],
        out_specs=pl.BlockSpec((TOK_TILE, tn), lambda j, i: (i, j)),
        compiler_params=_cparams(("arbitrary", "arbitrary")),
        name="in_proj",
    )(x2, g, w)


def _in_proj_t_kernel(x_ref, g_ref, w_ref, qv_ref, gt_ref):
    a = _rms(x_ref[...], g_ref[...]).astype(BF16)
    r = _dot_nt(w_ref[...], a)
    nqv = qv_ref.shape[1]
    for c in range(qv_ref.shape[0]):
        qv_ref[c] = r[:nqv, c * LANE:(c + 1) * LANE].astype(qv_ref.dtype)
        gt_ref[c] = r[nqv:, c * LANE:(c + 1) * LANE]


def _in_proj_t(x, g, wt):
    b, t, d = x.shape
    rows = wt.shape[0]
    nqv = QT_ROWS + VT_ROWS
    ngt = rows - nqv
    nc = TOK_TILE // LANE
    return pl.pallas_call(
        _in_proj_t_kernel,
        out_shape=(jax.ShapeDtypeStruct((b, t // LANE, nqv, LANE), BF16),
                   jax.ShapeDtypeStruct((b, t // LANE, ngt, LANE), F32)),
        grid=(b, t // TOK_TILE),
        in_specs=[pl.BlockSpec((None, TOK_TILE, d), lambda i, j: (i, j, 0)),
                  pl.BlockSpec((1, d), lambda i, j: (0, 0)),
                  pl.BlockSpec((rows, d), lambda i, j: (0, 0))],
        out_specs=(pl.BlockSpec((None, nc, nqv, LANE), lambda i, j: (i, j, 0, 0)),
                   pl.BlockSpec((None, nc, ngt, LANE), lambda i, j: (i, j, 0, 0))),
        compiler_params=_cparams(("arbitrary", "arbitrary")),
        name="in_proj_t",
    )(x, g, wt)


def _hgrn_kernel(r_ref, lb_ref, gn_ref, o_ref, qd_ref, oi_ref, ut_ref, dec_ref):
    c = RNN_CHUNK
    kd = RNN_DIM
    n_chunks = r_ref.shape[0] // c
    lb = lb_ref[...]
    gn = gn_ref[...]
    blk = HGRN_GROUP * c
    row = lax.broadcasted_iota(jnp.int32, (blk, blk), 0)
    col = lax.broadcasted_iota(jnp.int32, (blk, blk), 1)
    same_chunk = (row // c) == (col // c)
    causal = same_chunk & (row >= col)
    tril = causal.astype(BF16)
    ones_blk = same_chunk.astype(BF16)

    q = r_ref[:, 0:kd].astype(F32)
    fl = r_ref[:, kd:2 * kd].astype(F32)
    v = r_ref[:, 2 * kd:3 * kd]
    f = lb + (1.0 - lb) * _sigmoid(fl)
    k = 1.0 - f
    logf = jnp.log(f)
    hi = logf.astype(BF16)
    rem = logf - hi.astype(F32)
    mid = rem.astype(BF16)
    lo = (rem - mid.astype(F32)).astype(BF16)
    pieces = jnp.concatenate([hi, mid, lo], axis=1)

    def fold(x):
        return x[:, 0:kd] + x[:, kd:2 * kd] + x[:, 2 * kd:3 * kd]

    n_blk = r_ref.shape[0] // blk
    bcum = jnp.concatenate([fold(_dot(tril, pieces[g * blk:(g + 1) * blk])) for g in range(n_blk)], axis=0)
    b_last = jnp.concatenate([fold(_dot(ones_blk, pieces[g * blk:(g + 1) * blk])) for g in range(n_blk)], axis=0)
    q_dec = (_silu(q) * jnp.exp(bcum)).astype(BF16)
    k_dec = (k * jnp.exp(-bcum)).astype(BF16)
    k_end = (k * jnp.exp(b_last - bcum)).astype(BF16)
    qd_ref[...] = q_dec
    dec_ref[...] = jnp.exp(jnp.concatenate([b_last[n * c:n * c + 1, :] for n in range(n_chunks)], axis=0))
    for g in range(n_blk):
        rows = slice(g * blk, (g + 1) * blk)
        a = jnp.where(causal, _dot_nt(q_dec[rows], k_dec[rows]), 0.0)
        oi_ref[rows, :] = _dot(a.astype(BF16), v[rows])
    for n in range(n_chunks):
        rows = slice(n * c, (n + 1) * c)
        ut_ref[n] = _dot_tn(v[rows], k_end[rows])

    s_t = jnp.zeros((kd, kd), F32)
    for n in range(n_chunks):
        rows = slice(n * c, (n + 1) * c)
        o = oi_ref[rows, :] + _dot_nt(qd_ref[rows, :], s_t.astype(BF16))
        og = r_ref[rows, 3 * kd:4 * kd].astype(F32)
        o_ref[rows, :] = (_rms(o, gn) * _silu(og)).astype(o_ref.dtype)
        s_t = s_t * dec_ref[n:n + 1, :] + ut_ref[n]


def _hgrn(proj3, lb, gn):
    b, t, _ = proj3.shape
    return pl.pallas_call(
        _hgrn_kernel,
        out_shape=jax.ShapeDtypeStruct((b, t, H_RNN * RNN_DIM), BF16),
        grid=(b, H_RNN),
        in_specs=[pl.BlockSpec((None, t, 4 * RNN_DIM), lambda i, h: (i, 0, h)),
                  pl.BlockSpec((1, RNN_DIM), lambda i, h: (0, h)),
                  pl.BlockSpec((1, RNN_DIM), lambda i, h: (0, 0))],
        out_specs=pl.BlockSpec((None, t, RNN_DIM), lambda i, h: (i, 0, h)),
        scratch_shapes=[pltpu.VMEM((t, RNN_DIM), BF16),
                        pltpu.VMEM((t, RNN_DIM), F32),
                        pltpu.VMEM((t // RNN_CHUNK, RNN_DIM, RNN_DIM), F32),
                        pltpu.VMEM((t // RNN_CHUNK, RNN_DIM), F32)],
        compiler_params=_cparams(("arbitrary", "arbitrary")),
        name="hgrn",
    )(proj3, lb, gn)


def _compress_kernel(kv_ref, pek_ref, pev_ref, wk1_ref, wv1_ref, wk2_ref, wv2t_ref,
                     kc_ref, vct_ref, xs_ref, xk_ref, xv_ref):
    t = kv_ref.shape[0]
    hd = ATT_HD
    xs_ref[0:t, :] = kv_ref[...].astype(F32)
    xs_ref[t:, :] = jnp.zeros((xs_ref.shape[0] - t, xs_ref.shape[1]), F32)
    for l in range(CMP_LEN):
        blk = xs_ref[pl.ds(l, N_CMP_PAD, stride=CMP_STRIDE), :]
        xk_ref[:, l * hd:(l + 1) * hd] = (blk[:, 0:hd] + pek_ref[l:l + 1, :]).astype(BF16)
        xv_ref[:, l * hd:(l + 1) * hd] = (blk[:, hd:2 * hd] + pev_ref[l:l + 1, :]).astype(BF16)
    hk = _silu(_dot(xk_ref[...], wk1_ref[...])).astype(BF16)
    hv = _silu(_dot(xv_ref[...], wv1_ref[...])).astype(BF16)
    kc_ref[...] = _dot(hk, wk2_ref[...])
    vct_ref[...] = _dot_nt(wv2t_ref[...], hv)


def _compress(proj3, pek, pev, wk1, wv1, wk2, wv2t):
    b, t, _ = proj3.shape
    kv_unit0 = RNN_COLS // LANE
    flat = CMP_LEN * ATT_HD
    full = lambda shape: pl.BlockSpec(shape, lambda i, g: (0,) * len(shape))
    return pl.pallas_call(
        _compress_kernel,
        out_shape=(jax.ShapeDtypeStruct((b, G_KV, N_CMP_PAD, ATT_HD), F32),
                   jax.ShapeDtypeStruct((b, G_KV, ATT_HD, N_CMP_PAD), F32)),
        grid=(b, G_KV),
        in_specs=[pl.BlockSpec((None, t, LANE), lambda i, g: (i, 0, kv_unit0 + 2 * g)),
                  full((CMP_LEN, ATT_HD)), full((CMP_LEN, ATT_HD)),
                  full((flat, CMP_HIDDEN)), full((flat, CMP_HIDDEN)),
                  full((CMP_HIDDEN, ATT_HD)), full((ATT_HD, CMP_HIDDEN))],
        out_specs=(pl.BlockSpec((None, None, N_CMP_PAD, ATT_HD), lambda i, g: (i, g, 0, 0)),
                   pl.BlockSpec((None, None, ATT_HD, N_CMP_PAD), lambda i, g: (i, g, 0, 0))),
        scratch_shapes=[pltpu.VMEM((t + CMP_STRIDE, LANE), F32),
                        pltpu.VMEM((N_CMP_PAD, flat), BF16),
                        pltpu.VMEM((N_CMP_PAD, flat), BF16)],
        compiler_params=_cparams(("arbitrary", "arbitrary")),
        name="compress",
    )(proj3, pek, pev, wk1, wv1, wk2, wv2t)


def _cmp_select_kernel(qt_ref, kc_ref, vct_ref, slope_ref, ov_ref, ocmp_ref, sel_ref, *, n_cmp, n_sel):
    nch = qt_ref.shape[0]
    tt = nch * LANE
    t0 = pl.program_id(2) * tt
    qt = jnp.concatenate([qt_ref[c] for c in range(nch)], axis=1)
    kc = kc_ref[...].astype(BF16)
    vct = vct_ref[...].astype(BF16)
    n_i = lax.broadcasted_iota(jnp.int32, (N_CMP_PAD, tt), 0)
    t_i = lax.broadcasted_iota(jnp.int32, (N_CMP_PAD, tt), 1) + t0
    visible = (CMP_STRIDE * n_i + (CMP_LEN - 1) <= t_i) & (n_i < n_cmp)
    dist = t_i.astype(F32) - (n_i.astype(F32) * float(CMP_STRIDE) + (CMP_LEN - 1) / 2.0)
    psum = jnp.zeros((N_CMP_PAD, tt), F32)
    outs = []
    for h in range(HG):
        s = _dot(kc, qt[h * ATT_HD:(h + 1) * ATT_HD, :]) - slope_ref[h] * dist
        s = jnp.where(visible, s, -jnp.inf)
        m = jnp.max(s, axis=0, keepdims=True)
        m = jnp.where(m == -jnp.inf, 0.0, m)
        p = jnp.exp(s - m)
        d = jnp.sum(p, axis=0, keepdims=True)
        p = p / jnp.where(d > 0, d, 1.0)
        psum = psum + p
        outs.append(_dot(vct, p.astype(BF16)))
    ocmp = jnp.concatenate(outs, axis=0)
    for c in range(nch):
        ocmp_ref[c] = ocmp[:, c * LANE:(c + 1) * LANE]
    imp = jnp.dot(ov_ref[...], psum, preferred_element_type=F32,
                  precision=lax.Precision.HIGHEST)
    ns = imp.shape[0]
    j_i = lax.broadcasted_iota(jnp.int32, (ns, tt), 0)
    tq = lax.broadcasted_iota(jnp.int32, (ns, tt), 1) + t0
    cur = tq // SLC_LEN
    valid = j_i * SLC_LEN <= tq
    forced = (j_i == 0) | (j_i == cur) | (j_i == cur - 1)
    score = jnp.where(valid, imp + jnp.where(forced, FORCE_BONUS, 0.0), -jnp.inf)
    rank = jnp.zeros((ns, tt), jnp.int32)
    for jp in range(ns):
        sj = score[jp:jp + 1, :]
        ahead = (sj > score) | ((sj == score) & (jp < j_i))
        rank = rank + ahead.astype(jnp.int32)
    sel = (rank < n_sel).astype(F32)
    for c in range(nch):
        sel_ref[c] = sel[:, c * LANE:(c + 1) * LANE]


def _cmp_select(qv, kc, vct, slopes, overlap_t, t):
    b = qv.shape[0]
    ns = t // SLC_LEN
    nch = CMP_TILE // LANE
    n_cmp = (t - CMP_LEN) // CMP_STRIDE + 1
    n_sel = min(SLC_TOPK, ns)
    rows = HG * ATT_HD
    return pl.pallas_call(
        functools.partial(_cmp_select_kernel, n_cmp=n_cmp, n_sel=n_sel),
        out_shape=(jax.ShapeDtypeStruct((b, t // LANE, H_ATT * ATT_HD, LANE), F32),
                   jax.ShapeDtypeStruct((b, G_KV, t // LANE, ns, LANE), F32)),
        grid=(b, G_KV, t // CMP_TILE),
        in_specs=[pl.BlockSpec((None, nch, rows, LANE), lambda i, g, j: (i, j, g, 0)),
                  pl.BlockSpec((None, None, N_CMP_PAD, ATT_HD), lambda i, g, j: (i, g, 0, 0)),
                  pl.BlockSpec((None, None, ATT_HD, N_CMP_PAD), lambda i, g, j: (i, g, 0, 0)),
                  pl.BlockSpec((None, HG, 1, 1), lambda i, g, j: (g, 0, 0, 0)),
                  pl.BlockSpec((ns, N_CMP_PAD), lambda i, g, j: (0, 0))],
        out_specs=(pl.BlockSpec((None, nch, rows, LANE), lambda i, g, j: (i, j, g, 0)),
                   pl.BlockSpec((None, None, nch, ns, LANE), lambda i, g, j: (i, g, j, 0, 0))),
        compiler_params=_cparams(("arbitrary", "arbitrary", "arbitrary")),
        name="cmp_select",
    )(qv, kc, vct, slopes, overlap_t)


def _sparse_kernel(qt_ref, k_ref, vs_ref, vw_ref, gt_ref, ocmp_ref, sel_ref, slope_ref, o_ref,
                   sd_ref, qop_ref, sb_ref, s_ref, m_ref, acc_ref):
    tq = ATT_TILE
    kc = ATT_TILE
    hd = ATT_HD
    nl = HG * tq
    n_tiles = qt_ref.shape[0]
    wch = WINDOW // kc
    n_sel_items = n_tiles + 1
    blocks_per_chunk = kc // SLC_LEN
    slope = jnp.concatenate([jnp.broadcast_to(slope_ref[h], (1, tq)) for h in range(HG)], axis=1)
    k_i = lax.broadcasted_iota(jnp.int32, (kc, nl), 0)
    t_i = lax.broadcasted_iota(jnp.int32, (kc, nl), 1) % tq
    d0 = t_i - k_i
    sd0 = slope * d0.astype(F32)
    sd_ref[0] = sd0
    sd_ref[1] = jnp.where(d0 >= 0, sd0, -NEG_BIG)
    sd_ref[2] = jnp.where(d0 < 0, sd0, -NEG_BIG)
    chunk_bias = slope * float(kc)
    ones = jnp.ones((ONES_ROWS, kc), BF16)

    def pair(p, carry):
        tiles = (p, n_tiles - 1 - p)
        lo, hi = tiles
        for w, tile in enumerate(tiles):
            qt = qt_ref[tile]
            q_all = jnp.concatenate([qt[h * hd:(h + 1) * hd, :] for h in range(HG)], axis=1)
            zero = jnp.zeros_like(q_all)
            qop_ref[2 * w] = jnp.concatenate([q_all, zero], axis=0)
            qop_ref[2 * w + 1] = jnp.concatenate([zero, q_all], axis=0)
            sb = (sel_ref[tile] - 1.0) * (-NEG_BIG)
            sb_ref[w] = jnp.concatenate([sb] * HG, axis=1)
        m_ref[...] = jnp.full(m_ref.shape, NEG_BIG, F32)
        acc_ref[...] = jnp.zeros(acc_ref.shape, F32)

        def sel_item(i):
            which = jnp.where(i > lo, 1, 0)
            tile = jnp.where(i > lo, hi, lo)
            chunk = jnp.where(i > lo, i - lo - 1, i)
            return which, chunk, tile - chunk

        def win_item(w, j):
            chunk = tiles[w] - wch + j
            return jnp.maximum(chunk, 0), chunk >= 0, wch - j

        def col_max(r, delta_f):
            r8 = jnp.max(r.reshape(kc // 8, 8, nl), axis=0)
            return r8 - chunk_bias * delta_f

        for i in range(n_sel_items):
            which, chunk, delta = sel_item(i)
            rows = pl.ds(pl.multiple_of(chunk * kc, kc), kc)
            s = _dot(k_ref[rows, :], qop_ref[2 * which])
            sd = sd_ref[jnp.where(delta == 0, 1, 0)]
            bias = jnp.concatenate(
                [jnp.broadcast_to(sb_ref[which, pl.ds(chunk * blocks_per_chunk + jb, 1), :], (SLC_LEN, nl))
                 for jb in range(blocks_per_chunk)], axis=0)
            r = s - sd + bias
            s_ref[i] = r
            m_ref[2 * which] = jnp.maximum(m_ref[2 * which], col_max(r, delta.astype(F32)))
        for w in range(2):
            for j in range(wch + 1):
                chunk, valid, delta = win_item(w, j)
                rows = pl.ds(pl.multiple_of(chunk * kc, kc), kc)
                s = _dot(k_ref[rows, :], qop_ref[2 * w + 1])
                r = s - sd_ref[2 if j == 0 else (1 if j == wch else 0)]
                s_ref[n_sel_items + w * (wch + 1) + j] = r
                cm = jnp.where(valid, col_max(r, float(delta)), NEG_BIG)
                m_ref[2 * w + 1] = jnp.maximum(m_ref[2 * w + 1], cm)
        m_row = [jnp.max(m_ref[x], axis=0, keepdims=True) for x in range(4)]

        for i in range(n_sel_items):
            which, chunk, delta = sel_item(i)
            row = jnp.where(which == 1, m_row[2], m_row[0]) + chunk_bias * delta.astype(F32)
            pr = jnp.exp(s_ref[i] - row).astype(BF16)
            v_aug = jnp.concatenate([vs_ref[chunk], ones], axis=0)
            acc_ref[2 * which] += _dot(v_aug, pr)
        for w in range(2):
            for j in range(wch + 1):
                chunk, valid, delta = win_item(w, j)
                row = jnp.where(valid, m_row[2 * w + 1] + chunk_bias * float(delta), -NEG_BIG)
                pr = jnp.exp(s_ref[n_sel_items + w * (wch + 1) + j] - row).astype(BF16)
                v_aug = jnp.concatenate([vw_ref[chunk], ones], axis=0)
                acc_ref[2 * w + 1] += _dot(v_aug, pr)

        for w, tile in enumerate(tiles):
            a_s = acc_ref[2 * w]
            a_w = acc_ref[2 * w + 1]
            o_slc = a_s[:hd, :] / a_s[hd:hd + 1, :]
            o_win = a_w[:hd, :] / a_w[hd:hd + 1, :]
            gates = _sigmoid(gt_ref[tile])
            oc = ocmp_ref[tile]
            outs = []
            for h in range(HG):
                lanes = slice(h * tq, (h + 1) * tq)
                y = (gates[h:h + 1, :] * oc[h * hd:(h + 1) * hd, :]
                     + gates[HG + h:HG + h + 1, :] * o_slc[:, lanes]
                     + gates[2 * HG + h:2 * HG + h + 1, :] * o_win[:, lanes])
                outs.append(y.T)
            rows = pl.ds(pl.multiple_of(tile * tq, tq), tq)
            o_ref[rows, :] = jnp.concatenate(outs, axis=1).astype(o_ref.dtype)
        return carry

    lax.fori_loop(0, n_tiles // 2, pair, 0)


def _sparse(qv, gt, proj3, ocmp, sel, slopes, t):
    b = qv.shape[0]
    nchunk = t // LANE
    ns = t // SLC_LEN
    rows = HG * ATT_HD
    ksw_unit0 = RNN_COLS // LANE + 1
    vs_blk0 = QT_ROWS // ATT_HD
    vw_blk0 = vs_blk0 + G_KV
    nl = HG * ATT_TILE
    n_items = (nchunk + 1) + 2 * (WINDOW // ATT_TILE + 1)
    return pl.pallas_call(
        _sparse_kernel,
        out_shape=jax.ShapeDtypeStruct((b, t, H_ATT * ATT_HD), BF16),
        grid=(b, G_KV),
        in_specs=[pl.BlockSpec((None, nchunk, rows, LANE), lambda i, g: (i, 0, g, 0)),
                  pl.BlockSpec((None, t, LANE), lambda i, g: (i, 0, ksw_unit0 + 2 * g)),
                  pl.BlockSpec((None, nchunk, ATT_HD, LANE), lambda i, g: (i, 0, vs_blk0 + g, 0)),
                  pl.BlockSpec((None, nchunk, ATT_HD, LANE), lambda i, g: (i, 0, vw_blk0 + g, 0)),
                  pl.BlockSpec((None, nchunk, GATE_ROWS, LANE), lambda i, g: (i, 0, g, 0)),
                  pl.BlockSpec((None, nchunk, rows, LANE), lambda i, g: (i, 0, g, 0)),
                  pl.BlockSpec((None, None, nchunk, ns, LANE), lambda i, g: (i, g, 0, 0, 0)),
                  pl.BlockSpec((None, HG, 1, 1), lambda i, g: (g, 0, 0, 0))],
        out_specs=pl.BlockSpec((None, t, rows), lambda i, g: (i, 0, g)),
        scratch_shapes=[pltpu.VMEM((3, ATT_TILE, nl), F32),
                        pltpu.VMEM((4, LANE, nl), BF16),
                        pltpu.VMEM((2, ns, nl), F32),
                        pltpu.VMEM((n_items, ATT_TILE, nl), F32),
                        pltpu.VMEM((4, 8, nl), F32),
                        pltpu.VMEM((4, ATT_HD + ONES_ROWS, nl), F32)],
        compiler_params=_cparams(("arbitrary", "arbitrary")),
        name="sparse",
    )(qv, proj3, qv, qv, gt, ocmp, sel, slopes)


def _merge_kernel(x_ref, yr_ref, ya_ref, mgr_ref, mga_ref, wr_ref, wa_ref, wo_ref, o_ref):
    pr = _dot(yr_ref[...], wr_ref[...])
    pa = _dot(ya_ref[...], wa_ref[...])
    merged = _sigmoid(mgr_ref[...].astype(F32)) * pr + _sigmoid(mga_ref[...].astype(F32)) * pa
    o_ref[...] = x_ref[...] + _dot(merged.astype(BF16), wo_ref[...])


def _merge(x2, yr, ya, proj, wr, wa, wo):
    m, d = x2.shape
    mg0 = (RNN_COLS + KV_COLS) // d
    tile = lambda col: pl.BlockSpec((TOK_TILE, d), lambda i: (i, col))
    wfull = pl.BlockSpec((d, d), lambda i: (0, 0))
    return pl.pallas_call(
        _merge_kernel,
        out_shape=jax.ShapeDtypeStruct((m, d), F32),
        grid=(m // TOK_TILE,),
        in_specs=[tile(0), tile(0), tile(0), tile(mg0), tile(mg0 + 1), wfull, wfull, wfull],
        out_specs=tile(0),
        compiler_params=_cparams(("arbitrary",)),
        name="merge",
    )(x2, yr, ya, proj, proj, wr, wa, wo)


def _mem_kv_kernel(mem_ref, g_ref, wkt_ref, wv_ref, kt_ref, v_ref):
    a = _rms(mem_ref[...], g_ref[...]).astype(BF16)
    kt_ref[...] = _dot_nt(wkt_ref[...], a).astype(kt_ref.dtype)
    v_ref[...] = _dot(a, wv_ref[...]).astype(v_ref.dtype)


def _mem_kv(mem, g, wkt, wv):
    b, nm, d = mem.shape
    hw = H_X * X_HD
    return pl.pallas_call(
        _mem_kv_kernel,
        out_shape=(jax.ShapeDtypeStruct((b, hw, nm), BF16),
                   jax.ShapeDtypeStruct((b, nm, hw), BF16)),
        grid=(b,),
        in_specs=[pl.BlockSpec((None, nm, d), lambda i: (i, 0, 0)),
                  pl.BlockSpec((1, d), lambda i: (0, 0)),
                  pl.BlockSpec((hw, d), lambda i: (0, 0)),
                  pl.BlockSpec((d, hw), lambda i: (0, 0))],
        out_specs=(pl.BlockSpec((None, hw, nm), lambda i: (i, 0, 0)),
                   pl.BlockSpec((None, nm, hw), lambda i: (i, 0, 0))),
        compiler_params=_cparams(("arbitrary",)),
        name="mem_kv",
    )(mem, g, wkt, wv)


def _xattn_kernel(h_ref, g_ref, wq_ref, kt_ref, v_ref, wo_ref, o_ref):
    h = h_ref[...]
    a = _rms(h, g_ref[...]).astype(BF16)
    q = (_dot(a, wq_ref[...]) * (X_HD ** -0.5)).astype(BF16)
    outs = []
    for hh in range(H_X):
        cols = slice(hh * X_HD, (hh + 1) * X_HD)
        s = _dot(q[:, cols], kt_ref[cols, :])
        m = jnp.max(s, axis=-1, keepdims=True)
        p = jnp.exp(s - m)
        p = p / jnp.sum(p, axis=-1, keepdims=True)
        outs.append(_dot(p.astype(BF16), v_ref[:, cols]))
    o = jnp.concatenate(outs, axis=1).astype(BF16)
    o_ref[...] = h + _dot(o, wo_ref[...])


def _xattn(h3, g, wq, kt, v, wo):
    b, t, d = h3.shape
    hw = H_X * X_HD
    nm = v.shape[1]
    return pl.pallas_call(
        _xattn_kernel,
        out_shape=jax.ShapeDtypeStruct((b, t, d), F32),
        grid=(b, t // TOK_TILE),
        in_specs=[pl.BlockSpec((None, TOK_TILE, d), lambda i, j: (i, j, 0)),
                  pl.BlockSpec((1, d), lambda i, j: (0, 0)),
                  pl.BlockSpec((d, hw), lambda i, j: (0, 0)),
                  pl.BlockSpec((None, hw, nm), lambda i, j: (i, 0, 0)),
                  pl.BlockSpec((None, nm, hw), lambda i, j: (i, 0, 0)),
                  pl.BlockSpec((hw, d), lambda i, j: (0, 0))],
        out_specs=pl.BlockSpec((None, TOK_TILE, d), lambda i, j: (i, j, 0)),
        compiler_params=_cparams(("arbitrary", "arbitrary")),
        name="xattn",
    )(h3, g, wq, kt, v, wo)


def _ffn_kernel(h_ref, g_ref, wg_ref, wu_ref, wd_ref, gf_ref, o_ref, *, fc):
    h = h_ref[...]
    a = _rms(h, g_ref[...]).astype(BF16)
    acc = h
    for f in range(wg_ref.shape[1] // fc):
        cols = slice(f * fc, (f + 1) * fc)
        mid = _silu(_dot(a, wg_ref[:, cols])) * _dot(a, wu_ref[:, cols])
        acc = acc + _dot(mid.astype(BF16), wd_ref[cols, :])
    o_ref[...] = _rms(acc, gf_ref[...])


def _ffn(h2, g, wg, wu, wd, gf):
    m, d = h2.shape
    ff = wg.shape[1]
    const = lambda shape: pl.BlockSpec(shape, lambda i: (0, 0))
    return pl.pallas_call(
        functools.partial(_ffn_kernel, fc=ff // 2),
        out_shape=jax.ShapeDtypeStruct((m, d), F32),
        grid=(m // TOK_TILE,),
        in_specs=[pl.BlockSpec((TOK_TILE, d), lambda i: (i, 0)),
                  const((1, d)), const((d, ff)), const((d, ff)), const((ff, d)), const((1, d))],
        out_specs=pl.BlockSpec((TOK_TILE, d), lambda i: (i, 0)),
        compiler_params=_cparams(("arbitrary",)),
        name="ffn",
    )(h2, g, wg, wu, wd, gf)


def _split_w_in(w):
    sizes = (1024, 1024, 1024, 1024, H_ATT * ATT_HD, 6 * G_KV * ATT_HD, 3 * H_ATT, D_MODEL, D_MODEL)
    parts = []
    off = 0
    for s in sizes:
        parts.append(w[:, off:off + s])
        off += s
    return parts


def _layout_w_in(w):
    d = w.shape[0]
    q_r, f_r, i_r, og_r, q_a, kv, gate, mg_r, mg_a = _split_w_in(w)
    rnn = jnp.stack([q_r, f_r, i_r, og_r], axis=1).reshape(d, 4, H_RNN, RNN_DIM)
    rnn = rnn.transpose(0, 2, 1, 3).reshape(d, RNN_COLS)
    k_c, v_c, k_s, v_s, k_w, v_w = jnp.split(kv, 6, axis=1)
    kv4 = jnp.stack([k_c, v_c, k_s, k_w], axis=1).reshape(d, 4, G_KV, ATT_HD)
    kv4 = kv4.transpose(0, 2, 1, 3).reshape(d, KV_COLS)
    w_n = jnp.concatenate([rnn, kv4, mg_r, mg_a], axis=1).astype(BF16)
    gate_g = gate.reshape(d, G_KV, HG, 3).transpose(0, 1, 3, 2).reshape(d, G_KV, 3 * HG)
    gate_g = jnp.pad(gate_g, ((0, 0), (0, 0), (0, GATE_ROWS - 3 * HG))).reshape(d, G_KV * GATE_ROWS)
    w_t = jnp.concatenate([q_a * (ATT_HD ** -0.5), v_s, v_w, gate_g], axis=1).T.astype(BF16)
    return w_n, w_t


def _overlap_t(t):
    nc = (t - CMP_LEN) // CMP_STRIDE + 1
    ns = t // SLC_LEN
    starts = CMP_STRIDE * np.arange(N_CMP_PAD)
    s_start = SLC_LEN * np.arange(ns)
    ov = ((starts[None, :] + CMP_LEN > s_start[:, None]) & (starts[None, :] < s_start[:, None] + SLC_LEN)
          & (np.arange(N_CMP_PAD)[None, :] < nc))
    return jnp.asarray(ov.astype(np.float32))


def kernel(x, mem, g_mix, w_in, lower_bounds, g_rnn_out, pe_ck, w_ck1, w_ck2, pe_cv, w_cv1, w_cv2,
           w_proj_rnn, w_proj_att, w_out, g_xattn, g_mem, w_xq, w_xkv, w_xo, g_ffn, w_gate_up,
           w_down, g_final):
    b, t, d = x.shape
    depth = g_mix.shape[0]
    assert depth == 1, "the final RMSNorm is fused into the layer's FFN kernel"
    lbs = jnp.cumsum(jax.nn.softmax(lower_bounds.astype(F32), axis=0), axis=0)
    slopes = (2.0 ** (-8.0 * jnp.arange(1, H_ATT + 1, dtype=F32) / H_ATT)).reshape(G_KV, HG, 1, 1)
    overlap_t = _overlap_t(t)
    h = x
    for l in range(depth):
        w_n, w_t = _layout_w_in(w_in[l])
        x2 = h.reshape(b * t, d)
        proj = _in_proj(x2, g_mix[l][None, :], w_n)
        proj3 = proj.reshape(b, t, N_COLS)
        qv, gt = _in_proj_t(h, g_mix[l][None, :], w_t)
        y_r = _hgrn(proj3, lbs[l][None, :], g_rnn_out[l][None, :])
        kc, vct = _compress(proj3, pe_ck[l], pe_cv[l], w_ck1[l].astype(BF16), w_cv1[l].astype(BF16),
                            w_ck2[l].astype(BF16), w_cv2[l].T.astype(BF16))
        ocmp, sel = _cmp_select(qv, kc, vct, slopes, overlap_t, t)
        y_a = _sparse(qv, gt, proj3, ocmp, sel, slopes, t)
        h1 = _merge(x2, y_r.reshape(b * t, d), y_a.reshape(b * t, d), proj,
                    w_proj_rnn[l].astype(BF16), w_proj_att[l].astype(BF16), w_out[l].astype(BF16))
        w_xk, w_xv = jnp.split(w_xkv[l], 2, axis=1)
        kt, v = _mem_kv(mem, g_mem[l][None, :], w_xk.T.astype(BF16), w_xv.astype(BF16))
        h2 = _xattn(h1.reshape(b, t, d), g_xattn[l][None, :], w_xq[l].astype(BF16), kt, v,
                    w_xo[l].astype(BF16))
        w_g, w_u = jnp.split(w_gate_up[l], 2, axis=1)
        h = _ffn(h2.reshape(b * t, d), g_ffn[l][None, :], w_g.astype(BF16), w_u.astype(BF16),
                 w_down[l].astype(BF16), g_final[None, :]).reshape(b, t, d)
    return h
```

```python
import functools

import jax
import jax.numpy as jnp
import numpy as np
from jax import lax
from jax.experimental import pallas as pl
from jax.experimental.pallas import tpu as pltpu

F32 = jnp.float32
BF16 = jnp.bfloat16

D_MODEL = 1024
N_MEM = 256
H_RNN = 8
RNN_DIM = 128
RNN_CHUNK = 64
H_ATT = 16
ATT_HD = 64
G_KV = 4
HG = H_ATT // G_KV
CMP_LEN = 32
CMP_STRIDE = 16
CMP_HIDDEN = 128
SLC_LEN = 64
SLC_TOPK = 8
WINDOW = 512
FORCE_BONUS = 1.0e4
H_X = 4
X_HD = 128
D_FF = 2816
EPS = 1e-6

LANE = 128
VMEM_LIMIT = 56 * 1024 * 1024
TOK_TILE = 512
ATT_TILE = 128
CMP_TILE = 512
N_CMP_PAD = 128
HGRN_GROUP = 4
ONES_ROWS = 16
NEG_BIG = -1.0e30
LOG2E = 1.4426950408889634
AUG_SEL_ROW0 = 8

QT_ROWS = H_ATT * ATT_HD
VT_ROWS = 2 * G_KV * ATT_HD
GATE_ROWS = 16
RNN_COLS = 4 * H_RNN * RNN_DIM
KV_COLS = 4 * G_KV * ATT_HD
MG_COLS = 2 * D_MODEL
N_COLS = RNN_COLS + KV_COLS + MG_COLS


def _cparams(sem):
    return pltpu.CompilerParams(dimension_semantics=sem, vmem_limit_bytes=VMEM_LIMIT)


def _rms(xf, g):
    return xf * lax.rsqrt(jnp.mean(xf * xf, axis=-1, keepdims=True) + EPS) * g


def _sigmoid(x):
    return 1.0 / (1.0 + jnp.exp(-x))


def _silu(x):
    return x * _sigmoid(x)


def _dot(a, b):
    return jnp.dot(a, b, preferred_element_type=F32)


def _dot_nt(a, b):
    return lax.dot_general(a, b, (((1,), (1,)), ((), ())), preferred_element_type=F32)


def _dot_tn(a, b):
    return lax.dot_general(a, b, (((0,), (0,)), ((), ())), preferred_element_type=F32)


def _in_proj_kernel(x_ref, g_ref, w_ref, o_ref, *, sub):
    a = _rms(x_ref[...], g_ref[...]).astype(BF16)
    tn = o_ref.shape[1]
    for n in range(tn // sub):
        o_ref[:, n * sub:(n + 1) * sub] = _dot(a, w_ref[:, n * sub:(n + 1) * sub]).astype(o_ref.dtype)


def _in_proj(x2, g, w):
    m, d = x2.shape
    n = w.shape[1]
    tn = n // 2
    return pl.pallas_call(
        functools.partial(_in_proj_kernel, sub=512),
        out_shape=jax.ShapeDtypeStruct((m, n), BF16),
        grid=(2, m // TOK_TILE),
        in_specs=[pl.BlockSpec((TOK_TILE, d), lambda j, i: (i, 0)),
                  pl.BlockSpec((1, d), lambda j, i: (0, 0)),
                  pl.BlockSpec((d, tn), lambda j, i: (0, j))],
        out_specs=pl.BlockSpec((TOK_TILE, tn), lambda j, i: (i, j)),
        compiler_params=_cparams(("arbitrary", "arbitrary")),
        name="in_proj",
    )(x2, g, w)


def _in_proj_t_kernel(x_ref, g_ref, w_ref, qv_ref, gt_ref):
    a = _rms(x_ref[...], g_ref[...]).astype(BF16)
    r = _dot_nt(w_ref[...], a)
    nqv = qv_ref.shape[1]
    for c in range(qv_ref.shape[0]):
        qv_ref[c] = r[:nqv, c * LANE:(c + 1) * LANE].astype(qv_ref.dtype)
        gt_ref[c] = r[nqv:, c * LANE:(c + 1) * LANE]


def _in_proj_t(x, g, wt):
    b, t, d = x.shape
    rows = wt.shape[0]
    nqv = QT_ROWS + VT_ROWS
    ngt = rows - nqv
    nc = TOK_TILE // LANE
    return pl.pallas_call(
        _in_proj_t_kernel,
        out_shape=(jax.ShapeDtypeStruct((b, t // LANE, nqv, LANE), BF16),
                   jax.ShapeDtypeStruct((b, t // LANE, ngt, LANE), F32)),
        grid=(b, t // TOK_TILE),
        in_specs=[pl.BlockSpec((None, TOK_TILE, d), lambda i, j: (i, j, 0)),
                  pl.BlockSpec((1, d), lambda i, j: (0, 0)),
                  pl.BlockSpec((rows, d), lambda i, j: (0, 0))],
        out_specs=(pl.BlockSpec((None, nc, nqv, LANE), lambda i, j: (i, j, 0, 0)),
                   pl.BlockSpec((None, nc, ngt, LANE), lambda i, j: (i, j, 0, 0))),
        compiler_params=_cparams(("arbitrary", "arbitrary")),
        name="in_proj_t",
    )(x, g, wt)


def _hgrn_kernel(r_ref, lb_ref, gn_ref, o_ref, qd_ref, oi_ref, ut_ref, dec_ref):
    c = RNN_CHUNK
    kd = RNN_DIM
    n_chunks = r_ref.shape[0] // c
    lb = lb_ref[...]
    gn = gn_ref[...]
    blk = HGRN_GROUP * c
    row = lax.broadcasted_iota(jnp.int32, (blk, blk), 0)
    col = lax.broadcasted_iota(jnp.int32, (blk, blk), 1)
    same_chunk = (row // c) == (col // c)
    causal = same_chunk & (row >= col)
    tril = causal.astype(BF16)
    ones_blk = same_chunk.astype(BF16)

    q = r_ref[:, 0:kd].astype(F32)
    fl = r_ref[:, kd:2 * kd].astype(F32)
    v = r_ref[:, 2 * kd:3 * kd]
    f = lb + (1.0 - lb) * _sigmoid(fl)
    k = 1.0 - f
    logf = jnp.log(f)
    hi = logf.astype(BF16)
    rem = logf - hi.astype(F32)
    mid = rem.astype(BF16)
    lo = (rem - mid.astype(F32)).astype(BF16)
    pieces = jnp.concatenate([hi, mid, lo], axis=1)

    def fold(x):
        return x[:, 0:kd] + x[:, kd:2 * kd] + x[:, 2 * kd:3 * kd]

    n_blk = r_ref.shape[0] // blk
    bcum = jnp.concatenate([fold(_dot(tril, pieces[g * blk:(g + 1) * blk])) for g in range(n_blk)], axis=0)
    b_last = jnp.concatenate([fold(_dot(ones_blk, pieces[g * blk:(g + 1) * blk])) for g in range(n_blk)], axis=0)
    q_dec = (_silu(q) * jnp.exp(bcum)).astype(BF16)
    k_dec = (k * jnp.exp(-bcum)).astype(BF16)
    k_end = (k * jnp.exp(b_last - bcum)).astype(BF16)
    qd_ref[...] = q_dec
    dec_ref[...] = jnp.exp(jnp.concatenate([b_last[n * c:n * c + 1, :] for n in range(n_chunks)], axis=0))
    for g in range(n_blk):
        rows = slice(g * blk, (g + 1) * blk)
        a = jnp.where(causal, _dot_nt(q_dec[rows], k_dec[rows]), 0.0)
        oi_ref[rows, :] = _dot(a.astype(BF16), v[rows])
    for n in range(n_chunks):
        rows = slice(n * c, (n + 1) * c)
        ut_ref[n] = _dot_tn(v[rows], k_end[rows])

    s_t = jnp.zeros((kd, kd), F32)
    for n in range(n_chunks):
        rows = slice(n * c, (n + 1) * c)
        o = oi_ref[rows, :] + _dot_nt(qd_ref[rows, :], s_t.astype(BF16))
        og = r_ref[rows, 3 * kd:4 * kd].astype(F32)
        o_ref[rows, :] = (_rms(o, gn) * _silu(og)).astype(o_ref.dtype)
        s_t = s_t * dec_ref[n:n + 1, :] + ut_ref[n]


def _hgrn(proj3, lb, gn):
    b, t, _ = proj3.shape
    return pl.pallas_call(
        _hgrn_kernel,
        out_shape=jax.ShapeDtypeStruct((b, t, H_RNN * RNN_DIM), BF16),
        grid=(b, H_RNN),
        in_specs=[pl.BlockSpec((None, t, 4 * RNN_DIM), lambda i, h: (i, 0, h)),
                  pl.BlockSpec((1, RNN_DIM), lambda i, h: (0, h)),
                  pl.BlockSpec((1, RNN_DIM), lambda i, h: (0, 0))],
        out_specs=pl.BlockSpec((None, t, RNN_DIM), lambda i, h: (i, 0, h)),
        scratch_shapes=[pltpu.VMEM((t, RNN_DIM), BF16),
                        pltpu.VMEM((t, RNN_DIM), F32),
                        pltpu.VMEM((t // RNN_CHUNK, RNN_DIM, RNN_DIM), F32),
                        pltpu.VMEM((t // RNN_CHUNK, RNN_DIM), F32)],
        compiler_params=_cparams(("arbitrary", "arbitrary")),
        name="hgrn",
    )(proj3, lb, gn)


def _compress_kernel(kv_ref, pek_ref, pev_ref, wk1_ref, wv1_ref, wk2_ref, wv2t_ref,
                     kc_ref, vct_ref, xs_ref, xk_ref, xv_ref):
    t = kv_ref.shape[0]
    hd = ATT_HD
    xs_ref[0:t, :] = kv_ref[...].astype(F32)
    xs_ref[t:, :] = jnp.zeros((xs_ref.shape[0] - t, xs_ref.shape[1]), F32)
    for l in range(CMP_LEN):
        blk = xs_ref[pl.ds(l, N_CMP_PAD, stride=CMP_STRIDE), :]
        xk_ref[:, l * hd:(l + 1) * hd] = (blk[:, 0:hd] + pek_ref[l:l + 1, :]).astype(BF16)
        xv_ref[:, l * hd:(l + 1) * hd] = (blk[:, hd:2 * hd] + pev_ref[l:l + 1, :]).astype(BF16)
    hk = _silu(_dot(xk_ref[...], wk1_ref[...])).astype(BF16)
    hv = _silu(_dot(xv_ref[...], wv1_ref[...])).astype(BF16)
    kc_ref[...] = _dot(hk, wk2_ref[...])
    vct_ref[...] = _dot_nt(wv2t_ref[...], hv)


def _compress(proj3, pek, pev, wk1, wv1, wk2, wv2t):
    b, t, _ = proj3.shape
    kv_unit0 = RNN_COLS // LANE
    flat = CMP_LEN * ATT_HD
    full = lambda shape: pl.BlockSpec(shape, lambda i, g: (0,) * len(shape))
    return pl.pallas_call(
        _compress_kernel,
        out_shape=(jax.ShapeDtypeStruct((b, G_KV, N_CMP_PAD, ATT_HD), F32),
                   jax.ShapeDtypeStruct((b, G_KV, ATT_HD, N_CMP_PAD), F32)),
        grid=(b, G_KV),
        in_specs=[pl.BlockSpec((None, t, LANE), lambda i, g: (i, 0, kv_unit0 + 2 * g)),
                  full((CMP_LEN, ATT_HD)), full((CMP_LEN, ATT_HD)),
                  full((flat, CMP_HIDDEN)), full((flat, CMP_HIDDEN)),
                  full((CMP_HIDDEN, ATT_HD)), full((ATT_HD, CMP_HIDDEN))],
        out_specs=(pl.BlockSpec((None, None, N_CMP_PAD, ATT_HD), lambda i, g: (i, g, 0, 0)),
                   pl.BlockSpec((None, None, ATT_HD, N_CMP_PAD), lambda i, g: (i, g, 0, 0))),
        scratch_shapes=[pltpu.VMEM((t + CMP_STRIDE, LANE), F32),
                        pltpu.VMEM((N_CMP_PAD, flat), BF16),
                        pltpu.VMEM((N_CMP_PAD, flat), BF16)],
        compiler_params=_cparams(("arbitrary", "arbitrary")),
        name="compress",
    )(proj3, pek, pev, wk1, wv1, wk2, wv2t)


def _cmp_select_kernel(qt_ref, kc_ref, vct_ref, slope_ref, ov_ref, ocmp_ref, selb_ref, *, n_cmp, n_sel):
    nct = CMP_TILE // LANE
    tt = CMP_TILE
    ns = ov_ref.shape[0]
    kc = kc_ref[...].astype(BF16)
    vct = vct_ref[...].astype(BF16)
    ov = ov_ref[...]
    n_i = lax.broadcasted_iota(jnp.int32, (N_CMP_PAD, tt), 0)
    t_rel = lax.broadcasted_iota(jnp.int32, (N_CMP_PAD, tt), 1)
    centre = n_i.astype(F32) * float(CMP_STRIDE) + (CMP_LEN - 1) / 2.0
    bias = [slope_ref[h] * LOG2E * centre for h in range(HG)]
    j_i = lax.broadcasted_iota(jnp.int32, (8, tt), 0)
    tq_rel = lax.broadcasted_iota(jnp.int32, (8, tt), 1)
    for tile in range(qt_ref.shape[0] // nct):
        t0 = tile * tt
        qt = jnp.concatenate([qt_ref[tile * nct + c] for c in range(nct)], axis=1)
        visible = (CMP_STRIDE * n_i + (CMP_LEN - 1) - t0 <= t_rel) & (n_i < n_cmp)
        psum = jnp.zeros((N_CMP_PAD, tt), F32)
        for h in range(HG):
            s = _dot(kc, qt[h * ATT_HD:(h + 1) * ATT_HD, :]) + bias[h]
            s = jnp.where(visible, s, -jnp.inf)
            m = jnp.max(s, axis=0, keepdims=True)
            m = jnp.where(m == -jnp.inf, 0.0, m)
            p = jnp.exp2(s - m)
            d = jnp.sum(p, axis=0, keepdims=True)
            p = p * (1.0 / jnp.where(d > 0, d, 1.0))
            psum = psum + p
            o = _dot(vct, p.astype(BF16))
            for c in range(nct):
                ocmp_ref[tile * nct + c, h * ATT_HD:(h + 1) * ATT_HD, :] = o[:, c * LANE:(c + 1) * LANE]
        hi = psum.astype(BF16)
        rem = psum - hi.astype(F32)
        mid = rem.astype(BF16)
        lo = (rem - mid.astype(F32)).astype(BF16)
        imp3 = _dot(ov, jnp.concatenate([hi, mid, lo], axis=1))
        imp = imp3[:, 0:tt] + imp3[:, tt:2 * tt] + imp3[:, 2 * tt:3 * tt]
        score = []
        for rg in range(ns // 8):
            j = j_i + 8 * rg
            tq = tq_rel + t0
            cur = tq // SLC_LEN
            forced = (j == 0) | (j == cur) | (j == cur - 1)
            sc = imp[8 * rg:8 * rg + 8, :] + jnp.where(forced, FORCE_BONUS, 0.0)
            score.append(jnp.where(j * SLC_LEN <= tq, sc, -jnp.inf))
        rank = [jnp.zeros((8, tt), F32) for _ in score]
        for jp in range(ns):
            sj = jnp.broadcast_to(score[jp // 8][jp % 8:jp % 8 + 1, :], (8, tt))
            for rg in range(ns // 8):
                ge = jnp.where(sj >= score[rg], 1.0, 0.0)
                gt = jnp.where(sj > score[rg], 1.0, 0.0)
                if 8 * rg > jp:
                    ahead = ge
                elif 8 * rg + 7 <= jp:
                    ahead = gt
                else:
                    ahead = jnp.where(j_i + 8 * rg > jp, ge, gt)
                rank[rg] = rank[rg] + ahead
        for rg in range(ns // 8):
            selb = jnp.where(rank[rg] < float(n_sel), 0.0, NEG_BIG)
            for c in range(nct):
                selb_ref[tile * nct + c, 8 * rg:8 * rg + 8, :] = selb[:, c * LANE:(c + 1) * LANE]


def _cmp_select(qv, kc, vct, slopes, overlap_t, t):
    b = qv.shape[0]
    ns = t // SLC_LEN
    nch = t // LANE
    n_cmp = (t - CMP_LEN) // CMP_STRIDE + 1
    n_sel = min(SLC_TOPK, ns)
    rows = HG * ATT_HD
    return pl.pallas_call(
        functools.partial(_cmp_select_kernel, n_cmp=n_cmp, n_sel=n_sel),
        out_shape=(jax.ShapeDtypeStruct((b, nch, H_ATT * ATT_HD, LANE), F32),
                   jax.ShapeDtypeStruct((b, G_KV, nch, ns, LANE), F32)),
        grid=(b, G_KV),
        in_specs=[pl.BlockSpec((None, nch, rows, LANE), lambda i, g: (i, 0, g, 0)),
                  pl.BlockSpec((None, None, N_CMP_PAD, ATT_HD), lambda i, g: (i, g, 0, 0)),
                  pl.BlockSpec((None, None, ATT_HD, N_CMP_PAD), lambda i, g: (i, g, 0, 0)),
                  pl.BlockSpec((None, HG, 1, 1), lambda i, g: (g, 0, 0, 0)),
                  pl.BlockSpec((ns, N_CMP_PAD), lambda i, g: (0, 0))],
        out_specs=(pl.BlockSpec((None, nch, rows, LANE), lambda i, g: (i, 0, g, 0)),
                   pl.BlockSpec((None, None, nch, ns, LANE), lambda i, g: (i, g, 0, 0, 0))),
        compiler_params=_cparams(("arbitrary", "arbitrary")),
        name="cmp_select",
    )(qv, kc, vct, slopes, overlap_t)


def _sparse_kernel(qt_ref, k_ref, ka_ref, vs_ref, vw_ref, gt_ref, ocmp_ref, selb_ref, slope_ref, o_ref,
                   msk_ref, qop_ref, s_ref, m_ref, acc_ref):
    tq = ATT_TILE
    kc = ATT_TILE
    hd = ATT_HD
    nl = HG * tq
    n_tiles = qt_ref.shape[0]
    ns = selb_ref.shape[1]
    wch = WINDOW // kc
    n_sel_items = n_tiles + 1
    slope = jnp.concatenate([jnp.broadcast_to(slope_ref[h], (1, tq)) for h in range(HG)], axis=1) * LOG2E
    s_hi = slope.astype(BF16).astype(F32)
    s_mid = (slope - s_hi).astype(BF16).astype(F32)
    s_lo = (slope - s_hi - s_mid).astype(BF16).astype(F32)
    slope_rows = jnp.concatenate([s_hi, s_mid, s_lo, jnp.zeros((AUG_SEL_ROW0 - 3, nl), F32)], axis=0)
    aug_tail = jnp.zeros((LANE - AUG_SEL_ROW0 - ns, nl), F32)
    aug_win = jnp.concatenate([slope_rows, jnp.zeros((ns, nl), F32), aug_tail], axis=0).astype(BF16)
    k_i = lax.broadcasted_iota(jnp.int32, (kc, nl), 0)
    t_i = lax.broadcasted_iota(jnp.int32, (kc, nl), 1) % tq
    d0 = t_i - k_i
    msk_ref[0] = jnp.zeros((kc, nl), F32)
    msk_ref[1] = jnp.where(d0 >= 0, 0.0, NEG_BIG)
    msk_ref[2] = jnp.where(d0 < 0, 0.0, NEG_BIG)
    chunk_bias = slope * float(kc)
    ones = jnp.ones((ONES_ROWS, kc), BF16)

    def pair(p, carry):
        tiles = (p, n_tiles - 1 - p)
        lo, hi = tiles
        for w, tile in enumerate(tiles):
            qt = qt_ref[tile]
            q_all = jnp.concatenate([qt[h * hd:(h + 1) * hd, :] for h in range(HG)], axis=1)
            zero = jnp.zeros_like(q_all)
            selb = jnp.concatenate([selb_ref[tile]] * HG, axis=1)
            aug_sel = jnp.concatenate([slope_rows, selb, aug_tail], axis=0).astype(BF16)
            qop_ref[2 * w] = jnp.concatenate([q_all, zero, aug_sel], axis=0)
            qop_ref[2 * w + 1] = jnp.concatenate([zero, q_all, aug_win], axis=0)
        m_ref[...] = jnp.full(m_ref.shape, NEG_BIG, F32)
        acc_ref[...] = jnp.zeros(acc_ref.shape, F32)

        def sel_item(i):
            which = jnp.where(i > lo, 1, 0)
            tile = jnp.where(i > lo, hi, lo)
            chunk = jnp.where(i > lo, i - lo - 1, i)
            return which, chunk, tile - chunk

        def win_item(w, j):
            chunk = tiles[w] - wch + j
            return jnp.maximum(chunk, 0), chunk >= 0, wch - j

        def col_max(r, delta_f):
            r8 = jnp.max(r.reshape(kc // 8, 8, nl), axis=0)
            return r8 - chunk_bias * delta_f

        def scores(chunk, op):
            rows = pl.ds(pl.multiple_of(chunk * kc, kc), kc)
            keys = jnp.concatenate([k_ref[rows, :], ka_ref[rows, :]], axis=1)
            return _dot(keys, qop_ref[op])

        for i in range(n_sel_items):
            which, chunk, delta = sel_item(i)
            r = scores(chunk, 2 * which)
            if i < n_tiles // 2:
                r = r + msk_ref[jnp.where(i == lo, 1, 0)]
            elif i == n_sel_items - 1:
                r = r + msk_ref[1]
            s_ref[i] = r
            m_ref[2 * which] = jnp.maximum(m_ref[2 * which], col_max(r, delta.astype(F32)))
        for w in range(2):
            for j in range(wch + 1):
                chunk, valid, delta = win_item(w, j)
                r = scores(chunk, 2 * w + 1)
                if j == 0:
                    r = r + msk_ref[2]
                elif j == wch:
                    r = r + msk_ref[1]
                s_ref[n_sel_items + w * (wch + 1) + j] = r
                cm = jnp.where(valid, col_max(r, float(delta)), NEG_BIG)
                m_ref[2 * w + 1] = jnp.maximum(m_ref[2 * w + 1], cm)
        m_row = [jnp.max(m_ref[x], axis=0, keepdims=True) for x in range(4)]

        for i in range(n_sel_items):
            which, chunk, delta = sel_item(i)
            row = jnp.where(which == 1, m_row[2], m_row[0]) + chunk_bias * delta.astype(F32)
            pr = jnp.exp2(s_ref[i] - row).astype(BF16)
            v_aug = jnp.concatenate([vs_ref[chunk], ones], axis=0)
            acc_ref[2 * which] += _dot(v_aug, pr)
        for w in range(2):
            for j in range(wch + 1):
                chunk, valid, delta = win_item(w, j)
                row = jnp.where(valid, m_row[2 * w + 1] + chunk_bias * float(delta), -NEG_BIG)
                pr = jnp.exp2(s_ref[n_sel_items + w * (wch + 1) + j] - row).astype(BF16)
                v_aug = jnp.concatenate([vw_ref[chunk], ones], axis=0)
                acc_ref[2 * w + 1] += _dot(v_aug, pr)

        for w, tile in enumerate(tiles):
            a_s = acc_ref[2 * w]
            a_w = acc_ref[2 * w + 1]
            o_slc = a_s[:hd, :] / a_s[hd:hd + 1, :]
            o_win = a_w[:hd, :] / a_w[hd:hd + 1, :]
            gates = _sigmoid(gt_ref[tile])
            oc = ocmp_ref[tile]
            outs = []
            for h in range(HG):
                lanes = slice(h * tq, (h + 1) * tq)
                y = (gates[h:h + 1, :] * oc[h * hd:(h + 1) * hd, :]
                     + gates[HG + h:HG + h + 1, :] * o_slc[:, lanes]
                     + gates[2 * HG + h:2 * HG + h + 1, :] * o_win[:, lanes])
                outs.append(y.T)
            rows = pl.ds(pl.multiple_of(tile * tq, tq), tq)
            o_ref[rows, :] = jnp.concatenate(outs, axis=1).astype(o_ref.dtype)
        return carry

    lax.fori_loop(0, n_tiles // 2, pair, 0)


def _key_features(t):
    kp = np.arange(t)
    f = np.zeros((t, LANE), np.float32)
    f[:, 0:3] = (kp % ATT_TILE)[:, None]
    f[kp, AUG_SEL_ROW0 + kp // SLC_LEN] = 1.0
    return jnp.asarray(f, dtype=BF16)


def _sparse(qv, gt, proj3, ocmp, selb, slopes, t):
    b = qv.shape[0]
    nchunk = t // LANE
    ns = t // SLC_LEN
    rows = HG * ATT_HD
    ksw_unit0 = RNN_COLS // LANE + 1
    vs_blk0 = QT_ROWS // ATT_HD
    vw_blk0 = vs_blk0 + G_KV
    nl = HG * ATT_TILE
    n_items = (nchunk + 1) + 2 * (WINDOW // ATT_TILE + 1)
    return pl.pallas_call(
        _sparse_kernel,
        out_shape=jax.ShapeDtypeStruct((b, t, H_ATT * ATT_HD), BF16),
        grid=(b, G_KV),
        in_specs=[pl.BlockSpec((None, nchunk, rows, LANE), lambda i, g: (i, 0, g, 0)),
                  pl.BlockSpec((None, t, LANE), lambda i, g: (i, 0, ksw_unit0 + 2 * g)),
                  pl.BlockSpec((t, LANE), lambda i, g: (0, 0)),
                  pl.BlockSpec((None, nchunk, ATT_HD, LANE), lambda i, g: (i, 0, vs_blk0 + g, 0)),
                  pl.BlockSpec((None, nchunk, ATT_HD, LANE), lambda i, g: (i, 0, vw_blk0 + g, 0)),
                  pl.BlockSpec((None, nchunk, GATE_ROWS, LANE), lambda i, g: (i, 0, g, 0)),
                  pl.BlockSpec((None, nchunk, rows, LANE), lambda i, g: (i, 0, g, 0)),
                  pl.BlockSpec((None, None, nchunk, ns, LANE), lambda i, g: (i, g, 0, 0, 0)),
                  pl.BlockSpec((None, HG, 1, 1), lambda i, g: (g, 0, 0, 0))],
        out_specs=pl.BlockSpec((None, t, rows), lambda i, g: (i, 0, g)),
        scratch_shapes=[pltpu.VMEM((3, ATT_TILE, nl), F32),
                        pltpu.VMEM((4, 2 * LANE, nl), BF16),
                        pltpu.VMEM((n_items, ATT_TILE, nl), F32),
                        pltpu.VMEM((4, 8, nl), F32),
                        pltpu.VMEM((4, ATT_HD + ONES_ROWS, nl), F32)],
        compiler_params=_cparams(("arbitrary", "arbitrary")),
        name="sparse",
    )(qv, proj3, _key_features(t), qv, qv, gt, ocmp, selb, slopes)


def _merge_kernel(x_ref, yr_ref, ya_ref, mgr_ref, mga_ref, wr_ref, wa_ref, wo_ref, o_ref):
    pr = _dot(yr_ref[...], wr_ref[...])
    pa = _dot(ya_ref[...], wa_ref[...])
    merged = _sigmoid(mgr_ref[...].astype(F32)) * pr + _sigmoid(mga_ref[...].astype(F32)) * pa
    o_ref[...] = x_ref[...] + _dot(merged.astype(BF16), wo_ref[...])


def _merge(x2, yr, ya, proj, wr, wa, wo):
    m, d = x2.shape
    mg0 = (RNN_COLS + KV_COLS) // d
    tile = lambda col: pl.BlockSpec((TOK_TILE, d), lambda i: (i, col))
    wfull = pl.BlockSpec((d, d), lambda i: (0, 0))
    return pl.pallas_call(
        _merge_kernel,
        out_shape=jax.ShapeDtypeStruct((m, d), F32),
        grid=(m // TOK_TILE,),
        in_specs=[tile(0), tile(0), tile(0), tile(mg0), tile(mg0 + 1), wfull, wfull, wfull],
        out_specs=tile(0),
        compiler_params=_cparams(("arbitrary",)),
        name="merge",
    )(x2, yr, ya, proj, proj, wr, wa, wo)


def _mem_kv_kernel(mem_ref, g_ref, wkt_ref, wv_ref, kt_ref, v_ref):
    a = _rms(mem_ref[...], g_ref[...]).astype(BF16)
    kt_ref[...] = _dot_nt(wkt_ref[...], a).astype(kt_ref.dtype)
    v_ref[...] = _dot(a, wv_ref[...]).astype(v_ref.dtype)


def _mem_kv(mem, g, wkt, wv):
    b, nm, d = mem.shape
    hw = H_X * X_HD
    return pl.pallas_call(
        _mem_kv_kernel,
        out_shape=(jax.ShapeDtypeStruct((b, hw, nm), BF16),
                   jax.ShapeDtypeStruct((b, nm, hw), BF16)),
        grid=(b,),
        in_specs=[pl.BlockSpec((None, nm, d), lambda i: (i, 0, 0)),
                  pl.BlockSpec((1, d), lambda i: (0, 0)),
                  pl.BlockSpec((hw, d), lambda i: (0, 0)),
                  pl.BlockSpec((d, hw), lambda i: (0, 0))],
        out_specs=(pl.BlockSpec((None, hw, nm), lambda i: (i, 0, 0)),
                   pl.BlockSpec((None, nm, hw), lambda i: (i, 0, 0))),
        compiler_params=_cparams(("arbitrary",)),
        name="mem_kv",
    )(mem, g, wkt, wv)


def _xattn_kernel(h_ref, g_ref, wq_ref, kt_ref, v_ref, wo_ref, o_ref):
    h = h_ref[...]
    a = _rms(h, g_ref[...]).astype(BF16)
    q = (_dot(a, wq_ref[...]) * (X_HD ** -0.5)).astype(BF16)
    outs = []
    for hh in range(H_X):
        cols = slice(hh * X_HD, (hh + 1) * X_HD)
        s = _dot(q[:, cols], kt_ref[cols, :])
        m = jnp.max(s, axis=-1, keepdims=True)
        p = jnp.exp(s - m)
        p = p / jnp.sum(p, axis=-1, keepdims=True)
        outs.append(_dot(p.astype(BF16), v_ref[:, cols]))
    o = jnp.concatenate(outs, axis=1).astype(BF16)
    o_ref[...] = h + _dot(o, wo_ref[...])


def _xattn(h3, g, wq, kt, v, wo):
    b, t, d = h3.shape
    hw = H_X * X_HD
    nm = v.shape[1]
    return pl.pallas_call(
        _xattn_kernel,
        out_shape=jax.ShapeDtypeStruct((b, t, d), F32),
        grid=(b, t // TOK_TILE),
        in_specs=[pl.BlockSpec((None, TOK_TILE, d), lambda i, j: (i, j, 0)),
                  pl.BlockSpec((1, d), lambda i, j: (0, 0)),
                  pl.BlockSpec((d, hw), lambda i, j: (0, 0)),
                  pl.BlockSpec((None, hw, nm), lambda i, j: (i, 0, 0)),
                  pl.BlockSpec((None, nm, hw), lambda i, j: (i, 0, 0)),
                  pl.BlockSpec((hw, d), lambda i, j: (0, 0))],
        out_specs=pl.BlockSpec((None, TOK_TILE, d), lambda i, j: (i, j, 0)),
        compiler_params=_cparams(("arbitrary", "arbitrary")),
        name="xattn",
    )(h3, g, wq, kt, v, wo)


def _ffn_kernel(h_ref, g_ref, wg_ref, wu_ref, wd_ref, gf_ref, o_ref, *, fc):
    h = h_ref[...]
    a = _rms(h, g_ref[...]).astype(BF16)
    acc = h
    for f in range(wg_ref.shape[1] // fc):
        cols = slice(f * fc, (f + 1) * fc)
        mid = _silu(_dot(a, wg_ref[:, cols])) * _dot(a, wu_ref[:, cols])
        acc = acc + _dot(mid.astype(BF16), wd_ref[cols, :])
    o_ref[...] = _rms(acc, gf_ref[...])


def _ffn(h2, g, wg, wu, wd, gf):
    m, d = h2.shape
    ff = wg.shape[1]
    const = lambda shape: pl.BlockSpec(shape, lambda i: (0, 0))
    return pl.pallas_call(
        functools.partial(_ffn_kernel, fc=ff // 2),
        out_shape=jax.ShapeDtypeStruct((m, d), F32),
        grid=(m // TOK_TILE,),
        in_specs=[pl.BlockSpec((TOK_TILE, d), lambda i: (i, 0)),
                  const((1, d)), const((d, ff)), const((d, ff)), const((ff, d)), const((1, d))],
        out_specs=pl.BlockSpec((TOK_TILE, d), lambda i: (i, 0)),
        compiler_params=_cparams(("arbitrary",)),
        name="ffn",
    )(h2, g, wg, wu, wd, gf)


def _split_w_in(w):
    sizes = (1024, 1024, 1024, 1024, H_ATT * ATT_HD, 6 * G_KV * ATT_HD, 3 * H_ATT, D_MODEL, D_MODEL)
    parts = []
    off = 0
    for s in sizes:
        parts.append(w[:, off:off + s])
        off += s
    return parts


def _layout_w_in(w):
    d = w.shape[0]
    q_r, f_r, i_r, og_r, q_a, kv, gate, mg_r, mg_a = _split_w_in(w)
    rnn = jnp.stack([q_r, f_r, i_r, og_r], axis=1).reshape(d, 4, H_RNN, RNN_DIM)
    rnn = rnn.transpose(0, 2, 1, 3).reshape(d, RNN_COLS)
    k_c, v_c, k_s, v_s, k_w, v_w = jnp.split(kv, 6, axis=1)
    kv4 = jnp.stack([k_c, v_c, k_s, k_w], axis=1).reshape(d, 4, G_KV, ATT_HD)
    kv4 = kv4.transpose(0, 2, 1, 3).reshape(d, KV_COLS)
    w_n = jnp.concatenate([rnn, kv4, mg_r, mg_a], axis=1).astype(BF16)
    gate_g = gate.reshape(d, G_KV, HG, 3).transpose(0, 1, 3, 2).reshape(d, G_KV, 3 * HG)
    gate_g = jnp.pad(gate_g, ((0, 0), (0, 0), (0, GATE_ROWS - 3 * HG))).reshape(d, G_KV * GATE_ROWS)
    w_t = jnp.concatenate([q_a * (ATT_HD ** -0.5 * LOG2E), v_s, v_w, gate_g], axis=1).T.astype(BF16)
    return w_n, w_t


def _overlap_t(t):
    nc = (t - CMP_LEN) // CMP_STRIDE + 1
    ns = t // SLC_LEN
    starts = CMP_STRIDE * np.arange(N_CMP_PAD)
    s_start = SLC_LEN * np.arange(ns)
    ov = ((starts[None, :] + CMP_LEN > s_start[:, None]) & (starts[None, :] < s_start[:, None] + SLC_LEN)
          & (np.arange(N_CMP_PAD)[None, :] < nc))
    return jnp.asarray(ov.astype(np.float32), dtype=BF16)


def kernel(x, mem, g_mix, w_in, lower_bounds, g_rnn_out, pe_ck, w_ck1, w_ck2, pe_cv, w_cv1, w_cv2,
           w_proj_rnn, w_proj_att, w_out, g_xattn, g_mem, w_xq, w_xkv, w_xo, g_ffn, w_gate_up,
           w_down, g_final):
    b, t, d = x.shape
    depth = g_mix.shape[0]
    assert depth == 1, "the final RMSNorm is fused into the layer's FFN kernel"
    lbs = jnp.cumsum(jax.nn.softmax(lower_bounds.astype(F32), axis=0), axis=0)
    slopes = (2.0 ** (-8.0 * jnp.arange(1, H_ATT + 1, dtype=F32) / H_ATT)).reshape(G_KV, HG, 1, 1)
    overlap_t = _overlap_t(t)
    h = x
    for l in range(depth):
        w_n, w_t = _layout_w_in(w_in[l])
        x2 = h.reshape(b * t, d)
        proj = _in_proj(x2, g_mix[l][None, :], w_n)
        proj3 = proj.reshape(b, t, N_COLS)
        qv, gt = _in_proj_t(h, g_mix[l][None, :], w_t)
        y_r = _hgrn(proj3, lbs[l][None, :], g_rnn_out[l][None, :])
        kc, vct = _compress(proj3, pe_ck[l], pe_cv[l], w_ck1[l].astype(BF16), w_cv1[l].astype(BF16),
                            w_ck2[l].astype(BF16), w_cv2[l].T.astype(BF16))
        ocmp, selb = _cmp_select(qv, kc, vct, slopes, overlap_t, t)
        y_a = _sparse(qv, gt, proj3, ocmp, selb, slopes, t)
        h1 = _merge(x2, y_r.reshape(b * t, d), y_a.reshape(b * t, d), proj,
                    w_proj_rnn[l].astype(BF16), w_proj_att[l].astype(BF16), w_out[l].astype(BF16))
        w_xk, w_xv = jnp.split(w_xkv[l], 2, axis=1)
        kt, v = _mem_kv(mem, g_mem[l][None, :], w_xk.T.astype(BF16), w_xv.astype(BF16))
        h2 = _xattn(h1.reshape(b, t, d), g_xattn[l][None, :], w_xq[l].astype(BF16), kt, v,
                    w_xo[l].astype(BF16))
        w_g, w_u = jnp.split(w_gate_up[l], 2, axis=1)
        h = _ffn(h2.reshape(b * t, d), g_ffn[l][None, :], w_g.astype(BF16), w_u.astype(BF16),
                 w_down[l].astype(BF16), g_final[None, :]).reshape(b, t, d)
    return h
```

```python
import functools

import jax
import jax.numpy as jnp
import numpy as np
from jax import lax
from jax.experimental import pallas as pl
from jax.experimental.pallas import tpu as pltpu

F32 = jnp.float32
BF16 = jnp.bfloat16

D_MODEL = 1024
N_MEM = 256
H_RNN = 8
RNN_DIM = 128
RNN_CHUNK = 64
H_ATT = 16
ATT_HD = 64
G_KV = 4
HG = H_ATT // G_KV
CMP_LEN = 32
CMP_STRIDE = 16
CMP_HIDDEN = 128
SLC_LEN = 64
SLC_TOPK = 8
WINDOW = 512
FORCE_BONUS = 1.0e4
H_X = 4
X_HD = 128
D_FF = 2816
EPS = 1e-6

LANE = 128
VMEM_LIMIT = 56 * 1024 * 1024
TOK_TILE = 512
ATT_TILE = 128
CMP_TILE = 512
N_CMP_PAD = 128
HGRN_GROUP = 4
ONES_ROWS = 16
NEG_BIG = -1.0e30
LOG2E = 1.4426950408889634
AUG_SEL_ROW0 = 8

QT_ROWS = H_ATT * ATT_HD
VT_ROWS = 2 * G_KV * ATT_HD
GATE_ROWS = 16
RNN_COLS = 4 * H_RNN * RNN_DIM
KV_COLS = 4 * G_KV * ATT_HD
MG_COLS = 2 * D_MODEL
N_COLS = RNN_COLS + KV_COLS + MG_COLS


def _cparams(sem):
    return pltpu.CompilerParams(dimension_semantics=sem, vmem_limit_bytes=VMEM_LIMIT)


def _rms(xf, g):
    return xf * lax.rsqrt(jnp.mean(xf * xf, axis=-1, keepdims=True) + EPS) * g


def _sigmoid(x):
    return 1.0 / (1.0 + jnp.exp(-x))


def _silu(x):
    return x * _sigmoid(x)


def _dot(a, b):
    return jnp.dot(a, b, preferred_element_type=F32)


def _dot_nt(a, b):
    return lax.dot_general(a, b, (((1,), (1,)), ((), ())), preferred_element_type=F32)


def _dot_tn(a, b):
    return lax.dot_general(a, b, (((0,), (0,)), ((), ())), preferred_element_type=F32)


def _in_proj_kernel(x_ref, g_ref, w_ref, o_ref, *, sub):
    a = _rms(x_ref[...], g_ref[...]).astype(BF16)
    tn = o_ref.shape[1]
    for n in range(tn // sub):
        o_ref[:, n * sub:(n + 1) * sub] = _dot(a, w_ref[:, n * sub:(n + 1) * sub]).astype(o_ref.dtype)


def _in_proj(x2, g, w):
    m, d = x2.shape
    n = w.shape[1]
    tn = n // 2
    return pl.pallas_call(
        functools.partial(_in_proj_kernel, sub=512),
        out_shape=jax.ShapeDtypeStruct((m, n), BF16),
        grid=(2, m // TOK_TILE),
        in_specs=[pl.BlockSpec((TOK_TILE, d), lambda j, i: (i, 0)),
                  pl.BlockSpec((1, d), lambda j, i: (0, 0)),
                  pl.BlockSpec((d, tn), lambda j, i: (0, j))],
        out_specs=pl.BlockSpec((TOK_TILE, tn), lambda j, i: (i, j)),
        compiler_params=_cparams(("arbitrary", "arbitrary")),
        name="in_proj",
    )(x2, g, w)


def _in_proj_t_kernel(x_ref, g_ref, w_ref, qv_ref, gt_ref):
    a = _rms(x_ref[...], g_ref[...]).astype(BF16)
    r = _dot_nt(w_ref[...], a)
    nqv = qv_ref.shape[1]
    for c in range(qv_ref.shape[0]):
        qv_ref[c] = r[:nqv, c * LANE:(c + 1) * LANE].astype(qv_ref.dtype)
        gt_ref[c] = r[nqv:, c * LANE:(c + 1) * LANE]


def _in_proj_t(x, g, wt):
    b, t, d = x.shape
    rows = wt.shape[0]
    nqv = QT_ROWS + VT_ROWS
    ngt = rows - nqv
    nc = TOK_TILE // LANE
    return pl.pallas_call(
        _in_proj_t_kernel,
        out_shape=(jax.ShapeDtypeStruct((b, t // LANE, nqv, LANE), BF16),
                   jax.ShapeDtypeStruct((b, t // LANE, ngt, LANE), F32)),
        grid=(b, t // TOK_TILE),
        in_specs=[pl.BlockSpec((None, TOK_TILE, d), lambda i, j: (i, j, 0)),
                  pl.BlockSpec((1, d), lambda i, j: (0, 0)),
                  pl.BlockSpec((rows, d), lambda i, j: (0, 0))],
        out_specs=(pl.BlockSpec((None, nc, nqv, LANE), lambda i, j: (i, j, 0, 0)),
                   pl.BlockSpec((None, nc, ngt, LANE), lambda i, j: (i, j, 0, 0))),
        compiler_params=_cparams(("arbitrary", "arbitrary")),
        name="in_proj_t",
    )(x, g, wt)


def _hgrn_kernel(r_ref, lb_ref, gn_ref, o_ref, qd_ref, oi_ref, ut_ref, dec_ref):
    c = RNN_CHUNK
    kd = RNN_DIM
    n_chunks = r_ref.shape[0] // c
    lb = lb_ref[...]
    gn = gn_ref[...]
    blk = HGRN_GROUP * c
    row = lax.broadcasted_iota(jnp.int32, (blk, blk), 0)
    col = lax.broadcasted_iota(jnp.int32, (blk, blk), 1)
    same_chunk = (row // c) == (col // c)
    causal = same_chunk & (row >= col)
    tril = causal.astype(BF16)
    ones_blk = same_chunk.astype(BF16)

    q = r_ref[:, 0:kd].astype(F32)
    fl = r_ref[:, kd:2 * kd].astype(F32)
    v = r_ref[:, 2 * kd:3 * kd]
    f = lb + (1.0 - lb) * _sigmoid(fl)
    k = 1.0 - f
    logf = jnp.log(f)
    hi = logf.astype(BF16)
    rem = logf - hi.astype(F32)
    mid = rem.astype(BF16)
    lo = (rem - mid.astype(F32)).astype(BF16)
    pieces = jnp.concatenate([hi, mid, lo], axis=1)

    def fold(x):
        return x[:, 0:kd] + x[:, kd:2 * kd] + x[:, 2 * kd:3 * kd]

    n_blk = r_ref.shape[0] // blk
    bcum = jnp.concatenate([fold(_dot(tril, pieces[g * blk:(g + 1) * blk])) for g in range(n_blk)], axis=0)
    b_last = jnp.concatenate([fold(_dot(ones_blk, pieces[g * blk:(g + 1) * blk])) for g in range(n_blk)], axis=0)
    q_dec = (_silu(q) * jnp.exp(bcum)).astype(BF16)
    k_dec = (k * jnp.exp(-bcum)).astype(BF16)
    k_end = (k * jnp.exp(b_last - bcum)).astype(BF16)
    qd_ref[...] = q_dec
    dec_ref[...] = jnp.exp(jnp.concatenate([b_last[n * c:n * c + 1, :] for n in range(n_chunks)], axis=0))
    for g in range(n_blk):
        rows = slice(g * blk, (g + 1) * blk)
        a = jnp.where(causal, _dot_nt(q_dec[rows], k_dec[rows]), 0.0)
        oi_ref[rows, :] = _dot(a.astype(BF16), v[rows])
    for n in range(n_chunks):
        rows = slice(n * c, (n + 1) * c)
        ut_ref[n] = _dot_tn(v[rows], k_end[rows])

    s_t = jnp.zeros((kd, kd), F32)
    for n in range(n_chunks):
        rows = slice(n * c, (n + 1) * c)
        o = oi_ref[rows, :] + _dot_nt(qd_ref[rows, :], s_t.astype(BF16))
        og = r_ref[rows, 3 * kd:4 * kd].astype(F32)
        o_ref[rows, :] = (_rms(o, gn) * _silu(og)).astype(o_ref.dtype)
        s_t = s_t * dec_ref[n:n + 1, :] + ut_ref[n]


def _hgrn(proj3, lb, gn):
    b, t, _ = proj3.shape
    return pl.pallas_call(
        _hgrn_kernel,
        out_shape=jax.ShapeDtypeStruct((b, t, H_RNN * RNN_DIM), BF16),
        grid=(b, H_RNN),
        in_specs=[pl.BlockSpec((None, t, 4 * RNN_DIM), lambda i, h: (i, 0, h)),
                  pl.BlockSpec((1, RNN_DIM), lambda i, h: (0, h)),
                  pl.BlockSpec((1, RNN_DIM), lambda i, h: (0, 0))],
        out_specs=pl.BlockSpec((None, t, RNN_DIM), lambda i, h: (i, 0, h)),
        scratch_shapes=[pltpu.VMEM((t, RNN_DIM), BF16),
                        pltpu.VMEM((t, RNN_DIM), F32),
                        pltpu.VMEM((t // RNN_CHUNK, RNN_DIM, RNN_DIM), F32),
                        pltpu.VMEM((t // RNN_CHUNK, RNN_DIM), F32)],
        compiler_params=_cparams(("arbitrary", "arbitrary")),
        name="hgrn",
    )(proj3, lb, gn)


def _compress_kernel(kv_ref, pek_ref, pev_ref, wk1_ref, wv1_ref, wk2_ref, wv2t_ref,
                     kc_ref, vct_ref, xs_ref, xk_ref, xv_ref):
    t = kv_ref.shape[0]
    hd = ATT_HD
    xs_ref[0:t, :] = kv_ref[...].astype(F32)
    xs_ref[t:, :] = jnp.zeros((xs_ref.shape[0] - t, xs_ref.shape[1]), F32)
    for l in range(CMP_LEN):
        blk = xs_ref[pl.ds(l, N_CMP_PAD, stride=CMP_STRIDE), :]
        xk_ref[:, l * hd:(l + 1) * hd] = (blk[:, 0:hd] + pek_ref[l:l + 1, :]).astype(BF16)
        xv_ref[:, l * hd:(l + 1) * hd] = (blk[:, hd:2 * hd] + pev_ref[l:l + 1, :]).astype(BF16)
    hk = _silu(_dot(xk_ref[...], wk1_ref[...])).astype(BF16)
    hv = _silu(_dot(xv_ref[...], wv1_ref[...])).astype(BF16)
    kc_ref[...] = _dot(hk, wk2_ref[...])
    vct_ref[...] = _dot_nt(wv2t_ref[...], hv)


def _compress(proj3, pek, pev, wk1, wv1, wk2, wv2t):
    b, t, _ = proj3.shape
    kv_unit0 = RNN_COLS // LANE
    flat = CMP_LEN * ATT_HD
    full = lambda shape: pl.BlockSpec(shape, lambda i, g: (0,) * len(shape))
    return pl.pallas_call(
        _compress_kernel,
        out_shape=(jax.ShapeDtypeStruct((b, G_KV, N_CMP_PAD, ATT_HD), F32),
                   jax.ShapeDtypeStruct((b, G_KV, ATT_HD, N_CMP_PAD), F32)),
        grid=(b, G_KV),
        in_specs=[pl.BlockSpec((None, t, LANE), lambda i, g: (i, 0, kv_unit0 + 2 * g)),
                  full((CMP_LEN, ATT_HD)), full((CMP_LEN, ATT_HD)),
                  full((flat, CMP_HIDDEN)), full((flat, CMP_HIDDEN)),
                  full((CMP_HIDDEN, ATT_HD)), full((ATT_HD, CMP_HIDDEN))],
        out_specs=(pl.BlockSpec((None, None, N_CMP_PAD, ATT_HD), lambda i, g: (i, g, 0, 0)),
                   pl.BlockSpec((None, None, ATT_HD, N_CMP_PAD), lambda i, g: (i, g, 0, 0))),
        scratch_shapes=[pltpu.VMEM((t + CMP_STRIDE, LANE), F32),
                        pltpu.VMEM((N_CMP_PAD, flat), BF16),
                        pltpu.VMEM((N_CMP_PAD, flat), BF16)],
        compiler_params=_cparams(("arbitrary", "arbitrary")),
        name="compress",
    )(proj3, pek, pev, wk1, wv1, wk2, wv2t)


def _cmp_select_kernel(qt_ref, kc_ref, vct_ref, slope_ref, ov_ref, ocmp_ref, selb_ref, *, n_cmp, n_sel):
    nct = CMP_TILE // LANE
    tt = CMP_TILE
    ns = ov_ref.shape[0]
    kc = kc_ref[...].astype(BF16)
    vct = vct_ref[...].astype(BF16)
    ov = ov_ref[...]
    j_i = lax.broadcasted_iota(jnp.int32, (8, tt), 0)
    tq_rel = lax.broadcasted_iota(jnp.int32, (8, tt), 1)
    n_tiles = qt_ref.shape[0] // nct

    def ranges(tile):
        t0 = tile * tt
        n_any = min(N_CMP_PAD, (t0 + tt) // CMP_STRIDE)
        n_all = max(0, (t0 - (CMP_LEN - 1)) // CMP_STRIDE + 1) // 8 * 8
        return t0, n_any, n_all, (t0 + tt) // SLC_LEN

    def pad_rows(x):
        if x.shape[0] == N_CMP_PAD:
            return x
        return jnp.concatenate([x, jnp.zeros((N_CMP_PAD - x.shape[0], x.shape[1]), x.dtype)], axis=0)

    scores = []
    for tile in range(n_tiles):
        _, n_any, _, _ = ranges(tile)
        qt = [jnp.concatenate([qt_ref[tile * nct + c, h * ATT_HD:(h + 1) * ATT_HD, :] for c in range(nct)],
                              axis=1) for h in range(HG)]
        scores.append(_dot(kc[:n_any], jnp.concatenate(qt, axis=1)))

    probs, psums = [], []
    for tile in range(n_tiles):
        t0, n_any, n_all, _ = ranges(tile)
        n_b = lax.broadcasted_iota(jnp.int32, (n_any - n_all, tt), 0) + n_all
        t_b = lax.broadcasted_iota(jnp.int32, (n_any - n_all, tt), 1)
        visible = (CMP_STRIDE * n_b + (CMP_LEN - 1) - t0 <= t_b) & (n_b < n_cmp)
        centre = (lax.broadcasted_iota(jnp.int32, (n_any, tt), 0).astype(F32) * float(CMP_STRIDE)
                  + (CMP_LEN - 1) / 2.0)
        psum = jnp.zeros((n_any, tt), F32)
        ph = []
        for h in range(HG):
            s = scores[tile][:, h * tt:(h + 1) * tt] + slope_ref[h] * LOG2E * centre
            band = jnp.where(visible, s[n_all:], -jnp.inf)
            s = band if n_all == 0 else jnp.concatenate([s[:n_all], band], axis=0)
            m = jnp.max(s, axis=0, keepdims=True)
            m = jnp.where(m == -jnp.inf, 0.0, m)
            p = jnp.exp2(s - m)
            d = jnp.sum(p, axis=0, keepdims=True)
            p = p * (1.0 / jnp.where(d > 0, d, 1.0))
            psum = psum + p
            ph.append(pad_rows(p).astype(BF16))
        probs.append(jnp.concatenate(ph, axis=1))
        psums.append(pad_rows(psum))

    for tile in range(n_tiles):
        o = _dot(vct, probs[tile])
        for h in range(HG):
            for c in range(nct):
                lanes = slice(h * tt + c * LANE, h * tt + (c + 1) * LANE)
                ocmp_ref[tile * nct + c, h * ATT_HD:(h + 1) * ATT_HD, :] = o[:, lanes]

    imps = []
    for tile in range(n_tiles):
        psum = psums[tile]
        hi = psum.astype(BF16)
        rem = psum - hi.astype(F32)
        mid = rem.astype(BF16)
        lo = (rem - mid.astype(F32)).astype(BF16)
        imp3 = _dot(ov, jnp.concatenate([hi, mid, lo], axis=1))
        imps.append(imp3[:, 0:tt] + imp3[:, tt:2 * tt] + imp3[:, 2 * tt:3 * tt])

    for tile in range(n_tiles):
        t0, _, _, nvb = ranges(tile)
        imp = imps[tile]
        score = []
        for rg in range(nvb // 8):
            j = j_i + 8 * rg
            tq = tq_rel + t0
            cur = tq // SLC_LEN
            forced = (j == 0) | (j == cur) | (j == cur - 1)
            sc = imp[8 * rg:8 * rg + 8, :] + jnp.where(forced, FORCE_BONUS, 0.0)
            score.append(jnp.where(j * SLC_LEN <= tq, sc, -jnp.inf))
        rank = [jnp.zeros((8, tt), F32) for _ in score]
        for jp in range(nvb):
            sj = jnp.broadcast_to(score[jp // 8][jp % 8:jp % 8 + 1, :], (8, tt))
            for rg in range(nvb // 8):
                ge = jnp.where(sj >= score[rg], 1.0, 0.0)
                gt = jnp.where(sj > score[rg], 1.0, 0.0)
                if 8 * rg > jp:
                    ahead = ge
                elif 8 * rg + 7 <= jp:
                    ahead = gt
                else:
                    ahead = jnp.where(j_i + 8 * rg > jp, ge, gt)
                rank[rg] = rank[rg] + ahead
        for rg in range(ns // 8):
            for c in range(nct):
                if rg < nvb // 8:
                    selb = jnp.where(rank[rg][:, c * LANE:(c + 1) * LANE] < float(n_sel), 0.0, NEG_BIG)
                else:
                    selb = jnp.full((8, LANE), NEG_BIG, F32)
                selb_ref[tile * nct + c, 8 * rg:8 * rg + 8, :] = selb


def _cmp_select(qv, kc, vct, slopes, overlap_t, t):
    b = qv.shape[0]
    ns = t // SLC_LEN
    nch = t // LANE
    n_cmp = (t - CMP_LEN) // CMP_STRIDE + 1
    n_sel = min(SLC_TOPK, ns)
    rows = HG * ATT_HD
    return pl.pallas_call(
        functools.partial(_cmp_select_kernel, n_cmp=n_cmp, n_sel=n_sel),
        out_shape=(jax.ShapeDtypeStruct((b, nch, H_ATT * ATT_HD, LANE), F32),
                   jax.ShapeDtypeStruct((b, G_KV, nch, ns, LANE), F32)),
        grid=(b, G_KV),
        in_specs=[pl.BlockSpec((None, nch, rows, LANE), lambda i, g: (i, 0, g, 0)),
                  pl.BlockSpec((None, None, N_CMP_PAD, ATT_HD), lambda i, g: (i, g, 0, 0)),
                  pl.BlockSpec((None, None, ATT_HD, N_CMP_PAD), lambda i, g: (i, g, 0, 0)),
                  pl.BlockSpec((None, HG, 1, 1), lambda i, g: (g, 0, 0, 0)),
                  pl.BlockSpec((ns, N_CMP_PAD), lambda i, g: (0, 0))],
        out_specs=(pl.BlockSpec((None, nch, rows, LANE), lambda i, g: (i, 0, g, 0)),
                   pl.BlockSpec((None, None, nch, ns, LANE), lambda i, g: (i, g, 0, 0, 0))),
        compiler_params=_cparams(("arbitrary", "arbitrary")),
        name="cmp_select",
    )(qv, kc, vct, slopes, overlap_t)


def _sparse_kernel(qt_ref, k_ref, ka_ref, vs_ref, vw_ref, gt_ref, ocmp_ref, selb_ref, slope_ref, o_ref,
                   msk_ref, qop_ref, s_ref, m_ref, acc_ref):
    tq = ATT_TILE
    kc = ATT_TILE
    hd = ATT_HD
    nl = HG * tq
    n_tiles = qt_ref.shape[0]
    ns = selb_ref.shape[1]
    wch = WINDOW // kc
    n_sel_items = n_tiles + 1
    slope = jnp.concatenate([jnp.broadcast_to(slope_ref[h], (1, tq)) for h in range(HG)], axis=1) * LOG2E
    s_hi = slope.astype(BF16).astype(F32)
    s_mid = (slope - s_hi).astype(BF16).astype(F32)
    s_lo = (slope - s_hi - s_mid).astype(BF16).astype(F32)
    slope_rows = jnp.concatenate([s_hi, s_mid, s_lo, jnp.zeros((AUG_SEL_ROW0 - 3, nl), F32)], axis=0)
    aug_tail = jnp.zeros((LANE - AUG_SEL_ROW0 - ns, nl), F32)
    aug_win = jnp.concatenate([slope_rows, jnp.zeros((ns, nl), F32), aug_tail], axis=0).astype(BF16)
    k_i = lax.broadcasted_iota(jnp.int32, (kc, nl), 0)
    t_i = lax.broadcasted_iota(jnp.int32, (kc, nl), 1) % tq
    d0 = t_i - k_i
    msk_ref[0] = jnp.zeros((kc, nl), F32)
    msk_ref[1] = jnp.where(d0 >= 0, 0.0, NEG_BIG)
    msk_ref[2] = jnp.where(d0 < 0, 0.0, NEG_BIG)
    chunk_bias = slope * float(kc)
    ones = jnp.ones((ONES_ROWS, kc), BF16)

    def pair(p, carry):
        tiles = (p, n_tiles - 1 - p)
        lo, hi = tiles
        for w, tile in enumerate(tiles):
            qt = qt_ref[tile]
            q_all = jnp.concatenate([qt[h * hd:(h + 1) * hd, :] for h in range(HG)], axis=1)
            zero = jnp.zeros_like(q_all)
            selb = jnp.concatenate([selb_ref[tile]] * HG, axis=1)
            aug_sel = jnp.concatenate([slope_rows, selb, aug_tail], axis=0).astype(BF16)
            qop_ref[2 * w] = jnp.concatenate([q_all, zero, aug_sel], axis=0)
            qop_ref[2 * w + 1] = jnp.concatenate([zero, q_all, aug_win], axis=0)
        m_ref[...] = jnp.full(m_ref.shape, NEG_BIG, F32)
        acc_ref[...] = jnp.zeros(acc_ref.shape, F32)

        def sel_item(i):
            which = jnp.where(i > lo, 1, 0)
            tile = jnp.where(i > lo, hi, lo)
            chunk = jnp.where(i > lo, i - lo - 1, i)
            return which, chunk, tile - chunk

        def win_item(w, j):
            chunk = tiles[w] - wch + j
            return jnp.maximum(chunk, 0), chunk >= 0, wch - j

        def col_max(r, delta_f):
            r8 = jnp.max(r.reshape(kc // 8, 8, nl), axis=0)
            return r8 - chunk_bias * delta_f

        def scores(chunk, op):
            rows = pl.ds(pl.multiple_of(chunk * kc, kc), kc)
            keys = jnp.concatenate([k_ref[rows, :], ka_ref[rows, :]], axis=1)
            return _dot(keys, qop_ref[op])

        for i in range(n_sel_items):
            which, chunk, delta = sel_item(i)
            r = scores(chunk, 2 * which)
            if i < n_tiles // 2:
                r = r + msk_ref[jnp.where(i == lo, 1, 0)]
            elif i == n_sel_items - 1:
                r = r + msk_ref[1]
            s_ref[i] = r
            m_ref[2 * which] = jnp.maximum(m_ref[2 * which], col_max(r, delta.astype(F32)))
        for w in range(2):
            for j in range(wch + 1):
                chunk, valid, delta = win_item(w, j)
                r = scores(chunk, 2 * w + 1)
                if j == 0:
                    r = r + msk_ref[2]
                elif j == wch:
                    r = r + msk_ref[1]
                s_ref[n_sel_items + w * (wch + 1) + j] = r
                cm = jnp.where(valid, col_max(r, float(delta)), NEG_BIG)
                m_ref[2 * w + 1] = jnp.maximum(m_ref[2 * w + 1], cm)
        m_row = [jnp.max(m_ref[x], axis=0, keepdims=True) for x in range(4)]

        for i in range(n_sel_items):
            which, chunk, delta = sel_item(i)
            row = jnp.where(which == 1, m_row[2], m_row[0]) + chunk_bias * delta.astype(F32)
            pr = jnp.exp2(s_ref[i] - row).astype(BF16)
            v_aug = jnp.concatenate([vs_ref[chunk], ones], axis=0)
            acc_ref[2 * which] += _dot(v_aug, pr)
        for w in range(2):
            for j in range(wch + 1):
                chunk, valid, delta = win_item(w, j)
                row = jnp.where(valid, m_row[2 * w + 1] + chunk_bias * float(delta), -NEG_BIG)
                pr = jnp.exp2(s_ref[n_sel_items + w * (wch + 1) + j] - row).astype(BF16)
                v_aug = jnp.concatenate([vw_ref[chunk], ones], axis=0)
                acc_ref[2 * w + 1] += _dot(v_aug, pr)

        for w, tile in enumerate(tiles):
            a_s = acc_ref[2 * w]
            a_w = acc_ref[2 * w + 1]
            o_slc = a_s[:hd, :] / a_s[hd:hd + 1, :]
            o_win = a_w[:hd, :] / a_w[hd:hd + 1, :]
            gates = _sigmoid(gt_ref[tile])
            oc = ocmp_ref[tile]
            outs = []
            for h in range(HG):
                lanes = slice(h * tq, (h + 1) * tq)
                y = (gates[h:h + 1, :] * oc[h * hd:(h + 1) * hd, :]
                     + gates[HG + h:HG + h + 1, :] * o_slc[:, lanes]
                     + gates[2 * HG + h:2 * HG + h + 1, :] * o_win[:, lanes])
                outs.append(y.T)
            rows = pl.ds(pl.multiple_of(tile * tq, tq), tq)
            o_ref[rows, :] = jnp.concatenate(outs, axis=1).astype(o_ref.dtype)
        return carry

    lax.fori_loop(0, n_tiles // 2, pair, 0)


def _key_features(t):
    kp = np.arange(t)
    f = np.zeros((t, LANE), np.float32)
    f[:, 0:3] = (kp % ATT_TILE)[:, None]
    f[kp, AUG_SEL_ROW0 + kp // SLC_LEN] = 1.0
    return jnp.asarray(f, dtype=BF16)


def _sparse(qv, gt, proj3, ocmp, selb, slopes, t):
    b = qv.shape[0]
    nchunk = t // LANE
    ns = t // SLC_LEN
    rows = HG * ATT_HD
    ksw_unit0 = RNN_COLS // LANE + 1
    vs_blk0 = QT_ROWS // ATT_HD
    vw_blk0 = vs_blk0 + G_KV
    nl = HG * ATT_TILE
    n_items = (nchunk + 1) + 2 * (WINDOW // ATT_TILE + 1)
    return pl.pallas_call(
        _sparse_kernel,
        out_shape=jax.ShapeDtypeStruct((b, t, H_ATT * ATT_HD), BF16),
        grid=(b, G_KV),
        in_specs=[pl.BlockSpec((None, nchunk, rows, LANE), lambda i, g: (i, 0, g, 0)),
                  pl.BlockSpec((None, t, LANE), lambda i, g: (i, 0, ksw_unit0 + 2 * g)),
                  pl.BlockSpec((t, LANE), lambda i, g: (0, 0)),
                  pl.BlockSpec((None, nchunk, ATT_HD, LANE), lambda i, g: (i, 0, vs_blk0 + g, 0)),
                  pl.BlockSpec((None, nchunk, ATT_HD, LANE), lambda i, g: (i, 0, vw_blk0 + g, 0)),
                  pl.BlockSpec((None, nchunk, GATE_ROWS, LANE), lambda i, g: (i, 0, g, 0)),
                  pl.BlockSpec((None, nchunk, rows, LANE), lambda i, g: (i, 0, g, 0)),
                  pl.BlockSpec((None, None, nchunk, ns, LANE), lambda i, g: (i, g, 0, 0, 0)),
                  pl.BlockSpec((None, HG, 1, 1), lambda i, g: (g, 0, 0, 0))],
        out_specs=pl.BlockSpec((None, t, rows), lambda i, g: (i, 0, g)),
        scratch_shapes=[pltpu.VMEM((3, ATT_TILE, nl), F32),
                        pltpu.VMEM((4, 2 * LANE, nl), BF16),
                        pltpu.VMEM((n_items, ATT_TILE, nl), F32),
                        pltpu.VMEM((4, 8, nl), F32),
                        pltpu.VMEM((4, ATT_HD + ONES_ROWS, nl), F32)],
        compiler_params=_cparams(("arbitrary", "arbitrary")),
        name="sparse",
    )(qv, proj3, _key_features(t), qv, qv, gt, ocmp, selb, slopes)


def _merge_kernel(x_ref, yr_ref, ya_ref, mgr_ref, mga_ref, wr_ref, wa_ref, wo_ref, o_ref):
    pr = _dot(yr_ref[...], wr_ref[...])
    pa = _dot(ya_ref[...], wa_ref[...])
    merged = _sigmoid(mgr_ref[...].astype(F32)) * pr + _sigmoid(mga_ref[...].astype(F32)) * pa
    o_ref[...] = x_ref[...] + _dot(merged.astype(BF16), wo_ref[...])


def _merge(x2, yr, ya, proj, wr, wa, wo):
    m, d = x2.shape
    mg0 = (RNN_COLS + KV_COLS) // d
    tile = lambda col: pl.BlockSpec((TOK_TILE, d), lambda i: (i, col))
    wfull = pl.BlockSpec((d, d), lambda i: (0, 0))
    return pl.pallas_call(
        _merge_kernel,
        out_shape=jax.ShapeDtypeStruct((m, d), F32),
        grid=(m // TOK_TILE,),
        in_specs=[tile(0), tile(0), tile(0), tile(mg0), tile(mg0 + 1), wfull, wfull, wfull],
        out_specs=tile(0),
        compiler_params=_cparams(("arbitrary",)),
        name="merge",
    )(x2, yr, ya, proj, proj, wr, wa, wo)


def _mem_kv_kernel(mem_ref, g_ref, wkt_ref, wv_ref, kt_ref, v_ref):
    a = _rms(mem_ref[...], g_ref[...]).astype(BF16)
    kt_ref[...] = _dot_nt(wkt_ref[...], a).astype(kt_ref.dtype)
    v_ref[...] = _dot(a, wv_ref[...]).astype(v_ref.dtype)


def _mem_kv(mem, g, wkt, wv):
    b, nm, d = mem.shape
    hw = H_X * X_HD
    return pl.pallas_call(
        _mem_kv_kernel,
        out_shape=(jax.ShapeDtypeStruct((b, hw, nm), BF16),
                   jax.ShapeDtypeStruct((b, nm, hw), BF16)),
        grid=(b,),
        in_specs=[pl.BlockSpec((None, nm, d), lambda i: (i, 0, 0)),
                  pl.BlockSpec((1, d), lambda i: (0, 0)),
                  pl.BlockSpec((hw, d), lambda i: (0, 0)),
                  pl.BlockSpec((d, hw), lambda i: (0, 0))],
        out_specs=(pl.BlockSpec((None, hw, nm), lambda i: (i, 0, 0)),
                   pl.BlockSpec((None, nm, hw), lambda i: (i, 0, 0))),
        compiler_params=_cparams(("arbitrary",)),
        name="mem_kv",
    )(mem, g, wkt, wv)


def _xattn_kernel(h_ref, g_ref, wq_ref, kt_ref, v_ref, wo_ref, o_ref):
    h = h_ref[...]
    a = _rms(h, g_ref[...]).astype(BF16)
    q = (_dot(a, wq_ref[...]) * (X_HD ** -0.5)).astype(BF16)
    outs = []
    for hh in range(H_X):
        cols = slice(hh * X_HD, (hh + 1) * X_HD)
        s = _dot(q[:, cols], kt_ref[cols, :])
        m = jnp.max(s, axis=-1, keepdims=True)
        p = jnp.exp(s - m)
        p = p / jnp.sum(p, axis=-1, keepdims=True)
        outs.append(_dot(p.astype(BF16), v_ref[:, cols]))
    o = jnp.concatenate(outs, axis=1).astype(BF16)
    o_ref[...] = h + _dot(o, wo_ref[...])


def _xattn(h3, g, wq, kt, v, wo):
    b, t, d = h3.shape
    hw = H_X * X_HD
    nm = v.shape[1]
    return pl.pallas_call(
        _xattn_kernel,
        out_shape=jax.ShapeDtypeStruct((b, t, d), F32),
        grid=(b, t // TOK_TILE),
        in_specs=[pl.BlockSpec((None, TOK_TILE, d), lambda i, j: (i, j, 0)),
                  pl.BlockSpec((1, d), lambda i, j: (0, 0)),
                  pl.BlockSpec((d, hw), lambda i, j: (0, 0)),
                  pl.BlockSpec((None, hw, nm), lambda i, j: (i, 0, 0)),
                  pl.BlockSpec((None, nm, hw), lambda i, j: (i, 0, 0)),
                  pl.BlockSpec((hw, d), lambda i, j: (0, 0))],
        out_specs=pl.BlockSpec((None, TOK_TILE, d), lambda i, j: (i, j, 0)),
        compiler_params=_cparams(("arbitrary", "arbitrary")),
        name="xattn",
    )(h3, g, wq, kt, v, wo)


def _ffn_kernel(h_ref, g_ref, wg_ref, wu_ref, wd_ref, gf_ref, o_ref, *, fc):
    h = h_ref[...]
    a = _rms(h, g_ref[...]).astype(BF16)
    acc = h
    for f in range(wg_ref.shape[1] // fc):
        cols = slice(f * fc, (f + 1) * fc)
        mid = _silu(_dot(a, wg_ref[:, cols])) * _dot(a, wu_ref[:, cols])
        acc = acc + _dot(mid.astype(BF16), wd_ref[cols, :])
    o_ref[...] = _rms(acc, gf_ref[...])


def _ffn(h2, g, wg, wu, wd, gf):
    m, d = h2.shape
    ff = wg.shape[1]
    const = lambda shape: pl.BlockSpec(shape, lambda i: (0, 0))
    return pl.pallas_call(
        functools.partial(_ffn_kernel, fc=ff // 2),
        out_shape=jax.ShapeDtypeStruct((m, d), F32),
        grid=(m // TOK_TILE,),
        in_specs=[pl.BlockSpec((TOK_TILE, d), lambda i: (i, 0)),
                  const((1, d)), const((d, ff)), const((d, ff)), const((ff, d)), const((1, d))],
        out_specs=pl.BlockSpec((TOK_TILE, d), lambda i: (i, 0)),
        compiler_params=_cparams(("arbitrary",)),
        name="ffn",
    )(h2, g, wg, wu, wd, gf)


def _split_w_in(w):
    sizes = (1024, 1024, 1024, 1024, H_ATT * ATT_HD, 6 * G_KV * ATT_HD, 3 * H_ATT, D_MODEL, D_MODEL)
    parts = []
    off = 0
    for s in sizes:
        parts.append(w[:, off:off + s])
        off += s
    return parts


def _layout_w_in(w):
    d = w.shape[0]
    q_r, f_r, i_r, og_r, q_a, kv, gate, mg_r, mg_a = _split_w_in(w)
    rnn = jnp.stack([q_r, f_r, i_r, og_r], axis=1).reshape(d, 4, H_RNN, RNN_DIM)
    rnn = rnn.transpose(0, 2, 1, 3).reshape(d, RNN_COLS)
    k_c, v_c, k_s, v_s, k_w, v_w = jnp.split(kv, 6, axis=1)
    kv4 = jnp.stack([k_c, v_c, k_s, k_w], axis=1).reshape(d, 4, G_KV, ATT_HD)
    kv4 = kv4.transpose(0, 2, 1, 3).reshape(d, KV_COLS)
    w_n = jnp.concatenate([rnn, kv4, mg_r, mg_a], axis=1).astype(BF16)
    gate_g = gate.reshape(d, G_KV, HG, 3).transpose(0, 1, 3, 2).reshape(d, G_KV, 3 * HG)
    gate_g = jnp.pad(gate_g, ((0, 0), (0, 0), (0, GATE_ROWS - 3 * HG))).reshape(d, G_KV * GATE_ROWS)
    w_t = jnp.concatenate([q_a * (ATT_HD ** -0.5 * LOG2E), v_s, v_w, gate_g], axis=1).T.astype(BF16)
    return w_n, w_t


def _overlap_t(t):
    nc = (t - CMP_LEN) // CMP_STRIDE + 1
    ns = t // SLC_LEN
    starts = CMP_STRIDE * np.arange(N_CMP_PAD)
    s_start = SLC_LEN * np.arange(ns)
    ov = ((starts[None, :] + CMP_LEN > s_start[:, None]) & (starts[None, :] < s_start[:, None] + SLC_LEN)
          & (np.arange(N_CMP_PAD)[None, :] < nc))
    return jnp.asarray(ov.astype(np.float32), dtype=BF16)


def kernel(x, mem, g_mix, w_in, lower_bounds, g_rnn_out, pe_ck, w_ck1, w_ck2, pe_cv, w_cv1, w_cv2,
           w_proj_rnn, w_proj_att, w_out, g_xattn, g_mem, w_xq, w_xkv, w_xo, g_ffn, w_gate_up,
           w_down, g_final):
    b, t, d = x.shape
    depth = g_mix.shape[0]
    assert depth == 1, "the final RMSNorm is fused into the layer's FFN kernel"
    lbs = jnp.cumsum(jax.nn.softmax(lower_bounds.astype(F32), axis=0), axis=0)
    slopes = (2.0 ** (-8.0 * jnp.arange(1, H_ATT + 1, dtype=F32) / H_ATT)).reshape(G_KV, HG, 1, 1)
    overlap_t = _overlap_t(t)
    h = x
    for l in range(depth):
        w_n, w_t = _layout_w_in(w_in[l])
        x2 = h.reshape(b * t, d)
        proj = _in_proj(x2, g_mix[l][None, :], w_n)
        proj3 = proj.reshape(b, t, N_COLS)
        qv, gt = _in_proj_t(h, g_mix[l][None, :], w_t)
        y_r = _hgrn(proj3, lbs[l][None, :], g_rnn_out[l][None, :])
        kc, vct = _compress(proj3, pe_ck[l], pe_cv[l], w_ck1[l].astype(BF16), w_cv1[l].astype(BF16),
                            w_ck2[l].astype(BF16), w_cv2[l].T.astype(BF16))
        ocmp, selb = _cmp_select(qv, kc, vct, slopes, overlap_t, t)
        y_a = _sparse(qv, gt, proj3, ocmp, selb, slopes, t)
        h1 = _merge(x2, y_r.reshape(b * t, d), y_a.reshape(b * t, d), proj,
                    w_proj_rnn[l].astype(BF16), w_proj_att[l].astype(BF16), w_out[l].astype(BF16))
        w_xk, w_xv = jnp.split(w_xkv[l], 2, axis=1)
        kt, v = _mem_kv(mem, g_mem[l][None, :], w_xk.T.astype(BF16), w_xv.astype(BF16))
        h2 = _xattn(h1.reshape(b, t, d), g_xattn[l][None, :], w_xq[l].astype(BF16), kt, v,
                    w_xo[l].astype(BF16))
        w_g, w_u = jnp.split(w_gate_up[l], 2, axis=1)
        h = _ffn(h2.reshape(b * t, d), g_ffn[l][None, :], w_g.astype(BF16), w_u.astype(BF16),
                 w_down[l].astype(BF16), g_final[None, :]).reshape(b, t, d)
    return h
```

```python
import functools

import jax
import jax.numpy as jnp
import numpy as np
from jax import lax
from jax.experimental import pallas as pl
from jax.experimental.pallas import tpu as pltpu

F32 = jnp.float32
BF16 = jnp.bfloat16

D_MODEL = 1024
N_MEM = 256
H_RNN = 8
RNN_DIM = 128
RNN_CHUNK = 64
H_ATT = 16
ATT_HD = 64
G_KV = 4
HG = H_ATT // G_KV
CMP_LEN = 32
CMP_STRIDE = 16
CMP_HIDDEN = 128
SLC_LEN = 64
SLC_TOPK = 8
WINDOW = 512
FORCE_BONUS = 1.0e4
H_X = 4
X_HD = 128
D_FF = 2816
EPS = 1e-6

LANE = 128
VMEM_LIMIT = 56 * 1024 * 1024
TOK_TILE = 512
ATT_TILE = 128
CMP_TILE = 512
N_CMP_PAD = 128
HGRN_GROUP = 4
ONES_ROWS = 16
NEG_BIG = -1.0e30
LOG2E = 1.4426950408889634
AUG_SEL_ROW0 = 8

QT_ROWS = H_ATT * ATT_HD
VT_ROWS = 2 * G_KV * ATT_HD
GATE_ROWS = 16
RNN_COLS = 4 * H_RNN * RNN_DIM
KV_COLS = 4 * G_KV * ATT_HD
MG_COLS = 2 * D_MODEL
N_COLS = RNN_COLS + KV_COLS + MG_COLS


def _cparams(sem):
    return pltpu.CompilerParams(dimension_semantics=sem, vmem_limit_bytes=VMEM_LIMIT)


def _rms(xf, g):
    return xf * lax.rsqrt(jnp.mean(xf * xf, axis=-1, keepdims=True) + EPS) * g


def _sigmoid(x):
    return 1.0 / (1.0 + jnp.exp(-x))


def _silu(x):
    return x * _sigmoid(x)


def _dot(a, b):
    return jnp.dot(a, b, preferred_element_type=F32)


def _dot_nt(a, b):
    return lax.dot_general(a, b, (((1,), (1,)), ((), ())), preferred_element_type=F32)


def _dot_tn(a, b):
    return lax.dot_general(a, b, (((0,), (0,)), ((), ())), preferred_element_type=F32)


def _in_proj_kernel(x_ref, g_ref, w_ref, o_ref, *, sub):
    a = _rms(x_ref[...], g_ref[...]).astype(BF16)
    tn = o_ref.shape[1]
    for n in range(tn // sub):
        o_ref[:, n * sub:(n + 1) * sub] = _dot(a, w_ref[:, n * sub:(n + 1) * sub]).astype(o_ref.dtype)


def _in_proj(x2, g, w):
    m, d = x2.shape
    n = w.shape[1]
    tn = n // 2
    return pl.pallas_call(
        functools.partial(_in_proj_kernel, sub=512),
        out_shape=jax.ShapeDtypeStruct((m, n), BF16),
        grid=(2, m // TOK_TILE),
        in_specs=[pl.BlockSpec((TOK_TILE, d), lambda j, i: (i, 0)),
                  pl.BlockSpec((1, d), lambda j, i: (0, 0)),
                  pl.BlockSpec((d, tn), lambda j, i: (0, j))],
        out_specs=pl.BlockSpec((TOK_TILE, tn), lambda j, i: (i, j)),
        compiler_params=_cparams(("arbitrary", "arbitrary")),
        name="in_proj",
    )(x2, g, w)


def _in_proj_t_kernel(x_ref, g_ref, w_ref, qv_ref, gt_ref):
    a = _rms(x_ref[...], g_ref[...]).astype(BF16)
    r = _dot_nt(w_ref[...], a)
    nqv = qv_ref.shape[1]
    for c in range(qv_ref.shape[0]):
        qv_ref[c] = r[:nqv, c * LANE:(c + 1) * LANE].astype(qv_ref.dtype)
        gt_ref[c] = r[nqv:, c * LANE:(c + 1) * LANE]


def _in_proj_t(x, g, wt):
    b, t, d = x.shape
    rows = wt.shape[0]
    nqv = QT_ROWS + VT_ROWS
    ngt = rows - nqv
    nc = TOK_TILE // LANE
    return pl.pallas_call(
        _in_proj_t_kernel,
        out_shape=(jax.ShapeDtypeStruct((b, t // LANE, nqv, LANE), BF16),
                   jax.ShapeDtypeStruct((b, t // LANE, ngt, LANE), F32)),
        grid=(b, t // TOK_TILE),
        in_specs=[pl.BlockSpec((None, TOK_TILE, d), lambda i, j: (i, j, 0)),
                  pl.BlockSpec((1, d), lambda i, j: (0, 0)),
                  pl.BlockSpec((rows, d), lambda i, j: (0, 0))],
        out_specs=(pl.BlockSpec((None, nc, nqv, LANE), lambda i, j: (i, j, 0, 0)),
                   pl.BlockSpec((None, nc, ngt, LANE), lambda i, j: (i, j, 0, 0))),
        compiler_params=_cparams(("arbitrary", "arbitrary")),
        name="in_proj_t",
    )(x, g, wt)


def _hgrn_kernel(q_ref, f_ref, i_ref, og_ref, lb_ref, gn_ref, o_ref, qd_ref, oi_ref, ut_ref, dec_ref):
    c = RNN_CHUNK
    kd = RNN_DIM
    n_chunks = q_ref.shape[0] // c
    lb = lb_ref[...]
    gn = gn_ref[...]
    blk = HGRN_GROUP * c
    row = lax.broadcasted_iota(jnp.int32, (blk, blk), 0)
    col = lax.broadcasted_iota(jnp.int32, (blk, blk), 1)
    same_chunk = (row // c) == (col // c)
    causal = same_chunk & (row >= col)
    tril = causal.astype(BF16)
    ones_blk = same_chunk.astype(BF16)

    q = q_ref[...].astype(F32)
    fl = f_ref[...].astype(F32)
    v = i_ref[...]
    f = lb + (1.0 - lb) * _sigmoid(fl)
    k = 1.0 - f
    logf = jnp.log(f)
    hi = logf.astype(BF16)
    rem = logf - hi.astype(F32)
    mid = rem.astype(BF16)
    lo = (rem - mid.astype(F32)).astype(BF16)
    pieces = jnp.concatenate([hi, mid, lo], axis=1)

    def fold(x):
        return x[:, 0:kd] + x[:, kd:2 * kd] + x[:, 2 * kd:3 * kd]

    n_blk = q_ref.shape[0] // blk
    bcum = jnp.concatenate([fold(_dot(tril, pieces[g * blk:(g + 1) * blk])) for g in range(n_blk)], axis=0)
    b_last = jnp.concatenate([fold(_dot(ones_blk, pieces[g * blk:(g + 1) * blk])) for g in range(n_blk)], axis=0)
    q_dec = (_silu(q) * jnp.exp(bcum)).astype(BF16)
    k_dec = (k * jnp.exp(-bcum)).astype(BF16)
    k_end = (k * jnp.exp(b_last - bcum)).astype(BF16)
    qd_ref[...] = q_dec
    dec_ref[...] = jnp.exp(jnp.concatenate([b_last[n * c:n * c + 1, :] for n in range(n_chunks)], axis=0))
    for g in range(n_blk):
        rows = slice(g * blk, (g + 1) * blk)
        a = jnp.where(causal, _dot_nt(q_dec[rows], k_dec[rows]), 0.0)
        oi_ref[rows, :] = _dot(a.astype(BF16), v[rows])
    for n in range(n_chunks):
        rows = slice(n * c, (n + 1) * c)
        ut_ref[n] = _dot_tn(v[rows], k_end[rows])

    s_t = jnp.zeros((kd, kd), F32)
    for n in range(n_chunks):
        rows = slice(n * c, (n + 1) * c)
        o = oi_ref[rows, :] + _dot_nt(qd_ref[rows, :], s_t.astype(BF16))
        og = og_ref[rows, :].astype(F32)
        o_ref[rows, :] = (_rms(o, gn) * _silu(og)).astype(o_ref.dtype)
        s_t = s_t * dec_ref[n:n + 1, :] + ut_ref[n]


def _hgrn(proj3, lb, gn):
    b, t, _ = proj3.shape
    return pl.pallas_call(
        _hgrn_kernel,
        out_shape=jax.ShapeDtypeStruct((b, t, H_RNN * RNN_DIM), BF16),
        grid=(b, H_RNN),
        in_specs=[pl.BlockSpec((None, t, RNN_DIM), lambda i, h: (i, 0, h)),
                  pl.BlockSpec((None, t, RNN_DIM), lambda i, h: (i, 0, H_RNN + h)),
                  pl.BlockSpec((None, t, RNN_DIM), lambda i, h: (i, 0, 2 * H_RNN + h)),
                  pl.BlockSpec((None, t, RNN_DIM), lambda i, h: (i, 0, 3 * H_RNN + h)),
                  pl.BlockSpec((1, RNN_DIM), lambda i, h: (0, h)),
                  pl.BlockSpec((1, RNN_DIM), lambda i, h: (0, 0))],
        out_specs=pl.BlockSpec((None, t, RNN_DIM), lambda i, h: (i, 0, h)),
        scratch_shapes=[pltpu.VMEM((t, RNN_DIM), BF16),
                        pltpu.VMEM((t, RNN_DIM), F32),
                        pltpu.VMEM((t // RNN_CHUNK, RNN_DIM, RNN_DIM), F32),
                        pltpu.VMEM((t // RNN_CHUNK, RNN_DIM), F32)],
        compiler_params=_cparams(("arbitrary", "arbitrary")),
        name="hgrn",
    )(proj3, proj3, proj3, proj3, lb, gn)


def _compress_kernel(kv_ref, pek_ref, pev_ref, wk1_ref, wv1_ref, wk2_ref, wv2t_ref,
                     kc_ref, vct_ref, xs_ref, xk_ref, xv_ref):
    t = kv_ref.shape[0]
    hd = ATT_HD
    xs_ref[0:t, :] = kv_ref[...].astype(F32)
    xs_ref[t:, :] = jnp.zeros((xs_ref.shape[0] - t, xs_ref.shape[1]), F32)
    for l in range(CMP_LEN):
        blk = xs_ref[pl.ds(l, N_CMP_PAD, stride=CMP_STRIDE), :]
        xk_ref[:, l * hd:(l + 1) * hd] = (blk[:, 0:hd] + pek_ref[l:l + 1, :]).astype(BF16)
        xv_ref[:, l * hd:(l + 1) * hd] = (blk[:, hd:2 * hd] + pev_ref[l:l + 1, :]).astype(BF16)
    hk = _silu(_dot(xk_ref[...], wk1_ref[...])).astype(BF16)
    hv = _silu(_dot(xv_ref[...], wv1_ref[...])).astype(BF16)
    kc_ref[...] = _dot(hk, wk2_ref[...])
    vct_ref[...] = _dot_nt(wv2t_ref[...], hv)


def _compress(proj3, pek, pev, wk1, wv1, wk2, wv2t):
    b, t, _ = proj3.shape
    kv_unit0 = RNN_COLS // LANE
    flat = CMP_LEN * ATT_HD
    full = lambda shape: pl.BlockSpec(shape, lambda i, g: (0,) * len(shape))
    return pl.pallas_call(
        _compress_kernel,
        out_shape=(jax.ShapeDtypeStruct((b, G_KV, N_CMP_PAD, ATT_HD), F32),
                   jax.ShapeDtypeStruct((b, G_KV, ATT_HD, N_CMP_PAD), F32)),
        grid=(b, G_KV),
        in_specs=[pl.BlockSpec((None, t, LANE), lambda i, g: (i, 0, kv_unit0 + 2 * g)),
                  full((CMP_LEN, ATT_HD)), full((CMP_LEN, ATT_HD)),
                  full((flat, CMP_HIDDEN)), full((flat, CMP_HIDDEN)),
                  full((CMP_HIDDEN, ATT_HD)), full((ATT_HD, CMP_HIDDEN))],
        out_specs=(pl.BlockSpec((None, None, N_CMP_PAD, ATT_HD), lambda i, g: (i, g, 0, 0)),
                   pl.BlockSpec((None, None, ATT_HD, N_CMP_PAD), lambda i, g: (i, g, 0, 0))),
        scratch_shapes=[pltpu.VMEM((t + CMP_STRIDE, LANE), F32),
                        pltpu.VMEM((N_CMP_PAD, flat), BF16),
                        pltpu.VMEM((N_CMP_PAD, flat), BF16)],
        compiler_params=_cparams(("arbitrary", "arbitrary")),
        name="compress",
    )(proj3, pek, pev, wk1, wv1, wk2, wv2t)


def _cmp_select_kernel(qt_ref, kc_ref, vct_ref, slope_ref, ov_ref, ocmp_ref, selb_ref, *, n_cmp, n_sel):
    nct = CMP_TILE // LANE
    tt = CMP_TILE
    ns = ov_ref.shape[0]
    kc = kc_ref[...].astype(BF16)
    vct = vct_ref[...].astype(BF16)
    ov = ov_ref[...]
    j_i = lax.broadcasted_iota(jnp.int32, (8, tt), 0)
    tq_rel = lax.broadcasted_iota(jnp.int32, (8, tt), 1)
    n_tiles = qt_ref.shape[0] // nct

    def ranges(tile):
        t0 = tile * tt
        n_any = min(N_CMP_PAD, (t0 + tt) // CMP_STRIDE)
        n_all = max(0, (t0 - (CMP_LEN - 1)) // CMP_STRIDE + 1) // 8 * 8
        return t0, n_any, n_all, (t0 + tt) // SLC_LEN

    def pad_rows(x):
        if x.shape[0] == N_CMP_PAD:
            return x
        return jnp.concatenate([x, jnp.zeros((N_CMP_PAD - x.shape[0], x.shape[1]), x.dtype)], axis=0)

    scores = []
    for tile in range(n_tiles):
        _, n_any, _, _ = ranges(tile)
        qt = [jnp.concatenate([qt_ref[tile * nct + c, h * ATT_HD:(h + 1) * ATT_HD, :] for c in range(nct)],
                              axis=1) for h in range(HG)]
        scores.append(_dot(kc[:n_any], jnp.concatenate(qt, axis=1)))

    probs, psums = [], []
    for tile in range(n_tiles):
        t0, n_any, n_all, _ = ranges(tile)
        n_b = lax.broadcasted_iota(jnp.int32, (n_any - n_all, tt), 0) + n_all
        t_b = lax.broadcasted_iota(jnp.int32, (n_any - n_all, tt), 1)
        visible = (CMP_STRIDE * n_b + (CMP_LEN - 1) - t0 <= t_b) & (n_b < n_cmp)
        centre = (lax.broadcasted_iota(jnp.int32, (n_any, tt), 0).astype(F32) * float(CMP_STRIDE)
                  + (CMP_LEN - 1) / 2.0)
        psum = jnp.zeros((n_any, tt), F32)
        ph = []
        for h in range(HG):
            s = scores[tile][:, h * tt:(h + 1) * tt] + slope_ref[h] * LOG2E * centre
            band = jnp.where(visible, s[n_all:], -jnp.inf)
            s = band if n_all == 0 else jnp.concatenate([s[:n_all], band], axis=0)
            m = jnp.max(s, axis=0, keepdims=True)
            m = jnp.where(m == -jnp.inf, 0.0, m)
            p = jnp.exp2(s - m)
            d = jnp.sum(p, axis=0, keepdims=True)
            p = p * (1.0 / jnp.where(d > 0, d, 1.0))
            psum = psum + p
            ph.append(pad_rows(p).astype(BF16))
        probs.append(jnp.concatenate(ph, axis=1))
        psums.append(pad_rows(psum))

    for tile in range(n_tiles):
        o = _dot(vct, probs[tile])
        for h in range(HG):
            for c in range(nct):
                lanes = slice(h * tt + c * LANE, h * tt + (c + 1) * LANE)
                ocmp_ref[tile * nct + c, h * ATT_HD:(h + 1) * ATT_HD, :] = o[:, lanes]

    imps = []
    for tile in range(n_tiles):
        psum = psums[tile]
        hi = psum.astype(BF16)
        rem = psum - hi.astype(F32)
        mid = rem.astype(BF16)
        lo = (rem - mid.astype(F32)).astype(BF16)
        imp3 = _dot(ov, jnp.concatenate([hi, mid, lo], axis=1))
        imps.append(imp3[:, 0:tt] + imp3[:, tt:2 * tt] + imp3[:, 2 * tt:3 * tt])

    for tile in range(n_tiles):
        t0, _, _, nvb = ranges(tile)
        imp = imps[tile]
        score = []
        for rg in range(nvb // 8):
            j = j_i + 8 * rg
            tq = tq_rel + t0
            cur = tq // SLC_LEN
            forced = (j == 0) | (j == cur) | (j == cur - 1)
            sc = imp[8 * rg:8 * rg + 8, :] + jnp.where(forced, FORCE_BONUS, 0.0)
            score.append(jnp.where(j * SLC_LEN <= tq, sc, -jnp.inf))
        rank = [jnp.zeros((8, tt), F32) for _ in score]
        for jp in range(nvb):
            sj = jnp.broadcast_to(score[jp // 8][jp % 8:jp % 8 + 1, :], (8, tt))
            for rg in range(nvb // 8):
                ge = jnp.where(sj >= score[rg], 1.0, 0.0)
                gt = jnp.where(sj > score[rg], 1.0, 0.0)
                if 8 * rg > jp:
                    ahead = ge
                elif 8 * rg + 7 <= jp:
                    ahead = gt
                else:
                    ahead = jnp.where(j_i + 8 * rg > jp, ge, gt)
                rank[rg] = rank[rg] + ahead
        for rg in range(ns // 8):
            for c in range(nct):
                if rg < nvb // 8:
                    selb = jnp.where(rank[rg][:, c * LANE:(c + 1) * LANE] < float(n_sel), 0.0, NEG_BIG)
                else:
                    selb = jnp.full((8, LANE), NEG_BIG, F32)
                selb_ref[tile * nct + c, 8 * rg:8 * rg + 8, :] = selb


def _cmp_select(qv, kc, vct, slopes, overlap_t, t):
    b = qv.shape[0]
    ns = t // SLC_LEN
    nch = t // LANE
    n_cmp = (t - CMP_LEN) // CMP_STRIDE + 1
    n_sel = min(SLC_TOPK, ns)
    rows = HG * ATT_HD
    return pl.pallas_call(
        functools.partial(_cmp_select_kernel, n_cmp=n_cmp, n_sel=n_sel),
        out_shape=(jax.ShapeDtypeStruct((b, nch, H_ATT * ATT_HD, LANE), F32),
                   jax.ShapeDtypeStruct((b, G_KV, nch, ns, LANE), F32)),
        grid=(b, G_KV),
        in_specs=[pl.BlockSpec((None, nch, rows, LANE), lambda i, g: (i, 0, g, 0)),
                  pl.BlockSpec((None, None, N_CMP_PAD, ATT_HD), lambda i, g: (i, g, 0, 0)),
                  pl.BlockSpec((None, None, ATT_HD, N_CMP_PAD), lambda i, g: (i, g, 0, 0)),
                  pl.BlockSpec((None, HG, 1, 1), lambda i, g: (g, 0, 0, 0)),
                  pl.BlockSpec((ns, N_CMP_PAD), lambda i, g: (0, 0))],
        out_specs=(pl.BlockSpec((None, nch, rows, LANE), lambda i, g: (i, 0, g, 0)),
                   pl.BlockSpec((None, None, nch, ns, LANE), lambda i, g: (i, g, 0, 0, 0))),
        compiler_params=_cparams(("arbitrary", "arbitrary")),
        name="cmp_select",
    )(qv, kc, vct, slopes, overlap_t)


def _sparse_kernel(qt_ref, k_ref, ka_ref, vs_ref, vw_ref, gt_ref, ocmp_ref, selb_ref, slope_ref, o_ref,
                   msk_ref, qop_ref, s_ref, m_ref, acc_ref):
    tq = ATT_TILE
    kc = ATT_TILE
    hd = ATT_HD
    nl = HG * tq
    n_tiles = qt_ref.shape[0]
    ns = selb_ref.shape[1]
    wch = WINDOW // kc
    n_sel_items = n_tiles + 1
    slope = jnp.concatenate([jnp.broadcast_to(slope_ref[h], (1, tq)) for h in range(HG)], axis=1) * LOG2E
    s_hi = slope.astype(BF16).astype(F32)
    s_mid = (slope - s_hi).astype(BF16).astype(F32)
    s_lo = (slope - s_hi - s_mid).astype(BF16).astype(F32)
    slope_rows = jnp.concatenate([s_hi, s_mid, s_lo, jnp.zeros((AUG_SEL_ROW0 - 3, nl), F32)], axis=0)
    aug_tail = jnp.zeros((LANE - AUG_SEL_ROW0 - ns, nl), F32)
    aug_win = jnp.concatenate([slope_rows, jnp.zeros((ns, nl), F32), aug_tail], axis=0).astype(BF16)
    k_i = lax.broadcasted_iota(jnp.int32, (kc, nl), 0)
    t_i = lax.broadcasted_iota(jnp.int32, (kc, nl), 1) % tq
    d0 = t_i - k_i
    msk_ref[0] = jnp.zeros((kc, nl), F32)
    msk_ref[1] = jnp.where(d0 >= 0, 0.0, NEG_BIG)
    msk_ref[2] = jnp.where(d0 < 0, 0.0, NEG_BIG)
    chunk_bias = slope * float(kc)
    ones = jnp.ones((ONES_ROWS, kc), BF16)

    n_pairs = n_tiles // 2
    n_items = n_sel_items + 2 * (wch + 1)

    def tiles_of(p):
        return (p, n_tiles - 1 - p)

    def sel_item(p, i):
        lo, hi = tiles_of(p)
        which = jnp.where(i > lo, 1, 0)
        tile = jnp.where(i > lo, hi, lo)
        chunk = jnp.where(i > lo, i - lo - 1, i)
        return which, chunk, tile - chunk

    def win_item(p, w, j):
        chunk = tiles_of(p)[w] - wch + j
        return jnp.maximum(chunk, 0), chunk >= 0, wch - j

    def col_max(r, delta_f):
        r8 = jnp.max(r.reshape(kc // 8, 8, nl), axis=0)
        return r8 - chunk_bias * delta_f

    def scores(chunk, op):
        rows = pl.ds(pl.multiple_of(chunk * kc, kc), kc)
        keys = jnp.concatenate([k_ref[rows, :], ka_ref[rows, :]], axis=1)
        return _dot(keys, qop_ref[op])

    def build(p, slot):
        for w, tile in enumerate(tiles_of(p)):
            qt = qt_ref[tile]
            q_all = jnp.concatenate([qt[h * hd:(h + 1) * hd, :] for h in range(HG)], axis=1)
            zero = jnp.zeros_like(q_all)
            selb = jnp.concatenate([selb_ref[tile]] * HG, axis=1)
            aug_sel = jnp.concatenate([slope_rows, selb, aug_tail], axis=0).astype(BF16)
            qop_ref[4 * slot + 2 * w] = jnp.concatenate([q_all, zero, aug_sel], axis=0)
            qop_ref[4 * slot + 2 * w + 1] = jnp.concatenate([zero, q_all, aug_win], axis=0)
        for x in range(4):
            m_ref[4 * slot + x] = jnp.full((8, nl), NEG_BIG, F32)

    def phase_a(p, slot):
        lo, _ = tiles_of(p)
        items = []

        def sel(i):
            which, chunk, delta = sel_item(p, i)
            r = scores(chunk, 4 * slot + 2 * which)
            if i < n_pairs:
                r = r + msk_ref[jnp.where(i == lo, 1, 0)]
            elif i == n_sel_items - 1:
                r = r + msk_ref[1]
            s_ref[slot * n_items + i] = r
            x = 4 * slot + 2 * which
            m_ref[x] = jnp.maximum(m_ref[x], col_max(r, delta.astype(F32)))

        def win(w, j):
            chunk, valid, delta = win_item(p, w, j)
            r = scores(chunk, 4 * slot + 2 * w + 1)
            if j == 0:
                r = r + msk_ref[2]
            elif j == wch:
                r = r + msk_ref[1]
            s_ref[slot * n_items + n_sel_items + w * (wch + 1) + j] = r
            cm = jnp.where(valid, col_max(r, float(delta)), NEG_BIG)
            x = 4 * slot + 2 * w + 1
            m_ref[x] = jnp.maximum(m_ref[x], cm)

        for i in range(n_sel_items):
            items.append(functools.partial(sel, i))
        for w in range(2):
            for j in range(wch + 1):
                items.append(functools.partial(win, w, j))
        return items

    def phase_b(p, slot):
        m_row = [jnp.max(m_ref[4 * slot + x], axis=0, keepdims=True) for x in range(4)]
        for x in range(4):
            acc_ref[4 * slot + x] = jnp.zeros(acc_ref.shape[1:], F32)
        items = []

        def sel(i):
            which, chunk, delta = sel_item(p, i)
            row = jnp.where(which == 1, m_row[2], m_row[0]) + chunk_bias * delta.astype(F32)
            pr = jnp.exp2(s_ref[slot * n_items + i] - row).astype(BF16)
            v_aug = jnp.concatenate([vs_ref[chunk], ones], axis=0)
            acc_ref[4 * slot + 2 * which] += _dot(v_aug, pr)

        def win(w, j):
            chunk, valid, delta = win_item(p, w, j)
            row = jnp.where(valid, m_row[2 * w + 1] + chunk_bias * float(delta), -NEG_BIG)
            pr = jnp.exp2(s_ref[slot * n_items + n_sel_items + w * (wch + 1) + j] - row).astype(BF16)
            v_aug = jnp.concatenate([vw_ref[chunk], ones], axis=0)
            acc_ref[4 * slot + 2 * w + 1] += _dot(v_aug, pr)

        for i in range(n_sel_items):
            items.append(functools.partial(sel, i))
        for w in range(2):
            for j in range(wch + 1):
                items.append(functools.partial(win, w, j))
        return items

    def finalize(p, slot):
        for w, tile in enumerate(tiles_of(p)):
            a_s = acc_ref[4 * slot + 2 * w]
            a_w = acc_ref[4 * slot + 2 * w + 1]
            o_slc = a_s[:hd, :] / a_s[hd:hd + 1, :]
            o_win = a_w[:hd, :] / a_w[hd:hd + 1, :]
            gates = _sigmoid(gt_ref[tile])
            oc = ocmp_ref[tile]
            outs = []
            for h in range(HG):
                lanes = slice(h * tq, (h + 1) * tq)
                y = (gates[h:h + 1, :] * oc[h * hd:(h + 1) * hd, :]
                     + gates[HG + h:HG + h + 1, :] * o_slc[:, lanes]
                     + gates[2 * HG + h:2 * HG + h + 1, :] * o_win[:, lanes])
                outs.append(y.T)
            rows = pl.ds(pl.multiple_of(tile * tq, tq), tq)
            o_ref[rows, :] = jnp.concatenate(outs, axis=1).astype(o_ref.dtype)

    def step(k, slot, first=False, last=False):
        if not last:
            build(k + 1, 1 - slot)
        if not first:
            finalize(k - 1, 1 - slot)
        b_items = phase_b(k, slot)
        a_items = phase_a(k + 1, 1 - slot) if not last else [None] * n_items
        for a, b in zip(a_items, b_items):
            if a is not None:
                a()
            b()

    build(0, 0)
    for a in phase_a(0, 0):
        a()
    step(0, 0, first=True)

    def body(kk, carry):
        step(2 * kk + 1, 1)
        step(2 * kk + 2, 0)
        return carry

    lax.fori_loop(0, (n_pairs - 2) // 2, body, 0)
    step(n_pairs - 1, (n_pairs - 1) % 2, last=True)
    finalize(n_pairs - 1, (n_pairs - 1) % 2)


def _key_features(t):
    kp = np.arange(t)
    f = np.zeros((t, LANE), np.float32)
    f[:, 0:3] = (kp % ATT_TILE)[:, None]
    f[kp, AUG_SEL_ROW0 + kp // SLC_LEN] = 1.0
    return jnp.asarray(f, dtype=BF16)


def _sparse(qv, gt, proj3, ocmp, selb, slopes, t):
    b = qv.shape[0]
    nchunk = t // LANE
    ns = t // SLC_LEN
    rows = HG * ATT_HD
    ksw_unit0 = RNN_COLS // LANE + 1
    vs_blk0 = QT_ROWS // ATT_HD
    vw_blk0 = vs_blk0 + G_KV
    nl = HG * ATT_TILE
    n_items = (nchunk + 1) + 2 * (WINDOW // ATT_TILE + 1)
    return pl.pallas_call(
        _sparse_kernel,
        out_shape=jax.ShapeDtypeStruct((b, t, H_ATT * ATT_HD), BF16),
        grid=(b, G_KV),
        in_specs=[pl.BlockSpec((None, nchunk, rows, LANE), lambda i, g: (i, 0, g, 0)),
                  pl.BlockSpec((None, t, LANE), lambda i, g: (i, 0, ksw_unit0 + 2 * g)),
                  pl.BlockSpec((t, LANE), lambda i, g: (0, 0)),
                  pl.BlockSpec((None, nchunk, ATT_HD, LANE), lambda i, g: (i, 0, vs_blk0 + g, 0)),
                  pl.BlockSpec((None, nchunk, ATT_HD, LANE), lambda i, g: (i, 0, vw_blk0 + g, 0)),
                  pl.BlockSpec((None, nchunk, GATE_ROWS, LANE), lambda i, g: (i, 0, g, 0)),
                  pl.BlockSpec((None, nchunk, rows, LANE), lambda i, g: (i, 0, g, 0)),
                  pl.BlockSpec((None, None, nchunk, ns, LANE), lambda i, g: (i, g, 0, 0, 0)),
                  pl.BlockSpec((None, HG, 1, 1), lambda i, g: (g, 0, 0, 0))],
        out_specs=pl.BlockSpec((None, t, rows), lambda i, g: (i, 0, g)),
        scratch_shapes=[pltpu.VMEM((3, ATT_TILE, nl), F32),
                        pltpu.VMEM((2 * 4, 2 * LANE, nl), BF16),
                        pltpu.VMEM((2 * n_items, ATT_TILE, nl), F32),
                        pltpu.VMEM((2 * 4, 8, nl), F32),
                        pltpu.VMEM((2 * 4, ATT_HD + ONES_ROWS, nl), F32)],
        compiler_params=_cparams(("arbitrary", "arbitrary")),
        name="sparse",
    )(qv, proj3, _key_features(t), qv, qv, gt, ocmp, selb, slopes)


def _merge_kernel(x_ref, yr_ref, ya_ref, mgr_ref, mga_ref, wr_ref, wa_ref, wo_ref, o_ref):
    pr = _dot(yr_ref[...], wr_ref[...])
    pa = _dot(ya_ref[...], wa_ref[...])
    merged = _sigmoid(mgr_ref[...].astype(F32)) * pr + _sigmoid(mga_ref[...].astype(F32)) * pa
    o_ref[...] = x_ref[...] + _dot(merged.astype(BF16), wo_ref[...])


def _merge(x2, yr, ya, proj, wr, wa, wo):
    m, d = x2.shape
    mg0 = (RNN_COLS + KV_COLS) // d
    tile = lambda col: pl.BlockSpec((TOK_TILE, d), lambda i: (i, col))
    wfull = pl.BlockSpec((d, d), lambda i: (0, 0))
    return pl.pallas_call(
        _merge_kernel,
        out_shape=jax.ShapeDtypeStruct((m, d), F32),
        grid=(m // TOK_TILE,),
        in_specs=[tile(0), tile(0), tile(0), tile(mg0), tile(mg0 + 1), wfull, wfull, wfull],
        out_specs=tile(0),
        compiler_params=_cparams(("arbitrary",)),
        name="merge",
    )(x2, yr, ya, proj, proj, wr, wa, wo)


def _mem_kv_kernel(mem_ref, g_ref, wkt_ref, wv_ref, kt_ref, v_ref):
    a = _rms(mem_ref[...], g_ref[...]).astype(BF16)
    kt_ref[...] = _dot_nt(wkt_ref[...], a).astype(kt_ref.dtype)
    v_ref[...] = _dot(a, wv_ref[...]).astype(v_ref.dtype)


def _mem_kv(mem, g, wkt, wv):
    b, nm, d = mem.shape
    hw = H_X * X_HD
    return pl.pallas_call(
        _mem_kv_kernel,
        out_shape=(jax.ShapeDtypeStruct((b, hw, nm), BF16),
                   jax.ShapeDtypeStruct((b, nm, hw), BF16)),
        grid=(b,),
        in_specs=[pl.BlockSpec((None, nm, d), lambda i: (i, 0, 0)),
                  pl.BlockSpec((1, d), lambda i: (0, 0)),
                  pl.BlockSpec((hw, d), lambda i: (0, 0)),
                  pl.BlockSpec((d, hw), lambda i: (0, 0))],
        out_specs=(pl.BlockSpec((None, hw, nm), lambda i: (i, 0, 0)),
                   pl.BlockSpec((None, nm, hw), lambda i: (i, 0, 0))),
        compiler_params=_cparams(("arbitrary",)),
        name="mem_kv",
    )(mem, g, wkt, wv)


def _xattn_kernel(h_ref, g_ref, wq_ref, kt_ref, v_ref, wo_ref, o_ref):
    h = h_ref[...]
    a = _rms(h, g_ref[...]).astype(BF16)
    q = (_dot(a, wq_ref[...]) * (X_HD ** -0.5)).astype(BF16)
    outs = []
    for hh in range(H_X):
        cols = slice(hh * X_HD, (hh + 1) * X_HD)
        s = _dot(q[:, cols], kt_ref[cols, :])
        m = jnp.max(s, axis=-1, keepdims=True)
        p = jnp.exp(s - m)
        p = p / jnp.sum(p, axis=-1, keepdims=True)
        outs.append(_dot(p.astype(BF16), v_ref[:, cols]))
    o = jnp.concatenate(outs, axis=1).astype(BF16)
    o_ref[...] = h + _dot(o, wo_ref[...])


def _xattn(h3, g, wq, kt, v, wo):
    b, t, d = h3.shape
    hw = H_X * X_HD
    nm = v.shape[1]
    return pl.pallas_call(
        _xattn_kernel,
        out_shape=jax.ShapeDtypeStruct((b, t, d), F32),
        grid=(b, t // TOK_TILE),
        in_specs=[pl.BlockSpec((None, TOK_TILE, d), lambda i, j: (i, j, 0)),
                  pl.BlockSpec((1, d), lambda i, j: (0, 0)),
                  pl.BlockSpec((d, hw), lambda i, j: (0, 0)),
                  pl.BlockSpec((None, hw, nm), lambda i, j: (i, 0, 0)),
                  pl.BlockSpec((None, nm, hw), lambda i, j: (i, 0, 0)),
                  pl.BlockSpec((hw, d), lambda i, j: (0, 0))],
        out_specs=pl.BlockSpec((None, TOK_TILE, d), lambda i, j: (i, j, 0)),
        compiler_params=_cparams(("arbitrary", "arbitrary")),
        name="xattn",
    )(h3, g, wq, kt, v, wo)


def _ffn_kernel(h_ref, g_ref, wg_ref, wu_ref, wd_ref, gf_ref, o_ref, *, fc):
    h = h_ref[...]
    a = _rms(h, g_ref[...]).astype(BF16)
    acc = h
    for f in range(wg_ref.shape[1] // fc):
        cols = slice(f * fc, (f + 1) * fc)
        mid = _silu(_dot(a, wg_ref[:, cols])) * _dot(a, wu_ref[:, cols])
        acc = acc + _dot(mid.astype(BF16), wd_ref[cols, :])
    o_ref[...] = _rms(acc, gf_ref[...])


def _ffn(h2, g, wgu, wd, gf):
    m, d = h2.shape
    ff = wd.shape[0]
    const = lambda shape: pl.BlockSpec(shape, lambda i: (0, 0))
    return pl.pallas_call(
        functools.partial(_ffn_kernel, fc=ff // 2),
        out_shape=jax.ShapeDtypeStruct((m, d), F32),
        grid=(m // TOK_TILE,),
        in_specs=[pl.BlockSpec((TOK_TILE, d), lambda i: (i, 0)),
                  const((1, d)), const((d, ff)), pl.BlockSpec((d, ff), lambda i: (0, 1)),
                  const((ff, d)), const((1, d))],
        out_specs=pl.BlockSpec((TOK_TILE, d), lambda i: (i, 0)),
        compiler_params=_cparams(("arbitrary",)),
        name="ffn",
    )(h2, g, wgu, wgu, wd, gf)


def _layout_w_in(w):
    d = w.shape[0]
    kvw = G_KV * ATT_HD
    q_a = w[:, RNN_COLS:RNN_COLS + QT_ROWS]
    kv = w[:, RNN_COLS + QT_ROWS:RNN_COLS + QT_ROWS + 6 * kvw]
    gate = w[:, RNN_COLS + QT_ROWS + 6 * kvw:RNN_COLS + QT_ROWS + 6 * kvw + 3 * H_ATT]
    kv4 = kv.reshape(d, 6, G_KV, ATT_HD)[:, np.array([0, 1, 2, 4])].transpose(0, 2, 1, 3).reshape(d, KV_COLS)
    w_n = jnp.concatenate([w[:, :RNN_COLS], kv4, w[:, -MG_COLS:]], axis=1).astype(BF16)
    gate_g = gate.reshape(d, G_KV, HG, 3).transpose(0, 1, 3, 2).reshape(d, G_KV, 3 * HG)
    gate_g = jnp.pad(gate_g, ((0, 0), (0, 0), (0, GATE_ROWS - 3 * HG))).reshape(d, G_KV * GATE_ROWS)
    w_t = jnp.concatenate([q_a * (ATT_HD ** -0.5 * LOG2E), kv[:, 3 * kvw:4 * kvw], kv[:, 5 * kvw:6 * kvw], gate_g],
                          axis=1).T.astype(BF16)
    return w_n, w_t


def _overlap_t(t):
    nc = (t - CMP_LEN) // CMP_STRIDE + 1
    ns = t // SLC_LEN
    starts = CMP_STRIDE * np.arange(N_CMP_PAD)
    s_start = SLC_LEN * np.arange(ns)
    ov = ((starts[None, :] + CMP_LEN > s_start[:, None]) & (starts[None, :] < s_start[:, None] + SLC_LEN)
          & (np.arange(N_CMP_PAD)[None, :] < nc))
    return jnp.asarray(ov.astype(np.float32), dtype=BF16)


def kernel(x, mem, g_mix, w_in, lower_bounds, g_rnn_out, pe_ck, w_ck1, w_ck2, pe_cv, w_cv1, w_cv2,
           w_proj_rnn, w_proj_att, w_out, g_xattn, g_mem, w_xq, w_xkv, w_xo, g_ffn, w_gate_up,
           w_down, g_final):
    b, t, d = x.shape
    depth = g_mix.shape[0]
    assert depth == 1, "the final RMSNorm is fused into the layer's FFN kernel"
    lbs = jnp.cumsum(jax.nn.softmax(lower_bounds.astype(F32), axis=0), axis=0)
    slopes = (2.0 ** (-8.0 * jnp.arange(1, H_ATT + 1, dtype=F32) / H_ATT)).reshape(G_KV, HG, 1, 1)
    overlap_t = _overlap_t(t)
    h = x
    for l in range(depth):
        w_n, w_t = _layout_w_in(w_in[l])
        x2 = h.reshape(b * t, d)
        proj = _in_proj(x2, g_mix[l][None, :], w_n)
        proj3 = proj.reshape(b, t, N_COLS)
        qv, gt = _in_proj_t(h, g_mix[l][None, :], w_t)
        y_r = _hgrn(proj3, lbs[l][None, :], g_rnn_out[l][None, :])
        kc, vct = _compress(proj3, pe_ck[l], pe_cv[l], w_ck1[l].astype(BF16), w_cv1[l].astype(BF16),
                            w_ck2[l].astype(BF16), w_cv2[l].T.astype(BF16))
        ocmp, selb = _cmp_select(qv, kc, vct, slopes, overlap_t, t)
        y_a = _sparse(qv, gt, proj3, ocmp, selb, slopes, t)
        h1 = _merge(x2, y_r.reshape(b * t, d), y_a.reshape(b * t, d), proj,
                    w_proj_rnn[l].astype(BF16), w_proj_att[l].astype(BF16), w_out[l].astype(BF16))
        w_xk, w_xv = jnp.split(w_xkv[l], 2, axis=1)
        kt, v = _mem_kv(mem, g_mem[l][None, :], w_xk.T.astype(BF16), w_xv.astype(BF16))
        h2 = _xattn(h1.reshape(b, t, d), g_xattn[l][None, :], w_xq[l].astype(BF16), kt, v,
                    w_xo[l].astype(BF16))
        h = _ffn(h2.reshape(b * t, d), g_ffn[l][None, :], w_gate_up[l].astype(BF16),
                 w_down[l].astype(BF16), g_final[None, :]).reshape(b, t, d)
    return h
```

```python
import functools

import jax
import jax.numpy as jnp
import numpy as np
from jax import lax
from jax.experimental import pallas as pl
from jax.experimental.pallas import tpu as pltpu

F32 = jnp.float32
BF16 = jnp.bfloat16

D_MODEL = 1024
N_MEM = 256
H_RNN = 8
RNN_DIM = 128
RNN_CHUNK = 64
H_ATT = 16
ATT_HD = 64
G_KV = 4
HG = H_ATT // G_KV
CMP_LEN = 32
CMP_STRIDE = 16
CMP_HIDDEN = 128
SLC_LEN = 64
SLC_TOPK = 8
WINDOW = 512
FORCE_BONUS = 1.0e4
H_X = 4
X_HD = 128
D_FF = 2816
EPS = 1e-6

LANE = 128
VMEM_LIMIT = 56 * 1024 * 1024
TOK_TILE = 512
ATT_TILE = 128
CMP_TILE = 512
N_CMP_PAD = 128
HGRN_GROUP = 4
ONES_ROWS = 16
NEG_BIG = -1.0e30
LOG2E = 1.4426950408889634
AUG_SEL_ROW0 = 8

QT_ROWS = H_ATT * ATT_HD
VT_ROWS = 2 * G_KV * ATT_HD
GATE_ROWS = 16
RNN_COLS = 4 * H_RNN * RNN_DIM
KV_COLS = 4 * G_KV * ATT_HD
MG_COLS = 2 * D_MODEL
N_COLS = RNN_COLS + KV_COLS + MG_COLS


def _cparams(sem):
    return pltpu.CompilerParams(dimension_semantics=sem, vmem_limit_bytes=VMEM_LIMIT)


def _rms(xf, g):
    return xf * lax.rsqrt(jnp.mean(xf * xf, axis=-1, keepdims=True) + EPS) * g


def _sigmoid(x):
    return 1.0 / (1.0 + jnp.exp(-x))


def _silu(x):
    return x * _sigmoid(x)


def _dot(a, b):
    return jnp.dot(a, b, preferred_element_type=F32)


def _dot_nt(a, b):
    return lax.dot_general(a, b, (((1,), (1,)), ((), ())), preferred_element_type=F32)


def _dot_tn(a, b):
    return lax.dot_general(a, b, (((0,), (0,)), ((), ())), preferred_element_type=F32)


def _in_proj_kernel(x_ref, g_ref, w_ref, o_ref, *, sub):
    a = _rms(x_ref[...], g_ref[...]).astype(BF16)
    tn = o_ref.shape[1]
    for n in range(tn // sub):
        o_ref[:, n * sub:(n + 1) * sub] = _dot(a, w_ref[:, n * sub:(n + 1) * sub]).astype(o_ref.dtype)


def _in_proj(x2, g, w):
    m, d = x2.shape
    n = w.shape[1]
    tn = n // 2
    return pl.pallas_call(
        functools.partial(_in_proj_kernel, sub=512),
        out_shape=jax.ShapeDtypeStruct((m, n), BF16),
        grid=(2, m // TOK_TILE),
        in_specs=[pl.BlockSpec((TOK_TILE, d), lambda j, i: (i, 0)),
                  pl.BlockSpec((1, d), lambda j, i: (0, 0)),
                  pl.BlockSpec((d, tn), lambda j, i: (0, j))],
        out_specs=pl.BlockSpec((TOK_TILE, tn), lambda j, i: (i, j)),
        compiler_params=_cparams(("arbitrary", "arbitrary")),
        name="in_proj",
    )(x2, g, w)


def _in_proj_t_kernel(x_ref, g_ref, w_ref, qv_ref, gt_ref, wt_ref):
    @pl.when((pl.program_id(0) == 0) & (pl.program_id(1) == 0))
    def _():
        wt_ref[...] = w_ref[...].astype(F32).T.astype(BF16)

    a = _rms(x_ref[...], g_ref[...]).astype(BF16)
    r = _dot_nt(wt_ref[...], a)
    nqv = qv_ref.shape[1]
    ngt = gt_ref.shape[1]
    for c in range(qv_ref.shape[0]):
        qv_ref[c] = r[:nqv, c * LANE:(c + 1) * LANE].astype(qv_ref.dtype)
        gt_ref[c] = r[nqv:nqv + ngt, c * LANE:(c + 1) * LANE]


def _in_proj_t(x, g, w):
    b, t, d = x.shape
    rows = w.shape[1]
    nqv = QT_ROWS + VT_ROWS
    ngt = G_KV * GATE_ROWS
    nc = TOK_TILE // LANE
    return pl.pallas_call(
        _in_proj_t_kernel,
        out_shape=(jax.ShapeDtypeStruct((b, t // LANE, nqv, LANE), BF16),
                   jax.ShapeDtypeStruct((b, t // LANE, ngt, LANE), F32)),
        grid=(b, t // TOK_TILE),
        in_specs=[pl.BlockSpec((None, TOK_TILE, d), lambda i, j: (i, j, 0)),
                  pl.BlockSpec((1, d), lambda i, j: (0, 0)),
                  pl.BlockSpec((d, rows), lambda i, j: (0, 0))],
        out_specs=(pl.BlockSpec((None, nc, nqv, LANE), lambda i, j: (i, j, 0, 0)),
                   pl.BlockSpec((None, nc, ngt, LANE), lambda i, j: (i, j, 0, 0))),
        scratch_shapes=[pltpu.VMEM((rows, d), BF16)],
        compiler_params=_cparams(("arbitrary", "arbitrary")),
        name="in_proj_t",
    )(x, g, w)


def _hgrn_kernel(q_ref, f_ref, i_ref, og_ref, lb_ref, gn_ref, o_ref, qd_ref, oi_ref, ut_ref, dec_ref):
    c = RNN_CHUNK
    kd = RNN_DIM
    n_chunks = q_ref.shape[0] // c
    lb = lb_ref[...]
    gn = gn_ref[...]
    blk = HGRN_GROUP * c
    row = lax.broadcasted_iota(jnp.int32, (blk, blk), 0)
    col = lax.broadcasted_iota(jnp.int32, (blk, blk), 1)
    same_chunk = (row // c) == (col // c)
    causal = same_chunk & (row >= col)
    tril = causal.astype(BF16)

    q = q_ref[...].astype(F32)
    fl = f_ref[...].astype(F32)
    v = i_ref[...]
    f = lb + (1.0 - lb) * _sigmoid(fl)
    k = 1.0 - f
    logf = jnp.log(f)
    hi = logf.astype(BF16)
    lo = (logf - hi.astype(F32)).astype(BF16)
    pieces = jnp.concatenate([hi, lo], axis=1)
    n_blk = q_ref.shape[0] // blk
    cs = [_dot(tril, pieces[g * blk:(g + 1) * blk]) for g in range(n_blk)]
    bcum = jnp.concatenate([x[:, 0:kd] + x[:, kd:2 * kd] for x in cs], axis=0)
    e_neg = jnp.exp(-bcum)
    dec = jnp.exp(jnp.concatenate([bcum[n * c + c - 1:n * c + c, :] for n in range(n_chunks)], axis=0))
    dec_rows = jnp.concatenate([jnp.broadcast_to(dec[n:n + 1, :], (c, kd)) for n in range(n_chunks)], axis=0)
    q_dec = (_silu(q) * (1.0 / e_neg)).astype(BF16)
    k_neg = k * e_neg
    k_dec = k_neg.astype(BF16)
    k_end = (k_neg * dec_rows).astype(BF16)
    qd_ref[...] = q_dec
    dec_ref[...] = dec
    for g in range(n_blk):
        rows = slice(g * blk, (g + 1) * blk)
        a = jnp.where(causal, _dot_nt(q_dec[rows], k_dec[rows]), 0.0)
        oi_ref[rows, :] = _dot(a.astype(BF16), v[rows])
    for n in range(n_chunks):
        rows = slice(n * c, (n + 1) * c)
        ut_ref[n] = _dot_tn(v[rows], k_end[rows])

    s_t = jnp.zeros((kd, kd), F32)
    for n in range(n_chunks):
        rows = slice(n * c, (n + 1) * c)
        o = oi_ref[rows, :] + _dot_nt(qd_ref[rows, :], s_t.astype(BF16))
        og = og_ref[rows, :].astype(F32)
        o_ref[rows, :] = (_rms(o, gn) * _silu(og)).astype(o_ref.dtype)
        s_t = s_t * dec_ref[n:n + 1, :] + ut_ref[n]


def _hgrn(proj3, lb, gn):
    b, t, _ = proj3.shape
    return pl.pallas_call(
        _hgrn_kernel,
        out_shape=jax.ShapeDtypeStruct((b, t, H_RNN * RNN_DIM), BF16),
        grid=(b, H_RNN),
        in_specs=[pl.BlockSpec((None, t, RNN_DIM), lambda i, h: (i, 0, h)),
                  pl.BlockSpec((None, t, RNN_DIM), lambda i, h: (i, 0, H_RNN + h)),
                  pl.BlockSpec((None, t, RNN_DIM), lambda i, h: (i, 0, 2 * H_RNN + h)),
                  pl.BlockSpec((None, t, RNN_DIM), lambda i, h: (i, 0, 3 * H_RNN + h)),
                  pl.BlockSpec((1, RNN_DIM), lambda i, h: (0, h)),
                  pl.BlockSpec((1, RNN_DIM), lambda i, h: (0, 0))],
        out_specs=pl.BlockSpec((None, t, RNN_DIM), lambda i, h: (i, 0, h)),
        scratch_shapes=[pltpu.VMEM((t, RNN_DIM), BF16),
                        pltpu.VMEM((t, RNN_DIM), F32),
                        pltpu.VMEM((t // RNN_CHUNK, RNN_DIM, RNN_DIM), F32),
                        pltpu.VMEM((t // RNN_CHUNK, RNN_DIM), F32)],
        compiler_params=_cparams(("arbitrary", "arbitrary")),
        name="hgrn",
    )(proj3, proj3, proj3, proj3, lb, gn)


def _compress_kernel(kv_ref, pek_ref, pev_ref, wk1_ref, wv1_ref, wk2_ref, wv2t_ref,
                     kc_ref, vct_ref, xs_ref, xk_ref, xv_ref):
    t = kv_ref.shape[0]
    hd = ATT_HD
    xs_ref[0:t, :] = kv_ref[...].astype(F32)
    xs_ref[t:, :] = jnp.zeros((xs_ref.shape[0] - t, xs_ref.shape[1]), F32)
    for l in range(CMP_LEN):
        blk = xs_ref[pl.ds(l, N_CMP_PAD, stride=CMP_STRIDE), :]
        xk_ref[:, l * hd:(l + 1) * hd] = (blk[:, 0:hd] + pek_ref[l:l + 1, :]).astype(BF16)
        xv_ref[:, l * hd:(l + 1) * hd] = (blk[:, hd:2 * hd] + pev_ref[l:l + 1, :]).astype(BF16)
    hk = _silu(_dot(xk_ref[...], wk1_ref[...])).astype(BF16)
    hv = _silu(_dot(xv_ref[...], wv1_ref[...])).astype(BF16)
    kc_ref[...] = _dot(hk, wk2_ref[...])
    vct_ref[...] = _dot_nt(wv2t_ref[...], hv)


def _compress(proj3, pek, pev, wk1, wv1, wk2, wv2t):
    b, t, _ = proj3.shape
    kv_unit0 = RNN_COLS // LANE
    flat = CMP_LEN * ATT_HD
    full = lambda shape: pl.BlockSpec(shape, lambda i, g: (0,) * len(shape))
    return pl.pallas_call(
        _compress_kernel,
        out_shape=(jax.ShapeDtypeStruct((b, G_KV, N_CMP_PAD, ATT_HD), F32),
                   jax.ShapeDtypeStruct((b, G_KV, ATT_HD, N_CMP_PAD), F32)),
        grid=(b, G_KV),
        in_specs=[pl.BlockSpec((None, t, LANE), lambda i, g: (i, 0, kv_unit0 + 2 * g)),
                  full((CMP_LEN, ATT_HD)), full((CMP_LEN, ATT_HD)),
                  full((flat, CMP_HIDDEN)), full((flat, CMP_HIDDEN)),
                  full((CMP_HIDDEN, ATT_HD)), full((ATT_HD, CMP_HIDDEN))],
        out_specs=(pl.BlockSpec((None, None, N_CMP_PAD, ATT_HD), lambda i, g: (i, g, 0, 0)),
                   pl.BlockSpec((None, None, ATT_HD, N_CMP_PAD), lambda i, g: (i, g, 0, 0))),
        scratch_shapes=[pltpu.VMEM((t + CMP_STRIDE, LANE), F32),
                        pltpu.VMEM((N_CMP_PAD, flat), BF16),
                        pltpu.VMEM((N_CMP_PAD, flat), BF16)],
        compiler_params=_cparams(("arbitrary", "arbitrary")),
        name="compress",
    )(proj3, pek, pev, wk1, wv1, wk2, wv2t)


def _cmp_select_kernel(qt_ref, kc_ref, vct_ref, slope_ref, ov_ref, ocmp_ref, selb_ref, *, n_cmp, n_sel):
    nct = CMP_TILE // LANE
    tt = CMP_TILE
    ns = ov_ref.shape[0]
    kc = kc_ref[...].astype(BF16)
    vct = vct_ref[...].astype(BF16)
    ov = ov_ref[...]
    j_i = lax.broadcasted_iota(jnp.int32, (8, tt), 0)
    tq_rel = lax.broadcasted_iota(jnp.int32, (8, tt), 1)
    n_tiles = qt_ref.shape[0] // nct

    def ranges(tile):
        t0 = tile * tt
        n_any = min(N_CMP_PAD, (t0 + tt) // CMP_STRIDE)
        n_all = max(0, (t0 - (CMP_LEN - 1)) // CMP_STRIDE + 1) // 8 * 8
        return t0, n_any, n_all, (t0 + tt) // SLC_LEN

    def pad_rows(x):
        if x.shape[0] == N_CMP_PAD:
            return x
        return jnp.concatenate([x, jnp.zeros((N_CMP_PAD - x.shape[0], x.shape[1]), x.dtype)], axis=0)

    scores = []
    for tile in range(n_tiles):
        _, n_any, _, _ = ranges(tile)
        qt = [jnp.concatenate([qt_ref[tile * nct + c, h * ATT_HD:(h + 1) * ATT_HD, :] for c in range(nct)],
                              axis=1) for h in range(HG)]
        scores.append(_dot(kc[:n_any], jnp.concatenate(qt, axis=1)))

    probs, psums = [], []
    for tile in range(n_tiles):
        t0, n_any, n_all, _ = ranges(tile)
        n_b = lax.broadcasted_iota(jnp.int32, (n_any - n_all, tt), 0) + n_all
        t_b = lax.broadcasted_iota(jnp.int32, (n_any - n_all, tt), 1)
        visible = (CMP_STRIDE * n_b + (CMP_LEN - 1) - t0 <= t_b) & (n_b < n_cmp)
        centre = (lax.broadcasted_iota(jnp.int32, (n_any, tt), 0).astype(F32) * float(CMP_STRIDE)
                  + (CMP_LEN - 1) / 2.0)
        psum = jnp.zeros((n_any, tt), F32)
        ph = []
        for h in range(HG):
            s = scores[tile][:, h * tt:(h + 1) * tt] + slope_ref[h] * LOG2E * centre
            band = jnp.where(visible, s[n_all:], -jnp.inf)
            s = band if n_all == 0 else jnp.concatenate([s[:n_all], band], axis=0)
            m = jnp.max(s, axis=0, keepdims=True)
            m = jnp.where(m == -jnp.inf, 0.0, m)
            p = jnp.exp2(s - m)
            d = jnp.sum(p, axis=0, keepdims=True)
            p = p * (1.0 / jnp.where(d > 0, d, 1.0))
            psum = psum + p
            ph.append(pad_rows(p).astype(BF16))
        probs.append(jnp.concatenate(ph, axis=1))
        psums.append(pad_rows(psum))

    for tile in range(n_tiles):
        o = _dot(vct, probs[tile])
        for h in range(HG):
            for c in range(nct):
                lanes = slice(h * tt + c * LANE, h * tt + (c + 1) * LANE)
                ocmp_ref[tile * nct + c, h * ATT_HD:(h + 1) * ATT_HD, :] = o[:, lanes]

    imps = []
    for tile in range(n_tiles):
        psum = psums[tile]
        hi = psum.astype(BF16)
        rem = psum - hi.astype(F32)
        mid = rem.astype(BF16)
        lo = (rem - mid.astype(F32)).astype(BF16)
        imp3 = _dot(ov, jnp.concatenate([hi, mid, lo], axis=1))
        imps.append(imp3[:, 0:tt] + imp3[:, tt:2 * tt] + imp3[:, 2 * tt:3 * tt])

    for tile in range(n_tiles):
        t0, _, _, nvb = ranges(tile)
        imp = imps[tile]
        score = []
        for rg in range(nvb // 8):
            j = j_i + 8 * rg
            tq = tq_rel + t0
            cur = tq // SLC_LEN
            forced = (j == 0) | (j == cur) | (j == cur - 1)
            sc = imp[8 * rg:8 * rg + 8, :] + jnp.where(forced, FORCE_BONUS, 0.0)
            score.append(jnp.where(j * SLC_LEN <= tq, sc, -jnp.inf))
        rank = [jnp.zeros((8, tt), F32) for _ in score]
        for jp in range(nvb):
            sj = jnp.broadcast_to(score[jp // 8][jp % 8:jp % 8 + 1, :], (8, tt))
            for rg in range(nvb // 8):
                ge = jnp.where(sj >= score[rg], 1.0, 0.0)
                gt = jnp.where(sj > score[rg], 1.0, 0.0)
                if 8 * rg > jp:
                    ahead = ge
                elif 8 * rg + 7 <= jp:
                    ahead = gt
                else:
                    ahead = jnp.where(j_i + 8 * rg > jp, ge, gt)
                rank[rg] = rank[rg] + ahead
        for rg in range(ns // 8):
            for c in range(nct):
                if rg < nvb // 8:
                    selb = jnp.where(rank[rg][:, c * LANE:(c + 1) * LANE] < float(n_sel), 0.0, NEG_BIG)
                else:
                    selb = jnp.full((8, LANE), NEG_BIG, F32)
                selb_ref[tile * nct + c, 8 * rg:8 * rg + 8, :] = selb


def _cmp_select(qv, kc, vct, slopes, overlap_t, t):
    b = qv.shape[0]
    ns = t // SLC_LEN
    nch = t // LANE
    n_cmp = (t - CMP_LEN) // CMP_STRIDE + 1
    n_sel = min(SLC_TOPK, ns)
    rows = HG * ATT_HD
    return pl.pallas_call(
        functools.partial(_cmp_select_kernel, n_cmp=n_cmp, n_sel=n_sel),
        out_shape=(jax.ShapeDtypeStruct((b, nch, H_ATT * ATT_HD, LANE), F32),
                   jax.ShapeDtypeStruct((b, G_KV, nch, ns, LANE), F32)),
        grid=(b, G_KV),
        in_specs=[pl.BlockSpec((None, nch, rows, LANE), lambda i, g: (i, 0, g, 0)),
                  pl.BlockSpec((None, None, N_CMP_PAD, ATT_HD), lambda i, g: (i, g, 0, 0)),
                  pl.BlockSpec((None, None, ATT_HD, N_CMP_PAD), lambda i, g: (i, g, 0, 0)),
                  pl.BlockSpec((None, HG, 1, 1), lambda i, g: (g, 0, 0, 0)),
                  pl.BlockSpec((ns, N_CMP_PAD), lambda i, g: (0, 0))],
        out_specs=(pl.BlockSpec((None, nch, rows, LANE), lambda i, g: (i, 0, g, 0)),
                   pl.BlockSpec((None, None, nch, ns, LANE), lambda i, g: (i, g, 0, 0, 0))),
        compiler_params=_cparams(("arbitrary", "arbitrary")),
        name="cmp_select",
    )(qv, kc, vct, slopes, overlap_t)


def _sparse_kernel(qt_ref, k_ref, ka_ref, vs_ref, vw_ref, gt_ref, ocmp_ref, selb_ref, slope_ref, o_ref,
                   msk_ref, qop_ref, s_ref, m_ref, acc_ref):
    tq = ATT_TILE
    kc = ATT_TILE
    hd = ATT_HD
    nl = HG * tq
    n_tiles = qt_ref.shape[0]
    ns = selb_ref.shape[1]
    wch = WINDOW // kc
    n_sel_items = n_tiles + 1
    slope = jnp.concatenate([jnp.broadcast_to(slope_ref[h], (1, tq)) for h in range(HG)], axis=1) * LOG2E
    s_hi = slope.astype(BF16).astype(F32)
    s_mid = (slope - s_hi).astype(BF16).astype(F32)
    s_lo = (slope - s_hi - s_mid).astype(BF16).astype(F32)
    slope_rows = jnp.concatenate([s_hi, s_mid, s_lo, jnp.zeros((AUG_SEL_ROW0 - 3, nl), F32)], axis=0)
    aug_tail = jnp.zeros((LANE - AUG_SEL_ROW0 - ns, nl), F32)
    aug_win = jnp.concatenate([slope_rows, jnp.zeros((ns, nl), F32), aug_tail], axis=0).astype(BF16)
    k_i = lax.broadcasted_iota(jnp.int32, (kc, nl), 0)
    t_i = lax.broadcasted_iota(jnp.int32, (kc, nl), 1) % tq
    d0 = t_i - k_i
    msk_ref[0] = jnp.where(d0 >= 0, 0.0, NEG_BIG)
    msk_ref[1] = jnp.where(d0 < 0, 0.0, NEG_BIG)
    chunk_bias = slope * float(kc)
    ones = jnp.ones((ONES_ROWS, kc), BF16)

    n_pairs = n_tiles // 2
    n_items = n_sel_items + 2 * (wch + 1)

    def tiles_of(p):
        return (p, n_tiles - 1 - p)

    def sel_item(p, i):
        lo, hi = tiles_of(p)
        if i == 0:
            return 0, lo, 0
        if i == n_sel_items - 1:
            return 1, hi, 0
        which = jnp.where(i > lo, 1, 0)
        tile = jnp.where(i > lo, hi, lo)
        chunk = jnp.where(i > lo, i - lo - 1, i - 1)
        return which, chunk, tile - chunk

    def win_item(p, w, j):
        chunk = tiles_of(p)[w] - wch + j
        return jnp.maximum(chunk, 0), chunk >= 0, wch - j

    def col_max(r, delta_f):
        r8 = jnp.max(r.reshape(kc // 8, 8, nl), axis=0)
        return r8 - chunk_bias * delta_f

    def scores(chunk, op):
        rows = pl.ds(pl.multiple_of(chunk * kc, kc), kc)
        keys = jnp.concatenate([k_ref[rows, :], ka_ref[rows, :]], axis=1)
        return _dot(keys, qop_ref[op])

    def build(p, slot):
        for w, tile in enumerate(tiles_of(p)):
            qt = qt_ref[tile]
            q_all = jnp.concatenate([qt[h * hd:(h + 1) * hd, :] for h in range(HG)], axis=1)
            zero = jnp.zeros_like(q_all)
            selb = jnp.concatenate([selb_ref[tile]] * HG, axis=1)
            aug_sel = jnp.concatenate([slope_rows, selb, aug_tail], axis=0).astype(BF16)
            qop_ref[4 * slot + 2 * w] = jnp.concatenate([q_all, zero, aug_sel], axis=0)
            qop_ref[4 * slot + 2 * w + 1] = jnp.concatenate([zero, q_all, aug_win], axis=0)
        for x in range(4):
            m_ref[4 * slot + x] = jnp.full((8, nl), NEG_BIG, F32)

    def phase_a(p, slot):
        items = []

        def sel(i):
            which, chunk, delta = sel_item(p, i)
            r = scores(chunk, 4 * slot + 2 * which)
            if i in (0, n_sel_items - 1):
                r = r + msk_ref[0]
            s_ref[slot * n_items + i] = r
            x = 4 * slot + 2 * which
            m_ref[x] = jnp.maximum(m_ref[x], col_max(r, jnp.float32(delta)))

        def win(w, j):
            chunk, valid, delta = win_item(p, w, j)
            r = scores(chunk, 4 * slot + 2 * w + 1)
            if j == 0:
                r = r + msk_ref[1]
            elif j == wch:
                r = r + msk_ref[0]
            s_ref[slot * n_items + n_sel_items + w * (wch + 1) + j] = r
            cm = jnp.where(valid, col_max(r, float(delta)), NEG_BIG)
            x = 4 * slot + 2 * w + 1
            m_ref[x] = jnp.maximum(m_ref[x], cm)

        for i in range(n_sel_items):
            items.append(functools.partial(sel, i))
        for w in range(2):
            for j in range(wch + 1):
                items.append(functools.partial(win, w, j))
        return items

    def phase_b(p, slot):
        m_row = [jnp.max(m_ref[4 * slot + x], axis=0, keepdims=True) for x in range(4)]
        for x in range(4):
            acc_ref[4 * slot + x] = jnp.zeros(acc_ref.shape[1:], F32)
        items = []

        def sel(i):
            which, chunk, delta = sel_item(p, i)
            row = jnp.where(which == 1, m_row[2], m_row[0]) + chunk_bias * jnp.float32(delta)
            pr = jnp.exp2(s_ref[slot * n_items + i] - row).astype(BF16)
            v_aug = jnp.concatenate([vs_ref[chunk], ones], axis=0)
            acc_ref[4 * slot + 2 * which] += _dot(v_aug, pr)

        def win(w, j):
            chunk, valid, delta = win_item(p, w, j)
            row = jnp.where(valid, m_row[2 * w + 1] + chunk_bias * float(delta), -NEG_BIG)
            pr = jnp.exp2(s_ref[slot * n_items + n_sel_items + w * (wch + 1) + j] - row).astype(BF16)
            v_aug = jnp.concatenate([vw_ref[chunk], ones], axis=0)
            acc_ref[4 * slot + 2 * w + 1] += _dot(v_aug, pr)

        for i in range(n_sel_items):
            items.append(functools.partial(sel, i))
        for w in range(2):
            for j in range(wch + 1):
                items.append(functools.partial(win, w, j))
        return items

    def finalize(p, slot):
        for w, tile in enumerate(tiles_of(p)):
            a_s = acc_ref[4 * slot + 2 * w]
            a_w = acc_ref[4 * slot + 2 * w + 1]
            o_slc = a_s[:hd, :] / a_s[hd:hd + 1, :]
            o_win = a_w[:hd, :] / a_w[hd:hd + 1, :]
            gates = _sigmoid(gt_ref[tile])
            oc = ocmp_ref[tile]
            outs = []
            for h in range(HG):
                lanes = slice(h * tq, (h + 1) * tq)
                y = (gates[h:h + 1, :] * oc[h * hd:(h + 1) * hd, :]
                     + gates[HG + h:HG + h + 1, :] * o_slc[:, lanes]
                     + gates[2 * HG + h:2 * HG + h + 1, :] * o_win[:, lanes])
                outs.append(y.T)
            rows = pl.ds(pl.multiple_of(tile * tq, tq), tq)
            o_ref[rows, :] = jnp.concatenate(outs, axis=1).astype(o_ref.dtype)

    def step(k, slot, first=False, last=False):
        if not last:
            build(k + 1, 1 - slot)
        if not first:
            finalize(k - 1, 1 - slot)
        b_items = phase_b(k, slot)
        a_items = phase_a(k + 1, 1 - slot) if not last else [None] * n_items
        for a, b in zip(a_items, b_items):
            if a is not None:
                a()
            b()

    build(0, 0)
    for a in phase_a(0, 0):
        a()
    step(0, 0, first=True)

    def body(kk, carry):
        step(2 * kk + 1, 1)
        step(2 * kk + 2, 0)
        return carry

    lax.fori_loop(0, (n_pairs - 2) // 2, body, 0)
    step(n_pairs - 1, (n_pairs - 1) % 2, last=True)
    finalize(n_pairs - 1, (n_pairs - 1) % 2)


def _key_features(t):
    kp = np.arange(t)
    f = np.zeros((t, LANE), np.float32)
    f[:, 0:3] = (kp % ATT_TILE)[:, None]
    f[kp, AUG_SEL_ROW0 + kp // SLC_LEN] = 1.0
    return jnp.asarray(f, dtype=BF16)


def _sparse(qv, gt, proj3, ocmp, selb, slopes, t):
    b = qv.shape[0]
    nchunk = t // LANE
    ns = t // SLC_LEN
    rows = HG * ATT_HD
    ksw_unit0 = RNN_COLS // LANE + 1
    vs_blk0 = QT_ROWS // ATT_HD
    vw_blk0 = vs_blk0 + G_KV
    nl = HG * ATT_TILE
    n_items = (nchunk + 1) + 2 * (WINDOW // ATT_TILE + 1)
    return pl.pallas_call(
        _sparse_kernel,
        out_shape=jax.ShapeDtypeStruct((b, t, H_ATT * ATT_HD), BF16),
        grid=(b, G_KV),
        in_specs=[pl.BlockSpec((None, nchunk, rows, LANE), lambda i, g: (i, 0, g, 0)),
                  pl.BlockSpec((None, t, LANE), lambda i, g: (i, 0, ksw_unit0 + 2 * g)),
                  pl.BlockSpec((t, LANE), lambda i, g: (0, 0)),
                  pl.BlockSpec((None, nchunk, ATT_HD, LANE), lambda i, g: (i, 0, vs_blk0 + g, 0)),
                  pl.BlockSpec((None, nchunk, ATT_HD, LANE), lambda i, g: (i, 0, vw_blk0 + g, 0)),
                  pl.BlockSpec((None, nchunk, GATE_ROWS, LANE), lambda i, g: (i, 0, g, 0)),
                  pl.BlockSpec((None, nchunk, rows, LANE), lambda i, g: (i, 0, g, 0)),
                  pl.BlockSpec((None, None, nchunk, ns, LANE), lambda i, g: (i, g, 0, 0, 0)),
                  pl.BlockSpec((None, HG, 1, 1), lambda i, g: (g, 0, 0, 0))],
        out_specs=pl.BlockSpec((None, t, rows), lambda i, g: (i, 0, g)),
        scratch_shapes=[pltpu.VMEM((2, ATT_TILE, nl), F32),
                        pltpu.VMEM((2 * 4, 2 * LANE, nl), BF16),
                        pltpu.VMEM((2 * n_items, ATT_TILE, nl), F32),
                        pltpu.VMEM((2 * 4, 8, nl), F32),
                        pltpu.VMEM((2 * 4, ATT_HD + ONES_ROWS, nl), F32)],
        compiler_params=_cparams(("arbitrary", "arbitrary")),
        name="sparse",
    )(qv, proj3, _key_features(t), qv, qv, gt, ocmp, selb, slopes)


def _merge_kernel(x_ref, yr_ref, ya_ref, mgr_ref, mga_ref, wr_ref, wa_ref, wo_ref, o_ref):
    pr = _dot(yr_ref[...], wr_ref[...])
    pa = _dot(ya_ref[...], wa_ref[...])
    merged = _sigmoid(mgr_ref[...].astype(F32)) * pr + _sigmoid(mga_ref[...].astype(F32)) * pa
    o_ref[...] = x_ref[...] + _dot(merged.astype(BF16), wo_ref[...])


def _merge(x2, yr, ya, proj, wr, wa, wo):
    m, d = x2.shape
    mg0 = (RNN_COLS + KV_COLS) // d
    tile = lambda col: pl.BlockSpec((TOK_TILE, d), lambda i: (i, col))
    wfull = pl.BlockSpec((d, d), lambda i: (0, 0))
    return pl.pallas_call(
        _merge_kernel,
        out_shape=jax.ShapeDtypeStruct((m, d), F32),
        grid=(m // TOK_TILE,),
        in_specs=[tile(0), tile(0), tile(0), tile(mg0), tile(mg0 + 1), wfull, wfull, wfull],
        out_specs=tile(0),
        compiler_params=_cparams(("arbitrary",)),
        name="merge",
    )(x2, yr, ya, proj, proj, wr, wa, wo)


def _mem_kv_kernel(mem_ref, g_ref, wkt_ref, wv_ref, kt_ref, v_ref):
    a = _rms(mem_ref[...], g_ref[...]).astype(BF16)
    kt_ref[...] = _dot_nt(wkt_ref[...], a).astype(kt_ref.dtype)
    v_ref[...] = _dot(a, wv_ref[...]).astype(v_ref.dtype)


def _mem_kv(mem, g, wkt, wkv):
    b, nm, d = mem.shape
    hw = H_X * X_HD
    return pl.pallas_call(
        _mem_kv_kernel,
        out_shape=(jax.ShapeDtypeStruct((b, hw, nm), BF16),
                   jax.ShapeDtypeStruct((b, nm, hw), BF16)),
        grid=(b,),
        in_specs=[pl.BlockSpec((None, nm, d), lambda i: (i, 0, 0)),
                  pl.BlockSpec((1, d), lambda i: (0, 0)),
                  pl.BlockSpec((hw, d), lambda i: (0, 0)),
                  pl.BlockSpec((d, hw), lambda i: (0, 1))],
        out_specs=(pl.BlockSpec((None, hw, nm), lambda i: (i, 0, 0)),
                   pl.BlockSpec((None, nm, hw), lambda i: (i, 0, 0))),
        compiler_params=_cparams(("arbitrary",)),
        name="mem_kv",
    )(mem, g, wkt, wkv)


def _xattn_kernel(h_ref, g_ref, wq_ref, kt_ref, v_ref, wo_ref, o_ref):
    h = h_ref[...]
    a = _rms(h, g_ref[...]).astype(BF16)
    q = (_dot(a, wq_ref[...]) * (X_HD ** -0.5)).astype(BF16)
    outs = []
    for hh in range(H_X):
        cols = slice(hh * X_HD, (hh + 1) * X_HD)
        s = _dot(q[:, cols], kt_ref[cols, :])
        m = jnp.max(s, axis=-1, keepdims=True)
        p = jnp.exp(s - m)
        p = p / jnp.sum(p, axis=-1, keepdims=True)
        outs.append(_dot(p.astype(BF16), v_ref[:, cols]))
    o = jnp.concatenate(outs, axis=1).astype(BF16)
    o_ref[...] = h + _dot(o, wo_ref[...])


def _xattn(h3, g, wq, kt, v, wo):
    b, t, d = h3.shape
    hw = H_X * X_HD
    nm = v.shape[1]
    return pl.pallas_call(
        _xattn_kernel,
        out_shape=jax.ShapeDtypeStruct((b, t, d), F32),
        grid=(b, t // TOK_TILE),
        in_specs=[pl.BlockSpec((None, TOK_TILE, d), lambda i, j: (i, j, 0)),
                  pl.BlockSpec((1, d), lambda i, j: (0, 0)),
                  pl.BlockSpec((d, hw), lambda i, j: (0, 0)),
                  pl.BlockSpec((None, hw, nm), lambda i, j: (i, 0, 0)),
                  pl.BlockSpec((None, nm, hw), lambda i, j: (i, 0, 0)),
                  pl.BlockSpec((hw, d), lambda i, j: (0, 0))],
        out_specs=pl.BlockSpec((None, TOK_TILE, d), lambda i, j: (i, j, 0)),
        compiler_params=_cparams(("arbitrary", "arbitrary")),
        name="xattn",
    )(h3, g, wq, kt, v, wo)


def _ffn_kernel(h_ref, g_ref, wg_ref, wu_ref, wd_ref, gf_ref, o_ref, *, fc):
    h = h_ref[...]
    a = _rms(h, g_ref[...]).astype(BF16)
    acc = h
    for f in range(wg_ref.shape[1] // fc):
        cols = slice(f * fc, (f + 1) * fc)
        mid = _silu(_dot(a, wg_ref[:, cols])) * _dot(a, wu_ref[:, cols])
        acc = acc + _dot(mid.astype(BF16), wd_ref[cols, :])
    o_ref[...] = _rms(acc, gf_ref[...])


def _ffn(h2, g, wgu, wd, gf):
    m, d = h2.shape
    ff = wd.shape[0]
    const = lambda shape: pl.BlockSpec(shape, lambda i: (0, 0))
    return pl.pallas_call(
        functools.partial(_ffn_kernel, fc=ff // 2),
        out_shape=jax.ShapeDtypeStruct((m, d), F32),
        grid=(m // TOK_TILE,),
        in_specs=[pl.BlockSpec((TOK_TILE, d), lambda i: (i, 0)),
                  const((1, d)), const((d, ff)), pl.BlockSpec((d, ff), lambda i: (0, 1)),
                  const((ff, d)), const((1, d))],
        out_specs=pl.BlockSpec((TOK_TILE, d), lambda i: (i, 0)),
        compiler_params=_cparams(("arbitrary",)),
        name="ffn",
    )(h2, g, wgu, wgu, wd, gf)


def _layout_w_in(w):
    d = w.shape[0]
    kvw = G_KV * ATT_HD
    q_a = w[:, RNN_COLS:RNN_COLS + QT_ROWS]
    kv = w[:, RNN_COLS + QT_ROWS:RNN_COLS + QT_ROWS + 6 * kvw]
    gate = w[:, RNN_COLS + QT_ROWS + 6 * kvw:RNN_COLS + QT_ROWS + 6 * kvw + 3 * H_ATT]
    kv4 = kv.reshape(d, 6, G_KV, ATT_HD)[:, np.array([0, 1, 2, 4])].transpose(0, 2, 1, 3).reshape(d, KV_COLS)
    w_n = jnp.concatenate([w[:, :RNN_COLS], kv4, w[:, -MG_COLS:]], axis=1).astype(BF16)
    gate_g = gate.reshape(d, G_KV, HG, 3).transpose(0, 1, 3, 2).reshape(d, G_KV, 3 * HG)
    gate_g = jnp.pad(gate_g, ((0, 0), (0, 0), (0, GATE_ROWS - 3 * HG))).reshape(d, G_KV * GATE_ROWS)
    cols = QT_ROWS + VT_ROWS + G_KV * GATE_ROWS
    w_f = jnp.concatenate([q_a * (ATT_HD ** -0.5 * LOG2E), kv[:, 3 * kvw:4 * kvw], kv[:, 5 * kvw:6 * kvw], gate_g,
                           jnp.zeros((d, -cols % LANE), w.dtype)], axis=1).astype(BF16)
    return w_n, w_f


def _overlap_t(t):
    nc = (t - CMP_LEN) // CMP_STRIDE + 1
    ns = t // SLC_LEN
    starts = CMP_STRIDE * np.arange(N_CMP_PAD)
    s_start = SLC_LEN * np.arange(ns)
    ov = ((starts[None, :] + CMP_LEN > s_start[:, None]) & (starts[None, :] < s_start[:, None] + SLC_LEN)
          & (np.arange(N_CMP_PAD)[None, :] < nc))
    return jnp.asarray(ov.astype(np.float32), dtype=BF16)


def kernel(x, mem, g_mix, w_in, lower_bounds, g_rnn_out, pe_ck, w_ck1, w_ck2, pe_cv, w_cv1, w_cv2,
           w_proj_rnn, w_proj_att, w_out, g_xattn, g_mem, w_xq, w_xkv, w_xo, g_ffn, w_gate_up,
           w_down, g_final):
    b, t, d = x.shape
    depth = g_mix.shape[0]
    assert depth == 1, "the final RMSNorm is fused into the layer's FFN kernel"
    lbs = jnp.cumsum(jax.nn.softmax(lower_bounds.astype(F32), axis=0), axis=0)
    slopes = (2.0 ** (-8.0 * jnp.arange(1, H_ATT + 1, dtype=F32) / H_ATT)).reshape(G_KV, HG, 1, 1)
    overlap_t = _overlap_t(t)
    h = x
    for l in range(depth):
        w_n, w_t = _layout_w_in(w_in[l])
        x2 = h.reshape(b * t, d)
        proj = _in_proj(x2, g_mix[l][None, :], w_n)
        proj3 = proj.reshape(b, t, N_COLS)
        qv, gt = _in_proj_t(h, g_mix[l][None, :], w_t)
        y_r = _hgrn(proj3, lbs[l][None, :], g_rnn_out[l][None, :])
        kc, vct = _compress(proj3, pe_ck[l], pe_cv[l], w_ck1[l].astype(BF16), w_cv1[l].astype(BF16),
                            w_ck2[l].astype(BF16), w_cv2[l].T.astype(BF16))
        ocmp, selb = _cmp_select(qv, kc, vct, slopes, overlap_t, t)
        y_a = _sparse(qv, gt, proj3, ocmp, selb, slopes, t)
        h1 = _merge(x2, y_r.reshape(b * t, d), y_a.reshape(b * t, d), proj,
                    w_proj_rnn[l].astype(BF16), w_proj_att[l].astype(BF16), w_out[l].astype(BF16))
        w_kv = w_xkv[l].astype(BF16)
        kt, v = _mem_kv(mem, g_mem[l][None, :], w_kv[:, :H_X * X_HD].T, w_kv)
        h2 = _xattn(h1.reshape(b, t, d), g_xattn[l][None, :], w_xq[l].astype(BF16), kt, v,
                    w_xo[l].astype(BF16))
        h = _ffn(h2.reshape(b * t, d), g_ffn[l][None, :], w_gate_up[l].astype(BF16),
                 w_down[l].astype(BF16), g_final[None, :]).reshape(b, t, d)
    return h
```

```python
import functools

import jax
import jax.numpy as jnp
import numpy as np
from jax import lax
from jax.experimental import pallas as pl
from jax.experimental.pallas import tpu as pltpu

F32 = jnp.float32
BF16 = jnp.bfloat16

D_MODEL = 1024
N_MEM = 256
H_RNN = 8
RNN_DIM = 128
RNN_CHUNK = 64
H_ATT = 16
ATT_HD = 64
G_KV = 4
HG = H_ATT // G_KV
CMP_LEN = 32
CMP_STRIDE = 16
CMP_HIDDEN = 128
SLC_LEN = 64
SLC_TOPK = 8
WINDOW = 512
FORCE_BONUS = 1.0e4
H_X = 4
X_HD = 128
D_FF = 2816
EPS = 1e-6

LANE = 128
VMEM_LIMIT = 56 * 1024 * 1024
TOK_TILE = 512
ATT_TILE = 128
CMP_TILE = 512
N_CMP_PAD = 128
HGRN_GROUP = 4
ONES_ROWS = 16
NEG_BIG = -1.0e30
LOG2E = 1.4426950408889634
AUG_SEL_ROW0 = 8

QT_ROWS = H_ATT * ATT_HD
VT_ROWS = 2 * G_KV * ATT_HD
GATE_ROWS = 64
RNN_COLS = 4 * H_RNN * RNN_DIM
KV_COLS = 4 * G_KV * ATT_HD
MG_COLS = 2 * D_MODEL
N_COLS = RNN_COLS + KV_COLS + MG_COLS


def _cparams(sem):
    return pltpu.CompilerParams(dimension_semantics=sem, vmem_limit_bytes=VMEM_LIMIT)


def _rms(xf, g):
    return xf * lax.rsqrt(jnp.mean(xf * xf, axis=-1, keepdims=True) + EPS) * g


def _sigmoid(x):
    return 1.0 / (1.0 + jnp.exp(-x))


def _silu(x):
    return x * _sigmoid(x)


def _dot(a, b):
    return jnp.dot(a, b, preferred_element_type=F32)


def _dot_nt(a, b):
    return lax.dot_general(a, b, (((1,), (1,)), ((), ())), preferred_element_type=F32)


def _dot_tn(a, b):
    return lax.dot_general(a, b, (((0,), (0,)), ((), ())), preferred_element_type=F32)


def _interleave_groups(r):
    half = G_KV * ATT_HD
    parts = []
    for g in range(G_KV):
        parts += [r[:, g * ATT_HD:(g + 1) * ATT_HD], r[:, half + g * ATT_HD:half + (g + 1) * ATT_HD]]
    return jnp.concatenate(parts, axis=1)


def _in_proj_kernel(x_ref, g_ref, w_ref, o_ref, *, sub, kv_step, kv_subs):
    a = _rms(x_ref[...], g_ref[...]).astype(BF16)
    tn = o_ref.shape[1]
    for n in range(tn // sub):
        cols = slice(n * sub, (n + 1) * sub)
        r = _dot(a, w_ref[:, cols])
        if n in kv_subs:
            is_kv = pl.program_id(0) == kv_step
            r = jnp.where(is_kv, _interleave_groups(r), r)
        o_ref[:, cols] = r.astype(o_ref.dtype)


def _in_proj(x2, g, w):
    m, d = x2.shape
    n = w.shape[1]
    tn = n // 2
    sub = 2 * G_KV * ATT_HD
    kv_subs = tuple((RNN_COLS % tn) // sub + s for s in range(KV_COLS // sub))
    return pl.pallas_call(
        functools.partial(_in_proj_kernel, sub=sub, kv_step=RNN_COLS // tn, kv_subs=kv_subs),
        out_shape=jax.ShapeDtypeStruct((m, n), BF16),
        grid=(2, m // TOK_TILE),
        in_specs=[pl.BlockSpec((TOK_TILE, d), lambda j, i: (i, 0)),
                  pl.BlockSpec((1, d), lambda j, i: (0, 0)),
                  pl.BlockSpec((d, tn), lambda j, i: (0, j))],
        out_specs=pl.BlockSpec((TOK_TILE, tn), lambda j, i: (i, j)),
        compiler_params=_cparams(("arbitrary", "arbitrary")),
        name="in_proj",
    )(x2, g, w)


def _in_proj_t_kernel(x_ref, g_ref, w_ref, qv_ref, gt_ref, wt_ref):
    @pl.when((pl.program_id(0) == 0) & (pl.program_id(1) == 0))
    def _():
        wt_ref[...] = w_ref[...].astype(F32).T.astype(BF16)

    a = _rms(x_ref[...], g_ref[...]).astype(BF16)
    r = _dot_nt(wt_ref[...], a)
    nqv = qv_ref.shape[1]
    ngt = gt_ref.shape[1]
    for c in range(qv_ref.shape[0]):
        qv_ref[c] = r[:nqv, c * LANE:(c + 1) * LANE].astype(qv_ref.dtype)
        gt_ref[c] = r[nqv:nqv + ngt, c * LANE:(c + 1) * LANE]


def _in_proj_t(x, g, w):
    b, t, d = x.shape
    rows = w.shape[1]
    nqv = QT_ROWS + VT_ROWS
    ngt = GATE_ROWS
    nc = TOK_TILE // LANE
    return pl.pallas_call(
        _in_proj_t_kernel,
        out_shape=(jax.ShapeDtypeStruct((b, t // LANE, nqv, LANE), BF16),
                   jax.ShapeDtypeStruct((b, t // LANE, ngt, LANE), F32)),
        grid=(b, t // TOK_TILE),
        in_specs=[pl.BlockSpec((None, TOK_TILE, d), lambda i, j: (i, j, 0)),
                  pl.BlockSpec((1, d), lambda i, j: (0, 0)),
                  pl.BlockSpec((d, rows), lambda i, j: (0, 0))],
        out_specs=(pl.BlockSpec((None, nc, nqv, LANE), lambda i, j: (i, j, 0, 0)),
                   pl.BlockSpec((None, nc, ngt, LANE), lambda i, j: (i, j, 0, 0))),
        scratch_shapes=[pltpu.VMEM((rows, d), BF16)],
        compiler_params=_cparams(("arbitrary", "arbitrary")),
        name="in_proj_t",
    )(x, g, w)


def _hgrn_kernel(q_ref, f_ref, i_ref, og_ref, lb_ref, gn_ref, o_ref, qd_ref, oi_ref, ut_ref, dec_ref):
    c = RNN_CHUNK
    kd = RNN_DIM
    n_chunks = q_ref.shape[0] // c
    lb = lb_ref[...]
    gn = gn_ref[...]
    blk = HGRN_GROUP * c
    row = lax.broadcasted_iota(jnp.int32, (blk, blk), 0)
    col = lax.broadcasted_iota(jnp.int32, (blk, blk), 1)
    same_chunk = (row // c) == (col // c)
    causal = same_chunk & (row >= col)
    tril = causal.astype(BF16)

    q = q_ref[...].astype(F32)
    fl = f_ref[...].astype(F32)
    v = i_ref[...]
    f = lb + (1.0 - lb) * _sigmoid(fl)
    k = 1.0 - f
    logf = jnp.log(f)
    hi = logf.astype(BF16)
    lo = (logf - hi.astype(F32)).astype(BF16)
    pieces = jnp.concatenate([hi, lo], axis=1)
    n_blk = q_ref.shape[0] // blk
    cs = [_dot(tril, pieces[g * blk:(g + 1) * blk]) for g in range(n_blk)]
    bcum = jnp.concatenate([x[:, 0:kd] + x[:, kd:2 * kd] for x in cs], axis=0)
    e_neg = jnp.exp(-bcum)
    dec = jnp.exp(jnp.concatenate([bcum[n * c + c - 1:n * c + c, :] for n in range(n_chunks)], axis=0))
    dec_rows = jnp.concatenate([jnp.broadcast_to(dec[n:n + 1, :], (c, kd)) for n in range(n_chunks)], axis=0)
    q_dec = (_silu(q) * (1.0 / e_neg)).astype(BF16)
    k_neg = k * e_neg
    k_dec = k_neg.astype(BF16)
    k_end = (k_neg * dec_rows).astype(BF16)
    qd_ref[...] = q_dec
    dec_ref[...] = dec
    for g in range(n_blk):
        rows = slice(g * blk, (g + 1) * blk)
        a = jnp.where(causal, _dot_nt(q_dec[rows], k_dec[rows]), 0.0)
        oi_ref[rows, :] = _dot(a.astype(BF16), v[rows])
    for n in range(n_chunks):
        rows = slice(n * c, (n + 1) * c)
        ut_ref[n] = _dot_tn(v[rows], k_end[rows])

    s_t = jnp.zeros((kd, kd), F32)
    for n in range(n_chunks):
        rows = slice(n * c, (n + 1) * c)
        o = oi_ref[rows, :] + _dot_nt(qd_ref[rows, :], s_t.astype(BF16))
        og = og_ref[rows, :].astype(F32)
        o_ref[rows, :] = (_rms(o, gn) * _silu(og)).astype(o_ref.dtype)
        s_t = s_t * dec_ref[n:n + 1, :] + ut_ref[n]


def _hgrn(proj3, lb, gn):
    b, t, _ = proj3.shape
    return pl.pallas_call(
        _hgrn_kernel,
        out_shape=jax.ShapeDtypeStruct((b, t, H_RNN * RNN_DIM), BF16),
        grid=(b, H_RNN),
        in_specs=[pl.BlockSpec((None, t, RNN_DIM), lambda i, h: (i, 0, h)),
                  pl.BlockSpec((None, t, RNN_DIM), lambda i, h: (i, 0, H_RNN + h)),
                  pl.BlockSpec((None, t, RNN_DIM), lambda i, h: (i, 0, 2 * H_RNN + h)),
                  pl.BlockSpec((None, t, RNN_DIM), lambda i, h: (i, 0, 3 * H_RNN + h)),
                  pl.BlockSpec((1, RNN_DIM), lambda i, h: (0, h)),
                  pl.BlockSpec((1, RNN_DIM), lambda i, h: (0, 0))],
        out_specs=pl.BlockSpec((None, t, RNN_DIM), lambda i, h: (i, 0, h)),
        scratch_shapes=[pltpu.VMEM((t, RNN_DIM), BF16),
                        pltpu.VMEM((t, RNN_DIM), F32),
                        pltpu.VMEM((t // RNN_CHUNK, RNN_DIM, RNN_DIM), F32),
                        pltpu.VMEM((t // RNN_CHUNK, RNN_DIM), F32)],
        compiler_params=_cparams(("arbitrary", "arbitrary")),
        name="hgrn",
    )(proj3, proj3, proj3, proj3, lb, gn)


def _compress_kernel(kv_ref, pek_ref, pev_ref, wk1_ref, wv1_ref, wk2_ref, wv2t_ref,
                     kc_ref, vct_ref, xs_ref, xk_ref, xv_ref):
    t = kv_ref.shape[0]
    hd = ATT_HD
    xs_ref[0:t, :] = kv_ref[...].astype(F32)
    xs_ref[t:, :] = jnp.zeros((xs_ref.shape[0] - t, xs_ref.shape[1]), F32)
    for l in range(CMP_LEN):
        blk = xs_ref[pl.ds(l, N_CMP_PAD, stride=CMP_STRIDE), :]
        xk_ref[:, l * hd:(l + 1) * hd] = (blk[:, 0:hd] + pek_ref[l:l + 1, :]).astype(BF16)
        xv_ref[:, l * hd:(l + 1) * hd] = (blk[:, hd:2 * hd] + pev_ref[l:l + 1, :]).astype(BF16)
    hk = _silu(_dot(xk_ref[...], wk1_ref[...])).astype(BF16)
    hv = _silu(_dot(xv_ref[...], wv1_ref[...])).astype(BF16)
    kc_ref[...] = _dot(hk, wk2_ref[...])
    vct_ref[...] = _dot_nt(wv2t_ref[...], hv)


def _compress(proj3, pek, pev, wk1, wv1, wk2, wv2t):
    b, t, _ = proj3.shape
    kv_unit0 = RNN_COLS // LANE
    flat = CMP_LEN * ATT_HD
    full = lambda shape: pl.BlockSpec(shape, lambda i, g: (0,) * len(shape))
    return pl.pallas_call(
        _compress_kernel,
        out_shape=(jax.ShapeDtypeStruct((b, G_KV, N_CMP_PAD, ATT_HD), F32),
                   jax.ShapeDtypeStruct((b, G_KV, ATT_HD, N_CMP_PAD), F32)),
        grid=(b, G_KV),
        in_specs=[pl.BlockSpec((None, t, LANE), lambda i, g: (i, 0, kv_unit0 + g)),
                  full((CMP_LEN, ATT_HD)), full((CMP_LEN, ATT_HD)),
                  full((flat, CMP_HIDDEN)), full((flat, CMP_HIDDEN)),
                  full((CMP_HIDDEN, ATT_HD)), full((ATT_HD, CMP_HIDDEN))],
        out_specs=(pl.BlockSpec((None, None, N_CMP_PAD, ATT_HD), lambda i, g: (i, g, 0, 0)),
                   pl.BlockSpec((None, None, ATT_HD, N_CMP_PAD), lambda i, g: (i, g, 0, 0))),
        scratch_shapes=[pltpu.VMEM((t + CMP_STRIDE, LANE), F32),
                        pltpu.VMEM((N_CMP_PAD, flat), BF16),
                        pltpu.VMEM((N_CMP_PAD, flat), BF16)],
        compiler_params=_cparams(("arbitrary", "arbitrary")),
        name="compress",
    )(proj3, pek, pev, wk1, wv1, wk2, wv2t)


def _cmp_select_kernel(qt_ref, kc_ref, vct_ref, slope_ref, ov_ref, ocmp_ref, selb_ref, *, n_cmp, n_sel):
    nct = CMP_TILE // LANE
    tt = CMP_TILE
    ns = ov_ref.shape[0]
    kc = kc_ref[...].astype(BF16)
    vct = vct_ref[...].astype(BF16)
    ov = ov_ref[...]
    j_i = lax.broadcasted_iota(jnp.int32, (8, tt), 0)
    tq_rel = lax.broadcasted_iota(jnp.int32, (8, tt), 1)
    n_tiles = qt_ref.shape[0] // nct

    def ranges(tile):
        t0 = tile * tt
        n_any = min(N_CMP_PAD, (t0 + tt) // CMP_STRIDE)
        n_all = max(0, (t0 - (CMP_LEN - 1)) // CMP_STRIDE + 1) // 8 * 8
        return t0, n_any, n_all, (t0 + tt) // SLC_LEN

    def pad_rows(x):
        if x.shape[0] == N_CMP_PAD:
            return x
        return jnp.concatenate([x, jnp.zeros((N_CMP_PAD - x.shape[0], x.shape[1]), x.dtype)], axis=0)

    scores = []
    for tile in range(n_tiles):
        _, n_any, _, _ = ranges(tile)
        qt = [jnp.concatenate([qt_ref[tile * nct + c, h * ATT_HD:(h + 1) * ATT_HD, :] for c in range(nct)],
                              axis=1) for h in range(HG)]
        scores.append(_dot(kc[:n_any], jnp.concatenate(qt, axis=1)))

    probs, psums = [], []
    for tile in range(n_tiles):
        t0, n_any, n_all, _ = ranges(tile)
        n_b = lax.broadcasted_iota(jnp.int32, (n_any - n_all, tt), 0) + n_all
        t_b = lax.broadcasted_iota(jnp.int32, (n_any - n_all, tt), 1)
        visible = (CMP_STRIDE * n_b + (CMP_LEN - 1) - t0 <= t_b) & (n_b < n_cmp)
        centre = (lax.broadcasted_iota(jnp.int32, (n_any, tt), 0).astype(F32) * float(CMP_STRIDE)
                  + (CMP_LEN - 1) / 2.0)
        psum = jnp.zeros((n_any, tt), F32)
        ph = []
        for h in range(HG):
            s = scores[tile][:, h * tt:(h + 1) * tt] + slope_ref[h] * LOG2E * centre
            band = jnp.where(visible, s[n_all:], -jnp.inf)
            s = band if n_all == 0 else jnp.concatenate([s[:n_all], band], axis=0)
            m = jnp.max(s, axis=0, keepdims=True)
            m = jnp.where(m == -jnp.inf, 0.0, m)
            p = jnp.exp2(s - m)
            d = jnp.sum(p, axis=0, keepdims=True)
            p = p * (1.0 / jnp.where(d > 0, d, 1.0))
            psum = psum + p
            ph.append(pad_rows(p).astype(BF16))
        probs.append(jnp.concatenate(ph, axis=1))
        psums.append(pad_rows(psum))

    for tile in range(n_tiles):
        o = _dot(vct, probs[tile])
        for h in range(HG):
            for c in range(nct):
                lanes = slice(h * tt + c * LANE, h * tt + (c + 1) * LANE)
                ocmp_ref[tile * nct + c, h * ATT_HD:(h + 1) * ATT_HD, :] = o[:, lanes]

    imps = []
    for tile in range(n_tiles):
        psum = psums[tile]
        hi = psum.astype(BF16)
        rem = psum - hi.astype(F32)
        mid = rem.astype(BF16)
        lo = (rem - mid.astype(F32)).astype(BF16)
        imp3 = _dot(ov, jnp.concatenate([hi, mid, lo], axis=1))
        imps.append(imp3[:, 0:tt] + imp3[:, tt:2 * tt] + imp3[:, 2 * tt:3 * tt])

    for tile in range(n_tiles):
        t0, _, _, nvb = ranges(tile)
        imp = imps[tile]
        score = []
        for rg in range(nvb // 8):
            j = j_i + 8 * rg
            tq = tq_rel + t0
            cur = tq // SLC_LEN
            forced = (j == 0) | (j == cur) | (j == cur - 1)
            sc = imp[8 * rg:8 * rg + 8, :] + jnp.where(forced, FORCE_BONUS, 0.0)
            score.append(jnp.where(j * SLC_LEN <= tq, sc, -jnp.inf))
        rank = [jnp.zeros((8, tt), F32) for _ in score]
        for jp in range(nvb):
            sj = jnp.broadcast_to(score[jp // 8][jp % 8:jp % 8 + 1, :], (8, tt))
            for rg in range(nvb // 8):
                ge = jnp.where(sj >= score[rg], 1.0, 0.0)
                gt = jnp.where(sj > score[rg], 1.0, 0.0)
                if 8 * rg > jp:
                    ahead = ge
                elif 8 * rg + 7 <= jp:
                    ahead = gt
                else:
                    ahead = jnp.where(j_i + 8 * rg > jp, ge, gt)
                rank[rg] = rank[rg] + ahead
        for rg in range(ns // 8):
            for c in range(nct):
                if rg < nvb // 8:
                    selb = jnp.where(rank[rg][:, c * LANE:(c + 1) * LANE] < float(n_sel), 0.0, NEG_BIG)
                else:
                    selb = jnp.full((8, LANE), NEG_BIG, F32)
                selb_ref[tile * nct + c, 8 * rg:8 * rg + 8, :] = selb


def _cmp_select(qv, kc, vct, slopes, overlap_t, t):
    b = qv.shape[0]
    ns = t // SLC_LEN
    nch = t // LANE
    n_cmp = (t - CMP_LEN) // CMP_STRIDE + 1
    n_sel = min(SLC_TOPK, ns)
    rows = HG * ATT_HD
    return pl.pallas_call(
        functools.partial(_cmp_select_kernel, n_cmp=n_cmp, n_sel=n_sel),
        out_shape=(jax.ShapeDtypeStruct((b, nch, H_ATT * ATT_HD, LANE), F32),
                   jax.ShapeDtypeStruct((b, G_KV, nch, ns, LANE), F32)),
        grid=(b, G_KV),
        in_specs=[pl.BlockSpec((None, nch, rows, LANE), lambda i, g: (i, 0, g, 0)),
                  pl.BlockSpec((None, None, N_CMP_PAD, ATT_HD), lambda i, g: (i, g, 0, 0)),
                  pl.BlockSpec((None, None, ATT_HD, N_CMP_PAD), lambda i, g: (i, g, 0, 0)),
                  pl.BlockSpec((None, HG, 1, 1), lambda i, g: (g, 0, 0, 0)),
                  pl.BlockSpec((ns, N_CMP_PAD), lambda i, g: (0, 0))],
        out_specs=(pl.BlockSpec((None, nch, rows, LANE), lambda i, g: (i, 0, g, 0)),
                   pl.BlockSpec((None, None, nch, ns, LANE), lambda i, g: (i, g, 0, 0, 0))),
        compiler_params=_cparams(("arbitrary", "arbitrary")),
        name="cmp_select",
    )(qv, kc, vct, slopes, overlap_t)


def _sparse_kernel(qt_ref, k_ref, ka_ref, vs_ref, vw_ref, gt_ref, ocmp_ref, selb_ref, slope_ref, o_ref,
                   msk_ref, qop_ref, s_ref, m_ref, acc_ref):
    tq = ATT_TILE
    kc = ATT_TILE
    hd = ATT_HD
    nl = HG * tq
    n_tiles = qt_ref.shape[0]
    ns = selb_ref.shape[1]
    wch = WINDOW // kc
    n_sel_items = n_tiles + 1
    slope = jnp.concatenate([jnp.broadcast_to(slope_ref[h], (1, tq)) for h in range(HG)], axis=1) * LOG2E
    s_hi = slope.astype(BF16).astype(F32)
    s_mid = (slope - s_hi).astype(BF16).astype(F32)
    s_lo = (slope - s_hi - s_mid).astype(BF16).astype(F32)
    slope_rows = jnp.concatenate([s_hi, s_mid, s_lo, jnp.zeros((AUG_SEL_ROW0 - 3, nl), F32)], axis=0)
    aug_tail = jnp.zeros((LANE - AUG_SEL_ROW0 - ns, nl), F32)
    aug_win = jnp.concatenate([slope_rows, jnp.zeros((ns, nl), F32), aug_tail], axis=0).astype(BF16)
    k_i = lax.broadcasted_iota(jnp.int32, (kc, nl), 0)
    t_i = lax.broadcasted_iota(jnp.int32, (kc, nl), 1) % tq
    d0 = t_i - k_i
    msk_ref[0] = jnp.where(d0 >= 0, 0.0, NEG_BIG)
    msk_ref[1] = jnp.where(d0 < 0, 0.0, NEG_BIG)
    chunk_bias = slope * float(kc)
    ones = jnp.ones((ONES_ROWS, kc), BF16)

    n_pairs = n_tiles // 2
    n_items = n_sel_items + 2 * (wch + 1)

    def tiles_of(p):
        return (p, n_tiles - 1 - p)

    def sel_item(p, i):
        lo, hi = tiles_of(p)
        if i == 0:
            return 0, lo, 0
        if i == n_sel_items - 1:
            return 1, hi, 0
        which = jnp.where(i > lo, 1, 0)
        tile = jnp.where(i > lo, hi, lo)
        chunk = jnp.where(i > lo, i - lo - 1, i - 1)
        return which, chunk, tile - chunk

    def win_item(p, w, j):
        chunk = tiles_of(p)[w] - wch + j
        return jnp.maximum(chunk, 0), chunk >= 0, wch - j

    def col_max(r, delta_f):
        r8 = jnp.max(r.reshape(kc // 8, 8, nl), axis=0)
        return r8 - chunk_bias * delta_f

    def scores(chunk, op):
        rows = pl.ds(pl.multiple_of(chunk * kc, kc), kc)
        keys = jnp.concatenate([k_ref[rows, :], ka_ref[rows, :]], axis=1)
        return _dot(keys, qop_ref[op])

    def build(p, slot):
        for w, tile in enumerate(tiles_of(p)):
            qt = qt_ref[tile]
            q_all = jnp.concatenate([qt[h * hd:(h + 1) * hd, :] for h in range(HG)], axis=1)
            zero = jnp.zeros_like(q_all)
            selb = jnp.concatenate([selb_ref[tile]] * HG, axis=1)
            aug_sel = jnp.concatenate([slope_rows, selb, aug_tail], axis=0).astype(BF16)
            qop_ref[4 * slot + 2 * w] = jnp.concatenate([q_all, zero, aug_sel], axis=0)
            qop_ref[4 * slot + 2 * w + 1] = jnp.concatenate([zero, q_all, aug_win], axis=0)
        for x in range(4):
            m_ref[4 * slot + x] = jnp.full((8, nl), NEG_BIG, F32)

    def phase_a(p, slot):
        items = []

        def sel(i):
            which, chunk, delta = sel_item(p, i)
            r = scores(chunk, 4 * slot + 2 * which)
            if i in (0, n_sel_items - 1):
                r = r + msk_ref[0]
            s_ref[slot * n_items + i] = r
            x = 4 * slot + 2 * which
            m_ref[x] = jnp.maximum(m_ref[x], col_max(r, jnp.float32(delta)))

        def win(w, j):
            chunk, valid, delta = win_item(p, w, j)
            r = scores(chunk, 4 * slot + 2 * w + 1)
            if j == 0:
                r = r + msk_ref[1]
            elif j == wch:
                r = r + msk_ref[0]
            s_ref[slot * n_items + n_sel_items + w * (wch + 1) + j] = r
            cm = jnp.where(valid, col_max(r, float(delta)), NEG_BIG)
            x = 4 * slot + 2 * w + 1
            m_ref[x] = jnp.maximum(m_ref[x], cm)

        for i in range(n_sel_items):
            items.append(functools.partial(sel, i))
        for w in range(2):
            for j in range(wch + 1):
                items.append(functools.partial(win, w, j))
        return items

    def phase_b(p, slot):
        m_row = [jnp.max(m_ref[4 * slot + x], axis=0, keepdims=True) for x in range(4)]
        for x in range(4):
            acc_ref[4 * slot + x] = jnp.zeros(acc_ref.shape[1:], F32)
        items = []

        def sel(i):
            which, chunk, delta = sel_item(p, i)
            row = jnp.where(which == 1, m_row[2], m_row[0]) + chunk_bias * jnp.float32(delta)
            pr = jnp.exp2(s_ref[slot * n_items + i] - row).astype(BF16)
            v_aug = jnp.concatenate([vs_ref[chunk], ones], axis=0)
            acc_ref[4 * slot + 2 * which] += _dot(v_aug, pr)

        def win(w, j):
            chunk, valid, delta = win_item(p, w, j)
            row = jnp.where(valid, m_row[2 * w + 1] + chunk_bias * float(delta), -NEG_BIG)
            pr = jnp.exp2(s_ref[slot * n_items + n_sel_items + w * (wch + 1) + j] - row).astype(BF16)
            v_aug = jnp.concatenate([vw_ref[chunk], ones], axis=0)
            acc_ref[4 * slot + 2 * w + 1] += _dot(v_aug, pr)

        for i in range(n_sel_items):
            items.append(functools.partial(sel, i))
        for w in range(2):
            for j in range(wch + 1):
                items.append(functools.partial(win, w, j))
        return items

    def finalize(p, slot):
        for w, tile in enumerate(tiles_of(p)):
            a_s = acc_ref[4 * slot + 2 * w]
            a_w = acc_ref[4 * slot + 2 * w + 1]
            o_slc = a_s[:hd, :] / a_s[hd:hd + 1, :]
            o_win = a_w[:hd, :] / a_w[hd:hd + 1, :]
            oc = ocmp_ref[tile]
            outs = []
            for h in range(HG):
                lanes = slice(h * tq, (h + 1) * tq)
                row0 = (pl.program_id(1) * HG + h) * 3
                g_cmp, g_slc, g_win = [_sigmoid(gt_ref[tile, pl.ds(row0 + br, 1), :]) for br in range(3)]
                y = (g_cmp * oc[h * hd:(h + 1) * hd, :] + g_slc * o_slc[:, lanes]
                     + g_win * o_win[:, lanes])
                outs.append(y.T)
            rows = pl.ds(pl.multiple_of(tile * tq, tq), tq)
            o_ref[rows, :] = jnp.concatenate(outs, axis=1).astype(o_ref.dtype)

    def step(k, slot, first=False, last=False):
        if not last:
            build(k + 1, 1 - slot)
        if not first:
            finalize(k - 1, 1 - slot)
        b_items = phase_b(k, slot)
        a_items = phase_a(k + 1, 1 - slot) if not last else [None] * n_items
        for a, b in zip(a_items, b_items):
            if a is not None:
                a()
            b()

    build(0, 0)
    for a in phase_a(0, 0):
        a()
    step(0, 0, first=True)

    def body(kk, carry):
        step(2 * kk + 1, 1)
        step(2 * kk + 2, 0)
        return carry

    lax.fori_loop(0, (n_pairs - 2) // 2, body, 0)
    step(n_pairs - 1, (n_pairs - 1) % 2, last=True)
    finalize(n_pairs - 1, (n_pairs - 1) % 2)


def _key_features(t):
    kp = np.arange(t)
    f = np.zeros((t, LANE), np.float32)
    f[:, 0:3] = (kp % ATT_TILE)[:, None]
    f[kp, AUG_SEL_ROW0 + kp // SLC_LEN] = 1.0
    return jnp.asarray(f, dtype=BF16)


def _sparse(qv, gt, proj3, ocmp, selb, slopes, t):
    b = qv.shape[0]
    nchunk = t // LANE
    ns = t // SLC_LEN
    rows = HG * ATT_HD
    ksw_unit0 = RNN_COLS // LANE + G_KV
    vs_blk0 = QT_ROWS // ATT_HD
    vw_blk0 = vs_blk0 + G_KV
    nl = HG * ATT_TILE
    n_items = (nchunk + 1) + 2 * (WINDOW // ATT_TILE + 1)
    return pl.pallas_call(
        _sparse_kernel,
        out_shape=jax.ShapeDtypeStruct((b, t, H_ATT * ATT_HD), BF16),
        grid=(b, G_KV),
        in_specs=[pl.BlockSpec((None, nchunk, rows, LANE), lambda i, g: (i, 0, g, 0)),
                  pl.BlockSpec((None, t, LANE), lambda i, g: (i, 0, ksw_unit0 + g)),
                  pl.BlockSpec((t, LANE), lambda i, g: (0, 0)),
                  pl.BlockSpec((None, nchunk, ATT_HD, LANE), lambda i, g: (i, 0, vs_blk0 + g, 0)),
                  pl.BlockSpec((None, nchunk, ATT_HD, LANE), lambda i, g: (i, 0, vw_blk0 + g, 0)),
                  pl.BlockSpec((None, nchunk, GATE_ROWS, LANE), lambda i, g: (i, 0, 0, 0)),
                  pl.BlockSpec((None, nchunk, rows, LANE), lambda i, g: (i, 0, g, 0)),
                  pl.BlockSpec((None, None, nchunk, ns, LANE), lambda i, g: (i, g, 0, 0, 0)),
                  pl.BlockSpec((None, HG, 1, 1), lambda i, g: (g, 0, 0, 0))],
        out_specs=pl.BlockSpec((None, t, rows), lambda i, g: (i, 0, g)),
        scratch_shapes=[pltpu.VMEM((2, ATT_TILE, nl), F32),
                        pltpu.VMEM((2 * 4, 2 * LANE, nl), BF16),
                        pltpu.VMEM((2 * n_items, ATT_TILE, nl), F32),
                        pltpu.VMEM((2 * 4, 8, nl), F32),
                        pltpu.VMEM((2 * 4, ATT_HD + ONES_ROWS, nl), F32)],
        compiler_params=_cparams(("arbitrary", "arbitrary")),
        name="sparse",
    )(qv, proj3, _key_features(t), qv, qv, gt, ocmp, selb, slopes)


def _merge_kernel(x_ref, yr_ref, ya_ref, mgr_ref, mga_ref, wr_ref, wa_ref, wo_ref, o_ref):
    pr = _dot(yr_ref[...], wr_ref[...])
    pa = _dot(ya_ref[...], wa_ref[...])
    merged = _sigmoid(mgr_ref[...].astype(F32)) * pr + _sigmoid(mga_ref[...].astype(F32)) * pa
    o_ref[...] = x_ref[...] + _dot(merged.astype(BF16), wo_ref[...])


def _merge(x2, yr, ya, proj, wr, wa, wo):
    m, d = x2.shape
    mg0 = (RNN_COLS + KV_COLS) // d
    tile = lambda col: pl.BlockSpec((TOK_TILE, d), lambda i: (i, col))
    wfull = pl.BlockSpec((d, d), lambda i: (0, 0))
    return pl.pallas_call(
        _merge_kernel,
        out_shape=jax.ShapeDtypeStruct((m, d), F32),
        grid=(m // TOK_TILE,),
        in_specs=[tile(0), tile(0), tile(0), tile(mg0), tile(mg0 + 1), wfull, wfull, wfull],
        out_specs=tile(0),
        compiler_params=_cparams(("arbitrary",)),
        name="merge",
    )(x2, yr, ya, proj, proj, wr, wa, wo)


def _mem_kv_kernel(mem_ref, g_ref, wkt_ref, wv_ref, kt_ref, v_ref):
    a = _rms(mem_ref[...], g_ref[...]).astype(BF16)
    kt_ref[...] = _dot_nt(wkt_ref[...], a).astype(kt_ref.dtype)
    v_ref[...] = _dot(a, wv_ref[...]).astype(v_ref.dtype)


def _mem_kv(mem, g, wkt, wkv):
    b, nm, d = mem.shape
    hw = H_X * X_HD
    return pl.pallas_call(
        _mem_kv_kernel,
        out_shape=(jax.ShapeDtypeStruct((b, hw, nm), BF16),
                   jax.ShapeDtypeStruct((b, nm, hw), BF16)),
        grid=(b,),
        in_specs=[pl.BlockSpec((None, nm, d), lambda i: (i, 0, 0)),
                  pl.BlockSpec((1, d), lambda i: (0, 0)),
                  pl.BlockSpec((hw, d), lambda i: (0, 0)),
                  pl.BlockSpec((d, hw), lambda i: (0, 1))],
        out_specs=(pl.BlockSpec((None, hw, nm), lambda i: (i, 0, 0)),
                   pl.BlockSpec((None, nm, hw), lambda i: (i, 0, 0))),
        compiler_params=_cparams(("arbitrary",)),
        name="mem_kv",
    )(mem, g, wkt, wkv)


def _xattn_kernel(h_ref, g_ref, wq_ref, kt_ref, v_ref, wo_ref, o_ref):
    h = h_ref[...]
    a = _rms(h, g_ref[...]).astype(BF16)
    q = (_dot(a, wq_ref[...]) * (X_HD ** -0.5)).astype(BF16)
    heads = [slice(hh * X_HD, (hh + 1) * X_HD) for hh in range(H_X)]
    scores = [_dot(q[:, cols], kt_ref[cols, :]) for cols in heads]
    probs = []
    for s in scores:
        m = jnp.max(s, axis=-1, keepdims=True)
        p = jnp.exp(s - m)
        probs.append((p / jnp.sum(p, axis=-1, keepdims=True)).astype(BF16))
    outs = [_dot(p, v_ref[:, cols]) for p, cols in zip(probs, heads)]
    o = jnp.concatenate(outs, axis=1).astype(BF16)
    o_ref[...] = h + _dot(o, wo_ref[...])


def _xattn(h3, g, wq, kt, v, wo):
    b, t, d = h3.shape
    hw = H_X * X_HD
    nm = v.shape[1]
    return pl.pallas_call(
        _xattn_kernel,
        out_shape=jax.ShapeDtypeStruct((b, t, d), F32),
        grid=(b, t // TOK_TILE),
        in_specs=[pl.BlockSpec((None, TOK_TILE, d), lambda i, j: (i, j, 0)),
                  pl.BlockSpec((1, d), lambda i, j: (0, 0)),
                  pl.BlockSpec((d, hw), lambda i, j: (0, 0)),
                  pl.BlockSpec((None, hw, nm), lambda i, j: (i, 0, 0)),
                  pl.BlockSpec((None, nm, hw), lambda i, j: (i, 0, 0)),
                  pl.BlockSpec((hw, d), lambda i, j: (0, 0))],
        out_specs=pl.BlockSpec((None, TOK_TILE, d), lambda i, j: (i, j, 0)),
        compiler_params=_cparams(("arbitrary", "arbitrary")),
        name="xattn",
    )(h3, g, wq, kt, v, wo)


def _ffn_kernel(h_ref, g_ref, wg_ref, wu_ref, wd_ref, gf_ref, o_ref, *, fc):
    h = h_ref[...]
    a = _rms(h, g_ref[...]).astype(BF16)
    acc = h
    for f in range(wg_ref.shape[1] // fc):
        cols = slice(f * fc, (f + 1) * fc)
        mid = _silu(_dot(a, wg_ref[:, cols])) * _dot(a, wu_ref[:, cols])
        acc = acc + _dot(mid.astype(BF16), wd_ref[cols, :])
    o_ref[...] = _rms(acc, gf_ref[...])


def _ffn(h2, g, wgu, wd, gf):
    m, d = h2.shape
    ff = wd.shape[0]
    const = lambda shape: pl.BlockSpec(shape, lambda i: (0, 0))
    return pl.pallas_call(
        functools.partial(_ffn_kernel, fc=ff // 2),
        out_shape=jax.ShapeDtypeStruct((m, d), F32),
        grid=(m // TOK_TILE,),
        in_specs=[pl.BlockSpec((TOK_TILE, d), lambda i: (i, 0)),
                  const((1, d)), const((d, ff)), pl.BlockSpec((d, ff), lambda i: (0, 1)),
                  const((ff, d)), const((1, d))],
        out_specs=pl.BlockSpec((TOK_TILE, d), lambda i: (i, 0)),
        compiler_params=_cparams(("arbitrary",)),
        name="ffn",
    )(h2, g, wgu, wgu, wd, gf)


def _layout_w_in(w):
    d = w.shape[0]
    kvw = G_KV * ATT_HD
    q_a = w[:, RNN_COLS:RNN_COLS + QT_ROWS]
    kv = w[:, RNN_COLS + QT_ROWS:RNN_COLS + QT_ROWS + 6 * kvw]
    gate = w[:, RNN_COLS + QT_ROWS + 6 * kvw:RNN_COLS + QT_ROWS + 6 * kvw + 3 * H_ATT]
    w_n = jnp.concatenate([w[:, :RNN_COLS], kv[:, :3 * kvw], kv[:, 4 * kvw:5 * kvw], w[:, -MG_COLS:]],
                          axis=1).astype(BF16)
    cols = QT_ROWS + VT_ROWS + 3 * H_ATT
    w_f = jnp.concatenate([q_a * (ATT_HD ** -0.5 * LOG2E), kv[:, 3 * kvw:4 * kvw], kv[:, 5 * kvw:6 * kvw], gate,
                           jnp.zeros((d, -cols % LANE), w.dtype)], axis=1).astype(BF16)
    return w_n, w_f


def _overlap_t(t):
    nc = (t - CMP_LEN) // CMP_STRIDE + 1
    ns = t // SLC_LEN
    starts = CMP_STRIDE * np.arange(N_CMP_PAD)
    s_start = SLC_LEN * np.arange(ns)
    ov = ((starts[None, :] + CMP_LEN > s_start[:, None]) & (starts[None, :] < s_start[:, None] + SLC_LEN)
          & (np.arange(N_CMP_PAD)[None, :] < nc))
    return jnp.asarray(ov.astype(np.float32), dtype=BF16)


def kernel(x, mem, g_mix, w_in, lower_bounds, g_rnn_out, pe_ck, w_ck1, w_ck2, pe_cv, w_cv1, w_cv2,
           w_proj_rnn, w_proj_att, w_out, g_xattn, g_mem, w_xq, w_xkv, w_xo, g_ffn, w_gate_up,
           w_down, g_final):
    b, t, d = x.shape
    depth = g_mix.shape[0]
    assert depth == 1, "the final RMSNorm is fused into the layer's FFN kernel"
    lbs = jnp.cumsum(jax.nn.softmax(lower_bounds.astype(F32), axis=0), axis=0)
    slopes = (2.0 ** (-8.0 * jnp.arange(1, H_ATT + 1, dtype=F32) / H_ATT)).reshape(G_KV, HG, 1, 1)
    overlap_t = _overlap_t(t)
    h = x
    for l in range(depth):
        w_n, w_t = _layout_w_in(w_in[l])
        x2 = h.reshape(b * t, d)
        proj = _in_proj(x2, g_mix[l][None, :], w_n)
        proj3 = proj.reshape(b, t, N_COLS)
        qv, gt = _in_proj_t(h, g_mix[l][None, :], w_t)
        y_r = _hgrn(proj3, lbs[l][None, :], g_rnn_out[l][None, :])
        kc, vct = _compress(proj3, pe_ck[l], pe_cv[l], w_ck1[l].astype(BF16), w_cv1[l].astype(BF16),
                            w_ck2[l].astype(BF16), w_cv2[l].T.astype(BF16))
        ocmp, selb = _cmp_select(qv, kc, vct, slopes, overlap_t, t)
        y_a = _sparse(qv, gt, proj3, ocmp, selb, slopes, t)
        h1 = _merge(x2, y_r.reshape(b * t, d), y_a.reshape(b * t, d), proj,
                    w_proj_rnn[l].astype(BF16), w_proj_att[l].astype(BF16), w_out[l].astype(BF16))
        w_kv = w_xkv[l].astype(BF16)
        kt, v = _mem_kv(mem, g_mem[l][None, :], w_kv[:, :H_X * X_HD].T, w_kv)
        h2 = _xattn(h1.reshape(b, t, d), g_xattn[l][None, :], w_xq[l].astype(BF16), kt, v,
                    w_xo[l].astype(BF16))
        h = _ffn(h2.reshape(b * t, d), g_ffn[l][None, :], w_gate_up[l].astype(BF16),
                 w_down[l].astype(BF16), g_final[None, :]).reshape(b, t, d)
    return h
```

```python
import functools

import jax
import jax.numpy as jnp
import numpy as np
from jax import lax
from jax.experimental import pallas as pl
from jax.experimental.pallas import tpu as pltpu

F32 = jnp.float32
BF16 = jnp.bfloat16

D_MODEL = 1024
N_MEM = 256
H_RNN = 8
RNN_DIM = 128
RNN_CHUNK = 64
H_ATT = 16
ATT_HD = 64
G_KV = 4
HG = H_ATT // G_KV
CMP_LEN = 32
CMP_STRIDE = 16
CMP_HIDDEN = 128
SLC_LEN = 64
SLC_TOPK = 8
WINDOW = 512
FORCE_BONUS = 1.0e4
H_X = 4
X_HD = 128
D_FF = 2816
EPS = 1e-6

LANE = 128
VMEM_LIMIT = 56 * 1024 * 1024
TOK_TILE = 512
ATT_TILE = 128
CMP_TILE = 512
N_CMP_PAD = 128
HGRN_GROUP = 4
ONES_ROWS = 16
NEG_BIG = -1.0e30
LOG2E = 1.4426950408889634
AUG_SEL_ROW0 = 8

QT_ROWS = H_ATT * ATT_HD
VT_ROWS = 2 * G_KV * ATT_HD
GATE_ROWS = 64
RNN_COLS = 4 * H_RNN * RNN_DIM
KV_COLS = 4 * G_KV * ATT_HD
MG_COLS = 2 * D_MODEL
N_COLS = RNN_COLS + KV_COLS + MG_COLS


def _cparams(sem):
    return pltpu.CompilerParams(dimension_semantics=sem, vmem_limit_bytes=VMEM_LIMIT)


def _rms(xf, g):
    return xf * lax.rsqrt(jnp.mean(xf * xf, axis=-1, keepdims=True) + EPS) * g


def _sigmoid(x):
    return 1.0 / (1.0 + jnp.exp(-x))


def _silu(x):
    return x * _sigmoid(x)


def _dot(a, b):
    return jnp.dot(a, b, preferred_element_type=F32)


def _dot_nt(a, b):
    return lax.dot_general(a, b, (((1,), (1,)), ((), ())), preferred_element_type=F32)


def _dot_tn(a, b):
    return lax.dot_general(a, b, (((0,), (0,)), ((), ())), preferred_element_type=F32)


def _interleave_groups(w):
    half = G_KV * ATT_HD
    parts = []
    for g in range(G_KV):
        parts += [w[g * ATT_HD:(g + 1) * ATT_HD], w[half + g * ATT_HD:half + (g + 1) * ATT_HD]]
    return jnp.concatenate(parts, axis=0)


def _in_proj_kernel(x_ref, g_ref, wt_ref, o_ref, w_ref, *, sub, kv_step, kv_subs):
    @pl.when(pl.program_id(1) == 0)
    def _():
        for n in range(wt_ref.shape[0] // sub):
            rows = slice(n * sub, (n + 1) * sub)
            w = wt_ref[rows, :].astype(F32)
            if n in kv_subs:
                w = jnp.where(pl.program_id(0) == kv_step, _interleave_groups(w), w)
            w_ref[:, rows] = w.T.astype(BF16)

    a = _rms(x_ref[...], g_ref[...]).astype(BF16)
    for n in range(o_ref.shape[1] // sub):
        cols = slice(n * sub, (n + 1) * sub)
        o_ref[:, cols] = _dot(a, w_ref[:, cols]).astype(o_ref.dtype)


def _in_proj(x2, g, wt):
    m, d = x2.shape
    n = wt.shape[0]
    tn = n // 2
    sub = 2 * G_KV * ATT_HD
    kv_subs = tuple((RNN_COLS % tn) // sub + s for s in range(KV_COLS // sub))
    return pl.pallas_call(
        functools.partial(_in_proj_kernel, sub=sub, kv_step=RNN_COLS // tn, kv_subs=kv_subs),
        out_shape=jax.ShapeDtypeStruct((m, n), BF16),
        grid=(2, m // TOK_TILE),
        in_specs=[pl.BlockSpec((TOK_TILE, d), lambda j, i: (i, 0)),
                  pl.BlockSpec((1, d), lambda j, i: (0, 0)),
                  pl.BlockSpec((tn, d), lambda j, i: (j, 0))],
        out_specs=pl.BlockSpec((TOK_TILE, tn), lambda j, i: (i, j)),
        scratch_shapes=[pltpu.VMEM((d, tn), BF16)],
        compiler_params=_cparams(("arbitrary", "arbitrary")),
        name="in_proj",
    )(x2, g, wt)


def _in_proj_t_kernel(x_ref, g_ref, wt_ref, qv_ref, gt_ref):
    a = _rms(x_ref[...], g_ref[...]).astype(BF16)
    r = _dot_nt(wt_ref[...], a)
    nqv = qv_ref.shape[1]
    ngt = gt_ref.shape[1]
    for c in range(qv_ref.shape[0]):
        qv_ref[c] = r[:nqv, c * LANE:(c + 1) * LANE].astype(qv_ref.dtype)
        gt_ref[c] = r[nqv:nqv + ngt, c * LANE:(c + 1) * LANE]


def _in_proj_t(x, g, wt):
    b, t, d = x.shape
    rows = wt.shape[0]
    nqv = QT_ROWS + VT_ROWS
    ngt = GATE_ROWS
    nc = TOK_TILE // LANE
    return pl.pallas_call(
        _in_proj_t_kernel,
        out_shape=(jax.ShapeDtypeStruct((b, t // LANE, nqv, LANE), BF16),
                   jax.ShapeDtypeStruct((b, t // LANE, ngt, LANE), F32)),
        grid=(b, t // TOK_TILE),
        in_specs=[pl.BlockSpec((None, TOK_TILE, d), lambda i, j: (i, j, 0)),
                  pl.BlockSpec((1, d), lambda i, j: (0, 0)),
                  pl.BlockSpec((rows, d), lambda i, j: (0, 0))],
        out_specs=(pl.BlockSpec((None, nc, nqv, LANE), lambda i, j: (i, j, 0, 0)),
                   pl.BlockSpec((None, nc, ngt, LANE), lambda i, j: (i, j, 0, 0))),
        compiler_params=_cparams(("arbitrary", "arbitrary")),
        name="in_proj_t",
    )(x, g, wt)


def _hgrn_kernel(q_ref, f_ref, i_ref, og_ref, lb_ref, gn_ref, o_ref, qd_ref, oi_ref, ut_ref, dec_ref):
    c = RNN_CHUNK
    kd = RNN_DIM
    n_chunks = q_ref.shape[0] // c
    lb = lb_ref[...]
    gn = gn_ref[...]
    blk = HGRN_GROUP * c
    row = lax.broadcasted_iota(jnp.int32, (blk, blk), 0)
    col = lax.broadcasted_iota(jnp.int32, (blk, blk), 1)
    same_chunk = (row // c) == (col // c)
    causal = same_chunk & (row >= col)
    tril = causal.astype(BF16)

    q = q_ref[...].astype(F32)
    fl = f_ref[...].astype(F32)
    v = i_ref[...]
    f = lb + (1.0 - lb) * _sigmoid(fl)
    k = 1.0 - f
    logf = jnp.log(f)
    hi = logf.astype(BF16)
    lo = (logf - hi.astype(F32)).astype(BF16)
    pieces = jnp.concatenate([hi, lo], axis=1)
    n_blk = q_ref.shape[0] // blk
    cs = [_dot(tril, pieces[g * blk:(g + 1) * blk]) for g in range(n_blk)]
    bcum = jnp.concatenate([x[:, 0:kd] + x[:, kd:2 * kd] for x in cs], axis=0)
    e_neg = jnp.exp(-bcum)
    dec = jnp.exp(jnp.concatenate([bcum[n * c + c - 1:n * c + c, :] for n in range(n_chunks)], axis=0))
    dec_rows = jnp.concatenate([jnp.broadcast_to(dec[n:n + 1, :], (c, kd)) for n in range(n_chunks)], axis=0)
    q_dec = (_silu(q) * (1.0 / e_neg)).astype(BF16)
    k_neg = k * e_neg
    k_dec = k_neg.astype(BF16)
    k_end = (k_neg * dec_rows).astype(BF16)
    qd_ref[...] = q_dec
    dec_ref[...] = dec
    for g in range(n_blk):
        rows = slice(g * blk, (g + 1) * blk)
        a = jnp.where(causal, _dot_nt(q_dec[rows], k_dec[rows]), 0.0)
        oi_ref[rows, :] = _dot(a.astype(BF16), v[rows])
    for n in range(n_chunks):
        rows = slice(n * c, (n + 1) * c)
        ut_ref[n] = _dot_tn(v[rows], k_end[rows])

    s_t = jnp.zeros((kd, kd), F32)
    for n in range(n_chunks):
        rows = slice(n * c, (n + 1) * c)
        o = oi_ref[rows, :] + _dot_nt(qd_ref[rows, :], s_t.astype(BF16))
        og = og_ref[rows, :].astype(F32)
        o_ref[rows, :] = (_rms(o, gn) * _silu(og)).astype(o_ref.dtype)
        s_t = s_t * dec_ref[n:n + 1, :] + ut_ref[n]


def _hgrn(proj3, lb, gn):
    b, t, _ = proj3.shape
    return pl.pallas_call(
        _hgrn_kernel,
        out_shape=jax.ShapeDtypeStruct((b, t, H_RNN * RNN_DIM), BF16),
        grid=(b, H_RNN),
        in_specs=[pl.BlockSpec((None, t, RNN_DIM), lambda i, h: (i, 0, h)),
                  pl.BlockSpec((None, t, RNN_DIM), lambda i, h: (i, 0, H_RNN + h)),
                  pl.BlockSpec((None, t, RNN_DIM), lambda i, h: (i, 0, 2 * H_RNN + h)),
                  pl.BlockSpec((None, t, RNN_DIM), lambda i, h: (i, 0, 3 * H_RNN + h)),
                  pl.BlockSpec((1, RNN_DIM), lambda i, h: (0, h)),
                  pl.BlockSpec((1, RNN_DIM), lambda i, h: (0, 0))],
        out_specs=pl.BlockSpec((None, t, RNN_DIM), lambda i, h: (i, 0, h)),
        scratch_shapes=[pltpu.VMEM((t, RNN_DIM), BF16),
                        pltpu.VMEM((t, RNN_DIM), F32),
                        pltpu.VMEM((t // RNN_CHUNK, RNN_DIM, RNN_DIM), F32),
                        pltpu.VMEM((t // RNN_CHUNK, RNN_DIM), F32)],
        compiler_params=_cparams(("arbitrary", "arbitrary")),
        name="hgrn",
    )(proj3, proj3, proj3, proj3, lb, gn)


def _compress_kernel(kv_ref, pek_ref, pev_ref, wk1_ref, wv1_ref, wk2_ref, wv2t_ref,
                     kc_ref, vct_ref, xs_ref, xk_ref, xv_ref):
    t = kv_ref.shape[0]
    hd = ATT_HD
    xs_ref[0:t, :] = kv_ref[...].astype(F32)
    xs_ref[t:, :] = jnp.zeros((xs_ref.shape[0] - t, xs_ref.shape[1]), F32)
    for l in range(CMP_LEN):
        blk = xs_ref[pl.ds(l, N_CMP_PAD, stride=CMP_STRIDE), :]
        xk_ref[:, l * hd:(l + 1) * hd] = (blk[:, 0:hd] + pek_ref[l:l + 1, :]).astype(BF16)
        xv_ref[:, l * hd:(l + 1) * hd] = (blk[:, hd:2 * hd] + pev_ref[l:l + 1, :]).astype(BF16)
    hk = _silu(_dot(xk_ref[...], wk1_ref[...])).astype(BF16)
    hv = _silu(_dot(xv_ref[...], wv1_ref[...])).astype(BF16)
    kc_ref[...] = _dot(hk, wk2_ref[...])
    vct_ref[...] = _dot_nt(wv2t_ref[...], hv)


def _compress(proj3, pek, pev, wk1, wv1, wk2, wv2t):
    b, t, _ = proj3.shape
    kv_unit0 = RNN_COLS // LANE
    flat = CMP_LEN * ATT_HD
    full = lambda shape: pl.BlockSpec(shape, lambda i, g: (0,) * len(shape))
    return pl.pallas_call(
        _compress_kernel,
        out_shape=(jax.ShapeDtypeStruct((b, G_KV, N_CMP_PAD, ATT_HD), F32),
                   jax.ShapeDtypeStruct((b, G_KV, ATT_HD, N_CMP_PAD), F32)),
        grid=(b, G_KV),
        in_specs=[pl.BlockSpec((None, t, LANE), lambda i, g: (i, 0, kv_unit0 + g)),
                  full((CMP_LEN, ATT_HD)), full((CMP_LEN, ATT_HD)),
                  full((flat, CMP_HIDDEN)), full((flat, CMP_HIDDEN)),
                  full((CMP_HIDDEN, ATT_HD)), full((ATT_HD, CMP_HIDDEN))],
        out_specs=(pl.BlockSpec((None, None, N_CMP_PAD, ATT_HD), lambda i, g: (i, g, 0, 0)),
                   pl.BlockSpec((None, None, ATT_HD, N_CMP_PAD), lambda i, g: (i, g, 0, 0))),
        scratch_shapes=[pltpu.VMEM((t + CMP_STRIDE, LANE), F32),
                        pltpu.VMEM((N_CMP_PAD, flat), BF16),
                        pltpu.VMEM((N_CMP_PAD, flat), BF16)],
        compiler_params=_cparams(("arbitrary", "arbitrary")),
        name="compress",
    )(proj3, pek, pev, wk1, wv1, wk2, wv2t)


def _cmp_select_kernel(qt_ref, kc_ref, vct_ref, slope_ref, ov_ref, ocmp_ref, selb_ref, *, n_cmp, n_sel):
    nct = CMP_TILE // LANE
    tt = CMP_TILE
    ns = ov_ref.shape[0]
    kc = kc_ref[...].astype(BF16)
    vct = vct_ref[...].astype(BF16)
    ov = ov_ref[...]
    j_i = lax.broadcasted_iota(jnp.int32, (8, tt), 0)
    tq_rel = lax.broadcasted_iota(jnp.int32, (8, tt), 1)
    n_tiles = qt_ref.shape[0] // nct

    def ranges(tile):
        t0 = tile * tt
        n_any = min(N_CMP_PAD, (t0 + tt) // CMP_STRIDE)
        n_all = max(0, (t0 - (CMP_LEN - 1)) // CMP_STRIDE + 1) // 8 * 8
        return t0, n_any, n_all, (t0 + tt) // SLC_LEN

    def pad_rows(x):
        if x.shape[0] == N_CMP_PAD:
            return x
        return jnp.concatenate([x, jnp.zeros((N_CMP_PAD - x.shape[0], x.shape[1]), x.dtype)], axis=0)

    scores = []
    for tile in range(n_tiles):
        _, n_any, _, _ = ranges(tile)
        qt = [jnp.concatenate([qt_ref[tile * nct + c, h * ATT_HD:(h + 1) * ATT_HD, :] for c in range(nct)],
                              axis=1) for h in range(HG)]
        scores.append(_dot(kc[:n_any], jnp.concatenate(qt, axis=1)))

    probs, psums = [], []
    for tile in range(n_tiles):
        t0, n_any, n_all, _ = ranges(tile)
        n_b = lax.broadcasted_iota(jnp.int32, (n_any - n_all, tt), 0) + n_all
        t_b = lax.broadcasted_iota(jnp.int32, (n_any - n_all, tt), 1)
        visible = (CMP_STRIDE * n_b + (CMP_LEN - 1) - t0 <= t_b) & (n_b < n_cmp)
        centre = (lax.broadcasted_iota(jnp.int32, (n_any, tt), 0).astype(F32) * float(CMP_STRIDE)
                  + (CMP_LEN - 1) / 2.0)
        psum = jnp.zeros((n_any, tt), F32)
        ph = []
        for h in range(HG):
            s = scores[tile][:, h * tt:(h + 1) * tt] + slope_ref[h] * LOG2E * centre
            band = jnp.where(visible, s[n_all:], -jnp.inf)
            s = band if n_all == 0 else jnp.concatenate([s[:n_all], band], axis=0)
            m = jnp.max(s, axis=0, keepdims=True)
            m = jnp.where(m == -jnp.inf, 0.0, m)
            p = jnp.exp2(s - m)
            d = jnp.sum(p, axis=0, keepdims=True)
            p = p * (1.0 / jnp.where(d > 0, d, 1.0))
            psum = psum + p
            ph.append(pad_rows(p).astype(BF16))
        probs.append(jnp.concatenate(ph, axis=1))
        psums.append(pad_rows(psum))

    for tile in range(n_tiles):
        o = _dot(vct, probs[tile])
        for h in range(HG):
            for c in range(nct):
                lanes = slice(h * tt + c * LANE, h * tt + (c + 1) * LANE)
                ocmp_ref[tile * nct + c, h * ATT_HD:(h + 1) * ATT_HD, :] = o[:, lanes]

    imps = []
    for tile in range(n_tiles):
        psum = psums[tile]
        hi = psum.astype(BF16)
        rem = psum - hi.astype(F32)
        mid = rem.astype(BF16)
        lo = (rem - mid.astype(F32)).astype(BF16)
        imp3 = _dot(ov, jnp.concatenate([hi, mid, lo], axis=1))
        imps.append(imp3[:, 0:tt] + imp3[:, tt:2 * tt] + imp3[:, 2 * tt:3 * tt])

    for tile in range(n_tiles):
        t0, _, _, nvb = ranges(tile)
        imp = imps[tile]
        score = []
        for rg in range(nvb // 8):
            j = j_i + 8 * rg
            tq = tq_rel + t0
            cur = tq // SLC_LEN
            forced = (j == 0) | (j == cur) | (j == cur - 1)
            sc = imp[8 * rg:8 * rg + 8, :] + jnp.where(forced, FORCE_BONUS, 0.0)
            score.append(jnp.where(j * SLC_LEN <= tq, sc, -jnp.inf))
        rank = [jnp.zeros((8, tt), F32) for _ in score]
        for jp in range(nvb):
            sj = jnp.broadcast_to(score[jp // 8][jp % 8:jp % 8 + 1, :], (8, tt))
            for rg in range(nvb // 8):
                ge = jnp.where(sj >= score[rg], 1.0, 0.0)
                gt = jnp.where(sj > score[rg], 1.0, 0.0)
                if 8 * rg > jp:
                    ahead = ge
                elif 8 * rg + 7 <= jp:
                    ahead = gt
                else:
                    ahead = jnp.where(j_i + 8 * rg > jp, ge, gt)
                rank[rg] = rank[rg] + ahead
        for rg in range(ns // 8):
            for c in range(nct):
                if rg < nvb // 8:
                    selb = jnp.where(rank[rg][:, c * LANE:(c + 1) * LANE] < float(n_sel), 0.0, NEG_BIG)
                else:
                    selb = jnp.full((8, LANE), NEG_BIG, F32)
                selb_ref[tile * nct + c, 8 * rg:8 * rg + 8, :] = selb


def _cmp_select(qv, kc, vct, slopes, overlap_t, t):
    b = qv.shape[0]
    ns = t // SLC_LEN
    nch = t // LANE
    n_cmp = (t - CMP_LEN) // CMP_STRIDE + 1
    n_sel = min(SLC_TOPK, ns)
    rows = HG * ATT_HD
    return pl.pallas_call(
        functools.partial(_cmp_select_kernel, n_cmp=n_cmp, n_sel=n_sel),
        out_shape=(jax.ShapeDtypeStruct((b, nch, H_ATT * ATT_HD, LANE), F32),
                   jax.ShapeDtypeStruct((b, G_KV, nch, ns, LANE), F32)),
        grid=(b, G_KV),
        in_specs=[pl.BlockSpec((None, nch, rows, LANE), lambda i, g: (i, 0, g, 0)),
                  pl.BlockSpec((None, None, N_CMP_PAD, ATT_HD), lambda i, g: (i, g, 0, 0)),
                  pl.BlockSpec((None, None, ATT_HD, N_CMP_PAD), lambda i, g: (i, g, 0, 0)),
                  pl.BlockSpec((None, HG, 1, 1), lambda i, g: (g, 0, 0, 0)),
                  pl.BlockSpec((ns, N_CMP_PAD), lambda i, g: (0, 0))],
        out_specs=(pl.BlockSpec((None, nch, rows, LANE), lambda i, g: (i, 0, g, 0)),
                   pl.BlockSpec((None, None, nch, ns, LANE), lambda i, g: (i, g, 0, 0, 0))),
        compiler_params=_cparams(("arbitrary", "arbitrary")),
        name="cmp_select",
    )(qv, kc, vct, slopes, overlap_t)


def _sparse_kernel(qt_ref, k_ref, ka_ref, vs_ref, vw_ref, gt_ref, ocmp_ref, selb_ref, slope_ref, o_ref,
                   msk_ref, qop_ref, s_ref, m_ref, acc_ref):
    tq = ATT_TILE
    kc = ATT_TILE
    hd = ATT_HD
    nl = HG * tq
    n_tiles = qt_ref.shape[0]
    ns = selb_ref.shape[1]
    wch = WINDOW // kc
    n_sel_items = n_tiles + 1
    slope = jnp.concatenate([jnp.broadcast_to(slope_ref[h], (1, tq)) for h in range(HG)], axis=1) * LOG2E
    s_hi = slope.astype(BF16).astype(F32)
    s_mid = (slope - s_hi).astype(BF16).astype(F32)
    s_lo = (slope - s_hi - s_mid).astype(BF16).astype(F32)
    slope_rows = jnp.concatenate([s_hi, s_mid, s_lo, jnp.zeros((AUG_SEL_ROW0 - 3, nl), F32)], axis=0)
    aug_tail = jnp.zeros((LANE - AUG_SEL_ROW0 - ns, nl), F32)
    aug_win = jnp.concatenate([slope_rows, jnp.zeros((ns, nl), F32), aug_tail], axis=0).astype(BF16)
    k_i = lax.broadcasted_iota(jnp.int32, (kc, nl), 0)
    t_i = lax.broadcasted_iota(jnp.int32, (kc, nl), 1) % tq
    d0 = t_i - k_i
    msk_ref[0] = jnp.where(d0 >= 0, 0.0, NEG_BIG)
    msk_ref[1] = jnp.where(d0 < 0, 0.0, NEG_BIG)
    chunk_bias = slope * float(kc)
    ones = jnp.ones((ONES_ROWS, kc), BF16)

    n_pairs = n_tiles // 2
    n_items = n_sel_items + 2 * (wch + 1)

    def tiles_of(p):
        return (p, n_tiles - 1 - p)

    def sel_item(p, i):
        lo, hi = tiles_of(p)
        if i == 0:
            return 0, lo, 0
        if i == n_sel_items - 1:
            return 1, hi, 0
        which = jnp.where(i > lo, 1, 0)
        tile = jnp.where(i > lo, hi, lo)
        chunk = jnp.where(i > lo, i - lo - 1, i - 1)
        return which, chunk, tile - chunk

    def win_item(p, w, j):
        chunk = tiles_of(p)[w] - wch + j
        return jnp.maximum(chunk, 0), chunk >= 0, wch - j

    def col_max(r, delta_f):
        r8 = jnp.max(r.reshape(kc // 8, 8, nl), axis=0)
        return r8 - chunk_bias * delta_f

    def scores(chunk, op):
        rows = pl.ds(pl.multiple_of(chunk * kc, kc), kc)
        keys = jnp.concatenate([k_ref[rows, :], ka_ref[rows, :]], axis=1)
        return _dot(keys, qop_ref[op])

    def build(p, slot):
        for w, tile in enumerate(tiles_of(p)):
            qt = qt_ref[tile]
            q_all = jnp.concatenate([qt[h * hd:(h + 1) * hd, :] for h in range(HG)], axis=1)
            zero = jnp.zeros_like(q_all)
            selb = jnp.concatenate([selb_ref[tile]] * HG, axis=1)
            aug_sel = jnp.concatenate([slope_rows, selb, aug_tail], axis=0).astype(BF16)
            qop_ref[4 * slot + 2 * w] = jnp.concatenate([q_all, zero, aug_sel], axis=0)
            qop_ref[4 * slot + 2 * w + 1] = jnp.concatenate([zero, q_all, aug_win], axis=0)
        for x in range(4):
            m_ref[4 * slot + x] = jnp.full((8, nl), NEG_BIG, F32)

    def phase_a(p, slot):
        items = []

        def sel(i):
            which, chunk, delta = sel_item(p, i)
            r = scores(chunk, 4 * slot + 2 * which)
            if i in (0, n_sel_items - 1):
                r = r + msk_ref[0]
            s_ref[slot * n_items + i] = r
            x = 4 * slot + 2 * which
            m_ref[x] = jnp.maximum(m_ref[x], col_max(r, jnp.float32(delta)))

        def win(w, j):
            chunk, valid, delta = win_item(p, w, j)
            r = scores(chunk, 4 * slot + 2 * w + 1)
            if j == 0:
                r = r + msk_ref[1]
            elif j == wch:
                r = r + msk_ref[0]
            s_ref[slot * n_items + n_sel_items + w * (wch + 1) + j] = r
            cm = jnp.where(valid, col_max(r, float(delta)), NEG_BIG)
            x = 4 * slot + 2 * w + 1
            m_ref[x] = jnp.maximum(m_ref[x], cm)

        for i in range(n_sel_items):
            items.append(functools.partial(sel, i))
        for w in range(2):
            for j in range(wch + 1):
                items.append(functools.partial(win, w, j))
        return items

    def phase_b(p, slot):
        m_row = [jnp.max(m_ref[4 * slot + x], axis=0, keepdims=True) for x in range(4)]
        for x in range(4):
            acc_ref[4 * slot + x] = jnp.zeros(acc_ref.shape[1:], F32)
        items = []

        def sel(i):
            which, chunk, delta = sel_item(p, i)
            row = jnp.where(which == 1, m_row[2], m_row[0]) + chunk_bias * jnp.float32(delta)
            pr = jnp.exp2(s_ref[slot * n_items + i] - row).astype(BF16)
            v_aug = jnp.concatenate([vs_ref[chunk], ones], axis=0)
            acc_ref[4 * slot + 2 * which] += _dot(v_aug, pr)

        def win(w, j):
            chunk, valid, delta = win_item(p, w, j)
            row = jnp.where(valid, m_row[2 * w + 1] + chunk_bias * float(delta), -NEG_BIG)
            pr = jnp.exp2(s_ref[slot * n_items + n_sel_items + w * (wch + 1) + j] - row).astype(BF16)
            v_aug = jnp.concatenate([vw_ref[chunk], ones], axis=0)
            acc_ref[4 * slot + 2 * w + 1] += _dot(v_aug, pr)

        for i in range(n_sel_items):
            items.append(functools.partial(sel, i))
        for w in range(2):
            for j in range(wch + 1):
                items.append(functools.partial(win, w, j))
        return items

    def finalize(p, slot):
        for w, tile in enumerate(tiles_of(p)):
            a_s = acc_ref[4 * slot + 2 * w]
            a_w = acc_ref[4 * slot + 2 * w + 1]
            o_slc = a_s[:hd, :] / a_s[hd:hd + 1, :]
            o_win = a_w[:hd, :] / a_w[hd:hd + 1, :]
            oc = ocmp_ref[tile]
            outs = []
            for h in range(HG):
                lanes = slice(h * tq, (h + 1) * tq)
                row0 = (pl.program_id(1) * HG + h) * 3
                g_cmp, g_slc, g_win = [_sigmoid(gt_ref[tile, pl.ds(row0 + br, 1), :]) for br in range(3)]
                y = (g_cmp * oc[h * hd:(h + 1) * hd, :] + g_slc * o_slc[:, lanes]
                     + g_win * o_win[:, lanes])
                outs.append(y.T)
            rows = pl.ds(pl.multiple_of(tile * tq, tq), tq)
            o_ref[rows, :] = jnp.concatenate(outs, axis=1).astype(o_ref.dtype)

    def step(k, slot, first=False, last=False):
        if not last:
            build(k + 1, 1 - slot)
        if not first:
            finalize(k - 1, 1 - slot)
        b_items = phase_b(k, slot)
        a_items = phase_a(k + 1, 1 - slot) if not last else [None] * n_items
        for a, b in zip(a_items, b_items):
            if a is not None:
                a()
            b()

    build(0, 0)
    for a in phase_a(0, 0):
        a()
    step(0, 0, first=True)

    def body(kk, carry):
        step(2 * kk + 1, 1)
        step(2 * kk + 2, 0)
        return carry

    lax.fori_loop(0, (n_pairs - 2) // 2, body, 0)
    step(n_pairs - 1, (n_pairs - 1) % 2, last=True)
    finalize(n_pairs - 1, (n_pairs - 1) % 2)


def _key_features(t):
    kp = np.arange(t)
    f = np.zeros((t, LANE), np.float32)
    f[:, 0:3] = (kp % ATT_TILE)[:, None]
    f[kp, AUG_SEL_ROW0 + kp // SLC_LEN] = 1.0
    return jnp.asarray(f, dtype=BF16)


def _sparse(qv, gt, proj3, ocmp, selb, slopes, t):
    b = qv.shape[0]
    nchunk = t // LANE
    ns = t // SLC_LEN
    rows = HG * ATT_HD
    ksw_unit0 = RNN_COLS // LANE + G_KV
    vs_blk0 = QT_ROWS // ATT_HD
    vw_blk0 = vs_blk0 + G_KV
    nl = HG * ATT_TILE
    n_items = (nchunk + 1) + 2 * (WINDOW // ATT_TILE + 1)
    return pl.pallas_call(
        _sparse_kernel,
        out_shape=jax.ShapeDtypeStruct((b, t, H_ATT * ATT_HD), BF16),
        grid=(b, G_KV),
        in_specs=[pl.BlockSpec((None, nchunk, rows, LANE), lambda i, g: (i, 0, g, 0)),
                  pl.BlockSpec((None, t, LANE), lambda i, g: (i, 0, ksw_unit0 + g)),
                  pl.BlockSpec((t, LANE), lambda i, g: (0, 0)),
                  pl.BlockSpec((None, nchunk, ATT_HD, LANE), lambda i, g: (i, 0, vs_blk0 + g, 0)),
                  pl.BlockSpec((None, nchunk, ATT_HD, LANE), lambda i, g: (i, 0, vw_blk0 + g, 0)),
                  pl.BlockSpec((None, nchunk, GATE_ROWS, LANE), lambda i, g: (i, 0, 0, 0)),
                  pl.BlockSpec((None, nchunk, rows, LANE), lambda i, g: (i, 0, g, 0)),
                  pl.BlockSpec((None, None, nchunk, ns, LANE), lambda i, g: (i, g, 0, 0, 0)),
                  pl.BlockSpec((None, HG, 1, 1), lambda i, g: (g, 0, 0, 0))],
        out_specs=pl.BlockSpec((None, t, rows), lambda i, g: (i, 0, g)),
        scratch_shapes=[pltpu.VMEM((2, ATT_TILE, nl), F32),
                        pltpu.VMEM((2 * 4, 2 * LANE, nl), BF16),
                        pltpu.VMEM((2 * n_items, ATT_TILE, nl), F32),
                        pltpu.VMEM((2 * 4, 8, nl), F32),
                        pltpu.VMEM((2 * 4, ATT_HD + ONES_ROWS, nl), F32)],
        compiler_params=_cparams(("arbitrary", "arbitrary")),
        name="sparse",
    )(qv, proj3, _key_features(t), qv, qv, gt, ocmp, selb, slopes)


def _merge_kernel(x_ref, yr_ref, ya_ref, mgr_ref, mga_ref, wr_ref, wa_ref, wo_ref, o_ref):
    pr = _dot(yr_ref[...], wr_ref[...])
    pa = _dot(ya_ref[...], wa_ref[...])
    merged = _sigmoid(mgr_ref[...].astype(F32)) * pr + _sigmoid(mga_ref[...].astype(F32)) * pa
    o_ref[...] = x_ref[...] + _dot(merged.astype(BF16), wo_ref[...])


def _merge(x2, yr, ya, proj, wr, wa, wo):
    m, d = x2.shape
    mg0 = (RNN_COLS + KV_COLS) // d
    tile = lambda col: pl.BlockSpec((TOK_TILE, d), lambda i: (i, col))
    wfull = pl.BlockSpec((d, d), lambda i: (0, 0))
    return pl.pallas_call(
        _merge_kernel,
        out_shape=jax.ShapeDtypeStruct((m, d), F32),
        grid=(m // TOK_TILE,),
        in_specs=[tile(0), tile(0), tile(0), tile(mg0), tile(mg0 + 1), wfull, wfull, wfull],
        out_specs=tile(0),
        compiler_params=_cparams(("arbitrary",)),
        name="merge",
    )(x2, yr, ya, proj, proj, wr, wa, wo)


def _mem_kv_kernel(mem_ref, g_ref, wkt_ref, wv_ref, kt_ref, v_ref):
    a = _rms(mem_ref[...], g_ref[...]).astype(BF16)
    kt_ref[...] = _dot_nt(wkt_ref[...], a).astype(kt_ref.dtype)
    v_ref[...] = _dot(a, wv_ref[...]).astype(v_ref.dtype)


def _mem_kv(mem, g, wkt, wkv):
    b, nm, d = mem.shape
    hw = H_X * X_HD
    return pl.pallas_call(
        _mem_kv_kernel,
        out_shape=(jax.ShapeDtypeStruct((b, hw, nm), BF16),
                   jax.ShapeDtypeStruct((b, nm, hw), BF16)),
        grid=(b,),
        in_specs=[pl.BlockSpec((None, nm, d), lambda i: (i, 0, 0)),
                  pl.BlockSpec((1, d), lambda i: (0, 0)),
                  pl.BlockSpec((hw, d), lambda i: (0, 0)),
                  pl.BlockSpec((d, hw), lambda i: (0, 1))],
        out_specs=(pl.BlockSpec((None, hw, nm), lambda i: (i, 0, 0)),
                   pl.BlockSpec((None, nm, hw), lambda i: (i, 0, 0))),
        compiler_params=_cparams(("arbitrary",)),
        name="mem_kv",
    )(mem, g, wkt, wkv)


def _xattn_kernel(h_ref, g_ref, wq_ref, kt_ref, v_ref, wo_ref, o_ref):
    h = h_ref[...]
    a = _rms(h, g_ref[...]).astype(BF16)
    q = (_dot(a, wq_ref[...]) * (X_HD ** -0.5)).astype(BF16)
    heads = [slice(hh * X_HD, (hh + 1) * X_HD) for hh in range(H_X)]
    scores = [_dot(q[:, cols], kt_ref[cols, :]) for cols in heads]
    probs = []
    for s in scores:
        m = jnp.max(s, axis=-1, keepdims=True)
        p = jnp.exp(s - m)
        probs.append((p / jnp.sum(p, axis=-1, keepdims=True)).astype(BF16))
    outs = [_dot(p, v_ref[:, cols]) for p, cols in zip(probs, heads)]
    o = jnp.concatenate(outs, axis=1).astype(BF16)
    o_ref[...] = h + _dot(o, wo_ref[...])


def _xattn(h3, g, wq, kt, v, wo):
    b, t, d = h3.shape
    hw = H_X * X_HD
    nm = v.shape[1]
    return pl.pallas_call(
        _xattn_kernel,
        out_shape=jax.ShapeDtypeStruct((b, t, d), F32),
        grid=(b, t // TOK_TILE),
        in_specs=[pl.BlockSpec((None, TOK_TILE, d), lambda i, j: (i, j, 0)),
                  pl.BlockSpec((1, d), lambda i, j: (0, 0)),
                  pl.BlockSpec((d, hw), lambda i, j: (0, 0)),
                  pl.BlockSpec((None, hw, nm), lambda i, j: (i, 0, 0)),
                  pl.BlockSpec((None, nm, hw), lambda i, j: (i, 0, 0)),
                  pl.BlockSpec((hw, d), lambda i, j: (0, 0))],
        out_specs=pl.BlockSpec((None, TOK_TILE, d), lambda i, j: (i, j, 0)),
        compiler_params=_cparams(("arbitrary", "arbitrary")),
        name="xattn",
    )(h3, g, wq, kt, v, wo)


def _ffn_kernel(h_ref, g_ref, wg_ref, wu_ref, wd_ref, gf_ref, o_ref, *, fc):
    h = h_ref[...]
    a = _rms(h, g_ref[...]).astype(BF16)
    acc = h
    for f in range(wg_ref.shape[1] // fc):
        cols = slice(f * fc, (f + 1) * fc)
        mid = _silu(_dot(a, wg_ref[:, cols])) * _dot(a, wu_ref[:, cols])
        acc = acc + _dot(mid.astype(BF16), wd_ref[cols, :])
    o_ref[...] = _rms(acc, gf_ref[...])


def _ffn(h2, g, wgu, wd, gf):
    m, d = h2.shape
    ff = wd.shape[0]
    const = lambda shape: pl.BlockSpec(shape, lambda i: (0, 0))
    return pl.pallas_call(
        functools.partial(_ffn_kernel, fc=ff // 2),
        out_shape=jax.ShapeDtypeStruct((m, d), F32),
        grid=(m // TOK_TILE,),
        in_specs=[pl.BlockSpec((TOK_TILE, d), lambda i: (i, 0)),
                  const((1, d)), const((d, ff)), pl.BlockSpec((d, ff), lambda i: (0, 1)),
                  const((ff, d)), const((1, d))],
        out_specs=pl.BlockSpec((TOK_TILE, d), lambda i: (i, 0)),
        compiler_params=_cparams(("arbitrary",)),
        name="ffn",
    )(h2, g, wgu, wgu, wd, gf)


def _layout_w_in(w):
    wt = w.T
    d = wt.shape[1]
    kvw = G_KV * ATT_HD
    q_a = wt[RNN_COLS:RNN_COLS + QT_ROWS]
    kv = wt[RNN_COLS + QT_ROWS:RNN_COLS + QT_ROWS + 6 * kvw]
    gate = wt[RNN_COLS + QT_ROWS + 6 * kvw:RNN_COLS + QT_ROWS + 6 * kvw + 3 * H_ATT]
    w_n = jnp.concatenate([wt[:RNN_COLS], kv[:3 * kvw], kv[4 * kvw:5 * kvw], wt[-MG_COLS:]], axis=0).astype(BF16)
    rows = QT_ROWS + VT_ROWS + 3 * H_ATT
    w_f = jnp.concatenate([q_a * (ATT_HD ** -0.5 * LOG2E), kv[3 * kvw:4 * kvw], kv[5 * kvw:6 * kvw], gate,
                           jnp.zeros((-rows % LANE, d), w.dtype)], axis=0).astype(BF16)
    return w_n, w_f


def _overlap_t(t):
    nc = (t - CMP_LEN) // CMP_STRIDE + 1
    ns = t // SLC_LEN
    starts = CMP_STRIDE * np.arange(N_CMP_PAD)
    s_start = SLC_LEN * np.arange(ns)
    ov = ((starts[None, :] + CMP_LEN > s_start[:, None]) & (starts[None, :] < s_start[:, None] + SLC_LEN)
          & (np.arange(N_CMP_PAD)[None, :] < nc))
    return jnp.asarray(ov.astype(np.float32), dtype=BF16)


def kernel(x, mem, g_mix, w_in, lower_bounds, g_rnn_out, pe_ck, w_ck1, w_ck2, pe_cv, w_cv1, w_cv2,
           w_proj_rnn, w_proj_att, w_out, g_xattn, g_mem, w_xq, w_xkv, w_xo, g_ffn, w_gate_up,
           w_down, g_final):
    b, t, d = x.shape
    depth = g_mix.shape[0]
    assert depth == 1, "the final RMSNorm is fused into the layer's FFN kernel"
    lbs = jnp.cumsum(jax.nn.softmax(lower_bounds.astype(F32), axis=0), axis=0)
    slopes = (2.0 ** (-8.0 * jnp.arange(1, H_ATT + 1, dtype=F32) / H_ATT)).reshape(G_KV, HG, 1, 1)
    overlap_t = _overlap_t(t)
    h = x
    for l in range(depth):
        w_n, w_t = _layout_w_in(w_in[l])
        x2 = h.reshape(b * t, d)
        proj = _in_proj(x2, g_mix[l][None, :], w_n)
        proj3 = proj.reshape(b, t, N_COLS)
        qv, gt = _in_proj_t(h, g_mix[l][None, :], w_t)
        y_r = _hgrn(proj3, lbs[l][None, :], g_rnn_out[l][None, :])
        kc, vct = _compress(proj3, pe_ck[l], pe_cv[l], w_ck1[l].astype(BF16), w_cv1[l].astype(BF16),
                            w_ck2[l].astype(BF16), w_cv2[l].T.astype(BF16))
        ocmp, selb = _cmp_select(qv, kc, vct, slopes, overlap_t, t)
        y_a = _sparse(qv, gt, proj3, ocmp, selb, slopes, t)
        h1 = _merge(x2, y_r.reshape(b * t, d), y_a.reshape(b * t, d), proj,
                    w_proj_rnn[l].astype(BF16), w_proj_att[l].astype(BF16), w_out[l].astype(BF16))
        w_kv = w_xkv[l].astype(BF16)
        kt, v = _mem_kv(mem, g_mem[l][None, :], w_kv[:, :H_X * X_HD].T, w_kv)
        h2 = _xattn(h1.reshape(b, t, d), g_xattn[l][None, :], w_xq[l].astype(BF16), kt, v,
                    w_xo[l].astype(BF16))
        h = _ffn(h2.reshape(b * t, d), g_ffn[l][None, :], w_gate_up[l].astype(BF16),
                 w_down[l].astype(BF16), g_final[None, :]).reshape(b, t, d)
    return h
```

```python
import functools

import jax
import jax.numpy as jnp
import numpy as np
from jax import lax
from jax.experimental import pallas as pl
from jax.experimental.pallas import tpu as pltpu

F32 = jnp.float32
BF16 = jnp.bfloat16

D_MODEL = 1024
N_MEM = 256
H_RNN = 8
RNN_DIM = 128
RNN_CHUNK = 64
H_ATT = 16
ATT_HD = 64
G_KV = 4
HG = H_ATT // G_KV
CMP_LEN = 32
CMP_STRIDE = 16
CMP_HIDDEN = 128
SLC_LEN = 64
SLC_TOPK = 8
WINDOW = 512
FORCE_BONUS = 1.0e4
H_X = 4
X_HD = 128
D_FF = 2816
EPS = 1e-6

LANE = 128
VMEM_LIMIT = 56 * 1024 * 1024
TOK_TILE = 512
ATT_TILE = 128
CMP_TILE = 512
N_CMP_PAD = 128
HGRN_GROUP = 4
ONES_ROWS = 16
NEG_BIG = -1.0e30
LOG2E = 1.4426950408889634
AUG_SEL_ROW0 = 8

QT_ROWS = H_ATT * ATT_HD
VT_ROWS = 2 * G_KV * ATT_HD
GATE_ROWS = 64
F_ROWS = -(-(QT_ROWS + VT_ROWS + 3 * H_ATT) // LANE) * LANE
RNN_COLS = 4 * H_RNN * RNN_DIM
KV_COLS = 4 * G_KV * ATT_HD
MG_COLS = 2 * D_MODEL
N_COLS = RNN_COLS + KV_COLS + MG_COLS


def _cparams(sem):
    return pltpu.CompilerParams(dimension_semantics=sem, vmem_limit_bytes=VMEM_LIMIT)


def _rms(xf, g):
    return xf * lax.rsqrt(jnp.mean(xf * xf, axis=-1, keepdims=True) + EPS) * g


def _sigmoid(x):
    return 1.0 / (1.0 + jnp.exp(-x))


def _silu(x):
    return x * _sigmoid(x)


def _dot(a, b):
    return jnp.dot(a, b, preferred_element_type=F32)


def _dot_nt(a, b):
    return lax.dot_general(a, b, (((1,), (1,)), ((), ())), preferred_element_type=F32)


def _dot_tn(a, b):
    return lax.dot_general(a, b, (((0,), (0,)), ((), ())), preferred_element_type=F32)


def _w_prep_kernel(w_ref, wn_ref, wf_ref):
    kvw = G_KV * ATT_HD
    kv0 = RNN_COLS + QT_ROWS
    gate0 = kv0 + 6 * kvw
    mg0 = gate0 + 3 * H_ATT

    def pair_by_group(a0, b0):
        parts = []
        for g in range(G_KV):
            parts += [w_ref[a0 + g * ATT_HD:a0 + (g + 1) * ATT_HD, :], w_ref[b0 + g * ATT_HD:b0 + (g + 1) * ATT_HD, :]]
        return parts

    wn = jnp.concatenate([w_ref[0:RNN_COLS, :]] + pair_by_group(kv0, kv0 + kvw)
                         + pair_by_group(kv0 + 2 * kvw, kv0 + 4 * kvw) + [w_ref[mg0:mg0 + MG_COLS, :]], axis=0)
    wn_ref[...] = wn.T.astype(BF16)
    used = QT_ROWS + VT_ROWS + 3 * H_ATT
    wf = jnp.concatenate([w_ref[RNN_COLS:kv0, :] * (ATT_HD ** -0.5 * LOG2E),
                          w_ref[kv0 + 3 * kvw:kv0 + 4 * kvw, :], w_ref[kv0 + 5 * kvw:kv0 + 6 * kvw, :],
                          w_ref[gate0:mg0, :], jnp.zeros((wf_ref.shape[0] - used, w_ref.shape[1]), F32)], axis=0)
    wf_ref[...] = wf.astype(BF16)


def _w_prep(wt):
    n_in, d = wt.shape
    return pl.pallas_call(
        _w_prep_kernel,
        out_shape=(jax.ShapeDtypeStruct((d, N_COLS), BF16), jax.ShapeDtypeStruct((F_ROWS, d), BF16)),
        grid=(d // LANE,),
        in_specs=[pl.BlockSpec((n_in, LANE), lambda i: (0, i))],
        out_specs=(pl.BlockSpec((LANE, N_COLS), lambda i: (i, 0)),
                   pl.BlockSpec((F_ROWS, LANE), lambda i: (0, i))),
        compiler_params=_cparams(("arbitrary",)),
        name="w_prep",
    )(wt)


def _in_proj_kernel(x_ref, g_ref, w_ref, o_ref, *, sub):
    a = _rms(x_ref[...], g_ref[...]).astype(BF16)
    for n in range(o_ref.shape[1] // sub):
        cols = slice(n * sub, (n + 1) * sub)
        o_ref[:, cols] = _dot(a, w_ref[:, cols]).astype(o_ref.dtype)


def _in_proj(x2, g, w):
    m, d = x2.shape
    n = w.shape[1]
    tn = n // 2
    return pl.pallas_call(
        functools.partial(_in_proj_kernel, sub=512),
        out_shape=jax.ShapeDtypeStruct((m, n), BF16),
        grid=(2, m // TOK_TILE),
        in_specs=[pl.BlockSpec((TOK_TILE, d), lambda j, i: (i, 0)),
                  pl.BlockSpec((1, d), lambda j, i: (0, 0)),
                  pl.BlockSpec((d, tn), lambda j, i: (0, j))],
        out_specs=pl.BlockSpec((TOK_TILE, tn), lambda j, i: (i, j)),
        compiler_params=_cparams(("arbitrary", "arbitrary")),
        name="in_proj",
    )(x2, g, w)


def _in_proj_t_kernel(x_ref, g_ref, wt_ref, qv_ref, gt_ref):
    a = _rms(x_ref[...], g_ref[...]).astype(BF16)
    r = _dot_nt(wt_ref[...], a)
    nqv = qv_ref.shape[1]
    ngt = gt_ref.shape[1]
    for c in range(qv_ref.shape[0]):
        qv_ref[c] = r[:nqv, c * LANE:(c + 1) * LANE].astype(qv_ref.dtype)
        gt_ref[c] = r[nqv:nqv + ngt, c * LANE:(c + 1) * LANE]


def _in_proj_t(x, g, wt):
    b, t, d = x.shape
    rows = wt.shape[0]
    nqv = QT_ROWS + VT_ROWS
    ngt = GATE_ROWS
    nc = TOK_TILE // LANE
    return pl.pallas_call(
        _in_proj_t_kernel,
        out_shape=(jax.ShapeDtypeStruct((b, t // LANE, nqv, LANE), BF16),
                   jax.ShapeDtypeStruct((b, t // LANE, ngt, LANE), F32)),
        grid=(b, t // TOK_TILE),
        in_specs=[pl.BlockSpec((None, TOK_TILE, d), lambda i, j: (i, j, 0)),
                  pl.BlockSpec((1, d), lambda i, j: (0, 0)),
                  pl.BlockSpec((rows, d), lambda i, j: (0, 0))],
        out_specs=(pl.BlockSpec((None, nc, nqv, LANE), lambda i, j: (i, j, 0, 0)),
                   pl.BlockSpec((None, nc, ngt, LANE), lambda i, j: (i, j, 0, 0))),
        compiler_params=_cparams(("arbitrary", "arbitrary")),
        name="in_proj_t",
    )(x, g, wt)


def _hgrn_kernel(q_ref, f_ref, i_ref, og_ref, lb_ref, gn_ref, o_ref, qd_ref, oi_ref, ut_ref, dec_ref):
    c = RNN_CHUNK
    kd = RNN_DIM
    n_chunks = q_ref.shape[0] // c
    lb = lb_ref[...]
    gn = gn_ref[...]
    blk = HGRN_GROUP * c
    row = lax.broadcasted_iota(jnp.int32, (blk, blk), 0)
    col = lax.broadcasted_iota(jnp.int32, (blk, blk), 1)
    same_chunk = (row // c) == (col // c)
    causal = same_chunk & (row >= col)
    tril = causal.astype(BF16)

    q = q_ref[...].astype(F32)
    fl = f_ref[...].astype(F32)
    v = i_ref[...]
    f = lb + (1.0 - lb) * _sigmoid(fl)
    k = 1.0 - f
    logf = jnp.log(f)
    hi = logf.astype(BF16)
    lo = (logf - hi.astype(F32)).astype(BF16)
    pieces = jnp.concatenate([hi, lo], axis=1)
    n_blk = q_ref.shape[0] // blk
    cs = [_dot(tril, pieces[g * blk:(g + 1) * blk]) for g in range(n_blk)]
    bcum = jnp.concatenate([x[:, 0:kd] + x[:, kd:2 * kd] for x in cs], axis=0)
    e_neg = jnp.exp(-bcum)
    dec = jnp.exp(jnp.concatenate([bcum[n * c + c - 1:n * c + c, :] for n in range(n_chunks)], axis=0))
    dec_rows = jnp.concatenate([jnp.broadcast_to(dec[n:n + 1, :], (c, kd)) for n in range(n_chunks)], axis=0)
    q_dec = (_silu(q) * (1.0 / e_neg)).astype(BF16)
    k_neg = k * e_neg
    k_dec = k_neg.astype(BF16)
    k_end = (k_neg * dec_rows).astype(BF16)
    qd_ref[...] = q_dec
    dec_ref[...] = dec
    for g in range(n_blk):
        rows = slice(g * blk, (g + 1) * blk)
        a = jnp.where(causal, _dot_nt(q_dec[rows], k_dec[rows]), 0.0)
        oi_ref[rows, :] = _dot(a.astype(BF16), v[rows])
    for n in range(n_chunks):
        rows = slice(n * c, (n + 1) * c)
        ut_ref[n] = _dot_tn(v[rows], k_end[rows])

    s_t = jnp.zeros((kd, kd), F32)
    for n in range(n_chunks):
        rows = slice(n * c, (n + 1) * c)
        o = oi_ref[rows, :] + _dot_nt(qd_ref[rows, :], s_t.astype(BF16))
        og = og_ref[rows, :].astype(F32)
        o_ref[rows, :] = (_rms(o, gn) * _silu(og)).astype(o_ref.dtype)
        s_t = s_t * dec_ref[n:n + 1, :] + ut_ref[n]


def _hgrn(proj3, lb, gn):
    b, t, _ = proj3.shape
    return pl.pallas_call(
        _hgrn_kernel,
        out_shape=jax.ShapeDtypeStruct((b, t, H_RNN * RNN_DIM), BF16),
        grid=(b, H_RNN),
        in_specs=[pl.BlockSpec((None, t, RNN_DIM), lambda i, h: (i, 0, h)),
                  pl.BlockSpec((None, t, RNN_DIM), lambda i, h: (i, 0, H_RNN + h)),
                  pl.BlockSpec((None, t, RNN_DIM), lambda i, h: (i, 0, 2 * H_RNN + h)),
                  pl.BlockSpec((None, t, RNN_DIM), lambda i, h: (i, 0, 3 * H_RNN + h)),
                  pl.BlockSpec((1, RNN_DIM), lambda i, h: (0, h)),
                  pl.BlockSpec((1, RNN_DIM), lambda i, h: (0, 0))],
        out_specs=pl.BlockSpec((None, t, RNN_DIM), lambda i, h: (i, 0, h)),
        scratch_shapes=[pltpu.VMEM((t, RNN_DIM), BF16),
                        pltpu.VMEM((t, RNN_DIM), F32),
                        pltpu.VMEM((t // RNN_CHUNK, RNN_DIM, RNN_DIM), F32),
                        pltpu.VMEM((t // RNN_CHUNK, RNN_DIM), F32)],
        compiler_params=_cparams(("arbitrary", "arbitrary")),
        name="hgrn",
    )(proj3, proj3, proj3, proj3, lb, gn)


def _compress_kernel(kv_ref, pek_ref, pev_ref, wk1_ref, wv1_ref, wk2_ref, wv2t_ref,
                     kc_ref, vct_ref, xs_ref, xk_ref, xv_ref):
    t = kv_ref.shape[0]
    hd = ATT_HD
    xs_ref[0:t, :] = kv_ref[...].astype(F32)
    xs_ref[t:, :] = jnp.zeros((xs_ref.shape[0] - t, xs_ref.shape[1]), F32)
    for l in range(CMP_LEN):
        blk = xs_ref[pl.ds(l, N_CMP_PAD, stride=CMP_STRIDE), :]
        xk_ref[:, l * hd:(l + 1) * hd] = (blk[:, 0:hd] + pek_ref[l:l + 1, :]).astype(BF16)
        xv_ref[:, l * hd:(l + 1) * hd] = (blk[:, hd:2 * hd] + pev_ref[l:l + 1, :]).astype(BF16)
    hk = _silu(_dot(xk_ref[...], wk1_ref[...])).astype(BF16)
    hv = _silu(_dot(xv_ref[...], wv1_ref[...])).astype(BF16)
    kc_ref[...] = _dot(hk, wk2_ref[...])
    vct_ref[...] = _dot_nt(wv2t_ref[...], hv)


def _compress(proj3, pek, pev, wk1, wv1, wk2, wv2t):
    b, t, _ = proj3.shape
    kv_unit0 = RNN_COLS // LANE
    flat = CMP_LEN * ATT_HD
    full = lambda shape: pl.BlockSpec(shape, lambda i, g: (0,) * len(shape))
    return pl.pallas_call(
        _compress_kernel,
        out_shape=(jax.ShapeDtypeStruct((b, G_KV, N_CMP_PAD, ATT_HD), F32),
                   jax.ShapeDtypeStruct((b, G_KV, ATT_HD, N_CMP_PAD), F32)),
        grid=(b, G_KV),
        in_specs=[pl.BlockSpec((None, t, LANE), lambda i, g: (i, 0, kv_unit0 + g)),
                  full((CMP_LEN, ATT_HD)), full((CMP_LEN, ATT_HD)),
                  full((flat, CMP_HIDDEN)), full((flat, CMP_HIDDEN)),
                  full((CMP_HIDDEN, ATT_HD)), full((ATT_HD, CMP_HIDDEN))],
        out_specs=(pl.BlockSpec((None, None, N_CMP_PAD, ATT_HD), lambda i, g: (i, g, 0, 0)),
                   pl.BlockSpec((None, None, ATT_HD, N_CMP_PAD), lambda i, g: (i, g, 0, 0))),
        scratch_shapes=[pltpu.VMEM((t + CMP_STRIDE, LANE), F32),
                        pltpu.VMEM((N_CMP_PAD, flat), BF16),
                        pltpu.VMEM((N_CMP_PAD, flat), BF16)],
        compiler_params=_cparams(("arbitrary", "arbitrary")),
        name="compress",
    )(proj3, pek, pev, wk1, wv1, wk2, wv2t)


def _cmp_select_kernel(qt_ref, kc_ref, vct_ref, slope_ref, ov_ref, ocmp_ref, selb_ref, *, n_cmp, n_sel):
    nct = CMP_TILE // LANE
    tt = CMP_TILE
    ns = ov_ref.shape[0]
    kc = kc_ref[...].astype(BF16)
    vct = vct_ref[...].astype(BF16)
    ov = ov_ref[...]
    j_i = lax.broadcasted_iota(jnp.int32, (8, tt), 0)
    tq_rel = lax.broadcasted_iota(jnp.int32, (8, tt), 1)
    n_tiles = qt_ref.shape[0] // nct

    def ranges(tile):
        t0 = tile * tt
        n_any = min(N_CMP_PAD, (t0 + tt) // CMP_STRIDE)
        n_all = max(0, (t0 - (CMP_LEN - 1)) // CMP_STRIDE + 1) // 8 * 8
        return t0, n_any, n_all, (t0 + tt) // SLC_LEN

    def pad_rows(x):
        if x.shape[0] == N_CMP_PAD:
            return x
        return jnp.concatenate([x, jnp.zeros((N_CMP_PAD - x.shape[0], x.shape[1]), x.dtype)], axis=0)

    scores = []
    for tile in range(n_tiles):
        _, n_any, _, _ = ranges(tile)
        qt = [jnp.concatenate([qt_ref[tile * nct + c, h * ATT_HD:(h + 1) * ATT_HD, :] for c in range(nct)],
                              axis=1) for h in range(HG)]
        scores.append(_dot(kc[:n_any], jnp.concatenate(qt, axis=1)))

    probs, psums = [], []
    for tile in range(n_tiles):
        t0, n_any, n_all, _ = ranges(tile)
        n_b = lax.broadcasted_iota(jnp.int32, (n_any - n_all, tt), 0) + n_all
        t_b = lax.broadcasted_iota(jnp.int32, (n_any - n_all, tt), 1)
        visible = (CMP_STRIDE * n_b + (CMP_LEN - 1) - t0 <= t_b) & (n_b < n_cmp)
        centre = (lax.broadcasted_iota(jnp.int32, (n_any, tt), 0).astype(F32) * float(CMP_STRIDE)
                  + (CMP_LEN - 1) / 2.0)
        psum = jnp.zeros((n_any, tt), F32)
        ph = []
        for h in range(HG):
            s = scores[tile][:, h * tt:(h + 1) * tt] + slope_ref[h] * LOG2E * centre
            band = jnp.where(visible, s[n_all:], -jnp.inf)
            s = band if n_all == 0 else jnp.concatenate([s[:n_all], band], axis=0)
            m = jnp.max(s, axis=0, keepdims=True)
            m = jnp.where(m == -jnp.inf, 0.0, m)
            p = jnp.exp2(s - m)
            d = jnp.sum(p, axis=0, keepdims=True)
            p = p * (1.0 / jnp.where(d > 0, d, 1.0))
            psum = psum + p
            ph.append(pad_rows(p).astype(BF16))
        probs.append(jnp.concatenate(ph, axis=1))
        psums.append(pad_rows(psum))

    for tile in range(n_tiles):
        o = _dot(vct, probs[tile])
        for h in range(HG):
            for c in range(nct):
                lanes = slice(h * tt + c * LANE, h * tt + (c + 1) * LANE)
                ocmp_ref[tile * nct + c, h * ATT_HD:(h + 1) * ATT_HD, :] = o[:, lanes]

    imps = []
    for tile in range(n_tiles):
        psum = psums[tile]
        hi = psum.astype(BF16)
        rem = psum - hi.astype(F32)
        mid = rem.astype(BF16)
        lo = (rem - mid.astype(F32)).astype(BF16)
        imp3 = _dot(ov, jnp.concatenate([hi, mid, lo], axis=1))
        imps.append(imp3[:, 0:tt] + imp3[:, tt:2 * tt] + imp3[:, 2 * tt:3 * tt])

    for tile in range(n_tiles):
        t0, _, _, nvb = ranges(tile)
        imp = imps[tile]
        score = []
        for rg in range(nvb // 8):
            j = j_i + 8 * rg
            tq = tq_rel + t0
            cur = tq // SLC_LEN
            forced = (j == 0) | (j == cur) | (j == cur - 1)
            sc = imp[8 * rg:8 * rg + 8, :] + jnp.where(forced, FORCE_BONUS, 0.0)
            score.append(jnp.where(j * SLC_LEN <= tq, sc, -jnp.inf))
        rank = [jnp.zeros((8, tt), F32) for _ in score]
        for jp in range(nvb):
            sj = jnp.broadcast_to(score[jp // 8][jp % 8:jp % 8 + 1, :], (8, tt))
            for rg in range(nvb // 8):
                ge = jnp.where(sj >= score[rg], 1.0, 0.0)
                gt = jnp.where(sj > score[rg], 1.0, 0.0)
                if 8 * rg > jp:
                    ahead = ge
                elif 8 * rg + 7 <= jp:
                    ahead = gt
                else:
                    ahead = jnp.where(j_i + 8 * rg > jp, ge, gt)
                rank[rg] = rank[rg] + ahead
        for rg in range(ns // 8):
            for c in range(nct):
                if rg < nvb // 8:
                    selb = jnp.where(rank[rg][:, c * LANE:(c + 1) * LANE] < float(n_sel), 0.0, NEG_BIG)
                else:
                    selb = jnp.full((8, LANE), NEG_BIG, F32)
                selb_ref[tile * nct + c, 8 * rg:8 * rg + 8, :] = selb


def _cmp_select(qv, kc, vct, slopes, overlap_t, t):
    b = qv.shape[0]
    ns = t // SLC_LEN
    nch = t // LANE
    n_cmp = (t - CMP_LEN) // CMP_STRIDE + 1
    n_sel = min(SLC_TOPK, ns)
    rows = HG * ATT_HD
    return pl.pallas_call(
        functools.partial(_cmp_select_kernel, n_cmp=n_cmp, n_sel=n_sel),
        out_shape=(jax.ShapeDtypeStruct((b, nch, H_ATT * ATT_HD, LANE), F32),
                   jax.ShapeDtypeStruct((b, G_KV, nch, ns, LANE), F32)),
        grid=(b, G_KV),
        in_specs=[pl.BlockSpec((None, nch, rows, LANE), lambda i, g: (i, 0, g, 0)),
                  pl.BlockSpec((None, None, N_CMP_PAD, ATT_HD), lambda i, g: (i, g, 0, 0)),
                  pl.BlockSpec((None, None, ATT_HD, N_CMP_PAD), lambda i, g: (i, g, 0, 0)),
                  pl.BlockSpec((None, HG, 1, 1), lambda i, g: (g, 0, 0, 0)),
                  pl.BlockSpec((ns, N_CMP_PAD), lambda i, g: (0, 0))],
        out_specs=(pl.BlockSpec((None, nch, rows, LANE), lambda i, g: (i, 0, g, 0)),
                   pl.BlockSpec((None, None, nch, ns, LANE), lambda i, g: (i, g, 0, 0, 0))),
        compiler_params=_cparams(("arbitrary", "arbitrary")),
        name="cmp_select",
    )(qv, kc, vct, slopes, overlap_t)


def _sparse_kernel(qt_ref, k_ref, ka_ref, vs_ref, vw_ref, gt_ref, ocmp_ref, selb_ref, slope_ref, o_ref,
                   msk_ref, qop_ref, s_ref, m_ref, acc_ref):
    tq = ATT_TILE
    kc = ATT_TILE
    hd = ATT_HD
    nl = HG * tq
    n_tiles = qt_ref.shape[0]
    ns = selb_ref.shape[1]
    wch = WINDOW // kc
    n_sel_items = n_tiles + 1
    slope = jnp.concatenate([jnp.broadcast_to(slope_ref[h], (1, tq)) for h in range(HG)], axis=1) * LOG2E
    s_hi = slope.astype(BF16).astype(F32)
    s_mid = (slope - s_hi).astype(BF16).astype(F32)
    s_lo = (slope - s_hi - s_mid).astype(BF16).astype(F32)
    slope_rows = jnp.concatenate([s_hi, s_mid, s_lo, jnp.zeros((AUG_SEL_ROW0 - 3, nl), F32)], axis=0)
    aug_tail = jnp.zeros((LANE - AUG_SEL_ROW0 - ns, nl), F32)
    aug_win = jnp.concatenate([slope_rows, jnp.zeros((ns, nl), F32), aug_tail], axis=0).astype(BF16)
    k_i = lax.broadcasted_iota(jnp.int32, (kc, nl), 0)
    t_i = lax.broadcasted_iota(jnp.int32, (kc, nl), 1) % tq
    d0 = t_i - k_i
    msk_ref[0] = jnp.where(d0 >= 0, 0.0, NEG_BIG)
    msk_ref[1] = jnp.where(d0 < 0, 0.0, NEG_BIG)
    chunk_bias = slope * float(kc)
    ones = jnp.ones((ONES_ROWS, kc), BF16)

    n_pairs = n_tiles // 2
    n_items = n_sel_items + 2 * (wch + 1)

    def tiles_of(p):
        return (p, n_tiles - 1 - p)

    def sel_item(p, i):
        lo, hi = tiles_of(p)
        if i == 0:
            return 0, lo, 0
        if i == n_sel_items - 1:
            return 1, hi, 0
        which = jnp.where(i > lo, 1, 0)
        tile = jnp.where(i > lo, hi, lo)
        chunk = jnp.where(i > lo, i - lo - 1, i - 1)
        return which, chunk, tile - chunk

    def win_item(p, w, j):
        chunk = tiles_of(p)[w] - wch + j
        return jnp.maximum(chunk, 0), chunk >= 0, wch - j

    def col_max(r, delta_f):
        r8 = jnp.max(r.reshape(kc // 8, 8, nl), axis=0)
        return r8 - chunk_bias * delta_f

    def scores(chunk, op):
        rows = pl.ds(pl.multiple_of(chunk * kc, kc), kc)
        keys = jnp.concatenate([k_ref[rows, :], ka_ref[rows, :]], axis=1)
        return _dot(keys, qop_ref[op])

    def build(p, slot):
        for w, tile in enumerate(tiles_of(p)):
            qt = qt_ref[tile]
            q_all = jnp.concatenate([qt[h * hd:(h + 1) * hd, :] for h in range(HG)], axis=1)
            zero = jnp.zeros_like(q_all)
            selb = jnp.concatenate([selb_ref[tile]] * HG, axis=1)
            aug_sel = jnp.concatenate([slope_rows, selb, aug_tail], axis=0).astype(BF16)
            qop_ref[4 * slot + 2 * w] = jnp.concatenate([q_all, zero, aug_sel], axis=0)
            qop_ref[4 * slot + 2 * w + 1] = jnp.concatenate([zero, q_all, aug_win], axis=0)
        for x in range(4):
            m_ref[4 * slot + x] = jnp.full((8, nl), NEG_BIG, F32)

    def phase_a(p, slot):
        items = []

        def sel(i):
            which, chunk, delta = sel_item(p, i)
            r = scores(chunk, 4 * slot + 2 * which)
            if i in (0, n_sel_items - 1):
                r = r + msk_ref[0]
            s_ref[slot * n_items + i] = r
            x = 4 * slot + 2 * which
            m_ref[x] = jnp.maximum(m_ref[x], col_max(r, jnp.float32(delta)))

        def win(w, j):
            chunk, valid, delta = win_item(p, w, j)
            r = scores(chunk, 4 * slot + 2 * w + 1)
            if j == 0:
                r = r + msk_ref[1]
            elif j == wch:
                r = r + msk_ref[0]
            s_ref[slot * n_items + n_sel_items + w * (wch + 1) + j] = r
            cm = jnp.where(valid, col_max(r, float(delta)), NEG_BIG)
            x = 4 * slot + 2 * w + 1
            m_ref[x] = jnp.maximum(m_ref[x], cm)

        for i in range(n_sel_items):
            items.append(functools.partial(sel, i))
        for w in range(2):
            for j in range(wch + 1):
                items.append(functools.partial(win, w, j))
        return items

    def phase_b(p, slot):
        m_row = [jnp.max(m_ref[4 * slot + x], axis=0, keepdims=True) for x in range(4)]
        for x in range(4):
            acc_ref[4 * slot + x] = jnp.zeros(acc_ref.shape[1:], F32)
        items = []

        def sel(i):
            which, chunk, delta = sel_item(p, i)
            row = jnp.where(which == 1, m_row[2], m_row[0]) + chunk_bias * jnp.float32(delta)
            pr = jnp.exp2(s_ref[slot * n_items + i] - row).astype(BF16)
            v_aug = jnp.concatenate([vs_ref[chunk], ones], axis=0)
            acc_ref[4 * slot + 2 * which] += _dot(v_aug, pr)

        def win(w, j):
            chunk, valid, delta = win_item(p, w, j)
            row = jnp.where(valid, m_row[2 * w + 1] + chunk_bias * float(delta), -NEG_BIG)
            pr = jnp.exp2(s_ref[slot * n_items + n_sel_items + w * (wch + 1) + j] - row).astype(BF16)
            v_aug = jnp.concatenate([vw_ref[chunk], ones], axis=0)
            acc_ref[4 * slot + 2 * w + 1] += _dot(v_aug, pr)

        for i in range(n_sel_items):
            items.append(functools.partial(sel, i))
        for w in range(2):
            for j in range(wch + 1):
                items.append(functools.partial(win, w, j))
        return items

    def finalize(p, slot):
        for w, tile in enumerate(tiles_of(p)):
            a_s = acc_ref[4 * slot + 2 * w]
            a_w = acc_ref[4 * slot + 2 * w + 1]
            o_slc = a_s[:hd, :] / a_s[hd:hd + 1, :]
            o_win = a_w[:hd, :] / a_w[hd:hd + 1, :]
            oc = ocmp_ref[tile]
            outs = []
            for h in range(HG):
                lanes = slice(h * tq, (h + 1) * tq)
                row0 = (pl.program_id(1) * HG + h) * 3
                g_cmp, g_slc, g_win = [_sigmoid(gt_ref[tile, pl.ds(row0 + br, 1), :]) for br in range(3)]
                y = (g_cmp * oc[h * hd:(h + 1) * hd, :] + g_slc * o_slc[:, lanes]
                     + g_win * o_win[:, lanes])
                outs.append(y.T)
            rows = pl.ds(pl.multiple_of(tile * tq, tq), tq)
            o_ref[rows, :] = jnp.concatenate(outs, axis=1).astype(o_ref.dtype)

    def step(k, slot, first=False, last=False):
        if not last:
            build(k + 1, 1 - slot)
        if not first:
            finalize(k - 1, 1 - slot)
        b_items = phase_b(k, slot)
        a_items = phase_a(k + 1, 1 - slot) if not last else [None] * n_items
        for a, b in zip(a_items, b_items):
            if a is not None:
                a()
            b()

    build(0, 0)
    for a in phase_a(0, 0):
        a()
    step(0, 0, first=True)

    def body(kk, carry):
        step(2 * kk + 1, 1)
        step(2 * kk + 2, 0)
        return carry

    lax.fori_loop(0, (n_pairs - 2) // 2, body, 0)
    step(n_pairs - 1, (n_pairs - 1) % 2, last=True)
    finalize(n_pairs - 1, (n_pairs - 1) % 2)


def _key_features(t):
    kp = np.arange(t)
    f = np.zeros((t, LANE), np.float32)
    f[:, 0:3] = (kp % ATT_TILE)[:, None]
    f[kp, AUG_SEL_ROW0 + kp // SLC_LEN] = 1.0
    return jnp.asarray(f, dtype=BF16)


def _sparse(qv, gt, proj3, ocmp, selb, slopes, t):
    b = qv.shape[0]
    nchunk = t // LANE
    ns = t // SLC_LEN
    rows = HG * ATT_HD
    ksw_unit0 = RNN_COLS // LANE + G_KV
    vs_blk0 = QT_ROWS // ATT_HD
    vw_blk0 = vs_blk0 + G_KV
    nl = HG * ATT_TILE
    n_items = (nchunk + 1) + 2 * (WINDOW // ATT_TILE + 1)
    return pl.pallas_call(
        _sparse_kernel,
        out_shape=jax.ShapeDtypeStruct((b, t, H_ATT * ATT_HD), BF16),
        grid=(b, G_KV),
        in_specs=[pl.BlockSpec((None, nchunk, rows, LANE), lambda i, g: (i, 0, g, 0)),
                  pl.BlockSpec((None, t, LANE), lambda i, g: (i, 0, ksw_unit0 + g)),
                  pl.BlockSpec((t, LANE), lambda i, g: (0, 0)),
                  pl.BlockSpec((None, nchunk, ATT_HD, LANE), lambda i, g: (i, 0, vs_blk0 + g, 0)),
                  pl.BlockSpec((None, nchunk, ATT_HD, LANE), lambda i, g: (i, 0, vw_blk0 + g, 0)),
                  pl.BlockSpec((None, nchunk, GATE_ROWS, LANE), lambda i, g: (i, 0, 0, 0)),
                  pl.BlockSpec((None, nchunk, rows, LANE), lambda i, g: (i, 0, g, 0)),
                  pl.BlockSpec((None, None, nchunk, ns, LANE), lambda i, g: (i, g, 0, 0, 0)),
                  pl.BlockSpec((None, HG, 1, 1), lambda i, g: (g, 0, 0, 0))],
        out_specs=pl.BlockSpec((None, t, rows), lambda i, g: (i, 0, g)),
        scratch_shapes=[pltpu.VMEM((2, ATT_TILE, nl), F32),
                        pltpu.VMEM((2 * 4, 2 * LANE, nl), BF16),
                        pltpu.VMEM((2 * n_items, ATT_TILE, nl), F32),
                        pltpu.VMEM((2 * 4, 8, nl), F32),
                        pltpu.VMEM((2 * 4, ATT_HD + ONES_ROWS, nl), F32)],
        compiler_params=_cparams(("arbitrary", "arbitrary")),
        name="sparse",
    )(qv, proj3, _key_features(t), qv, qv, gt, ocmp, selb, slopes)


def _merge_kernel(x_ref, yr_ref, ya_ref, mgr_ref, mga_ref, wr_ref, wa_ref, wo_ref, o_ref):
    pr = _dot(yr_ref[...], wr_ref[...])
    pa = _dot(ya_ref[...], wa_ref[...])
    merged = _sigmoid(mgr_ref[...].astype(F32)) * pr + _sigmoid(mga_ref[...].astype(F32)) * pa
    o_ref[...] = x_ref[...] + _dot(merged.astype(BF16), wo_ref[...])


def _merge(x2, yr, ya, proj, wr, wa, wo):
    m, d = x2.shape
    mg0 = (RNN_COLS + KV_COLS) // d
    tile = lambda col: pl.BlockSpec((TOK_TILE, d), lambda i: (i, col))
    wfull = pl.BlockSpec((d, d), lambda i: (0, 0))
    return pl.pallas_call(
        _merge_kernel,
        out_shape=jax.ShapeDtypeStruct((m, d), F32),
        grid=(m // TOK_TILE,),
        in_specs=[tile(0), tile(0), tile(0), tile(mg0), tile(mg0 + 1), wfull, wfull, wfull],
        out_specs=tile(0),
        compiler_params=_cparams(("arbitrary",)),
        name="merge",
    )(x2, yr, ya, proj, proj, wr, wa, wo)


def _mem_kv_kernel(mem_ref, g_ref, wkt_ref, wv_ref, kt_ref, v_ref):
    a = _rms(mem_ref[...], g_ref[...]).astype(BF16)
    kt_ref[...] = _dot_nt(wkt_ref[...], a).astype(kt_ref.dtype)
    v_ref[...] = _dot(a, wv_ref[...]).astype(v_ref.dtype)


def _mem_kv(mem, g, wkt, wkv):
    b, nm, d = mem.shape
    hw = H_X * X_HD
    return pl.pallas_call(
        _mem_kv_kernel,
        out_shape=(jax.ShapeDtypeStruct((b, hw, nm), BF16),
                   jax.ShapeDtypeStruct((b, nm, hw), BF16)),
        grid=(b,),
        in_specs=[pl.BlockSpec((None, nm, d), lambda i: (i, 0, 0)),
                  pl.BlockSpec((1, d), lambda i: (0, 0)),
                  pl.BlockSpec((hw, d), lambda i: (0, 0)),
                  pl.BlockSpec((d, hw), lambda i: (0, 1))],
        out_specs=(pl.BlockSpec((None, hw, nm), lambda i: (i, 0, 0)),
                   pl.BlockSpec((None, nm, hw), lambda i: (i, 0, 0))),
        compiler_params=_cparams(("arbitrary",)),
        name="mem_kv",
    )(mem, g, wkt, wkv)


def _xattn_kernel(h_ref, g_ref, wq_ref, kt_ref, v_ref, wo_ref, o_ref):
    h = h_ref[...]
    a = _rms(h, g_ref[...]).astype(BF16)
    q = (_dot(a, wq_ref[...]) * (X_HD ** -0.5)).astype(BF16)
    heads = [slice(hh * X_HD, (hh + 1) * X_HD) for hh in range(H_X)]
    scores = [_dot(q[:, cols], kt_ref[cols, :]) for cols in heads]
    probs = []
    for s in scores:
        m = jnp.max(s, axis=-1, keepdims=True)
        p = jnp.exp(s - m)
        probs.append((p / jnp.sum(p, axis=-1, keepdims=True)).astype(BF16))
    outs = [_dot(p, v_ref[:, cols]) for p, cols in zip(probs, heads)]
    o = jnp.concatenate(outs, axis=1).astype(BF16)
    o_ref[...] = h + _dot(o, wo_ref[...])


def _xattn(h3, g, wq, kt, v, wo):
    b, t, d = h3.shape
    hw = H_X * X_HD
    nm = v.shape[1]
    return pl.pallas_call(
        _xattn_kernel,
        out_shape=jax.ShapeDtypeStruct((b, t, d), F32),
        grid=(b, t // TOK_TILE),
        in_specs=[pl.BlockSpec((None, TOK_TILE, d), lambda i, j: (i, j, 0)),
                  pl.BlockSpec((1, d), lambda i, j: (0, 0)),
                  pl.BlockSpec((d, hw), lambda i, j: (0, 0)),
                  pl.BlockSpec((None, hw, nm), lambda i, j: (i, 0, 0)),
                  pl.BlockSpec((None, nm, hw), lambda i, j: (i, 0, 0)),
                  pl.BlockSpec((hw, d), lambda i, j: (0, 0))],
        out_specs=pl.BlockSpec((None, TOK_TILE, d), lambda i, j: (i, j, 0)),
        compiler_params=_cparams(("arbitrary", "arbitrary")),
        name="xattn",
    )(h3, g, wq, kt, v, wo)


def _ffn_kernel(h_ref, g_ref, wg_ref, wu_ref, wd_ref, gf_ref, o_ref, *, fc):
    h = h_ref[...]
    a = _rms(h, g_ref[...]).astype(BF16)
    acc = h
    for f in range(wg_ref.shape[1] // fc):
        cols = slice(f * fc, (f + 1) * fc)
        mid = _silu(_dot(a, wg_ref[:, cols])) * _dot(a, wu_ref[:, cols])
        acc = acc + _dot(mid.astype(BF16), wd_ref[cols, :])
    o_ref[...] = _rms(acc, gf_ref[...])


def _ffn(h2, g, wgu, wd, gf):
    m, d = h2.shape
    ff = wd.shape[0]
    const = lambda shape: pl.BlockSpec(shape, lambda i: (0, 0))
    return pl.pallas_call(
        functools.partial(_ffn_kernel, fc=ff // 2),
        out_shape=jax.ShapeDtypeStruct((m, d), F32),
        grid=(m // TOK_TILE,),
        in_specs=[pl.BlockSpec((TOK_TILE, d), lambda i: (i, 0)),
                  const((1, d)), const((d, ff)), pl.BlockSpec((d, ff), lambda i: (0, 1)),
                  const((ff, d)), const((1, d))],
        out_specs=pl.BlockSpec((TOK_TILE, d), lambda i: (i, 0)),
        compiler_params=_cparams(("arbitrary",)),
        name="ffn",
    )(h2, g, wgu, wgu, wd, gf)


def _overlap_t(t):
    nc = (t - CMP_LEN) // CMP_STRIDE + 1
    ns = t // SLC_LEN
    starts = CMP_STRIDE * np.arange(N_CMP_PAD)
    s_start = SLC_LEN * np.arange(ns)
    ov = ((starts[None, :] + CMP_LEN > s_start[:, None]) & (starts[None, :] < s_start[:, None] + SLC_LEN)
          & (np.arange(N_CMP_PAD)[None, :] < nc))
    return jnp.asarray(ov.astype(np.float32), dtype=BF16)


def kernel(x, mem, g_mix, w_in, lower_bounds, g_rnn_out, pe_ck, w_ck1, w_ck2, pe_cv, w_cv1, w_cv2,
           w_proj_rnn, w_proj_att, w_out, g_xattn, g_mem, w_xq, w_xkv, w_xo, g_ffn, w_gate_up,
           w_down, g_final):
    b, t, d = x.shape
    depth = g_mix.shape[0]
    assert depth == 1, "the final RMSNorm is fused into the layer's FFN kernel"
    lbs = jnp.cumsum(jax.nn.softmax(lower_bounds.astype(F32), axis=0), axis=0)
    slopes = (2.0 ** (-8.0 * jnp.arange(1, H_ATT + 1, dtype=F32) / H_ATT)).reshape(G_KV, HG, 1, 1)
    overlap_t = _overlap_t(t)
    h = x
    for l in range(depth):
        w_n, w_t = _w_prep(w_in[l].T)
        x2 = h.reshape(b * t, d)
        proj = _in_proj(x2, g_mix[l][None, :], w_n)
        proj3 = proj.reshape(b, t, N_COLS)
        qv, gt = _in_proj_t(h, g_mix[l][None, :], w_t)
        y_r = _hgrn(proj3, lbs[l][None, :], g_rnn_out[l][None, :])
        kc, vct = _compress(proj3, pe_ck[l], pe_cv[l], w_ck1[l].astype(BF16), w_cv1[l].astype(BF16),
                            w_ck2[l].astype(BF16), w_cv2[l].T.astype(BF16))
        ocmp, selb = _cmp_select(qv, kc, vct, slopes, overlap_t, t)
        y_a = _sparse(qv, gt, proj3, ocmp, selb, slopes, t)
        h1 = _merge(x2, y_r.reshape(b * t, d), y_a.reshape(b * t, d), proj,
                    w_proj_rnn[l].astype(BF16), w_proj_att[l].astype(BF16), w_out[l].astype(BF16))
        w_kv = w_xkv[l].astype(BF16)
        kt, v = _mem_kv(mem, g_mem[l][None, :], w_kv[:, :H_X * X_HD].T, w_kv)
        h2 = _xattn(h1.reshape(b, t, d), g_xattn[l][None, :], w_xq[l].astype(BF16), kt, v,
                    w_xo[l].astype(BF16))
        h = _ffn(h2.reshape(b * t, d), g_ffn[l][None, :], w_gate_up[l].astype(BF16),
                 w_down[l].astype(BF16), g_final[None, :]).reshape(b, t, d)
    return h
```

```python
import functools

import jax
import jax.numpy as jnp
import numpy as np
from jax import lax
from jax.experimental import pallas as pl
from jax.experimental.pallas import tpu as pltpu

F32 = jnp.float32
BF16 = jnp.bfloat16

D_MODEL = 1024
N_MEM = 256
H_RNN = 8
RNN_DIM = 128
RNN_CHUNK = 64
H_ATT = 16
ATT_HD = 64
G_KV = 4
HG = H_ATT // G_KV
CMP_LEN = 32
CMP_STRIDE = 16
CMP_HIDDEN = 128
SLC_LEN = 64
SLC_TOPK = 8
WINDOW = 512
FORCE_BONUS = 1.0e4
H_X = 4
X_HD = 128
D_FF = 2816
EPS = 1e-6

LANE = 128
VMEM_LIMIT = 56 * 1024 * 1024
TOK_TILE = 512
ATT_TILE = 128
CMP_TILE = 512
N_CMP_PAD = 128
HGRN_GROUP = 4
ONES_ROWS = 16
NEG_BIG = -1.0e30
LOG2E = 1.4426950408889634
AUG_SEL_ROW0 = 8
SEL_NEAR = 5

QT_ROWS = H_ATT * ATT_HD
VT_ROWS = 2 * G_KV * ATT_HD
GATE_ROWS = 64
F_ROWS = -(-(QT_ROWS + VT_ROWS + 3 * H_ATT) // LANE) * LANE
RNN_COLS = 4 * H_RNN * RNN_DIM
KV_COLS = 4 * G_KV * ATT_HD
MG_COLS = 2 * D_MODEL
N_COLS = RNN_COLS + KV_COLS + MG_COLS


def _cparams(sem):
    return pltpu.CompilerParams(dimension_semantics=sem, vmem_limit_bytes=VMEM_LIMIT)


def _rms(xf, g):
    return xf * lax.rsqrt(jnp.mean(xf * xf, axis=-1, keepdims=True) + EPS) * g


def _sigmoid(x):
    return 1.0 / (1.0 + jnp.exp(-x))


def _silu(x):
    return x * _sigmoid(x)


def _dot(a, b):
    return jnp.dot(a, b, preferred_element_type=F32)


def _dot_nt(a, b):
    return lax.dot_general(a, b, (((1,), (1,)), ((), ())), preferred_element_type=F32)


def _dot_tn(a, b):
    return lax.dot_general(a, b, (((0,), (0,)), ((), ())), preferred_element_type=F32)


def _w_prep_kernel(w_ref, wn_ref, wf_ref):
    kvw = G_KV * ATT_HD
    kv0 = RNN_COLS + QT_ROWS
    gate0 = kv0 + 6 * kvw
    mg0 = gate0 + 3 * H_ATT

    def pair_by_group(a0, b0):
        parts = []
        for g in range(G_KV):
            parts += [w_ref[a0 + g * ATT_HD:a0 + (g + 1) * ATT_HD, :], w_ref[b0 + g * ATT_HD:b0 + (g + 1) * ATT_HD, :]]
        return parts

    wn = jnp.concatenate([w_ref[0:RNN_COLS, :]] + pair_by_group(kv0, kv0 + kvw)
                         + pair_by_group(kv0 + 2 * kvw, kv0 + 4 * kvw) + [w_ref[mg0:mg0 + MG_COLS, :]], axis=0)
    wn_ref[...] = wn.T.astype(BF16)
    used = QT_ROWS + VT_ROWS + 3 * H_ATT
    wf = jnp.concatenate([w_ref[RNN_COLS:kv0, :] * (ATT_HD ** -0.5 * LOG2E),
                          w_ref[kv0 + 3 * kvw:kv0 + 4 * kvw, :], w_ref[kv0 + 5 * kvw:kv0 + 6 * kvw, :],
                          w_ref[gate0:mg0, :], jnp.zeros((wf_ref.shape[0] - used, w_ref.shape[1]), F32)], axis=0)
    wf_ref[...] = wf.astype(BF16)


def _w_prep(wt):
    n_in, d = wt.shape
    return pl.pallas_call(
        _w_prep_kernel,
        out_shape=(jax.ShapeDtypeStruct((d, N_COLS), BF16), jax.ShapeDtypeStruct((F_ROWS, d), BF16)),
        grid=(d // LANE,),
        in_specs=[pl.BlockSpec((n_in, LANE), lambda i: (0, i))],
        out_specs=(pl.BlockSpec((LANE, N_COLS), lambda i: (i, 0)),
                   pl.BlockSpec((F_ROWS, LANE), lambda i: (0, i))),
        compiler_params=_cparams(("arbitrary",)),
        name="w_prep",
    )(wt)


def _in_proj_kernel(x_ref, g_ref, w_ref, o_ref, *, sub):
    a = _rms(x_ref[...], g_ref[...]).astype(BF16)
    for n in range(o_ref.shape[1] // sub):
        cols = slice(n * sub, (n + 1) * sub)
        o_ref[:, cols] = _dot(a, w_ref[:, cols]).astype(o_ref.dtype)


def _in_proj(x2, g, w):
    m, d = x2.shape
    n = w.shape[1]
    tn = n // 2
    return pl.pallas_call(
        functools.partial(_in_proj_kernel, sub=512),
        out_shape=jax.ShapeDtypeStruct((m, n), BF16),
        grid=(2, m // TOK_TILE),
        in_specs=[pl.BlockSpec((TOK_TILE, d), lambda j, i: (i, 0)),
                  pl.BlockSpec((1, d), lambda j, i: (0, 0)),
                  pl.BlockSpec((d, tn), lambda j, i: (0, j))],
        out_specs=pl.BlockSpec((TOK_TILE, tn), lambda j, i: (i, j)),
        compiler_params=_cparams(("arbitrary", "arbitrary")),
        name="in_proj",
    )(x2, g, w)


def _in_proj_t_kernel(x_ref, g_ref, wt_ref, qv_ref, gt_ref):
    a = _rms(x_ref[...], g_ref[...]).astype(BF16)
    r = _dot_nt(wt_ref[...], a)
    nqv = qv_ref.shape[1]
    ngt = gt_ref.shape[1]
    for c in range(qv_ref.shape[0]):
        qv_ref[c] = r[:nqv, c * LANE:(c + 1) * LANE].astype(qv_ref.dtype)
        gt_ref[c] = r[nqv:nqv + ngt, c * LANE:(c + 1) * LANE]


def _in_proj_t(x, g, wt):
    b, t, d = x.shape
    rows = wt.shape[0]
    nqv = QT_ROWS + VT_ROWS
    ngt = GATE_ROWS
    nc = TOK_TILE // LANE
    return pl.pallas_call(
        _in_proj_t_kernel,
        out_shape=(jax.ShapeDtypeStruct((b, t // LANE, nqv, LANE), BF16),
                   jax.ShapeDtypeStruct((b, t // LANE, ngt, LANE), F32)),
        grid=(b, t // TOK_TILE),
        in_specs=[pl.BlockSpec((None, TOK_TILE, d), lambda i, j: (i, j, 0)),
                  pl.BlockSpec((1, d), lambda i, j: (0, 0)),
                  pl.BlockSpec((rows, d), lambda i, j: (0, 0))],
        out_specs=(pl.BlockSpec((None, nc, nqv, LANE), lambda i, j: (i, j, 0, 0)),
                   pl.BlockSpec((None, nc, ngt, LANE), lambda i, j: (i, j, 0, 0))),
        compiler_params=_cparams(("arbitrary", "arbitrary")),
        name="in_proj_t",
    )(x, g, wt)


def _hgrn_kernel(q_ref, f_ref, i_ref, og_ref, lb_ref, gn_ref, o_ref, qd_ref, oi_ref, ut_ref, dec_ref):
    c = RNN_CHUNK
    kd = RNN_DIM
    n_chunks = q_ref.shape[0] // c
    lb = lb_ref[...]
    gn = gn_ref[...]
    blk = HGRN_GROUP * c
    row = lax.broadcasted_iota(jnp.int32, (blk, blk), 0)
    col = lax.broadcasted_iota(jnp.int32, (blk, blk), 1)
    same_chunk = (row // c) == (col // c)
    causal = same_chunk & (row >= col)
    tril = causal.astype(BF16)

    q = q_ref[...].astype(F32)
    fl = f_ref[...].astype(F32)
    v = i_ref[...]
    f = lb + (1.0 - lb) * _sigmoid(fl)
    k = 1.0 - f
    logf = jnp.log(f)
    hi = logf.astype(BF16)
    lo = (logf - hi.astype(F32)).astype(BF16)
    pieces = jnp.concatenate([hi, lo], axis=1)
    n_blk = q_ref.shape[0] // blk
    cs = [_dot(tril, pieces[g * blk:(g + 1) * blk]) for g in range(n_blk)]
    bcum = jnp.concatenate([x[:, 0:kd] + x[:, kd:2 * kd] for x in cs], axis=0)
    e_neg = jnp.exp(-bcum)
    dec = jnp.exp(jnp.concatenate([bcum[n * c + c - 1:n * c + c, :] for n in range(n_chunks)], axis=0))
    dec_rows = jnp.concatenate([jnp.broadcast_to(dec[n:n + 1, :], (c, kd)) for n in range(n_chunks)], axis=0)
    q_dec = (_silu(q) * (1.0 / e_neg)).astype(BF16)
    k_neg = k * e_neg
    k_dec = k_neg.astype(BF16)
    k_end = (k_neg * dec_rows).astype(BF16)
    qd_ref[...] = q_dec
    dec_ref[...] = dec
    for g in range(n_blk):
        rows = slice(g * blk, (g + 1) * blk)
        a = jnp.where(causal, _dot_nt(q_dec[rows], k_dec[rows]), 0.0)
        oi_ref[rows, :] = _dot(a.astype(BF16), v[rows])
    for n in range(n_chunks):
        rows = slice(n * c, (n + 1) * c)
        ut_ref[n] = _dot_tn(v[rows], k_end[rows])

    s_t = jnp.zeros((kd, kd), F32)
    for n in range(n_chunks):
        rows = slice(n * c, (n + 1) * c)
        o = oi_ref[rows, :] + _dot_nt(qd_ref[rows, :], s_t.astype(BF16))
        og = og_ref[rows, :].astype(F32)
        o_ref[rows, :] = (_rms(o, gn) * _silu(og)).astype(o_ref.dtype)
        s_t = s_t * dec_ref[n:n + 1, :] + ut_ref[n]


def _hgrn(proj3, lb, gn):
    b, t, _ = proj3.shape
    return pl.pallas_call(
        _hgrn_kernel,
        out_shape=jax.ShapeDtypeStruct((b, t, H_RNN * RNN_DIM), BF16),
        grid=(b, H_RNN),
        in_specs=[pl.BlockSpec((None, t, RNN_DIM), lambda i, h: (i, 0, h)),
                  pl.BlockSpec((None, t, RNN_DIM), lambda i, h: (i, 0, H_RNN + h)),
                  pl.BlockSpec((None, t, RNN_DIM), lambda i, h: (i, 0, 2 * H_RNN + h)),
                  pl.BlockSpec((None, t, RNN_DIM), lambda i, h: (i, 0, 3 * H_RNN + h)),
                  pl.BlockSpec((1, RNN_DIM), lambda i, h: (0, h)),
                  pl.BlockSpec((1, RNN_DIM), lambda i, h: (0, 0))],
        out_specs=pl.BlockSpec((None, t, RNN_DIM), lambda i, h: (i, 0, h)),
        scratch_shapes=[pltpu.VMEM((t, RNN_DIM), BF16),
                        pltpu.VMEM((t, RNN_DIM), F32),
                        pltpu.VMEM((t // RNN_CHUNK, RNN_DIM, RNN_DIM), F32),
                        pltpu.VMEM((t // RNN_CHUNK, RNN_DIM), F32)],
        compiler_params=_cparams(("arbitrary", "arbitrary")),
        name="hgrn",
    )(proj3, proj3, proj3, proj3, lb, gn)


def _compress_kernel(kv_ref, pek_ref, pev_ref, wk1_ref, wv1_ref, wk2_ref, wv2t_ref,
                     kc_ref, vct_ref, xs_ref, xk_ref, xv_ref):
    t = kv_ref.shape[0]
    hd = ATT_HD
    xs_ref[0:t, :] = kv_ref[...].astype(F32)
    xs_ref[t:, :] = jnp.zeros((xs_ref.shape[0] - t, xs_ref.shape[1]), F32)
    for l in range(CMP_LEN):
        blk = xs_ref[pl.ds(l, N_CMP_PAD, stride=CMP_STRIDE), :]
        xk_ref[:, l * hd:(l + 1) * hd] = (blk[:, 0:hd] + pek_ref[l:l + 1, :]).astype(BF16)
        xv_ref[:, l * hd:(l + 1) * hd] = (blk[:, hd:2 * hd] + pev_ref[l:l + 1, :]).astype(BF16)
    hk = _silu(_dot(xk_ref[...], wk1_ref[...])).astype(BF16)
    hv = _silu(_dot(xv_ref[...], wv1_ref[...])).astype(BF16)
    kc_ref[...] = _dot(hk, wk2_ref[...])
    vct_ref[...] = _dot_nt(wv2t_ref[...], hv)


def _compress(proj3, pek, pev, wk1, wv1, wk2, wv2t):
    b, t, _ = proj3.shape
    kv_unit0 = RNN_COLS // LANE
    flat = CMP_LEN * ATT_HD
    full = lambda shape: pl.BlockSpec(shape, lambda i, g: (0,) * len(shape))
    return pl.pallas_call(
        _compress_kernel,
        out_shape=(jax.ShapeDtypeStruct((b, G_KV, N_CMP_PAD, ATT_HD), F32),
                   jax.ShapeDtypeStruct((b, G_KV, ATT_HD, N_CMP_PAD), F32)),
        grid=(b, G_KV),
        in_specs=[pl.BlockSpec((None, t, LANE), lambda i, g: (i, 0, kv_unit0 + g)),
                  full((CMP_LEN, ATT_HD)), full((CMP_LEN, ATT_HD)),
                  full((flat, CMP_HIDDEN)), full((flat, CMP_HIDDEN)),
                  full((CMP_HIDDEN, ATT_HD)), full((ATT_HD, CMP_HIDDEN))],
        out_specs=(pl.BlockSpec((None, None, N_CMP_PAD, ATT_HD), lambda i, g: (i, g, 0, 0)),
                   pl.BlockSpec((None, None, ATT_HD, N_CMP_PAD), lambda i, g: (i, g, 0, 0))),
        scratch_shapes=[pltpu.VMEM((t + CMP_STRIDE, LANE), F32),
                        pltpu.VMEM((N_CMP_PAD, flat), BF16),
                        pltpu.VMEM((N_CMP_PAD, flat), BF16)],
        compiler_params=_cparams(("arbitrary", "arbitrary")),
        name="compress",
    )(proj3, pek, pev, wk1, wv1, wk2, wv2t)


def _cmp_select_kernel(qt_ref, kc_ref, vct_ref, slope_ref, ov_ref, ocmp_ref, selb_ref, used_ref, *, n_cmp, n_sel):
    nct = CMP_TILE // LANE
    tt = CMP_TILE
    ns = ov_ref.shape[0]
    kc = kc_ref[...].astype(BF16)
    vct = vct_ref[...].astype(BF16)
    ov = ov_ref[...]
    j_i = lax.broadcasted_iota(jnp.int32, (8, tt), 0)
    tq_rel = lax.broadcasted_iota(jnp.int32, (8, tt), 1)
    n_tiles = qt_ref.shape[0] // nct

    def ranges(tile):
        t0 = tile * tt
        n_any = min(N_CMP_PAD, (t0 + tt) // CMP_STRIDE)
        n_all = max(0, (t0 - (CMP_LEN - 1)) // CMP_STRIDE + 1) // 8 * 8
        return t0, n_any, n_all, (t0 + tt) // SLC_LEN

    def pad_rows(x):
        if x.shape[0] == N_CMP_PAD:
            return x
        return jnp.concatenate([x, jnp.zeros((N_CMP_PAD - x.shape[0], x.shape[1]), x.dtype)], axis=0)

    scores = []
    for tile in range(n_tiles):
        _, n_any, _, _ = ranges(tile)
        qt = [jnp.concatenate([qt_ref[tile * nct + c, h * ATT_HD:(h + 1) * ATT_HD, :] for c in range(nct)],
                              axis=1) for h in range(HG)]
        scores.append(_dot(kc[:n_any], jnp.concatenate(qt, axis=1)))

    probs, psums = [], []
    for tile in range(n_tiles):
        t0, n_any, n_all, _ = ranges(tile)
        n_b = lax.broadcasted_iota(jnp.int32, (n_any - n_all, tt), 0) + n_all
        t_b = lax.broadcasted_iota(jnp.int32, (n_any - n_all, tt), 1)
        visible = (CMP_STRIDE * n_b + (CMP_LEN - 1) - t0 <= t_b) & (n_b < n_cmp)
        centre = (lax.broadcasted_iota(jnp.int32, (n_any, tt), 0).astype(F32) * float(CMP_STRIDE)
                  + (CMP_LEN - 1) / 2.0)
        psum = jnp.zeros((n_any, tt), F32)
        ph = []
        for h in range(HG):
            s = scores[tile][:, h * tt:(h + 1) * tt] + slope_ref[h] * LOG2E * centre
            band = jnp.where(visible, s[n_all:], -jnp.inf)
            s = band if n_all == 0 else jnp.concatenate([s[:n_all], band], axis=0)
            m = jnp.max(s, axis=0, keepdims=True)
            m = jnp.where(m == -jnp.inf, 0.0, m)
            p = jnp.exp2(s - m)
            d = jnp.sum(p, axis=0, keepdims=True)
            p = p * (1.0 / jnp.where(d > 0, d, 1.0))
            psum = psum + p
            ph.append(pad_rows(p).astype(BF16))
        probs.append(jnp.concatenate(ph, axis=1))
        psums.append(pad_rows(psum))

    for tile in range(n_tiles):
        o = _dot(vct, probs[tile])
        for h in range(HG):
            for c in range(nct):
                lanes = slice(h * tt + c * LANE, h * tt + (c + 1) * LANE)
                ocmp_ref[tile * nct + c, h * ATT_HD:(h + 1) * ATT_HD, :] = o[:, lanes]

    imps = []
    for tile in range(n_tiles):
        psum = psums[tile]
        hi = psum.astype(BF16)
        rem = psum - hi.astype(F32)
        mid = rem.astype(BF16)
        lo = (rem - mid.astype(F32)).astype(BF16)
        imp3 = _dot(ov, jnp.concatenate([hi, mid, lo], axis=1))
        imps.append(imp3[:, 0:tt] + imp3[:, tt:2 * tt] + imp3[:, 2 * tt:3 * tt])

    for tile in range(n_tiles):
        t0, _, _, nvb = ranges(tile)
        imp = imps[tile]
        score = []
        for rg in range(nvb // 8):
            j = j_i + 8 * rg
            tq = tq_rel + t0
            cur = tq // SLC_LEN
            forced = (j == 0) | (j == cur) | (j == cur - 1)
            sc = imp[8 * rg:8 * rg + 8, :] + jnp.where(forced, FORCE_BONUS, 0.0)
            score.append(jnp.where(j * SLC_LEN <= tq, sc, -jnp.inf))
        rank = [jnp.zeros((8, tt), F32) for _ in score]
        for jp in range(nvb):
            sj = jnp.broadcast_to(score[jp // 8][jp % 8:jp % 8 + 1, :], (8, tt))
            for rg in range(nvb // 8):
                ge = jnp.where(sj >= score[rg], 1.0, 0.0)
                gt = jnp.where(sj > score[rg], 1.0, 0.0)
                if 8 * rg > jp:
                    ahead = ge
                elif 8 * rg + 7 <= jp:
                    ahead = gt
                else:
                    ahead = jnp.where(j_i + 8 * rg > jp, ge, gt)
                rank[rg] = rank[rg] + ahead
        for rg in range(ns // 8):
            for c in range(nct):
                if rg < nvb // 8:
                    picked = rank[rg][:, c * LANE:(c + 1) * LANE] < float(n_sel)
                    selb = jnp.where(picked, 0.0, NEG_BIG)
                    used = jnp.max(jnp.where(picked, 1.0, 0.0), axis=1, keepdims=True)
                else:
                    selb = jnp.full((8, LANE), NEG_BIG, F32)
                    used = jnp.zeros((8, 1), F32)
                selb_ref[tile * nct + c, 8 * rg:8 * rg + 8, :] = selb
                used_ref[tile * nct + c, 8 * rg:8 * rg + 8, :] = jnp.broadcast_to(used, (8, LANE))


def _cmp_select(qv, kc, vct, slopes, overlap_t, t):
    b = qv.shape[0]
    ns = t // SLC_LEN
    nch = t // LANE
    n_cmp = (t - CMP_LEN) // CMP_STRIDE + 1
    n_sel = min(SLC_TOPK, ns)
    rows = HG * ATT_HD
    return pl.pallas_call(
        functools.partial(_cmp_select_kernel, n_cmp=n_cmp, n_sel=n_sel),
        out_shape=(jax.ShapeDtypeStruct((b, nch, H_ATT * ATT_HD, LANE), F32),
                   jax.ShapeDtypeStruct((b, G_KV, nch, ns, LANE), F32),
                   jax.ShapeDtypeStruct((b, G_KV, nch, ns, LANE), F32)),
        grid=(b, G_KV),
        in_specs=[pl.BlockSpec((None, nch, rows, LANE), lambda i, g: (i, 0, g, 0)),
                  pl.BlockSpec((None, None, N_CMP_PAD, ATT_HD), lambda i, g: (i, g, 0, 0)),
                  pl.BlockSpec((None, None, ATT_HD, N_CMP_PAD), lambda i, g: (i, g, 0, 0)),
                  pl.BlockSpec((None, HG, 1, 1), lambda i, g: (g, 0, 0, 0)),
                  pl.BlockSpec((ns, N_CMP_PAD), lambda i, g: (0, 0))],
        out_specs=(pl.BlockSpec((None, nch, rows, LANE), lambda i, g: (i, 0, g, 0)),
                   pl.BlockSpec((None, None, nch, ns, LANE), lambda i, g: (i, g, 0, 0, 0)),
                   pl.BlockSpec((None, None, nch, ns, LANE), lambda i, g: (i, g, 0, 0, 0))),
        compiler_params=_cparams(("arbitrary", "arbitrary")),
        name="cmp_select",
    )(qv, kc, vct, slopes, overlap_t)


def _sparse_kernel(need_ref, qt_ref, k_ref, ka_ref, vs_ref, vw_ref, gt_ref, ocmp_ref, selb_ref, slope_ref,
                   o_ref, msk_ref, qop_ref, s_ref, m_ref, acc_ref):
    tq = ATT_TILE
    kc = ATT_TILE
    hd = ATT_HD
    nl = HG * tq
    n_tiles = qt_ref.shape[0]
    ns = selb_ref.shape[1]
    wch = WINDOW // kc
    n_sel_items = 2 * (SEL_NEAR + 1)
    need_base = (pl.program_id(0) * pl.num_programs(1) + pl.program_id(1)) * (n_tiles * n_tiles)
    slope = jnp.concatenate([jnp.broadcast_to(slope_ref[h], (1, tq)) for h in range(HG)], axis=1) * LOG2E
    s_hi = slope.astype(BF16).astype(F32)
    s_mid = (slope - s_hi).astype(BF16).astype(F32)
    s_lo = (slope - s_hi - s_mid).astype(BF16).astype(F32)
    slope_rows = jnp.concatenate([s_hi, s_mid, s_lo, jnp.zeros((AUG_SEL_ROW0 - 3, nl), F32)], axis=0)
    aug_tail = jnp.zeros((LANE - AUG_SEL_ROW0 - ns, nl), F32)
    aug_win = jnp.concatenate([slope_rows, jnp.zeros((ns, nl), F32), aug_tail], axis=0).astype(BF16)
    k_i = lax.broadcasted_iota(jnp.int32, (kc, nl), 0)
    t_i = lax.broadcasted_iota(jnp.int32, (kc, nl), 1) % tq
    d0 = t_i - k_i
    msk_ref[0] = jnp.where(d0 >= 0, 0.0, NEG_BIG)
    msk_ref[1] = jnp.where(d0 < 0, 0.0, NEG_BIG)
    chunk_bias = slope * float(kc)
    ones = jnp.ones((ONES_ROWS, kc), BF16)

    n_pairs = n_tiles // 2
    n_items = n_sel_items + 2 * (wch + 1)
    n_slots = s_ref.shape[0] // 2

    def tiles_of(p):
        return (p, n_tiles - 1 - p)

    def sel_item(p, w, j):
        tile = tiles_of(p)[w]
        if j == SEL_NEAR:
            return 0, tile >= SEL_NEAR, tile
        return jnp.maximum(tile - j, 0), tile - j >= 0, j

    def win_item(p, w, j):
        chunk = tiles_of(p)[w] - wch + j
        return jnp.maximum(chunk, 0), chunk >= 0, wch - j

    def col_max(r, delta_f):
        r8 = jnp.max(r.reshape(kc // 8, 8, nl), axis=0)
        return r8 - chunk_bias * delta_f

    def scores(chunk, op):
        rows = pl.ds(pl.multiple_of(chunk * kc, kc), kc)
        keys = jnp.concatenate([k_ref[rows, :], ka_ref[rows, :]], axis=1)
        return _dot(keys, qop_ref[op])

    def build(p, slot):
        for w, tile in enumerate(tiles_of(p)):
            qt = qt_ref[tile]
            q_all = jnp.concatenate([qt[h * hd:(h + 1) * hd, :] for h in range(HG)], axis=1)
            zero = jnp.zeros_like(q_all)
            selb = jnp.concatenate([selb_ref[tile]] * HG, axis=1)
            aug_sel = jnp.concatenate([slope_rows, selb, aug_tail], axis=0).astype(BF16)
            qop_ref[4 * slot + 2 * w] = jnp.concatenate([q_all, zero, aug_sel], axis=0)
            qop_ref[4 * slot + 2 * w + 1] = jnp.concatenate([zero, q_all, aug_win], axis=0)
        for x in range(4):
            m_ref[4 * slot + x] = jnp.full((8, nl), NEG_BIG, F32)

    def phase_a(p, slot):
        items = []

        def sel(w, j):
            chunk, valid, delta = sel_item(p, w, j)
            r = scores(chunk, 4 * slot + 2 * w)
            if j == 0:
                r = r + msk_ref[0]
            s_ref[slot * n_slots + w * (SEL_NEAR + 1) + j] = r
            cm = jnp.where(valid, col_max(r, jnp.float32(delta)), NEG_BIG)
            x = 4 * slot + 2 * w
            m_ref[x] = jnp.maximum(m_ref[x], cm)

        def win(w, j):
            chunk, valid, delta = win_item(p, w, j)
            r = scores(chunk, 4 * slot + 2 * w + 1)
            if j == 0:
                r = r + msk_ref[1]
            elif j == wch:
                r = r + msk_ref[0]
            s_ref[slot * n_slots + n_sel_items + w * (wch + 1) + j] = r
            cm = jnp.where(valid, col_max(r, float(delta)), NEG_BIG)
            x = 4 * slot + 2 * w + 1
            m_ref[x] = jnp.maximum(m_ref[x], cm)

        for w in range(2):
            for j in range(SEL_NEAR + 1):
                items.append(functools.partial(sel, w, j))
        for w in range(2):
            for j in range(wch + 1):
                items.append(functools.partial(win, w, j))
        return items

    def overflow(p, slot, visit):
        count = 0
        for w, tile in enumerate(tiles_of(p)):
            def body(c, n, w=w, tile=tile):
                flag = need_ref[need_base + tile * n_tiles + c]

                @pl.when(flag != 0)
                def _():
                    visit(w, tile, c, n)
                return n + flag

            count = lax.fori_loop(1, tile - SEL_NEAR + 1, body, count)

    def overflow_a(p, slot):
        def visit(w, tile, c, n):
            r = scores(c, 4 * slot + 2 * w)
            s_ref[slot * n_slots + n_items + n] = r
            x = 4 * slot + 2 * w
            m_ref[x] = jnp.maximum(m_ref[x], col_max(r, (tile - c).astype(F32)))

        overflow(p, slot, visit)

    def overflow_b(p, slot, m_row):
        def visit(w, tile, c, n):
            row = m_row[2 * w] + chunk_bias * (tile - c).astype(F32)
            pr = jnp.exp2(s_ref[slot * n_slots + n_items + n] - row).astype(BF16)
            v_aug = jnp.concatenate([vs_ref[c], ones], axis=0)
            acc_ref[4 * slot + 2 * w] += _dot(v_aug, pr)

        overflow(p, slot, visit)

    def phase_b(p, slot, m_row):
        for x in range(4):
            acc_ref[4 * slot + x] = jnp.zeros(acc_ref.shape[1:], F32)
        items = []

        def sel(w, j):
            chunk, valid, delta = sel_item(p, w, j)
            row = jnp.where(valid, m_row[2 * w] + chunk_bias * jnp.float32(delta), -NEG_BIG)
            pr = jnp.exp2(s_ref[slot * n_slots + w * (SEL_NEAR + 1) + j] - row).astype(BF16)
            v_aug = jnp.concatenate([vs_ref[chunk], ones], axis=0)
            acc_ref[4 * slot + 2 * w] += _dot(v_aug, pr)

        def win(w, j):
            chunk, valid, delta = win_item(p, w, j)
            row = jnp.where(valid, m_row[2 * w + 1] + chunk_bias * float(delta), -NEG_BIG)
            pr = jnp.exp2(s_ref[slot * n_slots + n_sel_items + w * (wch + 1) + j] - row).astype(BF16)
            v_aug = jnp.concatenate([vw_ref[chunk], ones], axis=0)
            acc_ref[4 * slot + 2 * w + 1] += _dot(v_aug, pr)

        for w in range(2):
            for j in range(SEL_NEAR + 1):
                items.append(functools.partial(sel, w, j))
        for w in range(2):
            for j in range(wch + 1):
                items.append(functools.partial(win, w, j))
        return items

    def finalize(p, slot):
        for w, tile in enumerate(tiles_of(p)):
            a_s = acc_ref[4 * slot + 2 * w]
            a_w = acc_ref[4 * slot + 2 * w + 1]
            o_slc = a_s[:hd, :] / a_s[hd:hd + 1, :]
            o_win = a_w[:hd, :] / a_w[hd:hd + 1, :]
            oc = ocmp_ref[tile]
            outs = []
            for h in range(HG):
                lanes = slice(h * tq, (h + 1) * tq)
                row0 = (pl.program_id(1) * HG + h) * 3
                g_cmp, g_slc, g_win = [_sigmoid(gt_ref[tile, pl.ds(row0 + br, 1), :]) for br in range(3)]
                y = (g_cmp * oc[h * hd:(h + 1) * hd, :] + g_slc * o_slc[:, lanes]
                     + g_win * o_win[:, lanes])
                outs.append(y.T)
            rows = pl.ds(pl.multiple_of(tile * tq, tq), tq)
            o_ref[rows, :] = jnp.concatenate(outs, axis=1).astype(o_ref.dtype)

    def step(k, slot, first=False, last=False):
        m_row = [jnp.max(m_ref[4 * slot + x], axis=0, keepdims=True) for x in range(4)]
        if not last:
            build(k + 1, 1 - slot)
        if not first:
            finalize(k - 1, 1 - slot)
        b_items = phase_b(k, slot, m_row)
        a_items = phase_a(k + 1, 1 - slot) if not last else [None] * n_items
        for a, b in zip(a_items, b_items):
            if a is not None:
                a()
            b()
        if not last:
            overflow_a(k + 1, 1 - slot)
        overflow_b(k, slot, m_row)

    build(0, 0)
    for a in phase_a(0, 0):
        a()
    overflow_a(0, 0)
    step(0, 0, first=True)

    def body(kk, carry):
        step(2 * kk + 1, 1)
        step(2 * kk + 2, 0)
        return carry

    lax.fori_loop(0, (n_pairs - 2) // 2, body, 0)
    step(n_pairs - 1, (n_pairs - 1) % 2, last=True)
    finalize(n_pairs - 1, (n_pairs - 1) % 2)


def _key_features(t):
    kp = np.arange(t)
    f = np.zeros((t, LANE), np.float32)
    f[:, 0:3] = (kp % ATT_TILE)[:, None]
    f[kp, AUG_SEL_ROW0 + kp // SLC_LEN] = 1.0
    return jnp.asarray(f, dtype=BF16)


def _sparse(need, qv, gt, proj3, ocmp, selb, slopes, t):
    b = qv.shape[0]
    nchunk = t // LANE
    ns = t // SLC_LEN
    rows = HG * ATT_HD
    ksw_unit0 = RNN_COLS // LANE + G_KV
    vs_blk0 = QT_ROWS // ATT_HD
    vw_blk0 = vs_blk0 + G_KV
    nl = HG * ATT_TILE
    n_slots = 2 * (SEL_NEAR + 1) + 2 * (WINDOW // ATT_TILE + 1) + (nchunk - 1 - SEL_NEAR)
    grid_spec = pltpu.PrefetchScalarGridSpec(
        num_scalar_prefetch=1,
        grid=(b, G_KV),
        in_specs=[pl.BlockSpec((None, nchunk, rows, LANE), lambda i, g, nd: (i, 0, g, 0)),
                  pl.BlockSpec((None, t, LANE), lambda i, g, nd: (i, 0, ksw_unit0 + g)),
                  pl.BlockSpec((t, LANE), lambda i, g, nd: (0, 0)),
                  pl.BlockSpec((None, nchunk, ATT_HD, LANE), lambda i, g, nd: (i, 0, vs_blk0 + g, 0)),
                  pl.BlockSpec((None, nchunk, ATT_HD, LANE), lambda i, g, nd: (i, 0, vw_blk0 + g, 0)),
                  pl.BlockSpec((None, nchunk, GATE_ROWS, LANE), lambda i, g, nd: (i, 0, 0, 0)),
                  pl.BlockSpec((None, nchunk, rows, LANE), lambda i, g, nd: (i, 0, g, 0)),
                  pl.BlockSpec((None, None, nchunk, ns, LANE), lambda i, g, nd: (i, g, 0, 0, 0)),
                  pl.BlockSpec((None, HG, 1, 1), lambda i, g, nd: (g, 0, 0, 0))],
        out_specs=pl.BlockSpec((None, t, rows), lambda i, g, nd: (i, 0, g)),
        scratch_shapes=[pltpu.VMEM((2, ATT_TILE, nl), F32),
                        pltpu.VMEM((2 * 4, 2 * LANE, nl), BF16),
                        pltpu.VMEM((2 * n_slots, ATT_TILE, nl), F32),
                        pltpu.VMEM((2 * 4, 8, nl), F32),
                        pltpu.VMEM((2 * 4, ATT_HD + ONES_ROWS, nl), F32)])
    return pl.pallas_call(
        _sparse_kernel,
        out_shape=jax.ShapeDtypeStruct((b, t, H_ATT * ATT_HD), BF16),
        grid_spec=grid_spec,
        compiler_params=_cparams(("arbitrary", "arbitrary")),
        name="sparse",
    )(need, qv, proj3, _key_features(t), qv, qv, gt, ocmp, selb, slopes)


def _merge_kernel(x_ref, yr_ref, ya_ref, mgr_ref, mga_ref, wr_ref, wa_ref, wo_ref, o_ref):
    pr = _dot(yr_ref[...], wr_ref[...])
    pa = _dot(ya_ref[...], wa_ref[...])
    merged = _sigmoid(mgr_ref[...].astype(F32)) * pr + _sigmoid(mga_ref[...].astype(F32)) * pa
    o_ref[...] = x_ref[...] + _dot(merged.astype(BF16), wo_ref[...])


def _merge(x2, yr, ya, proj, wr, wa, wo):
    m, d = x2.shape
    mg0 = (RNN_COLS + KV_COLS) // d
    tile = lambda col: pl.BlockSpec((TOK_TILE, d), lambda i: (i, col))
    wfull = pl.BlockSpec((d, d), lambda i: (0, 0))
    return pl.pallas_call(
        _merge_kernel,
        out_shape=jax.ShapeDtypeStruct((m, d), F32),
        grid=(m // TOK_TILE,),
        in_specs=[tile(0), tile(0), tile(0), tile(mg0), tile(mg0 + 1), wfull, wfull, wfull],
        out_specs=tile(0),
        compiler_params=_cparams(("arbitrary",)),
        name="merge",
    )(x2, yr, ya, proj, proj, wr, wa, wo)


def _mem_kv_kernel(mem_ref, g_ref, wkt_ref, wv_ref, kt_ref, v_ref):
    a = _rms(mem_ref[...], g_ref[...]).astype(BF16)
    kt_ref[...] = _dot_nt(wkt_ref[...], a).astype(kt_ref.dtype)
    v_ref[...] = _dot(a, wv_ref[...]).astype(v_ref.dtype)


def _mem_kv(mem, g, wkt, wkv):
    b, nm, d = mem.shape
    hw = H_X * X_HD
    return pl.pallas_call(
        _mem_kv_kernel,
        out_shape=(jax.ShapeDtypeStruct((b, hw, nm), BF16),
                   jax.ShapeDtypeStruct((b, nm, hw), BF16)),
        grid=(b,),
        in_specs=[pl.BlockSpec((None, nm, d), lambda i: (i, 0, 0)),
                  pl.BlockSpec((1, d), lambda i: (0, 0)),
                  pl.BlockSpec((hw, d), lambda i: (0, 0)),
                  pl.BlockSpec((d, hw), lambda i: (0, 1))],
        out_specs=(pl.BlockSpec((None, hw, nm), lambda i: (i, 0, 0)),
                   pl.BlockSpec((None, nm, hw), lambda i: (i, 0, 0))),
        compiler_params=_cparams(("arbitrary",)),
        name="mem_kv",
    )(mem, g, wkt, wkv)


def _xattn_kernel(h_ref, g_ref, wq_ref, kt_ref, v_ref, wo_ref, o_ref):
    h = h_ref[...]
    a = _rms(h, g_ref[...]).astype(BF16)
    q = (_dot(a, wq_ref[...]) * (X_HD ** -0.5)).astype(BF16)
    heads = [slice(hh * X_HD, (hh + 1) * X_HD) for hh in range(H_X)]
    scores = [_dot(q[:, cols], kt_ref[cols, :]) for cols in heads]
    probs = []
    for s in scores:
        m = jnp.max(s, axis=-1, keepdims=True)
        p = jnp.exp(s - m)
        probs.append((p / jnp.sum(p, axis=-1, keepdims=True)).astype(BF16))
    outs = [_dot(p, v_ref[:, cols]) for p, cols in zip(probs, heads)]
    o = jnp.concatenate(outs, axis=1).astype(BF16)
    o_ref[...] = h + _dot(o, wo_ref[...])


def _xattn(h3, g, wq, kt, v, wo):
    b, t, d = h3.shape
    hw = H_X * X_HD
    nm = v.shape[1]
    return pl.pallas_call(
        _xattn_kernel,
        out_shape=jax.ShapeDtypeStruct((b, t, d), F32),
        grid=(b, t // TOK_TILE),
        in_specs=[pl.BlockSpec((None, TOK_TILE, d), lambda i, j: (i, j, 0)),
                  pl.BlockSpec((1, d), lambda i, j: (0, 0)),
                  pl.BlockSpec((d, hw), lambda i, j: (0, 0)),
                  pl.BlockSpec((None, hw, nm), lambda i, j: (i, 0, 0)),
                  pl.BlockSpec((None, nm, hw), lambda i, j: (i, 0, 0)),
                  pl.BlockSpec((hw, d), lambda i, j: (0, 0))],
        out_specs=pl.BlockSpec((None, TOK_TILE, d), lambda i, j: (i, j, 0)),
        compiler_params=_cparams(("arbitrary", "arbitrary")),
        name="xattn",
    )(h3, g, wq, kt, v, wo)


def _ffn_kernel(h_ref, g_ref, wg_ref, wu_ref, wd_ref, gf_ref, o_ref, *, fc):
    h = h_ref[...]
    a = _rms(h, g_ref[...]).astype(BF16)
    acc = h
    for f in range(wg_ref.shape[1] // fc):
        cols = slice(f * fc, (f + 1) * fc)
        mid = _silu(_dot(a, wg_ref[:, cols])) * _dot(a, wu_ref[:, cols])
        acc = acc + _dot(mid.astype(BF16), wd_ref[cols, :])
    o_ref[...] = _rms(acc, gf_ref[...])


def _ffn(h2, g, wgu, wd, gf):
    m, d = h2.shape
    ff = wd.shape[0]
    const = lambda shape: pl.BlockSpec(shape, lambda i: (0, 0))
    return pl.pallas_call(
        functools.partial(_ffn_kernel, fc=ff // 2),
        out_shape=jax.ShapeDtypeStruct((m, d), F32),
        grid=(m // TOK_TILE,),
        in_specs=[pl.BlockSpec((TOK_TILE, d), lambda i: (i, 0)),
                  const((1, d)), const((d, ff)), pl.BlockSpec((d, ff), lambda i: (0, 1)),
                  const((ff, d)), const((1, d))],
        out_specs=pl.BlockSpec((TOK_TILE, d), lambda i: (i, 0)),
        compiler_params=_cparams(("arbitrary",)),
        name="ffn",
    )(h2, g, wgu, wgu, wd, gf)


def _overlap_t(t):
    nc = (t - CMP_LEN) // CMP_STRIDE + 1
    ns = t // SLC_LEN
    starts = CMP_STRIDE * np.arange(N_CMP_PAD)
    s_start = SLC_LEN * np.arange(ns)
    ov = ((starts[None, :] + CMP_LEN > s_start[:, None]) & (starts[None, :] < s_start[:, None] + SLC_LEN)
          & (np.arange(N_CMP_PAD)[None, :] < nc))
    return jnp.asarray(ov.astype(np.float32), dtype=BF16)


def kernel(x, mem, g_mix, w_in, lower_bounds, g_rnn_out, pe_ck, w_ck1, w_ck2, pe_cv, w_cv1, w_cv2,
           w_proj_rnn, w_proj_att, w_out, g_xattn, g_mem, w_xq, w_xkv, w_xo, g_ffn, w_gate_up,
           w_down, g_final):
    b, t, d = x.shape
    depth = g_mix.shape[0]
    assert depth == 1, "the final RMSNorm is fused into the layer's FFN kernel"
    lbs = jnp.cumsum(jax.nn.softmax(lower_bounds.astype(F32), axis=0), axis=0)
    slopes = (2.0 ** (-8.0 * jnp.arange(1, H_ATT + 1, dtype=F32) / H_ATT)).reshape(G_KV, HG, 1, 1)
    overlap_t = _overlap_t(t)
    h = x
    for l in range(depth):
        w_n, w_t = _w_prep(w_in[l].T)
        x2 = h.reshape(b * t, d)
        proj = _in_proj(x2, g_mix[l][None, :], w_n)
        proj3 = proj.reshape(b, t, N_COLS)
        qv, gt = _in_proj_t(h, g_mix[l][None, :], w_t)
        y_r = _hgrn(proj3, lbs[l][None, :], g_rnn_out[l][None, :])
        kc, vct = _compress(proj3, pe_ck[l], pe_cv[l], w_ck1[l].astype(BF16), w_cv1[l].astype(BF16),
                            w_ck2[l].astype(BF16), w_cv2[l].T.astype(BF16))
        ocmp, selb, used = _cmp_select(qv, kc, vct, slopes, overlap_t, t)
        blocks_per_chunk = ATT_TILE // SLC_LEN
        need = used[..., 0].reshape(b, G_KV, t // ATT_TILE, t // ATT_TILE, blocks_per_chunk).max(axis=-1)
        y_a = _sparse((need > 0.5).astype(jnp.int32).reshape(-1), qv, gt, proj3, ocmp, selb, slopes, t)
        h1 = _merge(x2, y_r.reshape(b * t, d), y_a.reshape(b * t, d), proj,
                    w_proj_rnn[l].astype(BF16), w_proj_att[l].astype(BF16), w_out[l].astype(BF16))
        w_kv = w_xkv[l].astype(BF16)
        kt, v = _mem_kv(mem, g_mem[l][None, :], w_kv[:, :H_X * X_HD].T, w_kv)
        h2 = _xattn(h1.reshape(b, t, d), g_xattn[l][None, :], w_xq[l].astype(BF16), kt, v,
                    w_xo[l].astype(BF16))
        h = _ffn(h2.reshape(b * t, d), g_ffn[l][None, :], w_gate_up[l].astype(BF16),
                 w_down[l].astype(BF16), g_final[None, :]).reshape(b, t, d)
    return h
```

```python
import functools

import jax
import jax.numpy as jnp
import numpy as np
from jax import lax
from jax.experimental import pallas as pl
from jax.experimental.pallas import tpu as pltpu

F32 = jnp.float32
BF16 = jnp.bfloat16

D_MODEL = 1024
N_MEM = 256
H_RNN = 8
RNN_DIM = 128
RNN_CHUNK = 64
H_ATT = 16
ATT_HD = 64
G_KV = 4
HG = H_ATT // G_KV
CMP_LEN = 32
CMP_STRIDE = 16
CMP_HIDDEN = 128
SLC_LEN = 64
SLC_TOPK = 8
WINDOW = 512
FORCE_BONUS = 1.0e4
H_X = 4
X_HD = 128
D_FF = 2816
EPS = 1e-6

LANE = 128
VMEM_LIMIT = 56 * 1024 * 1024
TOK_TILE = 512
ATT_TILE = 128
CMP_TILE = 512
N_CMP_PAD = 128
HGRN_GROUP = 4
ONES_ROWS = 16
NEG_BIG = -1.0e30
LOG2E = 1.4426950408889634
AUG_SEL_ROW0 = 8
SEL_NEAR = 5

QT_ROWS = H_ATT * ATT_HD
VT_ROWS = 2 * G_KV * ATT_HD
GATE_ROWS = 64
F_ROWS = -(-(QT_ROWS + VT_ROWS + 3 * H_ATT) // LANE) * LANE
RNN_COLS = 4 * H_RNN * RNN_DIM
KV_COLS = 4 * G_KV * ATT_HD
MG_COLS = 2 * D_MODEL
N_COLS = RNN_COLS + KV_COLS + MG_COLS


def _cparams(sem):
    return pltpu.CompilerParams(dimension_semantics=sem, vmem_limit_bytes=VMEM_LIMIT)


def _rms(xf, g):
    return xf * lax.rsqrt(jnp.mean(xf * xf, axis=-1, keepdims=True) + EPS) * g


def _sigmoid(x):
    return 1.0 / (1.0 + jnp.exp(-x))


def _silu(x):
    return x * _sigmoid(x)


def _dot(a, b):
    return jnp.dot(a, b, preferred_element_type=F32)


def _dot_nt(a, b):
    return lax.dot_general(a, b, (((1,), (1,)), ((), ())), preferred_element_type=F32)


def _dot_tn(a, b):
    return lax.dot_general(a, b, (((0,), (0,)), ((), ())), preferred_element_type=F32)


def _w_prep_kernel(w_ref, wn_ref, wf_ref):
    kvw = G_KV * ATT_HD
    kv0 = RNN_COLS + QT_ROWS
    gate0 = kv0 + 6 * kvw
    mg0 = gate0 + 3 * H_ATT

    def pair_by_group(a0, b0):
        parts = []
        for g in range(G_KV):
            parts += [w_ref[a0 + g * ATT_HD:a0 + (g + 1) * ATT_HD, :], w_ref[b0 + g * ATT_HD:b0 + (g + 1) * ATT_HD, :]]
        return parts

    wn = jnp.concatenate([w_ref[0:RNN_COLS, :]] + pair_by_group(kv0, kv0 + kvw)
                         + pair_by_group(kv0 + 2 * kvw, kv0 + 4 * kvw) + [w_ref[mg0:mg0 + MG_COLS, :]], axis=0)
    wn_ref[...] = wn.T.astype(BF16)
    used = QT_ROWS + VT_ROWS + 3 * H_ATT
    wf = jnp.concatenate([w_ref[RNN_COLS:kv0, :] * (ATT_HD ** -0.5 * LOG2E),
                          w_ref[kv0 + 3 * kvw:kv0 + 4 * kvw, :], w_ref[kv0 + 5 * kvw:kv0 + 6 * kvw, :],
                          w_ref[gate0:mg0, :], jnp.zeros((wf_ref.shape[0] - used, w_ref.shape[1]), F32)], axis=0)
    wf_ref[...] = wf.astype(BF16)


def _w_prep(wt):
    n_in, d = wt.shape
    return pl.pallas_call(
        _w_prep_kernel,
        out_shape=(jax.ShapeDtypeStruct((d, N_COLS), BF16), jax.ShapeDtypeStruct((F_ROWS, d), BF16)),
        grid=(d // LANE,),
        in_specs=[pl.BlockSpec((n_in, LANE), lambda i: (0, i))],
        out_specs=(pl.BlockSpec((LANE, N_COLS), lambda i: (i, 0)),
                   pl.BlockSpec((F_ROWS, LANE), lambda i: (0, i))),
        compiler_params=_cparams(("arbitrary",)),
        name="w_prep",
    )(wt)


def _in_proj_kernel(x_ref, g_ref, w_ref, o_ref, *, sub):
    a = _rms(x_ref[...], g_ref[...]).astype(BF16)
    for n in range(o_ref.shape[1] // sub):
        cols = slice(n * sub, (n + 1) * sub)
        o_ref[:, cols] = _dot(a, w_ref[:, cols]).astype(o_ref.dtype)


def _in_proj(x2, g, w):
    m, d = x2.shape
    n = w.shape[1]
    tn = n // 2
    return pl.pallas_call(
        functools.partial(_in_proj_kernel, sub=512),
        out_shape=jax.ShapeDtypeStruct((m, n), BF16),
        grid=(2, m // TOK_TILE),
        in_specs=[pl.BlockSpec((TOK_TILE, d), lambda j, i: (i, 0)),
                  pl.BlockSpec((1, d), lambda j, i: (0, 0)),
                  pl.BlockSpec((d, tn), lambda j, i: (0, j))],
        out_specs=pl.BlockSpec((TOK_TILE, tn), lambda j, i: (i, j)),
        compiler_params=_cparams(("arbitrary", "arbitrary")),
        name="in_proj",
    )(x2, g, w)


def _in_proj_t_kernel(x_ref, g_ref, wt_ref, qv_ref, gt_ref):
    a = _rms(x_ref[...], g_ref[...]).astype(BF16)
    r = _dot_nt(wt_ref[...], a)
    nqv = qv_ref.shape[1]
    ngt = gt_ref.shape[1]
    for c in range(qv_ref.shape[0]):
        qv_ref[c] = r[:nqv, c * LANE:(c + 1) * LANE].astype(qv_ref.dtype)
        gt_ref[c] = r[nqv:nqv + ngt, c * LANE:(c + 1) * LANE]


def _in_proj_t(x, g, wt):
    b, t, d = x.shape
    rows = wt.shape[0]
    nqv = QT_ROWS + VT_ROWS
    ngt = GATE_ROWS
    nc = TOK_TILE // LANE
    return pl.pallas_call(
        _in_proj_t_kernel,
        out_shape=(jax.ShapeDtypeStruct((b, t // LANE, nqv, LANE), BF16),
                   jax.ShapeDtypeStruct((b, t // LANE, ngt, LANE), F32)),
        grid=(b, t // TOK_TILE),
        in_specs=[pl.BlockSpec((None, TOK_TILE, d), lambda i, j: (i, j, 0)),
                  pl.BlockSpec((1, d), lambda i, j: (0, 0)),
                  pl.BlockSpec((rows, d), lambda i, j: (0, 0))],
        out_specs=(pl.BlockSpec((None, nc, nqv, LANE), lambda i, j: (i, j, 0, 0)),
                   pl.BlockSpec((None, nc, ngt, LANE), lambda i, j: (i, j, 0, 0))),
        compiler_params=_cparams(("arbitrary", "arbitrary")),
        name="in_proj_t",
    )(x, g, wt)


def _hgrn_kernel(q_ref, f_ref, i_ref, og_ref, lb_ref, gn_ref, o_ref, qd_ref, oi_ref, ut_ref, dec_ref):
    c = RNN_CHUNK
    kd = RNN_DIM
    n_chunks = q_ref.shape[0] // c
    lb = lb_ref[...]
    gn = gn_ref[...]
    blk = HGRN_GROUP * c
    row = lax.broadcasted_iota(jnp.int32, (blk, blk), 0)
    col = lax.broadcasted_iota(jnp.int32, (blk, blk), 1)
    same_chunk = (row // c) == (col // c)
    causal = same_chunk & (row >= col)
    tril = causal.astype(BF16)

    q = q_ref[...].astype(F32)
    fl = f_ref[...].astype(F32)
    v = i_ref[...]
    f = lb + (1.0 - lb) * _sigmoid(fl)
    k = 1.0 - f
    logf = jnp.log(f)
    hi = logf.astype(BF16)
    lo = (logf - hi.astype(F32)).astype(BF16)
    pieces = jnp.concatenate([hi, lo], axis=1)
    n_blk = q_ref.shape[0] // blk
    cs = [_dot(tril, pieces[g * blk:(g + 1) * blk]) for g in range(n_blk)]
    bcum = jnp.concatenate([x[:, 0:kd] + x[:, kd:2 * kd] for x in cs], axis=0)
    e_neg = jnp.exp(-bcum)
    dec = jnp.exp(jnp.concatenate([bcum[n * c + c - 1:n * c + c, :] for n in range(n_chunks)], axis=0))
    dec_rows = jnp.concatenate([jnp.broadcast_to(dec[n:n + 1, :], (c, kd)) for n in range(n_chunks)], axis=0)
    q_dec = (_silu(q) * (1.0 / e_neg)).astype(BF16)
    k_neg = k * e_neg
    k_dec = k_neg.astype(BF16)
    k_end = (k_neg * dec_rows).astype(BF16)
    qd_ref[...] = q_dec
    dec_ref[...] = dec
    for g in range(n_blk):
        rows = slice(g * blk, (g + 1) * blk)
        a = jnp.where(causal, _dot_nt(q_dec[rows], k_dec[rows]), 0.0)
        oi_ref[rows, :] = _dot(a.astype(BF16), v[rows])
    for n in range(n_chunks):
        rows = slice(n * c, (n + 1) * c)
        ut_ref[n] = _dot_tn(v[rows], k_end[rows])

    s_t = jnp.zeros((kd, kd), F32)
    for n in range(n_chunks):
        rows = slice(n * c, (n + 1) * c)
        o = oi_ref[rows, :] + _dot_nt(qd_ref[rows, :], s_t.astype(BF16))
        og = og_ref[rows, :].astype(F32)
        o_ref[rows, :] = (_rms(o, gn) * _silu(og)).astype(o_ref.dtype)
        s_t = s_t * dec_ref[n:n + 1, :] + ut_ref[n]


def _hgrn(proj3, lb, gn):
    b, t, _ = proj3.shape
    return pl.pallas_call(
        _hgrn_kernel,
        out_shape=jax.ShapeDtypeStruct((b, t, H_RNN * RNN_DIM), BF16),
        grid=(b, H_RNN),
        in_specs=[pl.BlockSpec((None, t, RNN_DIM), lambda i, h: (i, 0, h)),
                  pl.BlockSpec((None, t, RNN_DIM), lambda i, h: (i, 0, H_RNN + h)),
                  pl.BlockSpec((None, t, RNN_DIM), lambda i, h: (i, 0, 2 * H_RNN + h)),
                  pl.BlockSpec((None, t, RNN_DIM), lambda i, h: (i, 0, 3 * H_RNN + h)),
                  pl.BlockSpec((1, RNN_DIM), lambda i, h: (0, h)),
                  pl.BlockSpec((1, RNN_DIM), lambda i, h: (0, 0))],
        out_specs=pl.BlockSpec((None, t, RNN_DIM), lambda i, h: (i, 0, h)),
        scratch_shapes=[pltpu.VMEM((t, RNN_DIM), BF16),
                        pltpu.VMEM((t, RNN_DIM), F32),
                        pltpu.VMEM((t // RNN_CHUNK, RNN_DIM, RNN_DIM), F32),
                        pltpu.VMEM((t // RNN_CHUNK, RNN_DIM), F32)],
        compiler_params=_cparams(("arbitrary", "arbitrary")),
        name="hgrn",
    )(proj3, proj3, proj3, proj3, lb, gn)


def _compress_kernel(kv_ref, pek_ref, pev_ref, wk1_ref, wv1_ref, wk2_ref, wv2t_ref,
                     kc_ref, vct_ref, xs_ref, xk_ref, xv_ref):
    t = kv_ref.shape[0]
    hd = ATT_HD
    xs_ref[0:t, :] = kv_ref[...].astype(F32)
    xs_ref[t:, :] = jnp.zeros((xs_ref.shape[0] - t, xs_ref.shape[1]), F32)
    for l in range(CMP_LEN):
        blk = xs_ref[pl.ds(l, N_CMP_PAD, stride=CMP_STRIDE), :]
        xk_ref[:, l * hd:(l + 1) * hd] = (blk[:, 0:hd] + pek_ref[l:l + 1, :]).astype(BF16)
        xv_ref[:, l * hd:(l + 1) * hd] = (blk[:, hd:2 * hd] + pev_ref[l:l + 1, :]).astype(BF16)
    hk = _silu(_dot(xk_ref[...], wk1_ref[...])).astype(BF16)
    hv = _silu(_dot(xv_ref[...], wv1_ref[...])).astype(BF16)
    kc_ref[...] = _dot(hk, wk2_ref[...])
    vct_ref[...] = _dot_nt(wv2t_ref[...], hv)


def _compress(proj3, pek, pev, wk1, wv1, wk2, wv2t):
    b, t, _ = proj3.shape
    kv_unit0 = RNN_COLS // LANE
    flat = CMP_LEN * ATT_HD
    full = lambda shape: pl.BlockSpec(shape, lambda i, g: (0,) * len(shape))
    return pl.pallas_call(
        _compress_kernel,
        out_shape=(jax.ShapeDtypeStruct((b, G_KV, N_CMP_PAD, ATT_HD), F32),
                   jax.ShapeDtypeStruct((b, G_KV, ATT_HD, N_CMP_PAD), F32)),
        grid=(b, G_KV),
        in_specs=[pl.BlockSpec((None, t, LANE), lambda i, g: (i, 0, kv_unit0 + g)),
                  full((CMP_LEN, ATT_HD)), full((CMP_LEN, ATT_HD)),
                  full((flat, CMP_HIDDEN)), full((flat, CMP_HIDDEN)),
                  full((CMP_HIDDEN, ATT_HD)), full((ATT_HD, CMP_HIDDEN))],
        out_specs=(pl.BlockSpec((None, None, N_CMP_PAD, ATT_HD), lambda i, g: (i, g, 0, 0)),
                   pl.BlockSpec((None, None, ATT_HD, N_CMP_PAD), lambda i, g: (i, g, 0, 0))),
        scratch_shapes=[pltpu.VMEM((t + CMP_STRIDE, LANE), F32),
                        pltpu.VMEM((N_CMP_PAD, flat), BF16),
                        pltpu.VMEM((N_CMP_PAD, flat), BF16)],
        compiler_params=_cparams(("arbitrary", "arbitrary")),
        name="compress",
    )(proj3, pek, pev, wk1, wv1, wk2, wv2t)


def _cmp_select_kernel(qt_ref, kc_ref, vct_ref, slope_ref, ov_ref, ocmp_ref, selb_ref, used_ref, *, n_cmp, n_sel):
    nct = CMP_TILE // LANE
    tt = CMP_TILE
    ns = ov_ref.shape[0]
    kc = kc_ref[...].astype(BF16)
    vct = vct_ref[...].astype(BF16)
    ov = ov_ref[...]
    j_i = lax.broadcasted_iota(jnp.int32, (8, tt), 0)
    tq_rel = lax.broadcasted_iota(jnp.int32, (8, tt), 1)
    n_tiles = qt_ref.shape[0] // nct

    def ranges(tile):
        t0 = tile * tt
        n_any = min(N_CMP_PAD, (t0 + tt) // CMP_STRIDE)
        n_all = max(0, (t0 - (CMP_LEN - 1)) // CMP_STRIDE + 1) // 8 * 8
        return t0, n_any, n_all, (t0 + tt) // SLC_LEN

    def pad_rows(x):
        if x.shape[0] == N_CMP_PAD:
            return x
        return jnp.concatenate([x, jnp.zeros((N_CMP_PAD - x.shape[0], x.shape[1]), x.dtype)], axis=0)

    scores = []
    for tile in range(n_tiles):
        _, n_any, _, _ = ranges(tile)
        qt = [jnp.concatenate([qt_ref[tile * nct + c, h * ATT_HD:(h + 1) * ATT_HD, :] for c in range(nct)],
                              axis=1) for h in range(HG)]
        scores.append(_dot(kc[:n_any], jnp.concatenate(qt, axis=1)))

    probs, psums = [], []
    for tile in range(n_tiles):
        t0, n_any, n_all, _ = ranges(tile)
        n_b = lax.broadcasted_iota(jnp.int32, (n_any - n_all, tt), 0) + n_all
        t_b = lax.broadcasted_iota(jnp.int32, (n_any - n_all, tt), 1)
        visible = (CMP_STRIDE * n_b + (CMP_LEN - 1) - t0 <= t_b) & (n_b < n_cmp)
        centre = (lax.broadcasted_iota(jnp.int32, (n_any, tt), 0).astype(F32) * float(CMP_STRIDE)
                  + (CMP_LEN - 1) / 2.0)
        psum = jnp.zeros((n_any, tt), F32)
        ph = []
        for h in range(HG):
            s = scores[tile][:, h * tt:(h + 1) * tt] + slope_ref[h] * LOG2E * centre
            band = jnp.where(visible, s[n_all:], -jnp.inf)
            s = band if n_all == 0 else jnp.concatenate([s[:n_all], band], axis=0)
            m = jnp.max(s, axis=0, keepdims=True)
            m = jnp.where(m == -jnp.inf, 0.0, m)
            p = jnp.exp2(s - m)
            d = jnp.sum(p, axis=0, keepdims=True)
            p = p * (1.0 / jnp.where(d > 0, d, 1.0))
            psum = psum + p
            ph.append(pad_rows(p).astype(BF16))
        probs.append(jnp.concatenate(ph, axis=1))
        psums.append(pad_rows(psum))

    for tile in range(n_tiles):
        o = _dot(vct, probs[tile])
        for h in range(HG):
            for c in range(nct):
                lanes = slice(h * tt + c * LANE, h * tt + (c + 1) * LANE)
                ocmp_ref[tile * nct + c, h * ATT_HD:(h + 1) * ATT_HD, :] = o[:, lanes]

    imps = []
    for tile in range(n_tiles):
        psum = psums[tile]
        hi = psum.astype(BF16)
        rem = psum - hi.astype(F32)
        mid = rem.astype(BF16)
        lo = (rem - mid.astype(F32)).astype(BF16)
        imp3 = _dot(ov, jnp.concatenate([hi, mid, lo], axis=1))
        imps.append(imp3[:, 0:tt] + imp3[:, tt:2 * tt] + imp3[:, 2 * tt:3 * tt])

    for tile in range(n_tiles):
        t0, _, _, nvb = ranges(tile)
        imp = imps[tile]
        score = []
        for rg in range(nvb // 8):
            j = j_i + 8 * rg
            tq = tq_rel + t0
            cur = tq // SLC_LEN
            forced = (j == 0) | (j == cur) | (j == cur - 1)
            sc = imp[8 * rg:8 * rg + 8, :] + jnp.where(forced, FORCE_BONUS, 0.0)
            score.append(jnp.where(j * SLC_LEN <= tq, sc, -jnp.inf))
        rank = [jnp.zeros((8, tt), F32) for _ in score]
        for jp in range(nvb):
            sj = jnp.broadcast_to(score[jp // 8][jp % 8:jp % 8 + 1, :], (8, tt))
            for rg in range(nvb // 8):
                ge = jnp.where(sj >= score[rg], 1.0, 0.0)
                gt = jnp.where(sj > score[rg], 1.0, 0.0)
                if 8 * rg > jp:
                    ahead = ge
                elif 8 * rg + 7 <= jp:
                    ahead = gt
                else:
                    ahead = jnp.where(j_i + 8 * rg > jp, ge, gt)
                rank[rg] = rank[rg] + ahead
        for rg in range(ns // 8):
            for c in range(nct):
                if rg < nvb // 8:
                    picked = rank[rg][:, c * LANE:(c + 1) * LANE] < float(n_sel)
                    selb = jnp.where(picked, 0.0, NEG_BIG)
                    used = jnp.max(jnp.where(picked, 1.0, 0.0), axis=1, keepdims=True)
                else:
                    selb = jnp.full((8, LANE), NEG_BIG, F32)
                    used = jnp.zeros((8, 1), F32)
                selb_ref[tile * nct + c, 8 * rg:8 * rg + 8, :] = selb
                used_ref[tile * nct + c, 8 * rg:8 * rg + 8, :] = jnp.broadcast_to(used, (8, LANE))


def _cmp_select(qv, kc, vct, slopes, overlap_t, t):
    b = qv.shape[0]
    ns = t // SLC_LEN
    nch = t // LANE
    n_cmp = (t - CMP_LEN) // CMP_STRIDE + 1
    n_sel = min(SLC_TOPK, ns)
    rows = HG * ATT_HD
    return pl.pallas_call(
        functools.partial(_cmp_select_kernel, n_cmp=n_cmp, n_sel=n_sel),
        out_shape=(jax.ShapeDtypeStruct((b, nch, H_ATT * ATT_HD, LANE), F32),
                   jax.ShapeDtypeStruct((b, G_KV, nch, ns, LANE), F32),
                   jax.ShapeDtypeStruct((b, G_KV, nch, ns, LANE), F32)),
        grid=(b, G_KV),
        in_specs=[pl.BlockSpec((None, nch, rows, LANE), lambda i, g: (i, 0, g, 0)),
                  pl.BlockSpec((None, None, N_CMP_PAD, ATT_HD), lambda i, g: (i, g, 0, 0)),
                  pl.BlockSpec((None, None, ATT_HD, N_CMP_PAD), lambda i, g: (i, g, 0, 0)),
                  pl.BlockSpec((None, HG, 1, 1), lambda i, g: (g, 0, 0, 0)),
                  pl.BlockSpec((ns, N_CMP_PAD), lambda i, g: (0, 0))],
        out_specs=(pl.BlockSpec((None, nch, rows, LANE), lambda i, g: (i, 0, g, 0)),
                   pl.BlockSpec((None, None, nch, ns, LANE), lambda i, g: (i, g, 0, 0, 0)),
                   pl.BlockSpec((None, None, nch, ns, LANE), lambda i, g: (i, g, 0, 0, 0))),
        compiler_params=_cparams(("arbitrary", "arbitrary")),
        name="cmp_select",
    )(qv, kc, vct, slopes, overlap_t)


def _sparse_kernel(n_extra_ref, extra_ref, qt_ref, k_ref, ka_ref, vs_ref, vw_ref, gt_ref, ocmp_ref, selb_ref,
                   slope_ref, o_ref, msk_ref, qop_ref, s_ref, m_ref, acc_ref):
    tq = ATT_TILE
    kc = ATT_TILE
    hd = ATT_HD
    nl = HG * tq
    n_tiles = qt_ref.shape[0]
    ns = selb_ref.shape[1]
    wch = WINDOW // kc
    n_sel_items = 2 * (SEL_NEAR + 1)
    tile_base = (pl.program_id(0) * pl.num_programs(1) + pl.program_id(1)) * n_tiles
    slope = jnp.concatenate([jnp.broadcast_to(slope_ref[h], (1, tq)) for h in range(HG)], axis=1) * LOG2E
    s_hi = slope.astype(BF16).astype(F32)
    s_mid = (slope - s_hi).astype(BF16).astype(F32)
    s_lo = (slope - s_hi - s_mid).astype(BF16).astype(F32)
    slope_rows = jnp.concatenate([s_hi, s_mid, s_lo, jnp.zeros((AUG_SEL_ROW0 - 3, nl), F32)], axis=0)
    aug_tail = jnp.zeros((LANE - AUG_SEL_ROW0 - ns, nl), F32)
    aug_win = jnp.concatenate([slope_rows, jnp.zeros((ns, nl), F32), aug_tail], axis=0).astype(BF16)
    k_i = lax.broadcasted_iota(jnp.int32, (kc, nl), 0)
    t_i = lax.broadcasted_iota(jnp.int32, (kc, nl), 1) % tq
    d0 = t_i - k_i
    msk_ref[0] = jnp.where(d0 >= 0, 0.0, NEG_BIG)
    msk_ref[1] = jnp.where(d0 < 0, 0.0, NEG_BIG)
    chunk_bias = slope * float(kc)
    ones = jnp.ones((ONES_ROWS, kc), BF16)

    n_pairs = n_tiles // 2
    n_items = n_sel_items + 2 * (wch + 1)
    n_slots = s_ref.shape[0] // 2

    def tiles_of(p):
        return (p, n_tiles - 1 - p)

    def sel_item(p, w, j):
        tile = tiles_of(p)[w]
        if j == SEL_NEAR:
            return 0, tile >= SEL_NEAR, tile
        return jnp.maximum(tile - j, 0), tile - j >= 0, j

    def win_item(p, w, j):
        chunk = tiles_of(p)[w] - wch + j
        return jnp.maximum(chunk, 0), chunk >= 0, wch - j

    def col_max(r, delta_f):
        r8 = jnp.max(r.reshape(kc // 8, 8, nl), axis=0)
        return r8 - chunk_bias * delta_f

    def scores(chunk, op):
        rows = pl.ds(pl.multiple_of(chunk * kc, kc), kc)
        keys = jnp.concatenate([k_ref[rows, :], ka_ref[rows, :]], axis=1)
        return _dot(keys, qop_ref[op])

    def build(p, slot):
        for w, tile in enumerate(tiles_of(p)):
            qt = qt_ref[tile]
            q_all = jnp.concatenate([qt[h * hd:(h + 1) * hd, :] for h in range(HG)], axis=1)
            zero = jnp.zeros_like(q_all)
            selb = jnp.concatenate([selb_ref[tile]] * HG, axis=1)
            aug_sel = jnp.concatenate([slope_rows, selb, aug_tail], axis=0).astype(BF16)
            qop_ref[4 * slot + 2 * w] = jnp.concatenate([q_all, zero, aug_sel], axis=0)
            qop_ref[4 * slot + 2 * w + 1] = jnp.concatenate([zero, q_all, aug_win], axis=0)
        for x in range(4):
            m_ref[4 * slot + x] = jnp.full((8, nl), NEG_BIG, F32)

    def phase_a(p, slot):
        items = []

        def sel(w, j):
            chunk, valid, delta = sel_item(p, w, j)
            r = scores(chunk, 4 * slot + 2 * w)
            if j == 0:
                r = r + msk_ref[0]
            s_ref[slot * n_slots + w * (SEL_NEAR + 1) + j] = r
            cm = jnp.where(valid, col_max(r, jnp.float32(delta)), NEG_BIG)
            x = 4 * slot + 2 * w
            m_ref[x] = jnp.maximum(m_ref[x], cm)

        def win(w, j):
            chunk, valid, delta = win_item(p, w, j)
            r = scores(chunk, 4 * slot + 2 * w + 1)
            if j == 0:
                r = r + msk_ref[1]
            elif j == wch:
                r = r + msk_ref[0]
            s_ref[slot * n_slots + n_sel_items + w * (wch + 1) + j] = r
            cm = jnp.where(valid, col_max(r, float(delta)), NEG_BIG)
            x = 4 * slot + 2 * w + 1
            m_ref[x] = jnp.maximum(m_ref[x], cm)

        for w in range(2):
            for j in range(SEL_NEAR + 1):
                items.append(functools.partial(sel, w, j))
        for w in range(2):
            for j in range(wch + 1):
                items.append(functools.partial(win, w, j))
        return items

    def overflow(p, slot, visit):
        count = 0
        for w, tile in enumerate(tiles_of(p)):
            def body(e, n, w=w, tile=tile):
                visit(w, tile, extra_ref[(tile_base + tile) * n_tiles + e], n)
                return n + 1

            count = lax.fori_loop(0, n_extra_ref[tile_base + tile], body, count)

    def overflow_a(p, slot):
        def visit(w, tile, c, n):
            r = scores(c, 4 * slot + 2 * w)
            s_ref[slot * n_slots + n_items + n] = r
            x = 4 * slot + 2 * w
            m_ref[x] = jnp.maximum(m_ref[x], col_max(r, jnp.float32(tile - c)))

        overflow(p, slot, visit)

    def overflow_b(p, slot, m_row):
        def visit(w, tile, c, n):
            row = m_row[2 * w] + chunk_bias * jnp.float32(tile - c)
            pr = jnp.exp2(s_ref[slot * n_slots + n_items + n] - row).astype(BF16)
            v_aug = jnp.concatenate([vs_ref[c], ones], axis=0)
            acc_ref[4 * slot + 2 * w] += _dot(v_aug, pr)

        overflow(p, slot, visit)

    def phase_b(p, slot, m_row):
        for x in range(4):
            acc_ref[4 * slot + x] = jnp.zeros(acc_ref.shape[1:], F32)
        items = []

        def sel(w, j):
            chunk, valid, delta = sel_item(p, w, j)
            row = jnp.where(valid, m_row[2 * w] + chunk_bias * jnp.float32(delta), -NEG_BIG)
            pr = jnp.exp2(s_ref[slot * n_slots + w * (SEL_NEAR + 1) + j] - row).astype(BF16)
            v_aug = jnp.concatenate([vs_ref[chunk], ones], axis=0)
            acc_ref[4 * slot + 2 * w] += _dot(v_aug, pr)

        def win(w, j):
            chunk, valid, delta = win_item(p, w, j)
            row = jnp.where(valid, m_row[2 * w + 1] + chunk_bias * float(delta), -NEG_BIG)
            pr = jnp.exp2(s_ref[slot * n_slots + n_sel_items + w * (wch + 1) + j] - row).astype(BF16)
            v_aug = jnp.concatenate([vw_ref[chunk], ones], axis=0)
            acc_ref[4 * slot + 2 * w + 1] += _dot(v_aug, pr)

        for w in range(2):
            for j in range(SEL_NEAR + 1):
                items.append(functools.partial(sel, w, j))
        for w in range(2):
            for j in range(wch + 1):
                items.append(functools.partial(win, w, j))
        return items

    def finalize(p, slot):
        for w, tile in enumerate(tiles_of(p)):
            a_s = acc_ref[4 * slot + 2 * w]
            a_w = acc_ref[4 * slot + 2 * w + 1]
            o_slc = a_s[:hd, :] / a_s[hd:hd + 1, :]
            o_win = a_w[:hd, :] / a_w[hd:hd + 1, :]
            oc = ocmp_ref[tile]
            outs = []
            for h in range(HG):
                lanes = slice(h * tq, (h + 1) * tq)
                row0 = (pl.program_id(1) * HG + h) * 3
                g_cmp, g_slc, g_win = [_sigmoid(gt_ref[tile, pl.ds(row0 + br, 1), :]) for br in range(3)]
                y = (g_cmp * oc[h * hd:(h + 1) * hd, :] + g_slc * o_slc[:, lanes]
                     + g_win * o_win[:, lanes])
                outs.append(y.T)
            rows = pl.ds(pl.multiple_of(tile * tq, tq), tq)
            o_ref[rows, :] = jnp.concatenate(outs, axis=1).astype(o_ref.dtype)

    def step(k, slot, first=False, last=False):
        m_row = [jnp.max(m_ref[4 * slot + x], axis=0, keepdims=True) for x in range(4)]
        if not last:
            build(k + 1, 1 - slot)
        if not first:
            finalize(k - 1, 1 - slot)
        b_items = phase_b(k, slot, m_row)
        a_items = phase_a(k + 1, 1 - slot) if not last else [None] * n_items
        for a, b in zip(a_items, b_items):
            if a is not None:
                a()
            b()
        if not last:
            overflow_a(k + 1, 1 - slot)
        overflow_b(k, slot, m_row)

    build(0, 0)
    for a in phase_a(0, 0):
        a()
    overflow_a(0, 0)
    step(0, 0, first=True)

    def body(kk, carry):
        step(2 * kk + 1, 1)
        step(2 * kk + 2, 0)
        return carry

    lax.fori_loop(0, (n_pairs - 2) // 2, body, 0)
    step(n_pairs - 1, (n_pairs - 1) % 2, last=True)
    finalize(n_pairs - 1, (n_pairs - 1) % 2)


def _key_features(t):
    kp = np.arange(t)
    f = np.zeros((t, LANE), np.float32)
    f[:, 0:3] = (kp % ATT_TILE)[:, None]
    f[kp, AUG_SEL_ROW0 + kp // SLC_LEN] = 1.0
    return jnp.asarray(f, dtype=BF16)


def _extra_chunks(used, t):
    b = used.shape[0]
    nt = t // ATT_TILE
    need = used.max(axis=-1).reshape(b, G_KV, nt, nt, ATT_TILE // SLC_LEN).max(axis=-1) > 0.5
    chunk = jnp.arange(nt, dtype=jnp.int32)
    extra = need & (chunk >= 1) & (chunk[None, :] <= chunk[:, None] - SEL_NEAR)
    order = jnp.argsort(jnp.where(extra, chunk, chunk + nt), axis=-1).astype(jnp.int32)
    return extra.sum(axis=-1, dtype=jnp.int32).reshape(-1), order.reshape(-1)


def _sparse(n_extra, extra, qv, gt, proj3, ocmp, selb, slopes, t):
    b = qv.shape[0]
    nchunk = t // LANE
    ns = t // SLC_LEN
    rows = HG * ATT_HD
    ksw_unit0 = RNN_COLS // LANE + G_KV
    vs_blk0 = QT_ROWS // ATT_HD
    vw_blk0 = vs_blk0 + G_KV
    nl = HG * ATT_TILE
    n_slots = 2 * (SEL_NEAR + 1) + 2 * (WINDOW // ATT_TILE + 1) + (nchunk - 1 - SEL_NEAR)
    grid_spec = pltpu.PrefetchScalarGridSpec(
        num_scalar_prefetch=2,
        grid=(b, G_KV),
        in_specs=[pl.BlockSpec((None, nchunk, rows, LANE), lambda i, g, *_: (i, 0, g, 0)),
                  pl.BlockSpec((None, t, LANE), lambda i, g, *_: (i, 0, ksw_unit0 + g)),
                  pl.BlockSpec((t, LANE), lambda i, g, *_: (0, 0)),
                  pl.BlockSpec((None, nchunk, ATT_HD, LANE), lambda i, g, *_: (i, 0, vs_blk0 + g, 0)),
                  pl.BlockSpec((None, nchunk, ATT_HD, LANE), lambda i, g, *_: (i, 0, vw_blk0 + g, 0)),
                  pl.BlockSpec((None, nchunk, GATE_ROWS, LANE), lambda i, g, *_: (i, 0, 0, 0)),
                  pl.BlockSpec((None, nchunk, rows, LANE), lambda i, g, *_: (i, 0, g, 0)),
                  pl.BlockSpec((None, None, nchunk, ns, LANE), lambda i, g, *_: (i, g, 0, 0, 0)),
                  pl.BlockSpec((None, HG, 1, 1), lambda i, g, *_: (g, 0, 0, 0))],
        out_specs=pl.BlockSpec((None, t, rows), lambda i, g, *_: (i, 0, g)),
        scratch_shapes=[pltpu.VMEM((2, ATT_TILE, nl), F32),
                        pltpu.VMEM((2 * 4, 2 * LANE, nl), BF16),
                        pltpu.VMEM((2 * n_slots, ATT_TILE, nl), F32),
                        pltpu.VMEM((2 * 4, 8, nl), F32),
                        pltpu.VMEM((2 * 4, ATT_HD + ONES_ROWS, nl), F32)])
    return pl.pallas_call(
        _sparse_kernel,
        out_shape=jax.ShapeDtypeStruct((b, t, H_ATT * ATT_HD), BF16),
        grid_spec=grid_spec,
        compiler_params=_cparams(("arbitrary", "arbitrary")),
        name="sparse",
    )(n_extra, extra, qv, proj3, _key_features(t), qv, qv, gt, ocmp, selb, slopes)


def _merge_kernel(x_ref, yr_ref, ya_ref, mgr_ref, mga_ref, wr_ref, wa_ref, wo_ref, o_ref):
    pr = _dot(yr_ref[...], wr_ref[...])
    pa = _dot(ya_ref[...], wa_ref[...])
    merged = _sigmoid(mgr_ref[...].astype(F32)) * pr + _sigmoid(mga_ref[...].astype(F32)) * pa
    o_ref[...] = x_ref[...] + _dot(merged.astype(BF16), wo_ref[...])


def _merge(x2, yr, ya, proj, wr, wa, wo):
    m, d = x2.shape
    mg0 = (RNN_COLS + KV_COLS) // d
    tile = lambda col: pl.BlockSpec((TOK_TILE, d), lambda i: (i, col))
    wfull = pl.BlockSpec((d, d), lambda i: (0, 0))
    return pl.pallas_call(
        _merge_kernel,
        out_shape=jax.ShapeDtypeStruct((m, d), F32),
        grid=(m // TOK_TILE,),
        in_specs=[tile(0), tile(0), tile(0), tile(mg0), tile(mg0 + 1), wfull, wfull, wfull],
        out_specs=tile(0),
        compiler_params=_cparams(("arbitrary",)),
        name="merge",
    )(x2, yr, ya, proj, proj, wr, wa, wo)


def _mem_kv_kernel(mem_ref, g_ref, wkt_ref, wv_ref, kt_ref, v_ref):
    a = _rms(mem_ref[...], g_ref[...]).astype(BF16)
    kt_ref[...] = _dot_nt(wkt_ref[...], a).astype(kt_ref.dtype)
    v_ref[...] = _dot(a, wv_ref[...]).astype(v_ref.dtype)


def _mem_kv(mem, g, wkt, wkv):
    b, nm, d = mem.shape
    hw = H_X * X_HD
    return pl.pallas_call(
        _mem_kv_kernel,
        out_shape=(jax.ShapeDtypeStruct((b, hw, nm), BF16),
                   jax.ShapeDtypeStruct((b, nm, hw), BF16)),
        grid=(b,),
        in_specs=[pl.BlockSpec((None, nm, d), lambda i: (i, 0, 0)),
                  pl.BlockSpec((1, d), lambda i: (0, 0)),
                  pl.BlockSpec((hw, d), lambda i: (0, 0)),
                  pl.BlockSpec((d, hw), lambda i: (0, 1))],
        out_specs=(pl.BlockSpec((None, hw, nm), lambda i: (i, 0, 0)),
                   pl.BlockSpec((None, nm, hw), lambda i: (i, 0, 0))),
        compiler_params=_cparams(("arbitrary",)),
        name="mem_kv",
    )(mem, g, wkt, wkv)


def _xattn_kernel(h_ref, g_ref, wq_ref, kt_ref, v_ref, wo_ref, o_ref):
    h = h_ref[...]
    a = _rms(h, g_ref[...]).astype(BF16)
    q = (_dot(a, wq_ref[...]) * (X_HD ** -0.5)).astype(BF16)
    heads = [slice(hh * X_HD, (hh + 1) * X_HD) for hh in range(H_X)]
    scores = [_dot(q[:, cols], kt_ref[cols, :]) for cols in heads]
    probs = []
    for s in scores:
        m = jnp.max(s, axis=-1, keepdims=True)
        p = jnp.exp(s - m)
        probs.append((p / jnp.sum(p, axis=-1, keepdims=True)).astype(BF16))
    outs = [_dot(p, v_ref[:, cols]) for p, cols in zip(probs, heads)]
    o = jnp.concatenate(outs, axis=1).astype(BF16)
    o_ref[...] = h + _dot(o, wo_ref[...])


def _xattn(h3, g, wq, kt, v, wo):
    b, t, d = h3.shape
    hw = H_X * X_HD
    nm = v.shape[1]
    return pl.pallas_call(
        _xattn_kernel,
        out_shape=jax.ShapeDtypeStruct((b, t, d), F32),
        grid=(b, t // TOK_TILE),
        in_specs=[pl.BlockSpec((None, TOK_TILE, d), lambda i, j: (i, j, 0)),
                  pl.BlockSpec((1, d), lambda i, j: (0, 0)),
                  pl.BlockSpec((d, hw), lambda i, j: (0, 0)),
                  pl.BlockSpec((None, hw, nm), lambda i, j: (i, 0, 0)),
                  pl.BlockSpec((None, nm, hw), lambda i, j: (i, 0, 0)),
                  pl.BlockSpec((hw, d), lambda i, j: (0, 0))],
        out_specs=pl.BlockSpec((None, TOK_TILE, d), lambda i, j: (i, j, 0)),
        compiler_params=_cparams(("arbitrary", "arbitrary")),
        name="xattn",
    )(h3, g, wq, kt, v, wo)


def _ffn_kernel(h_ref, g_ref, wg_ref, wu_ref, wd_ref, gf_ref, o_ref, *, fc):
    h = h_ref[...]
    a = _rms(h, g_ref[...]).astype(BF16)
    acc = h
    for f in range(wg_ref.shape[1] // fc):
        cols = slice(f * fc, (f + 1) * fc)
        mid = _silu(_dot(a, wg_ref[:, cols])) * _dot(a, wu_ref[:, cols])
        acc = acc + _dot(mid.astype(BF16), wd_ref[cols, :])
    o_ref[...] = _rms(acc, gf_ref[...])


def _ffn(h2, g, wgu, wd, gf):
    m, d = h2.shape
    ff = wd.shape[0]
    const = lambda shape: pl.BlockSpec(shape, lambda i: (0, 0))
    return pl.pallas_call(
        functools.partial(_ffn_kernel, fc=ff // 2),
        out_shape=jax.ShapeDtypeStruct((m, d), F32),
        grid=(m // TOK_TILE,),
        in_specs=[pl.BlockSpec((TOK_TILE, d), lambda i: (i, 0)),
                  const((1, d)), const((d, ff)), pl.BlockSpec((d, ff), lambda i: (0, 1)),
                  const((ff, d)), const((1, d))],
        out_specs=pl.BlockSpec((TOK_TILE, d), lambda i: (i, 0)),
        compiler_params=_cparams(("arbitrary",)),
        name="ffn",
    )(h2, g, wgu, wgu, wd, gf)


def _overlap_t(t):
    nc = (t - CMP_LEN) // CMP_STRIDE + 1
    ns = t // SLC_LEN
    starts = CMP_STRIDE * np.arange(N_CMP_PAD)
    s_start = SLC_LEN * np.arange(ns)
    ov = ((starts[None, :] + CMP_LEN > s_start[:, None]) & (starts[None, :] < s_start[:, None] + SLC_LEN)
          & (np.arange(N_CMP_PAD)[None, :] < nc))
    return jnp.asarray(ov.astype(np.float32), dtype=BF16)


def kernel(x, mem, g_mix, w_in, lower_bounds, g_rnn_out, pe_ck, w_ck1, w_ck2, pe_cv, w_cv1, w_cv2,
           w_proj_rnn, w_proj_att, w_out, g_xattn, g_mem, w_xq, w_xkv, w_xo, g_ffn, w_gate_up,
           w_down, g_final):
    b, t, d = x.shape
    depth = g_mix.shape[0]
    assert depth == 1, "the final RMSNorm is fused into the layer's FFN kernel"
    lbs = jnp.cumsum(jax.nn.softmax(lower_bounds.astype(F32), axis=0), axis=0)
    slopes = (2.0 ** (-8.0 * jnp.arange(1, H_ATT + 1, dtype=F32) / H_ATT)).reshape(G_KV, HG, 1, 1)
    overlap_t = _overlap_t(t)
    h = x
    for l in range(depth):
        w_n, w_t = _w_prep(w_in[l].T)
        x2 = h.reshape(b * t, d)
        proj = _in_proj(x2, g_mix[l][None, :], w_n)
        proj3 = proj.reshape(b, t, N_COLS)
        qv, gt = _in_proj_t(h, g_mix[l][None, :], w_t)
        y_r = _hgrn(proj3, lbs[l][None, :], g_rnn_out[l][None, :])
        kc, vct = _compress(proj3, pe_ck[l], pe_cv[l], w_ck1[l].astype(BF16), w_cv1[l].astype(BF16),
                            w_ck2[l].astype(BF16), w_cv2[l].T.astype(BF16))
        ocmp, selb, used = _cmp_select(qv, kc, vct, slopes, overlap_t, t)
        n_extra, extra = _extra_chunks(used, t)
        y_a = _sparse(n_extra, extra, qv, gt, proj3, ocmp, selb, slopes, t)
        h1 = _merge(x2, y_r.reshape(b * t, d), y_a.reshape(b * t, d), proj,
                    w_proj_rnn[l].astype(BF16), w_proj_att[l].astype(BF16), w_out[l].astype(BF16))
        w_kv = w_xkv[l].astype(BF16)
        kt, v = _mem_kv(mem, g_mem[l][None, :], w_kv[:, :H_X * X_HD].T, w_kv)
        h2 = _xattn(h1.reshape(b, t, d), g_xattn[l][None, :], w_xq[l].astype(BF16), kt, v,
                    w_xo[l].astype(BF16))
        h = _ffn(h2.reshape(b * t, d), g_ffn[l][None, :], w_gate_up[l].astype(BF16),
                 w_down[l].astype(BF16), g_final[None, :]).reshape(b, t, d)
    return h
```

```python
import functools
import itertools

import jax
import jax.numpy as jnp
import numpy as np
from jax import lax
from jax.experimental import pallas as pl
from jax.experimental.pallas import tpu as pltpu

F32 = jnp.float32
BF16 = jnp.bfloat16

D_MODEL = 1024
N_MEM = 256
H_RNN = 8
RNN_DIM = 128
RNN_CHUNK = 64
H_ATT = 16
ATT_HD = 64
G_KV = 4
HG = H_ATT // G_KV
CMP_LEN = 32
CMP_STRIDE = 16
CMP_HIDDEN = 128
SLC_LEN = 64
SLC_TOPK = 8
WINDOW = 512
FORCE_BONUS = 1.0e4
H_X = 4
X_HD = 128
D_FF = 2816
EPS = 1e-6

LANE = 128
VMEM_LIMIT = 56 * 1024 * 1024
TOK_TILE = 512
ATT_TILE = 128
CMP_TILE = 512
N_CMP_PAD = 128
HGRN_GROUP = 4
ONES_ROWS = 16
NEG_BIG = -1.0e30
LOG2E = 1.4426950408889634
AUG_SEL_ROW0 = 8
SEL_NEAR = 5

QT_ROWS = H_ATT * ATT_HD
VT_ROWS = 2 * G_KV * ATT_HD
GATE_ROWS = 64
F_ROWS = -(-(QT_ROWS + VT_ROWS + 3 * H_ATT) // LANE) * LANE
RNN_COLS = 4 * H_RNN * RNN_DIM
KV_COLS = 4 * G_KV * ATT_HD
MG_COLS = 2 * D_MODEL
N_COLS = RNN_COLS + KV_COLS + MG_COLS


def _cparams(sem):
    return pltpu.CompilerParams(dimension_semantics=sem, vmem_limit_bytes=VMEM_LIMIT)


def _rms(xf, g):
    return xf * lax.rsqrt(jnp.mean(xf * xf, axis=-1, keepdims=True) + EPS) * g


def _sigmoid(x):
    return 1.0 / (1.0 + jnp.exp(-x))


def _silu(x):
    return x * _sigmoid(x)


def _dot(a, b):
    return jnp.dot(a, b, preferred_element_type=F32)


def _dot_nt(a, b):
    return lax.dot_general(a, b, (((1,), (1,)), ((), ())), preferred_element_type=F32)


def _dot_tn(a, b):
    return lax.dot_general(a, b, (((0,), (0,)), ((), ())), preferred_element_type=F32)


def _w_prep_kernel(w_ref, wn_ref, wf_ref):
    kvw = G_KV * ATT_HD
    kv0 = RNN_COLS + QT_ROWS
    gate0 = kv0 + 6 * kvw
    mg0 = gate0 + 3 * H_ATT

    def pair_by_group(a0, b0):
        parts = []
        for g in range(G_KV):
            parts += [w_ref[a0 + g * ATT_HD:a0 + (g + 1) * ATT_HD, :], w_ref[b0 + g * ATT_HD:b0 + (g + 1) * ATT_HD, :]]
        return parts

    wn = jnp.concatenate([w_ref[0:RNN_COLS, :]] + pair_by_group(kv0, kv0 + kvw)
                         + pair_by_group(kv0 + 2 * kvw, kv0 + 4 * kvw) + [w_ref[mg0:mg0 + MG_COLS, :]], axis=0)
    wn_ref[...] = wn.T.astype(BF16)
    used = QT_ROWS + VT_ROWS + 3 * H_ATT
    wf = jnp.concatenate([w_ref[RNN_COLS:kv0, :] * (ATT_HD ** -0.5 * LOG2E),
                          w_ref[kv0 + 3 * kvw:kv0 + 4 * kvw, :], w_ref[kv0 + 5 * kvw:kv0 + 6 * kvw, :],
                          w_ref[gate0:mg0, :], jnp.zeros((wf_ref.shape[0] - used, w_ref.shape[1]), F32)], axis=0)
    wf_ref[...] = wf.astype(BF16)


def _w_prep(wt):
    n_in, d = wt.shape
    return pl.pallas_call(
        _w_prep_kernel,
        out_shape=(jax.ShapeDtypeStruct((d, N_COLS), BF16), jax.ShapeDtypeStruct((F_ROWS, d), BF16)),
        grid=(d // LANE,),
        in_specs=[pl.BlockSpec((n_in, LANE), lambda i: (0, i))],
        out_specs=(pl.BlockSpec((LANE, N_COLS), lambda i: (i, 0)),
                   pl.BlockSpec((F_ROWS, LANE), lambda i: (0, i))),
        compiler_params=_cparams(("arbitrary",)),
        name="w_prep",
    )(wt)


def _in_proj_kernel(x_ref, g_ref, w_ref, o_ref, *, sub):
    a = _rms(x_ref[...], g_ref[...]).astype(BF16)
    for n in range(o_ref.shape[1] // sub):
        cols = slice(n * sub, (n + 1) * sub)
        o_ref[:, cols] = _dot(a, w_ref[:, cols]).astype(o_ref.dtype)


def _in_proj(x2, g, w):
    m, d = x2.shape
    n = w.shape[1]
    tn = n // 2
    return pl.pallas_call(
        functools.partial(_in_proj_kernel, sub=512),
        out_shape=jax.ShapeDtypeStruct((m, n), BF16),
        grid=(2, m // TOK_TILE),
        in_specs=[pl.BlockSpec((TOK_TILE, d), lambda j, i: (i, 0)),
                  pl.BlockSpec((1, d), lambda j, i: (0, 0)),
                  pl.BlockSpec((d, tn), lambda j, i: (0, j))],
        out_specs=pl.BlockSpec((TOK_TILE, tn), lambda j, i: (i, j)),
        compiler_params=_cparams(("arbitrary", "arbitrary")),
        name="in_proj",
    )(x2, g, w)


def _in_proj_t_kernel(x_ref, g_ref, wt_ref, qv_ref, gt_ref):
    a = _rms(x_ref[...], g_ref[...]).astype(BF16)
    r = _dot_nt(wt_ref[...], a)
    nqv = qv_ref.shape[1]
    ngt = gt_ref.shape[1]
    for c in range(qv_ref.shape[0]):
        qv_ref[c] = r[:nqv, c * LANE:(c + 1) * LANE].astype(qv_ref.dtype)
        gt_ref[c] = r[nqv:nqv + ngt, c * LANE:(c + 1) * LANE]


def _in_proj_t(x, g, wt):
    b, t, d = x.shape
    rows = wt.shape[0]
    nqv = QT_ROWS + VT_ROWS
    ngt = GATE_ROWS
    nc = TOK_TILE // LANE
    return pl.pallas_call(
        _in_proj_t_kernel,
        out_shape=(jax.ShapeDtypeStruct((b, t // LANE, nqv, LANE), BF16),
                   jax.ShapeDtypeStruct((b, t // LANE, ngt, LANE), F32)),
        grid=(b, t // TOK_TILE),
        in_specs=[pl.BlockSpec((None, TOK_TILE, d), lambda i, j: (i, j, 0)),
                  pl.BlockSpec((1, d), lambda i, j: (0, 0)),
                  pl.BlockSpec((rows, d), lambda i, j: (0, 0))],
        out_specs=(pl.BlockSpec((None, nc, nqv, LANE), lambda i, j: (i, j, 0, 0)),
                   pl.BlockSpec((None, nc, ngt, LANE), lambda i, j: (i, j, 0, 0))),
        compiler_params=_cparams(("arbitrary", "arbitrary")),
        name="in_proj_t",
    )(x, g, wt)


def _hgrn_kernel(q_ref, f_ref, i_ref, og_ref, lb_ref, gn_ref, o_ref, qd_ref, oi_ref, ut_ref, dec_ref):
    c = RNN_CHUNK
    kd = RNN_DIM
    n_chunks = q_ref.shape[0] // c
    lb = lb_ref[...]
    gn = gn_ref[...]
    blk = HGRN_GROUP * c
    row = lax.broadcasted_iota(jnp.int32, (blk, blk), 0)
    col = lax.broadcasted_iota(jnp.int32, (blk, blk), 1)
    same_chunk = (row // c) == (col // c)
    causal = same_chunk & (row >= col)
    tril = causal.astype(BF16)

    q = q_ref[...].astype(F32)
    fl = f_ref[...].astype(F32)
    v = i_ref[...]
    f = lb + (1.0 - lb) * _sigmoid(fl)
    k = 1.0 - f
    logf = jnp.log(f)
    hi = logf.astype(BF16)
    lo = (logf - hi.astype(F32)).astype(BF16)
    pieces = jnp.concatenate([hi, lo], axis=1)
    n_blk = q_ref.shape[0] // blk
    cs = [_dot(tril, pieces[g * blk:(g + 1) * blk]) for g in range(n_blk)]
    bcum = jnp.concatenate([x[:, 0:kd] + x[:, kd:2 * kd] for x in cs], axis=0)
    e_neg = jnp.exp(-bcum)
    dec = jnp.exp(jnp.concatenate([bcum[n * c + c - 1:n * c + c, :] for n in range(n_chunks)], axis=0))
    dec_rows = jnp.concatenate([jnp.broadcast_to(dec[n:n + 1, :], (c, kd)) for n in range(n_chunks)], axis=0)
    q_dec = (_silu(q) * (1.0 / e_neg)).astype(BF16)
    k_neg = k * e_neg
    k_dec = k_neg.astype(BF16)
    k_end = (k_neg * dec_rows).astype(BF16)
    qd_ref[...] = q_dec
    dec_ref[...] = dec
    for g in range(n_blk):
        rows = slice(g * blk, (g + 1) * blk)
        a = jnp.where(causal, _dot_nt(q_dec[rows], k_dec[rows]), 0.0)
        oi_ref[rows, :] = _dot(a.astype(BF16), v[rows])
    for n in range(n_chunks):
        rows = slice(n * c, (n + 1) * c)
        ut_ref[n] = _dot_tn(v[rows], k_end[rows])

    s_t = jnp.zeros((kd, kd), F32)
    for n in range(n_chunks):
        rows = slice(n * c, (n + 1) * c)
        o = oi_ref[rows, :] + _dot_nt(qd_ref[rows, :], s_t.astype(BF16))
        og = og_ref[rows, :].astype(F32)
        o_ref[rows, :] = (_rms(o, gn) * _silu(og)).astype(o_ref.dtype)
        s_t = s_t * dec_ref[n:n + 1, :] + ut_ref[n]


def _hgrn(proj3, lb, gn):
    b, t, _ = proj3.shape
    return pl.pallas_call(
        _hgrn_kernel,
        out_shape=jax.ShapeDtypeStruct((b, t, H_RNN * RNN_DIM), BF16),
        grid=(b, H_RNN),
        in_specs=[pl.BlockSpec((None, t, RNN_DIM), lambda i, h: (i, 0, h)),
                  pl.BlockSpec((None, t, RNN_DIM), lambda i, h: (i, 0, H_RNN + h)),
                  pl.BlockSpec((None, t, RNN_DIM), lambda i, h: (i, 0, 2 * H_RNN + h)),
                  pl.BlockSpec((None, t, RNN_DIM), lambda i, h: (i, 0, 3 * H_RNN + h)),
                  pl.BlockSpec((1, RNN_DIM), lambda i, h: (0, h)),
                  pl.BlockSpec((1, RNN_DIM), lambda i, h: (0, 0))],
        out_specs=pl.BlockSpec((None, t, RNN_DIM), lambda i, h: (i, 0, h)),
        scratch_shapes=[pltpu.VMEM((t, RNN_DIM), BF16),
                        pltpu.VMEM((t, RNN_DIM), F32),
                        pltpu.VMEM((t // RNN_CHUNK, RNN_DIM, RNN_DIM), F32),
                        pltpu.VMEM((t // RNN_CHUNK, RNN_DIM), F32)],
        compiler_params=_cparams(("arbitrary", "arbitrary")),
        name="hgrn",
    )(proj3, proj3, proj3, proj3, lb, gn)


def _compress_kernel(kv_ref, pek_ref, pev_ref, wk1_ref, wv1_ref, wk2_ref, wv2t_ref,
                     kc_ref, vct_ref, xs_ref, xk_ref, xv_ref):
    t = kv_ref.shape[0]
    hd = ATT_HD
    xs_ref[0:t, :] = kv_ref[...].astype(F32)
    xs_ref[t:, :] = jnp.zeros((xs_ref.shape[0] - t, xs_ref.shape[1]), F32)
    for l in range(CMP_LEN):
        blk = xs_ref[pl.ds(l, N_CMP_PAD, stride=CMP_STRIDE), :]
        xk_ref[:, l * hd:(l + 1) * hd] = (blk[:, 0:hd] + pek_ref[l:l + 1, :]).astype(BF16)
        xv_ref[:, l * hd:(l + 1) * hd] = (blk[:, hd:2 * hd] + pev_ref[l:l + 1, :]).astype(BF16)
    hk = _silu(_dot(xk_ref[...], wk1_ref[...])).astype(BF16)
    hv = _silu(_dot(xv_ref[...], wv1_ref[...])).astype(BF16)
    kc_ref[...] = _dot(hk, wk2_ref[...])
    vct_ref[...] = _dot_nt(wv2t_ref[...], hv)


def _compress(proj3, pek, pev, wk1, wv1, wk2, wv2t):
    b, t, _ = proj3.shape
    kv_unit0 = RNN_COLS // LANE
    flat = CMP_LEN * ATT_HD
    full = lambda shape: pl.BlockSpec(shape, lambda i, g: (0,) * len(shape))
    return pl.pallas_call(
        _compress_kernel,
        out_shape=(jax.ShapeDtypeStruct((b, G_KV, N_CMP_PAD, ATT_HD), F32),
                   jax.ShapeDtypeStruct((b, G_KV, ATT_HD, N_CMP_PAD), F32)),
        grid=(b, G_KV),
        in_specs=[pl.BlockSpec((None, t, LANE), lambda i, g: (i, 0, kv_unit0 + g)),
                  full((CMP_LEN, ATT_HD)), full((CMP_LEN, ATT_HD)),
                  full((flat, CMP_HIDDEN)), full((flat, CMP_HIDDEN)),
                  full((CMP_HIDDEN, ATT_HD)), full((ATT_HD, CMP_HIDDEN))],
        out_specs=(pl.BlockSpec((None, None, N_CMP_PAD, ATT_HD), lambda i, g: (i, g, 0, 0)),
                   pl.BlockSpec((None, None, ATT_HD, N_CMP_PAD), lambda i, g: (i, g, 0, 0))),
        scratch_shapes=[pltpu.VMEM((t + CMP_STRIDE, LANE), F32),
                        pltpu.VMEM((N_CMP_PAD, flat), BF16),
                        pltpu.VMEM((N_CMP_PAD, flat), BF16)],
        compiler_params=_cparams(("arbitrary", "arbitrary")),
        name="compress",
    )(proj3, pek, pev, wk1, wv1, wk2, wv2t)


def _cmp_select_kernel(qt_ref, kc_ref, vct_ref, slope_ref, ov_ref, ocmp_ref, selb_ref, used_ref, *, n_cmp, n_sel):
    nct = CMP_TILE // LANE
    tt = CMP_TILE
    ns = ov_ref.shape[0]
    kc = kc_ref[...].astype(BF16)
    vct = vct_ref[...].astype(BF16)
    ov = ov_ref[...]
    j_i = lax.broadcasted_iota(jnp.int32, (8, tt), 0)
    tq_rel = lax.broadcasted_iota(jnp.int32, (8, tt), 1)
    n_tiles = qt_ref.shape[0] // nct

    def ranges(tile):
        t0 = tile * tt
        n_any = min(N_CMP_PAD, (t0 + tt) // CMP_STRIDE)
        n_all = max(0, (t0 - (CMP_LEN - 1)) // CMP_STRIDE + 1) // 8 * 8
        return t0, n_any, n_all, (t0 + tt) // SLC_LEN

    def pad_rows(x):
        if x.shape[0] == N_CMP_PAD:
            return x
        return jnp.concatenate([x, jnp.zeros((N_CMP_PAD - x.shape[0], x.shape[1]), x.dtype)], axis=0)

    scores = []
    for tile in range(n_tiles):
        _, n_any, _, _ = ranges(tile)
        qt = [jnp.concatenate([qt_ref[tile * nct + c, h * ATT_HD:(h + 1) * ATT_HD, :] for c in range(nct)],
                              axis=1) for h in range(HG)]
        scores.append(_dot(kc[:n_any], jnp.concatenate(qt, axis=1)))

    probs, psums = [], []
    for tile in range(n_tiles):
        t0, n_any, n_all, _ = ranges(tile)
        n_b = lax.broadcasted_iota(jnp.int32, (n_any - n_all, tt), 0) + n_all
        t_b = lax.broadcasted_iota(jnp.int32, (n_any - n_all, tt), 1)
        visible = (CMP_STRIDE * n_b + (CMP_LEN - 1) - t0 <= t_b) & (n_b < n_cmp)
        centre = (lax.broadcasted_iota(jnp.int32, (n_any, tt), 0).astype(F32) * float(CMP_STRIDE)
                  + (CMP_LEN - 1) / 2.0)
        psum = jnp.zeros((n_any, tt), F32)
        ph = []
        for h in range(HG):
            s = scores[tile][:, h * tt:(h + 1) * tt] + slope_ref[h] * LOG2E * centre
            band = jnp.where(visible, s[n_all:], -jnp.inf)
            s = band if n_all == 0 else jnp.concatenate([s[:n_all], band], axis=0)
            m = jnp.max(s, axis=0, keepdims=True)
            m = jnp.where(m == -jnp.inf, 0.0, m)
            p = jnp.exp2(s - m)
            d = jnp.sum(p, axis=0, keepdims=True)
            p = p * (1.0 / jnp.where(d > 0, d, 1.0))
            psum = psum + p
            ph.append(pad_rows(p).astype(BF16))
        probs.append(jnp.concatenate(ph, axis=1))
        psums.append(pad_rows(psum))

    for tile in range(n_tiles):
        o = _dot(vct, probs[tile])
        for h in range(HG):
            for c in range(nct):
                lanes = slice(h * tt + c * LANE, h * tt + (c + 1) * LANE)
                ocmp_ref[tile * nct + c, h * ATT_HD:(h + 1) * ATT_HD, :] = o[:, lanes]

    imps = []
    for tile in range(n_tiles):
        psum = psums[tile]
        hi = psum.astype(BF16)
        rem = psum - hi.astype(F32)
        mid = rem.astype(BF16)
        lo = (rem - mid.astype(F32)).astype(BF16)
        imp3 = _dot(ov, jnp.concatenate([hi, mid, lo], axis=1))
        imps.append(imp3[:, 0:tt] + imp3[:, tt:2 * tt] + imp3[:, 2 * tt:3 * tt])

    for tile in range(n_tiles):
        t0, _, _, nvb = ranges(tile)
        imp = imps[tile]
        score = []
        for rg in range(nvb // 8):
            j = j_i + 8 * rg
            tq = tq_rel + t0
            cur = tq // SLC_LEN
            forced = (j == 0) | (j == cur) | (j == cur - 1)
            sc = imp[8 * rg:8 * rg + 8, :] + jnp.where(forced, FORCE_BONUS, 0.0)
            score.append(jnp.where(j * SLC_LEN <= tq, sc, -jnp.inf))
        rank = [jnp.zeros((8, tt), F32) for _ in score]
        for jp in range(nvb):
            sj = jnp.broadcast_to(score[jp // 8][jp % 8:jp % 8 + 1, :], (8, tt))
            for rg in range(nvb // 8):
                ge = jnp.where(sj >= score[rg], 1.0, 0.0)
                gt = jnp.where(sj > score[rg], 1.0, 0.0)
                if 8 * rg > jp:
                    ahead = ge
                elif 8 * rg + 7 <= jp:
                    ahead = gt
                else:
                    ahead = jnp.where(j_i + 8 * rg > jp, ge, gt)
                rank[rg] = rank[rg] + ahead
        for rg in range(ns // 8):
            for c in range(nct):
                if rg < nvb // 8:
                    picked = rank[rg][:, c * LANE:(c + 1) * LANE] < float(n_sel)
                    selb = jnp.where(picked, 0.0, NEG_BIG)
                    used = jnp.max(jnp.where(picked, 1.0, 0.0), axis=1, keepdims=True)
                else:
                    selb = jnp.full((8, LANE), NEG_BIG, F32)
                    used = jnp.zeros((8, 1), F32)
                selb_ref[tile * nct + c, 8 * rg:8 * rg + 8, :] = selb
                used_ref[tile * nct + c, 8 * rg:8 * rg + 8, :] = jnp.broadcast_to(used, (8, LANE))


def _cmp_select(qv, kc, vct, slopes, overlap_t, t):
    b = qv.shape[0]
    ns = t // SLC_LEN
    nch = t // LANE
    n_cmp = (t - CMP_LEN) // CMP_STRIDE + 1
    n_sel = min(SLC_TOPK, ns)
    rows = HG * ATT_HD
    return pl.pallas_call(
        functools.partial(_cmp_select_kernel, n_cmp=n_cmp, n_sel=n_sel),
        out_shape=(jax.ShapeDtypeStruct((b, nch, H_ATT * ATT_HD, LANE), F32),
                   jax.ShapeDtypeStruct((b, G_KV, nch, ns, LANE), F32),
                   jax.ShapeDtypeStruct((b, G_KV, nch, ns, LANE), F32)),
        grid=(b, G_KV),
        in_specs=[pl.BlockSpec((None, nch, rows, LANE), lambda i, g: (i, 0, g, 0)),
                  pl.BlockSpec((None, None, N_CMP_PAD, ATT_HD), lambda i, g: (i, g, 0, 0)),
                  pl.BlockSpec((None, None, ATT_HD, N_CMP_PAD), lambda i, g: (i, g, 0, 0)),
                  pl.BlockSpec((None, HG, 1, 1), lambda i, g: (g, 0, 0, 0)),
                  pl.BlockSpec((ns, N_CMP_PAD), lambda i, g: (0, 0))],
        out_specs=(pl.BlockSpec((None, nch, rows, LANE), lambda i, g: (i, 0, g, 0)),
                   pl.BlockSpec((None, None, nch, ns, LANE), lambda i, g: (i, g, 0, 0, 0)),
                   pl.BlockSpec((None, None, nch, ns, LANE), lambda i, g: (i, g, 0, 0, 0))),
        compiler_params=_cparams(("arbitrary", "arbitrary")),
        name="cmp_select",
    )(qv, kc, vct, slopes, overlap_t)


def _sparse_kernel(n_extra_ref, extra_ref, qt_ref, k_ref, ka_ref, vs_ref, vw_ref, gt_ref, ocmp_ref, selb_ref,
                   slope_ref, o_ref, msk_ref, qop_ref, s_ref, m_ref, acc_ref):
    tq = ATT_TILE
    kc = ATT_TILE
    hd = ATT_HD
    nl = HG * tq
    n_tiles = qt_ref.shape[0]
    ns = selb_ref.shape[1]
    wch = WINDOW // kc
    n_sel_items = 2 * (SEL_NEAR + 1)
    tile_base = (pl.program_id(0) * pl.num_programs(1) + pl.program_id(1)) * n_tiles
    slope = jnp.concatenate([jnp.broadcast_to(slope_ref[h], (1, tq)) for h in range(HG)], axis=1) * LOG2E
    s_hi = slope.astype(BF16).astype(F32)
    s_mid = (slope - s_hi).astype(BF16).astype(F32)
    s_lo = (slope - s_hi - s_mid).astype(BF16).astype(F32)
    slope_rows = jnp.concatenate([s_hi, s_mid, s_lo, jnp.zeros((AUG_SEL_ROW0 - 3, nl), F32)], axis=0)
    aug_tail = jnp.zeros((LANE - AUG_SEL_ROW0 - ns, nl), F32)
    aug_win = jnp.concatenate([slope_rows, jnp.zeros((ns, nl), F32), aug_tail], axis=0).astype(BF16)
    k_i = lax.broadcasted_iota(jnp.int32, (kc, nl), 0)
    t_i = lax.broadcasted_iota(jnp.int32, (kc, nl), 1) % tq
    d0 = t_i - k_i
    msk_ref[0] = jnp.where(d0 >= 0, 0.0, NEG_BIG)
    msk_ref[1] = jnp.where(d0 < 0, 0.0, NEG_BIG)
    chunk_bias = slope * float(kc)
    ones = jnp.ones((ONES_ROWS, kc), BF16)

    n_pairs = n_tiles // 2
    n_items = n_sel_items + 2 * (wch + 1)
    n_slots = s_ref.shape[0] // 2

    def tiles_of(p):
        return (p, n_tiles - 1 - p)

    def sel_item(p, w, j):
        tile = tiles_of(p)[w]
        if j == SEL_NEAR:
            return 0, tile >= SEL_NEAR, tile
        return max(tile - j, 0), tile - j >= 0, j

    def win_item(p, w, j):
        chunk = tiles_of(p)[w] - wch + j
        return max(chunk, 0), chunk >= 0, wch - j

    def col_max(r, delta_f):
        r8 = jnp.max(r.reshape(kc // 8, 8, nl), axis=0)
        return r8 - chunk_bias * delta_f

    def scores(chunk, op):
        rows = pl.ds(pl.multiple_of(chunk * kc, kc), kc)
        keys = jnp.concatenate([k_ref[rows, :], ka_ref[rows, :]], axis=1)
        return _dot(keys, qop_ref[op])

    def build(p, slot):
        for w, tile in enumerate(tiles_of(p)):
            qt = qt_ref[tile]
            q_all = jnp.concatenate([qt[h * hd:(h + 1) * hd, :] for h in range(HG)], axis=1)
            zero = jnp.zeros_like(q_all)
            selb = jnp.concatenate([selb_ref[tile]] * HG, axis=1)
            aug_sel = jnp.concatenate([slope_rows, selb, aug_tail], axis=0).astype(BF16)
            qop_ref[4 * slot + 2 * w] = jnp.concatenate([q_all, zero, aug_sel], axis=0)
            qop_ref[4 * slot + 2 * w + 1] = jnp.concatenate([zero, q_all, aug_win], axis=0)
        for x in range(4):
            m_ref[4 * slot + x] = jnp.full((8, nl), NEG_BIG, F32)

    def phase_a(p, slot):
        items = []

        def sel(w, j):
            chunk, _, delta = sel_item(p, w, j)
            r = scores(chunk, 4 * slot + 2 * w)
            if j == 0:
                r = r + msk_ref[0]
            s_ref[slot * n_slots + w * (SEL_NEAR + 1) + j] = r
            x = 4 * slot + 2 * w
            m_ref[x] = jnp.maximum(m_ref[x], col_max(r, float(delta)))

        def win(w, j):
            chunk, _, delta = win_item(p, w, j)
            r = scores(chunk, 4 * slot + 2 * w + 1)
            if j == 0:
                r = r + msk_ref[1]
            elif j == wch:
                r = r + msk_ref[0]
            s_ref[slot * n_slots + n_sel_items + w * (wch + 1) + j] = r
            x = 4 * slot + 2 * w + 1
            m_ref[x] = jnp.maximum(m_ref[x], col_max(r, float(delta)))

        for w in range(2):
            for j in range(SEL_NEAR + 1):
                if sel_item(p, w, j)[1]:
                    items.append(functools.partial(sel, w, j))
        for w in range(2):
            for j in range(wch + 1):
                if win_item(p, w, j)[1]:
                    items.append(functools.partial(win, w, j))
        return items

    def overflow(p, slot, visit):
        count = 0
        for w, tile in enumerate(tiles_of(p)):
            def body(e, n, w=w, tile=tile):
                visit(w, tile, extra_ref[(tile_base + tile) * n_tiles + e], n)
                return n + 1

            count = lax.fori_loop(0, n_extra_ref[tile_base + tile], body, count)

    def overflow_a(p, slot):
        def visit(w, tile, c, n):
            r = scores(c, 4 * slot + 2 * w)
            s_ref[slot * n_slots + n_items + n] = r
            x = 4 * slot + 2 * w
            m_ref[x] = jnp.maximum(m_ref[x], col_max(r, jnp.float32(tile - c)))

        overflow(p, slot, visit)

    def overflow_b(p, slot, m_row):
        def visit(w, tile, c, n):
            row = m_row[2 * w] + chunk_bias * jnp.float32(tile - c)
            pr = jnp.exp2(s_ref[slot * n_slots + n_items + n] - row).astype(BF16)
            v_aug = jnp.concatenate([vs_ref[c], ones], axis=0)
            acc_ref[4 * slot + 2 * w] += _dot(v_aug, pr)

        overflow(p, slot, visit)

    def phase_b(p, slot, m_row):
        for x in range(4):
            acc_ref[4 * slot + x] = jnp.zeros(acc_ref.shape[1:], F32)
        items = []

        def sel(w, j):
            chunk, _, delta = sel_item(p, w, j)
            row = m_row[2 * w] + chunk_bias * float(delta)
            pr = jnp.exp2(s_ref[slot * n_slots + w * (SEL_NEAR + 1) + j] - row).astype(BF16)
            v_aug = jnp.concatenate([vs_ref[chunk], ones], axis=0)
            acc_ref[4 * slot + 2 * w] += _dot(v_aug, pr)

        def win(w, j):
            chunk, _, delta = win_item(p, w, j)
            row = m_row[2 * w + 1] + chunk_bias * float(delta)
            pr = jnp.exp2(s_ref[slot * n_slots + n_sel_items + w * (wch + 1) + j] - row).astype(BF16)
            v_aug = jnp.concatenate([vw_ref[chunk], ones], axis=0)
            acc_ref[4 * slot + 2 * w + 1] += _dot(v_aug, pr)

        for w in range(2):
            for j in range(SEL_NEAR + 1):
                if sel_item(p, w, j)[1]:
                    items.append(functools.partial(sel, w, j))
        for w in range(2):
            for j in range(wch + 1):
                if win_item(p, w, j)[1]:
                    items.append(functools.partial(win, w, j))
        return items

    def finalize(p, slot):
        for w, tile in enumerate(tiles_of(p)):
            a_s = acc_ref[4 * slot + 2 * w]
            a_w = acc_ref[4 * slot + 2 * w + 1]
            o_slc = a_s[:hd, :] / a_s[hd:hd + 1, :]
            o_win = a_w[:hd, :] / a_w[hd:hd + 1, :]
            oc = ocmp_ref[tile]
            outs = []
            for h in range(HG):
                lanes = slice(h * tq, (h + 1) * tq)
                row0 = (pl.program_id(1) * HG + h) * 3
                g_cmp, g_slc, g_win = [_sigmoid(gt_ref[tile, pl.ds(row0 + br, 1), :]) for br in range(3)]
                y = (g_cmp * oc[h * hd:(h + 1) * hd, :] + g_slc * o_slc[:, lanes]
                     + g_win * o_win[:, lanes])
                outs.append(y.T)
            rows = pl.ds(pl.multiple_of(tile * tq, tq), tq)
            o_ref[rows, :] = jnp.concatenate(outs, axis=1).astype(o_ref.dtype)

    def step(k, slot, first=False, last=False):
        m_row = [jnp.max(m_ref[4 * slot + x], axis=0, keepdims=True) for x in range(4)]
        if not last:
            build(k + 1, 1 - slot)
        if not first:
            finalize(k - 1, 1 - slot)
        b_items = phase_b(k, slot, m_row)
        a_items = phase_a(k + 1, 1 - slot) if not last else []
        for a, b in itertools.zip_longest(a_items, b_items):
            if a is not None:
                a()
            if b is not None:
                b()
        if not last:
            overflow_a(k + 1, 1 - slot)
        overflow_b(k, slot, m_row)

    build(0, 0)
    for a in phase_a(0, 0):
        a()
    overflow_a(0, 0)
    for k in range(n_pairs):
        step(k, k % 2, first=k == 0, last=k == n_pairs - 1)
    finalize(n_pairs - 1, (n_pairs - 1) % 2)


def _key_features(t):
    kp = np.arange(t)
    f = np.zeros((t, LANE), np.float32)
    f[:, 0:3] = (kp % ATT_TILE)[:, None]
    f[kp, AUG_SEL_ROW0 + kp // SLC_LEN] = 1.0
    return jnp.asarray(f, dtype=BF16)


def _extra_chunks(used, t):
    b = used.shape[0]
    nt = t // ATT_TILE
    need = used.max(axis=-1).reshape(b, G_KV, nt, nt, ATT_TILE // SLC_LEN).max(axis=-1) > 0.5
    chunk = jnp.arange(nt, dtype=jnp.int32)
    extra = need & (chunk >= 1) & (chunk[None, :] <= chunk[:, None] - SEL_NEAR)
    place = jnp.cumsum(extra.astype(jnp.int32), axis=-1) - 1
    hit = extra[..., :, None] & (place[..., :, None] == chunk)
    lst = jnp.sum(jnp.where(hit, chunk[:, None], 0), axis=-2)
    return extra.sum(axis=-1, dtype=jnp.int32).reshape(-1), lst.reshape(-1)


def _sparse(n_extra, extra, qv, gt, proj3, ocmp, selb, slopes, t):
    b = qv.shape[0]
    nchunk = t // LANE
    ns = t // SLC_LEN
    rows = HG * ATT_HD
    ksw_unit0 = RNN_COLS // LANE + G_KV
    vs_blk0 = QT_ROWS // ATT_HD
    vw_blk0 = vs_blk0 + G_KV
    nl = HG * ATT_TILE
    n_slots = 2 * (SEL_NEAR + 1) + 2 * (WINDOW // ATT_TILE + 1) + (nchunk - 1 - SEL_NEAR)
    grid_spec = pltpu.PrefetchScalarGridSpec(
        num_scalar_prefetch=2,
        grid=(b, G_KV),
        in_specs=[pl.BlockSpec((None, nchunk, rows, LANE), lambda i, g, *_: (i, 0, g, 0)),
                  pl.BlockSpec((None, t, LANE), lambda i, g, *_: (i, 0, ksw_unit0 + g)),
                  pl.BlockSpec((t, LANE), lambda i, g, *_: (0, 0)),
                  pl.BlockSpec((None, nchunk, ATT_HD, LANE), lambda i, g, *_: (i, 0, vs_blk0 + g, 0)),
                  pl.BlockSpec((None, nchunk, ATT_HD, LANE), lambda i, g, *_: (i, 0, vw_blk0 + g, 0)),
                  pl.BlockSpec((None, nchunk, GATE_ROWS, LANE), lambda i, g, *_: (i, 0, 0, 0)),
                  pl.BlockSpec((None, nchunk, rows, LANE), lambda i, g, *_: (i, 0, g, 0)),
                  pl.BlockSpec((None, None, nchunk, ns, LANE), lambda i, g, *_: (i, g, 0, 0, 0)),
                  pl.BlockSpec((None, HG, 1, 1), lambda i, g, *_: (g, 0, 0, 0))],
        out_specs=pl.BlockSpec((None, t, rows), lambda i, g, *_: (i, 0, g)),
        scratch_shapes=[pltpu.VMEM((2, ATT_TILE, nl), F32),
                        pltpu.VMEM((2 * 4, 2 * LANE, nl), BF16),
                        pltpu.VMEM((2 * n_slots, ATT_TILE, nl), F32),
                        pltpu.VMEM((2 * 4, 8, nl), F32),
                        pltpu.VMEM((2 * 4, ATT_HD + ONES_ROWS, nl), F32)])
    return pl.pallas_call(
        _sparse_kernel,
        out_shape=jax.ShapeDtypeStruct((b, t, H_ATT * ATT_HD), BF16),
        grid_spec=grid_spec,
        compiler_params=_cparams(("arbitrary", "arbitrary")),
        name="sparse",
    )(n_extra, extra, qv, proj3, _key_features(t), qv, qv, gt, ocmp, selb, slopes)


def _merge_kernel(x_ref, yr_ref, ya_ref, mgr_ref, mga_ref, wr_ref, wa_ref, wo_ref, o_ref):
    pr = _dot(yr_ref[...], wr_ref[...])
    pa = _dot(ya_ref[...], wa_ref[...])
    merged = _sigmoid(mgr_ref[...].astype(F32)) * pr + _sigmoid(mga_ref[...].astype(F32)) * pa
    o_ref[...] = x_ref[...] + _dot(merged.astype(BF16), wo_ref[...])


def _merge(x2, yr, ya, proj, wr, wa, wo):
    m, d = x2.shape
    mg0 = (RNN_COLS + KV_COLS) // d
    tile = lambda col: pl.BlockSpec((TOK_TILE, d), lambda i: (i, col))
    wfull = pl.BlockSpec((d, d), lambda i: (0, 0))
    return pl.pallas_call(
        _merge_kernel,
        out_shape=jax.ShapeDtypeStruct((m, d), F32),
        grid=(m // TOK_TILE,),
        in_specs=[tile(0), tile(0), tile(0), tile(mg0), tile(mg0 + 1), wfull, wfull, wfull],
        out_specs=tile(0),
        compiler_params=_cparams(("arbitrary",)),
        name="merge",
    )(x2, yr, ya, proj, proj, wr, wa, wo)


def _mem_kv_kernel(mem_ref, g_ref, wkt_ref, wv_ref, kt_ref, v_ref):
    a = _rms(mem_ref[...], g_ref[...]).astype(BF16)
    kt_ref[...] = _dot_nt(wkt_ref[...], a).astype(kt_ref.dtype)
    v_ref[...] = _dot(a, wv_ref[...]).astype(v_ref.dtype)


def _mem_kv(mem, g, wkt, wkv):
    b, nm, d = mem.shape
    hw = H_X * X_HD
    return pl.pallas_call(
        _mem_kv_kernel,
        out_shape=(jax.ShapeDtypeStruct((b, hw, nm), BF16),
                   jax.ShapeDtypeStruct((b, nm, hw), BF16)),
        grid=(b,),
        in_specs=[pl.BlockSpec((None, nm, d), lambda i: (i, 0, 0)),
                  pl.BlockSpec((1, d), lambda i: (0, 0)),
                  pl.BlockSpec((hw, d), lambda i: (0, 0)),
                  pl.BlockSpec((d, hw), lambda i: (0, 1))],
        out_specs=(pl.BlockSpec((None, hw, nm), lambda i: (i, 0, 0)),
                   pl.BlockSpec((None, nm, hw), lambda i: (i, 0, 0))),
        compiler_params=_cparams(("arbitrary",)),
        name="mem_kv",
    )(mem, g, wkt, wkv)


def _xattn_kernel(h_ref, g_ref, wq_ref, kt_ref, v_ref, wo_ref, o_ref):
    h = h_ref[...]
    a = _rms(h, g_ref[...]).astype(BF16)
    q = (_dot(a, wq_ref[...]) * (X_HD ** -0.5)).astype(BF16)
    heads = [slice(hh * X_HD, (hh + 1) * X_HD) for hh in range(H_X)]
    scores = [_dot(q[:, cols], kt_ref[cols, :]) for cols in heads]
    probs = []
    for s in scores:
        m = jnp.max(s, axis=-1, keepdims=True)
        p = jnp.exp(s - m)
        probs.append((p / jnp.sum(p, axis=-1, keepdims=True)).astype(BF16))
    outs = [_dot(p, v_ref[:, cols]) for p, cols in zip(probs, heads)]
    o = jnp.concatenate(outs, axis=1).astype(BF16)
    o_ref[...] = h + _dot(o, wo_ref[...])


def _xattn(h3, g, wq, kt, v, wo):
    b, t, d = h3.shape
    hw = H_X * X_HD
    nm = v.shape[1]
    return pl.pallas_call(
        _xattn_kernel,
        out_shape=jax.ShapeDtypeStruct((b, t, d), F32),
        grid=(b, t // TOK_TILE),
        in_specs=[pl.BlockSpec((None, TOK_TILE, d), lambda i, j: (i, j, 0)),
                  pl.BlockSpec((1, d), lambda i, j: (0, 0)),
                  pl.BlockSpec((d, hw), lambda i, j: (0, 0)),
                  pl.BlockSpec((None, hw, nm), lambda i, j: (i, 0, 0)),
                  pl.BlockSpec((None, nm, hw), lambda i, j: (i, 0, 0)),
                  pl.BlockSpec((hw, d), lambda i, j: (0, 0))],
        out_specs=pl.BlockSpec((None, TOK_TILE, d), lambda i, j: (i, j, 0)),
        compiler_params=_cparams(("arbitrary", "arbitrary")),
        name="xattn",
    )(h3, g, wq, kt, v, wo)


def _ffn_kernel(h_ref, g_ref, wg_ref, wu_ref, wd_ref, gf_ref, o_ref, *, fc):
    h = h_ref[...]
    a = _rms(h, g_ref[...]).astype(BF16)
    acc = h
    for f in range(wg_ref.shape[1] // fc):
        cols = slice(f * fc, (f + 1) * fc)
        mid = _silu(_dot(a, wg_ref[:, cols])) * _dot(a, wu_ref[:, cols])
        acc = acc + _dot(mid.astype(BF16), wd_ref[cols, :])
    o_ref[...] = _rms(acc, gf_ref[...])


def _ffn(h2, g, wgu, wd, gf):
    m, d = h2.shape
    ff = wd.shape[0]
    const = lambda shape: pl.BlockSpec(shape, lambda i: (0, 0))
    return pl.pallas_call(
        functools.partial(_ffn_kernel, fc=ff // 2),
        out_shape=jax.ShapeDtypeStruct((m, d), F32),
        grid=(m // TOK_TILE,),
        in_specs=[pl.BlockSpec((TOK_TILE, d), lambda i: (i, 0)),
                  const((1, d)), const((d, ff)), pl.BlockSpec((d, ff), lambda i: (0, 1)),
                  const((ff, d)), const((1, d))],
        out_specs=pl.BlockSpec((TOK_TILE, d), lambda i: (i, 0)),
        compiler_params=_cparams(("arbitrary",)),
        name="ffn",
    )(h2, g, wgu, wgu, wd, gf)


def _overlap_t(t):
    nc = (t - CMP_LEN) // CMP_STRIDE + 1
    ns = t // SLC_LEN
    starts = CMP_STRIDE * np.arange(N_CMP_PAD)
    s_start = SLC_LEN * np.arange(ns)
    ov = ((starts[None, :] + CMP_LEN > s_start[:, None]) & (starts[None, :] < s_start[:, None] + SLC_LEN)
          & (np.arange(N_CMP_PAD)[None, :] < nc))
    return jnp.asarray(ov.astype(np.float32), dtype=BF16)


def kernel(x, mem, g_mix, w_in, lower_bounds, g_rnn_out, pe_ck, w_ck1, w_ck2, pe_cv, w_cv1, w_cv2,
           w_proj_rnn, w_proj_att, w_out, g_xattn, g_mem, w_xq, w_xkv, w_xo, g_ffn, w_gate_up,
           w_down, g_final):
    b, t, d = x.shape
    depth = g_mix.shape[0]
    assert depth == 1, "the final RMSNorm is fused into the layer's FFN kernel"
    lbs = jnp.cumsum(jax.nn.softmax(lower_bounds.astype(F32), axis=0), axis=0)
    slopes = (2.0 ** (-8.0 * jnp.arange(1, H_ATT + 1, dtype=F32) / H_ATT)).reshape(G_KV, HG, 1, 1)
    overlap_t = _overlap_t(t)
    h = x
    for l in range(depth):
        w_n, w_t = _w_prep(w_in[l].T)
        x2 = h.reshape(b * t, d)
        proj = _in_proj(x2, g_mix[l][None, :], w_n)
        proj3 = proj.reshape(b, t, N_COLS)
        qv, gt = _in_proj_t(h, g_mix[l][None, :], w_t)
        y_r = _hgrn(proj3, lbs[l][None, :], g_rnn_out[l][None, :])
        kc, vct = _compress(proj3, pe_ck[l], pe_cv[l], w_ck1[l].astype(BF16), w_cv1[l].astype(BF16),
                            w_ck2[l].astype(BF16), w_cv2[l].T.astype(BF16))
        ocmp, selb, used = _cmp_select(qv, kc, vct, slopes, overlap_t, t)
        n_extra, extra = _extra_chunks(used, t)
        y_a = _sparse(n_extra, extra, qv, gt, proj3, ocmp, selb, slopes, t)
        h1 = _merge(x2, y_r.reshape(b * t, d), y_a.reshape(b * t, d), proj,
                    w_proj_rnn[l].astype(BF16), w_proj_att[l].astype(BF16), w_out[l].astype(BF16))
        w_kv = w_xkv[l].astype(BF16)
        kt, v = _mem_kv(mem, g_mem[l][None, :], w_kv[:, :H_X * X_HD].T, w_kv)
        h2 = _xattn(h1.reshape(b, t, d), g_xattn[l][None, :], w_xq[l].astype(BF16), kt, v,
                    w_xo[l].astype(BF16))
        h = _ffn(h2.reshape(b * t, d), g_ffn[l][None, :], w_gate_up[l].astype(BF16),
                 w_down[l].astype(BF16), g_final[None, :]).reshape(b, t, d)
    return h
```

```python
import functools
import itertools

import jax
import jax.numpy as jnp
import numpy as np
from jax import lax
from jax.experimental import pallas as pl
from jax.experimental.pallas import tpu as pltpu

F32 = jnp.float32
BF16 = jnp.bfloat16

D_MODEL = 1024
N_MEM = 256
H_RNN = 8
RNN_DIM = 128
RNN_CHUNK = 64
H_ATT = 16
ATT_HD = 64
G_KV = 4
HG = H_ATT // G_KV
CMP_LEN = 32
CMP_STRIDE = 16
CMP_HIDDEN = 128
SLC_LEN = 64
SLC_TOPK = 8
WINDOW = 512
FORCE_BONUS = 1.0e4
H_X = 4
X_HD = 128
D_FF = 2816
EPS = 1e-6

LANE = 128
VMEM_LIMIT = 56 * 1024 * 1024
TOK_TILE = 512
ATT_TILE = 128
CMP_TILE = 512
N_CMP_PAD = 128
HGRN_GROUP = 4
ONES_ROWS = 16
NEG_BIG = -1.0e30
LOG2E = 1.4426950408889634
AUG_SEL_ROW0 = 8
SEL_NEAR = 5

QT_ROWS = H_ATT * ATT_HD
VT_ROWS = 2 * G_KV * ATT_HD
GATE_ROWS = 64
F_ROWS = -(-(QT_ROWS + VT_ROWS + 3 * H_ATT) // LANE) * LANE
RNN_COLS = 4 * H_RNN * RNN_DIM
KV_COLS = 4 * G_KV * ATT_HD
MG_COLS = 2 * D_MODEL
N_COLS = RNN_COLS + KV_COLS + MG_COLS


def _cparams(sem):
    return pltpu.CompilerParams(dimension_semantics=sem, vmem_limit_bytes=VMEM_LIMIT)


def _rms(xf, g):
    return xf * lax.rsqrt(jnp.mean(xf * xf, axis=-1, keepdims=True) + EPS) * g


def _sigmoid(x):
    return 1.0 / (1.0 + jnp.exp(-x))


def _silu(x):
    return x * _sigmoid(x)


def _dot(a, b):
    return jnp.dot(a, b, preferred_element_type=F32)


def _dot_nt(a, b):
    return lax.dot_general(a, b, (((1,), (1,)), ((), ())), preferred_element_type=F32)


def _dot_tn(a, b):
    return lax.dot_general(a, b, (((0,), (0,)), ((), ())), preferred_element_type=F32)


def _w_prep_kernel(w_ref, wn_ref, wf_ref):
    kvw = G_KV * ATT_HD
    kv0 = RNN_COLS + QT_ROWS
    gate0 = kv0 + 6 * kvw
    mg0 = gate0 + 3 * H_ATT

    def pair_by_group(a0, b0):
        parts = []
        for g in range(G_KV):
            parts += [w_ref[a0 + g * ATT_HD:a0 + (g + 1) * ATT_HD, :], w_ref[b0 + g * ATT_HD:b0 + (g + 1) * ATT_HD, :]]
        return parts

    wn = jnp.concatenate([w_ref[0:RNN_COLS, :]] + pair_by_group(kv0, kv0 + kvw)
                         + pair_by_group(kv0 + 2 * kvw, kv0 + 4 * kvw) + [w_ref[mg0:mg0 + MG_COLS, :]], axis=0)
    wn_ref[...] = wn.T.astype(BF16)
    used = QT_ROWS + VT_ROWS + 3 * H_ATT
    wf = jnp.concatenate([w_ref[RNN_COLS:kv0, :] * (ATT_HD ** -0.5 * LOG2E),
                          w_ref[kv0 + 3 * kvw:kv0 + 4 * kvw, :], w_ref[kv0 + 5 * kvw:kv0 + 6 * kvw, :],
                          w_ref[gate0:mg0, :], jnp.zeros((wf_ref.shape[0] - used, w_ref.shape[1]), F32)], axis=0)
    wf_ref[...] = wf.astype(BF16)


def _w_prep(wt):
    n_in, d = wt.shape
    return pl.pallas_call(
        _w_prep_kernel,
        out_shape=(jax.ShapeDtypeStruct((d, N_COLS), BF16), jax.ShapeDtypeStruct((F_ROWS, d), BF16)),
        grid=(d // LANE,),
        in_specs=[pl.BlockSpec((n_in, LANE), lambda i: (0, i))],
        out_specs=(pl.BlockSpec((LANE, N_COLS), lambda i: (i, 0)),
                   pl.BlockSpec((F_ROWS, LANE), lambda i: (0, i))),
        compiler_params=_cparams(("arbitrary",)),
        name="w_prep",
    )(wt)


def _in_proj_kernel(x_ref, g_ref, w_ref, o_ref, *, sub):
    a = _rms(x_ref[...], g_ref[...]).astype(BF16)
    for n in range(o_ref.shape[1] // sub):
        cols = slice(n * sub, (n + 1) * sub)
        o_ref[:, cols] = _dot(a, w_ref[:, cols]).astype(o_ref.dtype)


def _in_proj(x2, g, w):
    m, d = x2.shape
    n = w.shape[1]
    tn = n // 2
    return pl.pallas_call(
        functools.partial(_in_proj_kernel, sub=512),
        out_shape=jax.ShapeDtypeStruct((m, n), BF16),
        grid=(2, m // TOK_TILE),
        in_specs=[pl.BlockSpec((TOK_TILE, d), lambda j, i: (i, 0)),
                  pl.BlockSpec((1, d), lambda j, i: (0, 0)),
                  pl.BlockSpec((d, tn), lambda j, i: (0, j))],
        out_specs=pl.BlockSpec((TOK_TILE, tn), lambda j, i: (i, j)),
        compiler_params=_cparams(("arbitrary", "arbitrary")),
        name="in_proj",
    )(x2, g, w)


def _in_proj_t_kernel(x_ref, g_ref, wt_ref, qv_ref, gt_ref):
    a = _rms(x_ref[...], g_ref[...]).astype(BF16)
    r = _dot_nt(wt_ref[...], a)
    nqv = qv_ref.shape[1]
    ngt = gt_ref.shape[1]
    for c in range(qv_ref.shape[0]):
        qv_ref[c] = r[:nqv, c * LANE:(c + 1) * LANE].astype(qv_ref.dtype)
        gt_ref[c] = r[nqv:nqv + ngt, c * LANE:(c + 1) * LANE]


def _in_proj_t(x, g, wt):
    b, t, d = x.shape
    rows = wt.shape[0]
    nqv = QT_ROWS + VT_ROWS
    ngt = GATE_ROWS
    nc = TOK_TILE // LANE
    return pl.pallas_call(
        _in_proj_t_kernel,
        out_shape=(jax.ShapeDtypeStruct((b, t // LANE, nqv, LANE), BF16),
                   jax.ShapeDtypeStruct((b, t // LANE, ngt, LANE), F32)),
        grid=(b, t // TOK_TILE),
        in_specs=[pl.BlockSpec((None, TOK_TILE, d), lambda i, j: (i, j, 0)),
                  pl.BlockSpec((1, d), lambda i, j: (0, 0)),
                  pl.BlockSpec((rows, d), lambda i, j: (0, 0))],
        out_specs=(pl.BlockSpec((None, nc, nqv, LANE), lambda i, j: (i, j, 0, 0)),
                   pl.BlockSpec((None, nc, ngt, LANE), lambda i, j: (i, j, 0, 0))),
        compiler_params=_cparams(("arbitrary", "arbitrary")),
        name="in_proj_t",
    )(x, g, wt)


def _hgrn_kernel(q_ref, f_ref, i_ref, og_ref, lb_ref, gn_ref, o_ref, qd_ref, oi_ref, ut_ref, dec_ref):
    c = RNN_CHUNK
    kd = RNN_DIM
    n_chunks = q_ref.shape[0] // c
    lb = lb_ref[...]
    gn = gn_ref[...]
    blk = HGRN_GROUP * c
    row = lax.broadcasted_iota(jnp.int32, (blk, blk), 0)
    col = lax.broadcasted_iota(jnp.int32, (blk, blk), 1)
    same_chunk = (row // c) == (col // c)
    causal = same_chunk & (row >= col)
    tril = causal.astype(BF16)

    q = q_ref[...].astype(F32)
    fl = f_ref[...].astype(F32)
    v = i_ref[...]
    f = lb + (1.0 - lb) * _sigmoid(fl)
    k = 1.0 - f
    logf = jnp.log(f)
    hi = logf.astype(BF16)
    lo = (logf - hi.astype(F32)).astype(BF16)
    pieces = jnp.concatenate([hi, lo], axis=1)
    n_blk = q_ref.shape[0] // blk
    cs = [_dot(tril, pieces[g * blk:(g + 1) * blk]) for g in range(n_blk)]
    bcum = jnp.concatenate([x[:, 0:kd] + x[:, kd:2 * kd] for x in cs], axis=0)
    e_neg = jnp.exp(-bcum)
    dec = jnp.exp(jnp.concatenate([bcum[n * c + c - 1:n * c + c, :] for n in range(n_chunks)], axis=0))
    dec_rows = jnp.concatenate([jnp.broadcast_to(dec[n:n + 1, :], (c, kd)) for n in range(n_chunks)], axis=0)
    q_dec = (_silu(q) * (1.0 / e_neg)).astype(BF16)
    k_neg = k * e_neg
    k_dec = k_neg.astype(BF16)
    k_end = (k_neg * dec_rows).astype(BF16)
    qd_ref[...] = q_dec
    dec_ref[...] = dec
    for g in range(n_blk):
        rows = slice(g * blk, (g + 1) * blk)
        a = jnp.where(causal, _dot_nt(q_dec[rows], k_dec[rows]), 0.0)
        oi_ref[rows, :] = _dot(a.astype(BF16), v[rows])
    for n in range(n_chunks):
        rows = slice(n * c, (n + 1) * c)
        ut_ref[n] = _dot_tn(v[rows], k_end[rows])

    s_t = jnp.zeros((kd, kd), F32)
    for n in range(n_chunks):
        rows = slice(n * c, (n + 1) * c)
        o = oi_ref[rows, :] + _dot_nt(qd_ref[rows, :], s_t.astype(BF16))
        og = og_ref[rows, :].astype(F32)
        o_ref[rows, :] = (_rms(o, gn) * _silu(og)).astype(o_ref.dtype)
        s_t = s_t * dec_ref[n:n + 1, :] + ut_ref[n]


def _hgrn(proj3, lb, gn):
    b, t, _ = proj3.shape
    return pl.pallas_call(
        _hgrn_kernel,
        out_shape=jax.ShapeDtypeStruct((b, t, H_RNN * RNN_DIM), BF16),
        grid=(b, H_RNN),
        in_specs=[pl.BlockSpec((None, t, RNN_DIM), lambda i, h: (i, 0, h)),
                  pl.BlockSpec((None, t, RNN_DIM), lambda i, h: (i, 0, H_RNN + h)),
                  pl.BlockSpec((None, t, RNN_DIM), lambda i, h: (i, 0, 2 * H_RNN + h)),
                  pl.BlockSpec((None, t, RNN_DIM), lambda i, h: (i, 0, 3 * H_RNN + h)),
                  pl.BlockSpec((1, RNN_DIM), lambda i, h: (0, h)),
                  pl.BlockSpec((1, RNN_DIM), lambda i, h: (0, 0))],
        out_specs=pl.BlockSpec((None, t, RNN_DIM), lambda i, h: (i, 0, h)),
        scratch_shapes=[pltpu.VMEM((t, RNN_DIM), BF16),
                        pltpu.VMEM((t, RNN_DIM), F32),
                        pltpu.VMEM((t // RNN_CHUNK, RNN_DIM, RNN_DIM), F32),
                        pltpu.VMEM((t // RNN_CHUNK, RNN_DIM), F32)],
        compiler_params=_cparams(("arbitrary", "arbitrary")),
        name="hgrn",
    )(proj3, proj3, proj3, proj3, lb, gn)


def _compress_kernel(kv_ref, pek_ref, pev_ref, wk1_ref, wv1_ref, wk2_ref, wv2t_ref,
                     kc_ref, vct_ref, xs_ref, xk_ref, xv_ref):
    t = kv_ref.shape[0]
    hd = ATT_HD
    xs_ref[0:t, :] = kv_ref[...].astype(F32)
    xs_ref[t:, :] = jnp.zeros((xs_ref.shape[0] - t, xs_ref.shape[1]), F32)
    for l in range(CMP_LEN):
        blk = xs_ref[pl.ds(l, N_CMP_PAD, stride=CMP_STRIDE), :]
        xk_ref[:, l * hd:(l + 1) * hd] = (blk[:, 0:hd] + pek_ref[l:l + 1, :]).astype(BF16)
        xv_ref[:, l * hd:(l + 1) * hd] = (blk[:, hd:2 * hd] + pev_ref[l:l + 1, :]).astype(BF16)
    hk = _silu(_dot(xk_ref[...], wk1_ref[...])).astype(BF16)
    hv = _silu(_dot(xv_ref[...], wv1_ref[...])).astype(BF16)
    kc_ref[...] = _dot(hk, wk2_ref[...])
    vct_ref[...] = _dot_nt(wv2t_ref[...], hv)


def _compress(proj3, pek, pev, wk1, wv1, wk2, wv2t):
    b, t, _ = proj3.shape
    kv_unit0 = RNN_COLS // LANE
    flat = CMP_LEN * ATT_HD
    full = lambda shape: pl.BlockSpec(shape, lambda i, g: (0,) * len(shape))
    return pl.pallas_call(
        _compress_kernel,
        out_shape=(jax.ShapeDtypeStruct((b, G_KV, N_CMP_PAD, ATT_HD), F32),
                   jax.ShapeDtypeStruct((b, G_KV, ATT_HD, N_CMP_PAD), F32)),
        grid=(b, G_KV),
        in_specs=[pl.BlockSpec((None, t, LANE), lambda i, g: (i, 0, kv_unit0 + g)),
                  full((CMP_LEN, ATT_HD)), full((CMP_LEN, ATT_HD)),
                  full((flat, CMP_HIDDEN)), full((flat, CMP_HIDDEN)),
                  full((CMP_HIDDEN, ATT_HD)), full((ATT_HD, CMP_HIDDEN))],
        out_specs=(pl.BlockSpec((None, None, N_CMP_PAD, ATT_HD), lambda i, g: (i, g, 0, 0)),
                   pl.BlockSpec((None, None, ATT_HD, N_CMP_PAD), lambda i, g: (i, g, 0, 0))),
        scratch_shapes=[pltpu.VMEM((t + CMP_STRIDE, LANE), F32),
                        pltpu.VMEM((N_CMP_PAD, flat), BF16),
                        pltpu.VMEM((N_CMP_PAD, flat), BF16)],
        compiler_params=_cparams(("arbitrary", "arbitrary")),
        name="compress",
    )(proj3, pek, pev, wk1, wv1, wk2, wv2t)


def _cmp_select_kernel(qt_ref, kc_ref, vct_ref, slope_ref, ov_ref, ocmp_ref, selb_ref, used_ref, *, n_cmp, n_sel):
    nct = CMP_TILE // LANE
    tt = CMP_TILE
    ns = ov_ref.shape[0]
    kc = kc_ref[...].astype(BF16)
    vct = vct_ref[...].astype(BF16)
    ov = ov_ref[...]
    j_i = lax.broadcasted_iota(jnp.int32, (8, tt), 0)
    tq_rel = lax.broadcasted_iota(jnp.int32, (8, tt), 1)
    n_tiles = qt_ref.shape[0] // nct

    def ranges(tile):
        t0 = tile * tt
        n_any = min(N_CMP_PAD, (t0 + tt) // CMP_STRIDE)
        n_all = max(0, (t0 - (CMP_LEN - 1)) // CMP_STRIDE + 1) // 8 * 8
        return t0, n_any, n_all, (t0 + tt) // SLC_LEN

    def pad_rows(x):
        if x.shape[0] == N_CMP_PAD:
            return x
        return jnp.concatenate([x, jnp.zeros((N_CMP_PAD - x.shape[0], x.shape[1]), x.dtype)], axis=0)

    scores = []
    for tile in range(n_tiles):
        _, n_any, _, _ = ranges(tile)
        qt = [jnp.concatenate([qt_ref[tile * nct + c, h * ATT_HD:(h + 1) * ATT_HD, :] for c in range(nct)],
                              axis=1) for h in range(HG)]
        scores.append(_dot(kc[:n_any], jnp.concatenate(qt, axis=1)))

    probs, psums = [], []
    for tile in range(n_tiles):
        t0, n_any, n_all, _ = ranges(tile)
        n_b = lax.broadcasted_iota(jnp.int32, (n_any - n_all, tt), 0) + n_all
        t_b = lax.broadcasted_iota(jnp.int32, (n_any - n_all, tt), 1)
        visible = (CMP_STRIDE * n_b + (CMP_LEN - 1) - t0 <= t_b) & (n_b < n_cmp)
        centre = (lax.broadcasted_iota(jnp.int32, (n_any, tt), 0).astype(F32) * float(CMP_STRIDE)
                  + (CMP_LEN - 1) / 2.0)
        psum = jnp.zeros((n_any, tt), F32)
        ph = []
        for h in range(HG):
            s = scores[tile][:, h * tt:(h + 1) * tt] + slope_ref[h] * LOG2E * centre
            band = jnp.where(visible, s[n_all:], -jnp.inf)
            s = band if n_all == 0 else jnp.concatenate([s[:n_all], band], axis=0)
            m = jnp.max(s, axis=0, keepdims=True)
            m = jnp.where(m == -jnp.inf, 0.0, m)
            p = jnp.exp2(s - m)
            d = jnp.sum(p, axis=0, keepdims=True)
            p = p * (1.0 / jnp.where(d > 0, d, 1.0))
            psum = psum + p
            ph.append(pad_rows(p).astype(BF16))
        probs.append(jnp.concatenate(ph, axis=1))
        psums.append(pad_rows(psum))

    for tile in range(n_tiles):
        o = _dot(vct, probs[tile])
        for h in range(HG):
            for c in range(nct):
                lanes = slice(h * tt + c * LANE, h * tt + (c + 1) * LANE)
                ocmp_ref[tile * nct + c, h * ATT_HD:(h + 1) * ATT_HD, :] = o[:, lanes]

    imps = []
    for tile in range(n_tiles):
        psum = psums[tile]
        hi = psum.astype(BF16)
        rem = psum - hi.astype(F32)
        mid = rem.astype(BF16)
        lo = (rem - mid.astype(F32)).astype(BF16)
        imp3 = _dot(ov, jnp.concatenate([hi, mid, lo], axis=1))
        imps.append(imp3[:, 0:tt] + imp3[:, tt:2 * tt] + imp3[:, 2 * tt:3 * tt])

    for tile in range(n_tiles):
        t0, _, _, nvb = ranges(tile)
        imp = imps[tile]
        score = []
        for rg in range(nvb // 8):
            j = j_i + 8 * rg
            tq = tq_rel + t0
            cur = tq // SLC_LEN
            forced = (j == 0) | (j == cur) | (j == cur - 1)
            sc = imp[8 * rg:8 * rg + 8, :] + jnp.where(forced, FORCE_BONUS, 0.0)
            score.append(jnp.where(j * SLC_LEN <= tq, sc, -jnp.inf))
        rank = [jnp.zeros((8, tt), F32) for _ in score]
        for jp in range(nvb):
            sj = jnp.broadcast_to(score[jp // 8][jp % 8:jp % 8 + 1, :], (8, tt))
            for rg in range(nvb // 8):
                ge = jnp.where(sj >= score[rg], 1.0, 0.0)
                gt = jnp.where(sj > score[rg], 1.0, 0.0)
                if 8 * rg > jp:
                    ahead = ge
                elif 8 * rg + 7 <= jp:
                    ahead = gt
                else:
                    ahead = jnp.where(j_i + 8 * rg > jp, ge, gt)
                rank[rg] = rank[rg] + ahead
        for rg in range(ns // 8):
            for c in range(nct):
                if rg < nvb // 8:
                    picked = rank[rg][:, c * LANE:(c + 1) * LANE] < float(n_sel)
                    selb = jnp.where(picked, 0.0, NEG_BIG)
                    used = jnp.max(jnp.where(picked, 1.0, 0.0), axis=1, keepdims=True)
                else:
                    selb = jnp.full((8, LANE), NEG_BIG, F32)
                    used = jnp.zeros((8, 1), F32)
                selb_ref[tile * nct + c, 8 * rg:8 * rg + 8, :] = selb
                used_ref[tile * nct + c, 8 * rg:8 * rg + 8, :] = jnp.broadcast_to(used, (8, LANE))


def _cmp_select(qv, kc, vct, slopes, overlap_t, t):
    b = qv.shape[0]
    ns = t // SLC_LEN
    nch = t // LANE
    n_cmp = (t - CMP_LEN) // CMP_STRIDE + 1
    n_sel = min(SLC_TOPK, ns)
    rows = HG * ATT_HD
    return pl.pallas_call(
        functools.partial(_cmp_select_kernel, n_cmp=n_cmp, n_sel=n_sel),
        out_shape=(jax.ShapeDtypeStruct((b, nch, H_ATT * ATT_HD, LANE), F32),
                   jax.ShapeDtypeStruct((b, G_KV, nch, ns, LANE), F32),
                   jax.ShapeDtypeStruct((b, G_KV, nch, ns, LANE), F32)),
        grid=(b, G_KV),
        in_specs=[pl.BlockSpec((None, nch, rows, LANE), lambda i, g: (i, 0, g, 0)),
                  pl.BlockSpec((None, None, N_CMP_PAD, ATT_HD), lambda i, g: (i, g, 0, 0)),
                  pl.BlockSpec((None, None, ATT_HD, N_CMP_PAD), lambda i, g: (i, g, 0, 0)),
                  pl.BlockSpec((None, HG, 1, 1), lambda i, g: (g, 0, 0, 0)),
                  pl.BlockSpec((ns, N_CMP_PAD), lambda i, g: (0, 0))],
        out_specs=(pl.BlockSpec((None, nch, rows, LANE), lambda i, g: (i, 0, g, 0)),
                   pl.BlockSpec((None, None, nch, ns, LANE), lambda i, g: (i, g, 0, 0, 0)),
                   pl.BlockSpec((None, None, nch, ns, LANE), lambda i, g: (i, g, 0, 0, 0))),
        compiler_params=_cparams(("arbitrary", "arbitrary")),
        name="cmp_select",
    )(qv, kc, vct, slopes, overlap_t)


def _sparse_kernel(n_extra_ref, extra_ref, qt_ref, k_ref, ka_ref, vs_ref, vw_ref, gt_ref, ocmp_ref, selb_ref,
                   slope_ref, o_ref, msk_ref, qop_ref, s_ref, m_ref, acc_ref):
    tq = ATT_TILE
    kc = ATT_TILE
    hd = ATT_HD
    nl = HG * tq
    n_tiles = qt_ref.shape[0]
    ns = selb_ref.shape[1]
    wch = WINDOW // kc
    n_sel_items = 2 * (SEL_NEAR + 1)
    tile_base = (pl.program_id(0) * pl.num_programs(1) + pl.program_id(1)) * n_tiles
    slope = jnp.concatenate([jnp.broadcast_to(slope_ref[h], (1, tq)) for h in range(HG)], axis=1) * LOG2E
    s_hi = slope.astype(BF16).astype(F32)
    s_mid = (slope - s_hi).astype(BF16).astype(F32)
    s_lo = (slope - s_hi - s_mid).astype(BF16).astype(F32)
    slope_rows = jnp.concatenate([s_hi, s_mid, s_lo, jnp.zeros((AUG_SEL_ROW0 - 3, nl), F32)], axis=0)
    aug_tail = jnp.zeros((LANE - AUG_SEL_ROW0 - ns, nl), F32)
    aug_win = jnp.concatenate([slope_rows, jnp.zeros((ns, nl), F32), aug_tail], axis=0).astype(BF16)
    k_i = lax.broadcasted_iota(jnp.int32, (kc, nl), 0)
    t_i = lax.broadcasted_iota(jnp.int32, (kc, nl), 1) % tq
    d0 = t_i - k_i
    msk_ref[0] = jnp.where(d0 >= 0, 0.0, NEG_BIG)
    msk_ref[1] = jnp.where(d0 < 0, 0.0, NEG_BIG)
    chunk_bias = slope * float(kc)
    ones = jnp.ones((ONES_ROWS, kc), BF16)

    n_pairs = n_tiles // 2
    n_items = n_sel_items + 2 * (wch + 1)
    n_slots = s_ref.shape[0] // 2

    def tiles_of(p):
        return (p, n_tiles - 1 - p)

    def sel_item(p, w, j):
        tile = tiles_of(p)[w]
        if j == SEL_NEAR:
            return 0, tile >= SEL_NEAR, tile
        return max(tile - j, 0), tile - j >= 0, j

    def win_item(p, w, j):
        chunk = tiles_of(p)[w] - wch + j
        return max(chunk, 0), chunk >= 0, wch - j

    def col_max(r, delta_f):
        r8 = jnp.max(r.reshape(kc // 8, 8, nl), axis=0)
        return r8 - chunk_bias * delta_f

    def scores(chunk, op):
        rows = pl.ds(pl.multiple_of(chunk * kc, kc), kc)
        keys = jnp.concatenate([k_ref[rows, :], ka_ref[rows, :]], axis=1)
        return _dot(keys, qop_ref[op])

    def build(p, slot):
        for w, tile in enumerate(tiles_of(p)):
            qt = qt_ref[tile]
            q_all = jnp.concatenate([qt[h * hd:(h + 1) * hd, :] for h in range(HG)], axis=1)
            zero = jnp.zeros_like(q_all)
            selb = jnp.concatenate([selb_ref[tile]] * HG, axis=1)
            aug_sel = jnp.concatenate([slope_rows, selb, aug_tail], axis=0).astype(BF16)
            qop_ref[4 * slot + 2 * w] = jnp.concatenate([q_all, zero, aug_sel], axis=0)
            qop_ref[4 * slot + 2 * w + 1] = jnp.concatenate([zero, q_all, aug_win], axis=0)
        for x in range(4):
            m_ref[4 * slot + x] = jnp.full((8, nl), NEG_BIG, F32)

    def phase_a(p, slot):
        items = []

        def sel(w, j):
            chunk, _, delta = sel_item(p, w, j)
            r = scores(chunk, 4 * slot + 2 * w)
            if j == 0:
                r = r + msk_ref[0]
            s_ref[slot * n_slots + w * (SEL_NEAR + 1) + j] = r
            x = 4 * slot + 2 * w
            m_ref[x] = jnp.maximum(m_ref[x], col_max(r, float(delta)))

        def win(w, j):
            chunk, _, delta = win_item(p, w, j)
            r = scores(chunk, 4 * slot + 2 * w + 1)
            if j == 0:
                r = r + msk_ref[1]
            elif j == wch:
                r = r + msk_ref[0]
            s_ref[slot * n_slots + n_sel_items + w * (wch + 1) + j] = r
            x = 4 * slot + 2 * w + 1
            m_ref[x] = jnp.maximum(m_ref[x], col_max(r, float(delta)))

        for w in range(2):
            for j in range(SEL_NEAR + 1):
                if sel_item(p, w, j)[1]:
                    items.append(functools.partial(sel, w, j))
        for w in range(2):
            for j in range(wch + 1):
                if win_item(p, w, j)[1]:
                    items.append(functools.partial(win, w, j))
        return items

    def overflow(p, slot, visit):
        count = 0
        for w, tile in enumerate(tiles_of(p)):
            def body(e, n, w=w, tile=tile):
                visit(w, tile, extra_ref[(tile_base + tile) * n_tiles + e], n)
                return n + 1

            count = lax.fori_loop(0, n_extra_ref[tile_base + tile], body, count)

    def overflow_a(p, slot):
        def visit(w, tile, c, n):
            r = scores(c, 4 * slot + 2 * w)
            s_ref[slot * n_slots + n_items + n] = r
            x = 4 * slot + 2 * w
            m_ref[x] = jnp.maximum(m_ref[x], col_max(r, jnp.float32(tile - c)))

        overflow(p, slot, visit)

    def overflow_b(p, slot, m_row):
        def visit(w, tile, c, n):
            row = m_row[2 * w] + chunk_bias * jnp.float32(tile - c)
            pr = jnp.exp2(s_ref[slot * n_slots + n_items + n] - row).astype(BF16)
            v_aug = jnp.concatenate([vs_ref[c], ones], axis=0)
            acc_ref[4 * slot + 2 * w] += _dot(v_aug, pr)

        overflow(p, slot, visit)

    def phase_b(p, slot, m_row):
        for x in range(4):
            acc_ref[4 * slot + x] = jnp.zeros(acc_ref.shape[1:], F32)
        items = []

        def pv(x, v_ref, group):
            prs, vs = [], []
            for s_idx, chunk, delta in group:
                row = m_row[x] + chunk_bias * float(delta)
                prs.append(jnp.exp2(s_ref[slot * n_slots + s_idx] - row).astype(BF16))
                vs.append(jnp.concatenate([v_ref[chunk], ones], axis=0))
            acc_ref[4 * slot + x] += _dot(jnp.concatenate(vs, axis=1), jnp.concatenate(prs, axis=0))

        for w in range(2):
            sel = [(w * (SEL_NEAR + 1) + j,) + sel_item(p, w, j)[::2]
                   for j in range(SEL_NEAR + 1) if sel_item(p, w, j)[1]]
            win = [(n_sel_items + w * (wch + 1) + j,) + win_item(p, w, j)[::2]
                   for j in range(wch + 1) if win_item(p, w, j)[1]]
            for x, v_ref, group in ((2 * w, vs_ref, sel), (2 * w + 1, vw_ref, win)):
                for i in range(0, len(group), 2):
                    items.append(functools.partial(pv, x, v_ref, group[i:i + 2]))
        return items

    def finalize(p, slot):
        for w, tile in enumerate(tiles_of(p)):
            a_s = acc_ref[4 * slot + 2 * w]
            a_w = acc_ref[4 * slot + 2 * w + 1]
            o_slc = a_s[:hd, :] / a_s[hd:hd + 1, :]
            o_win = a_w[:hd, :] / a_w[hd:hd + 1, :]
            oc = ocmp_ref[tile]
            outs = []
            for h in range(HG):
                lanes = slice(h * tq, (h + 1) * tq)
                row0 = (pl.program_id(1) * HG + h) * 3
                g_cmp, g_slc, g_win = [_sigmoid(gt_ref[tile, pl.ds(row0 + br, 1), :]) for br in range(3)]
                y = (g_cmp * oc[h * hd:(h + 1) * hd, :] + g_slc * o_slc[:, lanes]
                     + g_win * o_win[:, lanes])
                outs.append(y.T)
            rows = pl.ds(pl.multiple_of(tile * tq, tq), tq)
            o_ref[rows, :] = jnp.concatenate(outs, axis=1).astype(o_ref.dtype)

    def step(k, slot, first=False, last=False):
        m_row = [jnp.max(m_ref[4 * slot + x], axis=0, keepdims=True) for x in range(4)]
        if not last:
            build(k + 1, 1 - slot)
        if not first:
            finalize(k - 1, 1 - slot)
        b_items = phase_b(k, slot, m_row)
        a_items = phase_a(k + 1, 1 - slot) if not last else []
        for a, b in itertools.zip_longest(a_items, b_items):
            if a is not None:
                a()
            if b is not None:
                b()
        if not last:
            overflow_a(k + 1, 1 - slot)
        overflow_b(k, slot, m_row)

    build(0, 0)
    for a in phase_a(0, 0):
        a()
    overflow_a(0, 0)
    for k in range(n_pairs):
        step(k, k % 2, first=k == 0, last=k == n_pairs - 1)
    finalize(n_pairs - 1, (n_pairs - 1) % 2)


def _key_features(t):
    kp = np.arange(t)
    f = np.zeros((t, LANE), np.float32)
    f[:, 0:3] = (kp % ATT_TILE)[:, None]
    f[kp, AUG_SEL_ROW0 + kp // SLC_LEN] = 1.0
    return jnp.asarray(f, dtype=BF16)


def _extra_chunks(used, t):
    b = used.shape[0]
    nt = t // ATT_TILE
    need = used.max(axis=-1).reshape(b, G_KV, nt, nt, ATT_TILE // SLC_LEN).max(axis=-1) > 0.5
    chunk = jnp.arange(nt, dtype=jnp.int32)
    extra = need & (chunk >= 1) & (chunk[None, :] <= chunk[:, None] - SEL_NEAR)
    place = jnp.cumsum(extra.astype(jnp.int32), axis=-1) - 1
    hit = extra[..., :, None] & (place[..., :, None] == chunk)
    lst = jnp.sum(jnp.where(hit, chunk[:, None], 0), axis=-2)
    return extra.sum(axis=-1, dtype=jnp.int32).reshape(-1), lst.reshape(-1)


def _sparse(n_extra, extra, qv, gt, proj3, ocmp, selb, slopes, t):
    b = qv.shape[0]
    nchunk = t // LANE
    ns = t // SLC_LEN
    rows = HG * ATT_HD
    ksw_unit0 = RNN_COLS // LANE + G_KV
    vs_blk0 = QT_ROWS // ATT_HD
    vw_blk0 = vs_blk0 + G_KV
    nl = HG * ATT_TILE
    n_slots = 2 * (SEL_NEAR + 1) + 2 * (WINDOW // ATT_TILE + 1) + (nchunk - 1 - SEL_NEAR)
    grid_spec = pltpu.PrefetchScalarGridSpec(
        num_scalar_prefetch=2,
        grid=(b, G_KV),
        in_specs=[pl.BlockSpec((None, nchunk, rows, LANE), lambda i, g, *_: (i, 0, g, 0)),
                  pl.BlockSpec((None, t, LANE), lambda i, g, *_: (i, 0, ksw_unit0 + g)),
                  pl.BlockSpec((t, LANE), lambda i, g, *_: (0, 0)),
                  pl.BlockSpec((None, nchunk, ATT_HD, LANE), lambda i, g, *_: (i, 0, vs_blk0 + g, 0)),
                  pl.BlockSpec((None, nchunk, ATT_HD, LANE), lambda i, g, *_: (i, 0, vw_blk0 + g, 0)),
                  pl.BlockSpec((None, nchunk, GATE_ROWS, LANE), lambda i, g, *_: (i, 0, 0, 0)),
                  pl.BlockSpec((None, nchunk, rows, LANE), lambda i, g, *_: (i, 0, g, 0)),
                  pl.BlockSpec((None, None, nchunk, ns, LANE), lambda i, g, *_: (i, g, 0, 0, 0)),
                  pl.BlockSpec((None, HG, 1, 1), lambda i, g, *_: (g, 0, 0, 0))],
        out_specs=pl.BlockSpec((None, t, rows), lambda i, g, *_: (i, 0, g)),
        scratch_shapes=[pltpu.VMEM((2, ATT_TILE, nl), F32),
                        pltpu.VMEM((2 * 4, 2 * LANE, nl), BF16),
                        pltpu.VMEM((2 * n_slots, ATT_TILE, nl), F32),
                        pltpu.VMEM((2 * 4, 8, nl), F32),
                        pltpu.VMEM((2 * 4, ATT_HD + ONES_ROWS, nl), F32)])
    return pl.pallas_call(
        _sparse_kernel,
        out_shape=jax.ShapeDtypeStruct((b, t, H_ATT * ATT_HD), BF16),
        grid_spec=grid_spec,
        compiler_params=_cparams(("arbitrary", "arbitrary")),
        name="sparse",
    )(n_extra, extra, qv, proj3, _key_features(t), qv, qv, gt, ocmp, selb, slopes)


def _merge_kernel(x_ref, yr_ref, ya_ref, mgr_ref, mga_ref, wr_ref, wa_ref, wo_ref, o_ref):
    pr = _dot(yr_ref[...], wr_ref[...])
    pa = _dot(ya_ref[...], wa_ref[...])
    merged = _sigmoid(mgr_ref[...].astype(F32)) * pr + _sigmoid(mga_ref[...].astype(F32)) * pa
    o_ref[...] = x_ref[...] + _dot(merged.astype(BF16), wo_ref[...])


def _merge(x2, yr, ya, proj, wr, wa, wo):
    m, d = x2.shape
    mg0 = (RNN_COLS + KV_COLS) // d
    tile = lambda col: pl.BlockSpec((TOK_TILE, d), lambda i: (i, col))
    wfull = pl.BlockSpec((d, d), lambda i: (0, 0))
    return pl.pallas_call(
        _merge_kernel,
        out_shape=jax.ShapeDtypeStruct((m, d), F32),
        grid=(m // TOK_TILE,),
        in_specs=[tile(0), tile(0), tile(0), tile(mg0), tile(mg0 + 1), wfull, wfull, wfull],
        out_specs=tile(0),
        compiler_params=_cparams(("arbitrary",)),
        name="merge",
    )(x2, yr, ya, proj, proj, wr, wa, wo)


def _mem_kv_kernel(mem_ref, g_ref, wkt_ref, wv_ref, kt_ref, v_ref):
    a = _rms(mem_ref[...], g_ref[...]).astype(BF16)
    kt_ref[...] = _dot_nt(wkt_ref[...], a).astype(kt_ref.dtype)
    v_ref[...] = _dot(a, wv_ref[...]).astype(v_ref.dtype)


def _mem_kv(mem, g, wkt, wkv):
    b, nm, d = mem.shape
    hw = H_X * X_HD
    return pl.pallas_call(
        _mem_kv_kernel,
        out_shape=(jax.ShapeDtypeStruct((b, hw, nm), BF16),
                   jax.ShapeDtypeStruct((b, nm, hw), BF16)),
        grid=(b,),
        in_specs=[pl.BlockSpec((None, nm, d), lambda i: (i, 0, 0)),
                  pl.BlockSpec((1, d), lambda i: (0, 0)),
                  pl.BlockSpec((hw, d), lambda i: (0, 0)),
                  pl.BlockSpec((d, hw), lambda i: (0, 1))],
        out_specs=(pl.BlockSpec((None, hw, nm), lambda i: (i, 0, 0)),
                   pl.BlockSpec((None, nm, hw), lambda i: (i, 0, 0))),
        compiler_params=_cparams(("arbitrary",)),
        name="mem_kv",
    )(mem, g, wkt, wkv)


def _xattn_kernel(h_ref, g_ref, wq_ref, kt_ref, v_ref, wo_ref, o_ref):
    h = h_ref[...]
    a = _rms(h, g_ref[...]).astype(BF16)
    q = (_dot(a, wq_ref[...]) * (X_HD ** -0.5)).astype(BF16)
    heads = [slice(hh * X_HD, (hh + 1) * X_HD) for hh in range(H_X)]
    scores = [_dot(q[:, cols], kt_ref[cols, :]) for cols in heads]
    probs = []
    for s in scores:
        m = jnp.max(s, axis=-1, keepdims=True)
        p = jnp.exp(s - m)
        probs.append((p / jnp.sum(p, axis=-1, keepdims=True)).astype(BF16))
    outs = [_dot(p, v_ref[:, cols]) for p, cols in zip(probs, heads)]
    o = jnp.concatenate(outs, axis=1).astype(BF16)
    o_ref[...] = h + _dot(o, wo_ref[...])


def _xattn(h3, g, wq, kt, v, wo):
    b, t, d = h3.shape
    hw = H_X * X_HD
    nm = v.shape[1]
    return pl.pallas_call(
        _xattn_kernel,
        out_shape=jax.ShapeDtypeStruct((b, t, d), F32),
        grid=(b, t // TOK_TILE),
        in_specs=[pl.BlockSpec((None, TOK_TILE, d), lambda i, j: (i, j, 0)),
                  pl.BlockSpec((1, d), lambda i, j: (0, 0)),
                  pl.BlockSpec((d, hw), lambda i, j: (0, 0)),
                  pl.BlockSpec((None, hw, nm), lambda i, j: (i, 0, 0)),
                  pl.BlockSpec((None, nm, hw), lambda i, j: (i, 0, 0)),
                  pl.BlockSpec((hw, d), lambda i, j: (0, 0))],
        out_specs=pl.BlockSpec((None, TOK_TILE, d), lambda i, j: (i, j, 0)),
        compiler_params=_cparams(("arbitrary", "arbitrary")),
        name="xattn",
    )(h3, g, wq, kt, v, wo)


def _ffn_kernel(h_ref, g_ref, wg_ref, wu_ref, wd_ref, gf_ref, o_ref, *, fc):
    h = h_ref[...]
    a = _rms(h, g_ref[...]).astype(BF16)
    acc = h
    for f in range(wg_ref.shape[1] // fc):
        cols = slice(f * fc, (f + 1) * fc)
        mid = _silu(_dot(a, wg_ref[:, cols])) * _dot(a, wu_ref[:, cols])
        acc = acc + _dot(mid.astype(BF16), wd_ref[cols, :])
    o_ref[...] = _rms(acc, gf_ref[...])


def _ffn(h2, g, wgu, wd, gf):
    m, d = h2.shape
    ff = wd.shape[0]
    const = lambda shape: pl.BlockSpec(shape, lambda i: (0, 0))
    return pl.pallas_call(
        functools.partial(_ffn_kernel, fc=ff // 2),
        out_shape=jax.ShapeDtypeStruct((m, d), F32),
        grid=(m // TOK_TILE,),
        in_specs=[pl.BlockSpec((TOK_TILE, d), lambda i: (i, 0)),
                  const((1, d)), const((d, ff)), pl.BlockSpec((d, ff), lambda i: (0, 1)),
                  const((ff, d)), const((1, d))],
        out_specs=pl.BlockSpec((TOK_TILE, d), lambda i: (i, 0)),
        compiler_params=_cparams(("arbitrary",)),
        name="ffn",
    )(h2, g, wgu, wgu, wd, gf)


def _overlap_t(t):
    nc = (t - CMP_LEN) // CMP_STRIDE + 1
    ns = t // SLC_LEN
    starts = CMP_STRIDE * np.arange(N_CMP_PAD)
    s_start = SLC_LEN * np.arange(ns)
    ov = ((starts[None, :] + CMP_LEN > s_start[:, None]) & (starts[None, :] < s_start[:, None] + SLC_LEN)
          & (np.arange(N_CMP_PAD)[None, :] < nc))
    return jnp.asarray(ov.astype(np.float32), dtype=BF16)


def kernel(x, mem, g_mix, w_in, lower_bounds, g_rnn_out, pe_ck, w_ck1, w_ck2, pe_cv, w_cv1, w_cv2,
           w_proj_rnn, w_proj_att, w_out, g_xattn, g_mem, w_xq, w_xkv, w_xo, g_ffn, w_gate_up,
           w_down, g_final):
    b, t, d = x.shape
    depth = g_mix.shape[0]
    assert depth == 1, "the final RMSNorm is fused into the layer's FFN kernel"
    lbs = jnp.cumsum(jax.nn.softmax(lower_bounds.astype(F32), axis=0), axis=0)
    slopes = (2.0 ** (-8.0 * jnp.arange(1, H_ATT + 1, dtype=F32) / H_ATT)).reshape(G_KV, HG, 1, 1)
    overlap_t = _overlap_t(t)
    h = x
    for l in range(depth):
        w_n, w_t = _w_prep(w_in[l].T)
        x2 = h.reshape(b * t, d)
        proj = _in_proj(x2, g_mix[l][None, :], w_n)
        proj3 = proj.reshape(b, t, N_COLS)
        qv, gt = _in_proj_t(h, g_mix[l][None, :], w_t)
        y_r = _hgrn(proj3, lbs[l][None, :], g_rnn_out[l][None, :])
        kc, vct = _compress(proj3, pe_ck[l], pe_cv[l], w_ck1[l].astype(BF16), w_cv1[l].astype(BF16),
                            w_ck2[l].astype(BF16), w_cv2[l].T.astype(BF16))
        ocmp, selb, used = _cmp_select(qv, kc, vct, slopes, overlap_t, t)
        n_extra, extra = _extra_chunks(used, t)
        y_a = _sparse(n_extra, extra, qv, gt, proj3, ocmp, selb, slopes, t)
        h1 = _merge(x2, y_r.reshape(b * t, d), y_a.reshape(b * t, d), proj,
                    w_proj_rnn[l].astype(BF16), w_proj_att[l].astype(BF16), w_out[l].astype(BF16))
        w_kv = w_xkv[l].astype(BF16)
        kt, v = _mem_kv(mem, g_mem[l][None, :], w_kv[:, :H_X * X_HD].T, w_kv)
        h2 = _xattn(h1.reshape(b, t, d), g_xattn[l][None, :], w_xq[l].astype(BF16), kt, v,
                    w_xo[l].astype(BF16))
        h = _ffn(h2.reshape(b * t, d), g_ffn[l][None, :], w_gate_up[l].astype(BF16),
                 w_down[l].astype(BF16), g_final[None, :]).reshape(b, t, d)
    return h
```

```python
import functools
import itertools

import jax
import jax.numpy as jnp
import numpy as np
from jax import lax
from jax.experimental import pallas as pl
from jax.experimental.pallas import tpu as pltpu

F32 = jnp.float32
BF16 = jnp.bfloat16

D_MODEL = 1024
N_MEM = 256
H_RNN = 8
RNN_DIM = 128
RNN_CHUNK = 64
H_ATT = 16
ATT_HD = 64
G_KV = 4
HG = H_ATT // G_KV
CMP_LEN = 32
CMP_STRIDE = 16
CMP_HIDDEN = 128
SLC_LEN = 64
SLC_TOPK = 8
WINDOW = 512
FORCE_BONUS = 1.0e4
H_X = 4
X_HD = 128
D_FF = 2816
EPS = 1e-6

LANE = 128
VMEM_LIMIT = 56 * 1024 * 1024
TOK_TILE = 512
ATT_TILE = 128
CMP_TILE = 512
N_CMP_PAD = 128
HGRN_GROUP = 4
ONES_ROWS = 16
NEG_BIG = -1.0e30
LOG2E = 1.4426950408889634
AUG_SEL_ROW0 = 8
SEL_NEAR = 5

QT_ROWS = H_ATT * ATT_HD
VT_ROWS = 2 * G_KV * ATT_HD
GATE_ROWS = 64
F_ROWS = -(-(QT_ROWS + VT_ROWS + 3 * H_ATT) // LANE) * LANE
RNN_COLS = 4 * H_RNN * RNN_DIM
KV_COLS = 4 * G_KV * ATT_HD
MG_COLS = 2 * D_MODEL
N_COLS = RNN_COLS + KV_COLS + MG_COLS


def _cparams(sem):
    return pltpu.CompilerParams(dimension_semantics=sem, vmem_limit_bytes=VMEM_LIMIT)


def _rms(xf, g):
    return xf * lax.rsqrt(jnp.mean(xf * xf, axis=-1, keepdims=True) + EPS) * g


def _sigmoid(x):
    return 1.0 / (1.0 + jnp.exp(-x))


def _silu(x):
    return x * _sigmoid(x)


def _dot(a, b):
    return jnp.dot(a, b, preferred_element_type=F32)


def _dot_nt(a, b):
    return lax.dot_general(a, b, (((1,), (1,)), ((), ())), preferred_element_type=F32)


def _dot_tn(a, b):
    return lax.dot_general(a, b, (((0,), (0,)), ((), ())), preferred_element_type=F32)


def _w_prep_kernel(w_ref, wn_ref, wf_ref):
    kvw = G_KV * ATT_HD
    kv0 = RNN_COLS + QT_ROWS
    gate0 = kv0 + 6 * kvw
    mg0 = gate0 + 3 * H_ATT

    def pair_by_group(a0, b0):
        parts = []
        for g in range(G_KV):
            parts += [w_ref[a0 + g * ATT_HD:a0 + (g + 1) * ATT_HD, :], w_ref[b0 + g * ATT_HD:b0 + (g + 1) * ATT_HD, :]]
        return parts

    wn = jnp.concatenate([w_ref[0:RNN_COLS, :]] + pair_by_group(kv0, kv0 + kvw)
                         + pair_by_group(kv0 + 2 * kvw, kv0 + 4 * kvw) + [w_ref[mg0:mg0 + MG_COLS, :]], axis=0)
    wn_ref[...] = wn.T.astype(BF16)
    used = QT_ROWS + VT_ROWS + 3 * H_ATT
    wf = jnp.concatenate([w_ref[RNN_COLS:kv0, :] * (ATT_HD ** -0.5 * LOG2E),
                          w_ref[kv0 + 3 * kvw:kv0 + 4 * kvw, :], w_ref[kv0 + 5 * kvw:kv0 + 6 * kvw, :],
                          w_ref[gate0:mg0, :], jnp.zeros((wf_ref.shape[0] - used, w_ref.shape[1]), F32)], axis=0)
    wf_ref[...] = wf.astype(BF16)


def _w_prep(wt):
    n_in, d = wt.shape
    return pl.pallas_call(
        _w_prep_kernel,
        out_shape=(jax.ShapeDtypeStruct((d, N_COLS), BF16), jax.ShapeDtypeStruct((F_ROWS, d), BF16)),
        grid=(d // LANE,),
        in_specs=[pl.BlockSpec((n_in, LANE), lambda i: (0, i))],
        out_specs=(pl.BlockSpec((LANE, N_COLS), lambda i: (i, 0)),
                   pl.BlockSpec((F_ROWS, LANE), lambda i: (0, i))),
        compiler_params=_cparams(("arbitrary",)),
        name="w_prep",
    )(wt)


def _in_proj_kernel(x_ref, g_ref, w_ref, o_ref, *, sub):
    a = _rms(x_ref[...], g_ref[...]).astype(BF16)
    for n in range(o_ref.shape[1] // sub):
        cols = slice(n * sub, (n + 1) * sub)
        o_ref[:, cols] = _dot(a, w_ref[:, cols]).astype(o_ref.dtype)


def _in_proj(x2, g, w):
    m, d = x2.shape
    n = w.shape[1]
    tn = n // 2
    return pl.pallas_call(
        functools.partial(_in_proj_kernel, sub=512),
        out_shape=jax.ShapeDtypeStruct((m, n), BF16),
        grid=(2, m // TOK_TILE),
        in_specs=[pl.BlockSpec((TOK_TILE, d), lambda j, i: (i, 0)),
                  pl.BlockSpec((1, d), lambda j, i: (0, 0)),
                  pl.BlockSpec((d, tn), lambda j, i: (0, j))],
        out_specs=pl.BlockSpec((TOK_TILE, tn), lambda j, i: (i, j)),
        compiler_params=_cparams(("arbitrary", "arbitrary")),
        name="in_proj",
    )(x2, g, w)


def _in_proj_t_kernel(x_ref, g_ref, wt_ref, qv_ref, gt_ref):
    a = _rms(x_ref[...], g_ref[...]).astype(BF16)
    r = _dot_nt(wt_ref[...], a)
    nqv = qv_ref.shape[1]
    ngt = gt_ref.shape[1]
    for c in range(qv_ref.shape[0]):
        qv_ref[c] = r[:nqv, c * LANE:(c + 1) * LANE].astype(qv_ref.dtype)
        gt_ref[c] = r[nqv:nqv + ngt, c * LANE:(c + 1) * LANE]


def _in_proj_t(x, g, wt):
    b, t, d = x.shape
    rows = wt.shape[0]
    nqv = QT_ROWS + VT_ROWS
    ngt = GATE_ROWS
    nc = TOK_TILE // LANE
    return pl.pallas_call(
        _in_proj_t_kernel,
        out_shape=(jax.ShapeDtypeStruct((b, t // LANE, nqv, LANE), BF16),
                   jax.ShapeDtypeStruct((b, t // LANE, ngt, LANE), F32)),
        grid=(b, t // TOK_TILE),
        in_specs=[pl.BlockSpec((None, TOK_TILE, d), lambda i, j: (i, j, 0)),
                  pl.BlockSpec((1, d), lambda i, j: (0, 0)),
                  pl.BlockSpec((rows, d), lambda i, j: (0, 0))],
        out_specs=(pl.BlockSpec((None, nc, nqv, LANE), lambda i, j: (i, j, 0, 0)),
                   pl.BlockSpec((None, nc, ngt, LANE), lambda i, j: (i, j, 0, 0))),
        compiler_params=_cparams(("arbitrary", "arbitrary")),
        name="in_proj_t",
    )(x, g, wt)


def _hgrn_kernel(q_ref, f_ref, i_ref, og_ref, lb_ref, gn_ref, o_ref, qd_ref, oi_ref, ut_ref, dec_ref):
    c = RNN_CHUNK
    kd = RNN_DIM
    n_chunks = q_ref.shape[0] // c
    lb = lb_ref[...]
    gn = gn_ref[...]
    blk = HGRN_GROUP * c
    row = lax.broadcasted_iota(jnp.int32, (blk, blk), 0)
    col = lax.broadcasted_iota(jnp.int32, (blk, blk), 1)
    same_chunk = (row // c) == (col // c)
    causal = same_chunk & (row >= col)
    tril = causal.astype(BF16)

    q = q_ref[...].astype(F32)
    fl = f_ref[...].astype(F32)
    v = i_ref[...]
    f = lb + (1.0 - lb) * _sigmoid(fl)
    k = 1.0 - f
    logf = jnp.log(f)
    hi = logf.astype(BF16)
    lo = (logf - hi.astype(F32)).astype(BF16)
    pieces = jnp.concatenate([hi, lo], axis=1)
    n_blk = q_ref.shape[0] // blk
    cs = [_dot(tril, pieces[g * blk:(g + 1) * blk]) for g in range(n_blk)]
    bcum = jnp.concatenate([x[:, 0:kd] + x[:, kd:2 * kd] for x in cs], axis=0)
    e_neg = jnp.exp(-bcum)
    dec = jnp.exp(jnp.concatenate([bcum[n * c + c - 1:n * c + c, :] for n in range(n_chunks)], axis=0))
    dec_rows = jnp.concatenate([jnp.broadcast_to(dec[n:n + 1, :], (c, kd)) for n in range(n_chunks)], axis=0)
    q_dec = (_silu(q) * (1.0 / e_neg)).astype(BF16)
    k_neg = k * e_neg
    k_dec = k_neg.astype(BF16)
    k_end = (k_neg * dec_rows).astype(BF16)
    qd_ref[...] = q_dec
    dec_ref[...] = dec
    for g in range(n_blk):
        rows = slice(g * blk, (g + 1) * blk)
        a = jnp.where(causal, _dot_nt(q_dec[rows], k_dec[rows]), 0.0)
        oi_ref[rows, :] = _dot(a.astype(BF16), v[rows])
    for n in range(n_chunks):
        rows = slice(n * c, (n + 1) * c)
        ut_ref[n] = _dot_tn(v[rows], k_end[rows])

    s_t = jnp.zeros((kd, kd), F32)
    for n in range(n_chunks):
        rows = slice(n * c, (n + 1) * c)
        o = oi_ref[rows, :] + _dot_nt(qd_ref[rows, :], s_t.astype(BF16))
        og = og_ref[rows, :].astype(F32)
        o_ref[rows, :] = (_rms(o, gn) * _silu(og)).astype(o_ref.dtype)
        s_t = s_t * dec_ref[n:n + 1, :] + ut_ref[n]


def _hgrn(proj3, lb, gn):
    b, t, _ = proj3.shape
    return pl.pallas_call(
        _hgrn_kernel,
        out_shape=jax.ShapeDtypeStruct((b, t, H_RNN * RNN_DIM), BF16),
        grid=(b, H_RNN),
        in_specs=[pl.BlockSpec((None, t, RNN_DIM), lambda i, h: (i, 0, h)),
                  pl.BlockSpec((None, t, RNN_DIM), lambda i, h: (i, 0, H_RNN + h)),
                  pl.BlockSpec((None, t, RNN_DIM), lambda i, h: (i, 0, 2 * H_RNN + h)),
                  pl.BlockSpec((None, t, RNN_DIM), lambda i, h: (i, 0, 3 * H_RNN + h)),
                  pl.BlockSpec((1, RNN_DIM), lambda i, h: (0, h)),
                  pl.BlockSpec((1, RNN_DIM), lambda i, h: (0, 0))],
        out_specs=pl.BlockSpec((None, t, RNN_DIM), lambda i, h: (i, 0, h)),
        scratch_shapes=[pltpu.VMEM((t, RNN_DIM), BF16),
                        pltpu.VMEM((t, RNN_DIM), F32),
                        pltpu.VMEM((t // RNN_CHUNK, RNN_DIM, RNN_DIM), F32),
                        pltpu.VMEM((t // RNN_CHUNK, RNN_DIM), F32)],
        compiler_params=_cparams(("arbitrary", "arbitrary")),
        name="hgrn",
    )(proj3, proj3, proj3, proj3, lb, gn)


def _compress_kernel(kv_ref, pek_ref, pev_ref, wk1_ref, wv1_ref, wk2_ref, wv2t_ref,
                     kc_ref, vct_ref, xs_ref, xk_ref, xv_ref):
    t = kv_ref.shape[0]
    hd = ATT_HD
    xs_ref[0:t, :] = kv_ref[...].astype(F32)
    xs_ref[t:, :] = jnp.zeros((xs_ref.shape[0] - t, xs_ref.shape[1]), F32)
    for l in range(CMP_LEN):
        blk = xs_ref[pl.ds(l, N_CMP_PAD, stride=CMP_STRIDE), :]
        xk_ref[:, l * hd:(l + 1) * hd] = (blk[:, 0:hd] + pek_ref[l:l + 1, :]).astype(BF16)
        xv_ref[:, l * hd:(l + 1) * hd] = (blk[:, hd:2 * hd] + pev_ref[l:l + 1, :]).astype(BF16)
    hk = _silu(_dot(xk_ref[...], wk1_ref[...])).astype(BF16)
    hv = _silu(_dot(xv_ref[...], wv1_ref[...])).astype(BF16)
    kc_ref[...] = _dot(hk, wk2_ref[...])
    vct_ref[...] = _dot_nt(wv2t_ref[...], hv)


def _compress(proj3, pek, pev, wk1, wv1, wk2, wv2t):
    b, t, _ = proj3.shape
    kv_unit0 = RNN_COLS // LANE
    flat = CMP_LEN * ATT_HD
    full = lambda shape: pl.BlockSpec(shape, lambda i, g: (0,) * len(shape))
    return pl.pallas_call(
        _compress_kernel,
        out_shape=(jax.ShapeDtypeStruct((b, G_KV, N_CMP_PAD, ATT_HD), F32),
                   jax.ShapeDtypeStruct((b, G_KV, ATT_HD, N_CMP_PAD), F32)),
        grid=(b, G_KV),
        in_specs=[pl.BlockSpec((None, t, LANE), lambda i, g: (i, 0, kv_unit0 + g)),
                  full((CMP_LEN, ATT_HD)), full((CMP_LEN, ATT_HD)),
                  full((flat, CMP_HIDDEN)), full((flat, CMP_HIDDEN)),
                  full((CMP_HIDDEN, ATT_HD)), full((ATT_HD, CMP_HIDDEN))],
        out_specs=(pl.BlockSpec((None, None, N_CMP_PAD, ATT_HD), lambda i, g: (i, g, 0, 0)),
                   pl.BlockSpec((None, None, ATT_HD, N_CMP_PAD), lambda i, g: (i, g, 0, 0))),
        scratch_shapes=[pltpu.VMEM((t + CMP_STRIDE, LANE), F32),
                        pltpu.VMEM((N_CMP_PAD, flat), BF16),
                        pltpu.VMEM((N_CMP_PAD, flat), BF16)],
        compiler_params=_cparams(("arbitrary", "arbitrary")),
        name="compress",
    )(proj3, pek, pev, wk1, wv1, wk2, wv2t)


def _cmp_select_kernel(qt_ref, kc_ref, vct_ref, slope_ref, ov_ref, ocmp_ref, selb_ref, used_ref, *, n_cmp, n_sel):
    nct = CMP_TILE // LANE
    tt = CMP_TILE
    ns = ov_ref.shape[0]
    kc = kc_ref[...].astype(BF16)
    vct = vct_ref[...].astype(BF16)
    ov = ov_ref[...]
    j_i = lax.broadcasted_iota(jnp.int32, (8, tt), 0)
    tq_rel = lax.broadcasted_iota(jnp.int32, (8, tt), 1)
    n_tiles = qt_ref.shape[0] // nct

    def ranges(tile):
        t0 = tile * tt
        n_any = min(N_CMP_PAD, (t0 + tt) // CMP_STRIDE)
        n_all = max(0, (t0 - (CMP_LEN - 1)) // CMP_STRIDE + 1) // 8 * 8
        return t0, n_any, n_all, (t0 + tt) // SLC_LEN

    def pad_rows(x):
        if x.shape[0] == N_CMP_PAD:
            return x
        return jnp.concatenate([x, jnp.zeros((N_CMP_PAD - x.shape[0], x.shape[1]), x.dtype)], axis=0)

    scores = []
    for tile in range(n_tiles):
        _, n_any, _, _ = ranges(tile)
        qt = [jnp.concatenate([qt_ref[tile * nct + c, h * ATT_HD:(h + 1) * ATT_HD, :] for c in range(nct)],
                              axis=1) for h in range(HG)]
        scores.append(_dot(kc[:n_any], jnp.concatenate(qt, axis=1)))

    probs, psums = [], []
    for tile in range(n_tiles):
        t0, n_any, n_all, _ = ranges(tile)
        n_b = lax.broadcasted_iota(jnp.int32, (n_any - n_all, tt), 0) + n_all
        t_b = lax.broadcasted_iota(jnp.int32, (n_any - n_all, tt), 1)
        visible = (CMP_STRIDE * n_b + (CMP_LEN - 1) - t0 <= t_b) & (n_b < n_cmp)
        centre = (lax.broadcasted_iota(jnp.int32, (n_any, tt), 0).astype(F32) * float(CMP_STRIDE)
                  + (CMP_LEN - 1) / 2.0)
        psum = jnp.zeros((n_any, tt), F32)
        ph = []
        for h in range(HG):
            s = scores[tile][:, h * tt:(h + 1) * tt] + slope_ref[h] * LOG2E * centre
            band = jnp.where(visible, s[n_all:], -jnp.inf)
            s = band if n_all == 0 else jnp.concatenate([s[:n_all], band], axis=0)
            m = jnp.max(s, axis=0, keepdims=True)
            m = jnp.where(m == -jnp.inf, 0.0, m)
            p = jnp.exp2(s - m)
            d = jnp.sum(p, axis=0, keepdims=True)
            p = p * (1.0 / jnp.where(d > 0, d, 1.0))
            psum = psum + p
            ph.append(pad_rows(p).astype(BF16))
        probs.append(jnp.concatenate(ph, axis=1))
        psums.append(pad_rows(psum))

    for tile in range(n_tiles):
        o = _dot(vct, probs[tile])
        for h in range(HG):
            for c in range(nct):
                lanes = slice(h * tt + c * LANE, h * tt + (c + 1) * LANE)
                ocmp_ref[tile * nct + c, h * ATT_HD:(h + 1) * ATT_HD, :] = o[:, lanes]

    imps = []
    for tile in range(n_tiles):
        psum = psums[tile]
        hi = psum.astype(BF16)
        rem = psum - hi.astype(F32)
        mid = rem.astype(BF16)
        lo = (rem - mid.astype(F32)).astype(BF16)
        imp3 = _dot(ov, jnp.concatenate([hi, mid, lo], axis=1))
        imps.append(imp3[:, 0:tt] + imp3[:, tt:2 * tt] + imp3[:, 2 * tt:3 * tt])

    for tile in range(n_tiles):
        t0, _, _, nvb = ranges(tile)
        imp = imps[tile]
        score = []
        for rg in range(nvb // 8):
            j = j_i + 8 * rg
            tq = tq_rel + t0
            cur = tq // SLC_LEN
            forced = (j == 0) | (j == cur) | (j == cur - 1)
            sc = imp[8 * rg:8 * rg + 8, :] + jnp.where(forced, FORCE_BONUS, 0.0)
            score.append(jnp.where(j * SLC_LEN <= tq, sc, -jnp.inf))
        rank = [jnp.zeros((8, tt), F32) for _ in score]
        for jp in range(nvb):
            sj = jnp.broadcast_to(score[jp // 8][jp % 8:jp % 8 + 1, :], (8, tt))
            for rg in range(nvb // 8):
                ge = jnp.where(sj >= score[rg], 1.0, 0.0)
                gt = jnp.where(sj > score[rg], 1.0, 0.0)
                if 8 * rg > jp:
                    ahead = ge
                elif 8 * rg + 7 <= jp:
                    ahead = gt
                else:
                    ahead = jnp.where(j_i + 8 * rg > jp, ge, gt)
                rank[rg] = rank[rg] + ahead
        for rg in range(ns // 8):
            for c in range(nct):
                if rg < nvb // 8:
                    picked = rank[rg][:, c * LANE:(c + 1) * LANE] < float(n_sel)
                    selb = jnp.where(picked, 0.0, NEG_BIG)
                    used = jnp.max(jnp.where(picked, 1.0, 0.0), axis=1, keepdims=True)
                else:
                    selb = jnp.full((8, LANE), NEG_BIG, F32)
                    used = jnp.zeros((8, 1), F32)
                selb_ref[tile * nct + c, 8 * rg:8 * rg + 8, :] = selb
                used_ref[tile * nct + c, 8 * rg:8 * rg + 8, :] = jnp.broadcast_to(used, (8, LANE))


def _cmp_select(qv, kc, vct, slopes, overlap_t, t):
    b = qv.shape[0]
    ns = t // SLC_LEN
    nch = t // LANE
    n_cmp = (t - CMP_LEN) // CMP_STRIDE + 1
    n_sel = min(SLC_TOPK, ns)
    rows = HG * ATT_HD
    return pl.pallas_call(
        functools.partial(_cmp_select_kernel, n_cmp=n_cmp, n_sel=n_sel),
        out_shape=(jax.ShapeDtypeStruct((b, nch, H_ATT * ATT_HD, LANE), F32),
                   jax.ShapeDtypeStruct((b, G_KV, nch, ns, LANE), F32),
                   jax.ShapeDtypeStruct((b, G_KV, nch, ns, LANE), F32)),
        grid=(b, G_KV),
        in_specs=[pl.BlockSpec((None, nch, rows, LANE), lambda i, g: (i, 0, g, 0)),
                  pl.BlockSpec((None, None, N_CMP_PAD, ATT_HD), lambda i, g: (i, g, 0, 0)),
                  pl.BlockSpec((None, None, ATT_HD, N_CMP_PAD), lambda i, g: (i, g, 0, 0)),
                  pl.BlockSpec((None, HG, 1, 1), lambda i, g: (g, 0, 0, 0)),
                  pl.BlockSpec((ns, N_CMP_PAD), lambda i, g: (0, 0))],
        out_specs=(pl.BlockSpec((None, nch, rows, LANE), lambda i, g: (i, 0, g, 0)),
                   pl.BlockSpec((None, None, nch, ns, LANE), lambda i, g: (i, g, 0, 0, 0)),
                   pl.BlockSpec((None, None, nch, ns, LANE), lambda i, g: (i, g, 0, 0, 0))),
        compiler_params=_cparams(("arbitrary", "arbitrary")),
        name="cmp_select",
    )(qv, kc, vct, slopes, overlap_t)


def _sparse_kernel(n_extra_ref, extra_ref, qt_ref, k_ref, ka_ref, vs_ref, vw_ref, gt_ref, ocmp_ref, selb_ref,
                   slope_ref, o_ref, msk_ref, qop_ref, s_ref, m_ref, acc_ref):
    tq = ATT_TILE
    kc = ATT_TILE
    hd = ATT_HD
    nl = HG * tq
    n_tiles = qt_ref.shape[0]
    ns = selb_ref.shape[1]
    wch = WINDOW // kc
    n_sel_items = 2 * (SEL_NEAR + 1)
    tile_base = (pl.program_id(0) * pl.num_programs(1) + pl.program_id(1)) * n_tiles
    slope = jnp.concatenate([jnp.broadcast_to(slope_ref[h], (1, tq)) for h in range(HG)], axis=1) * LOG2E
    s_hi = slope.astype(BF16).astype(F32)
    s_mid = (slope - s_hi).astype(BF16).astype(F32)
    s_lo = (slope - s_hi - s_mid).astype(BF16).astype(F32)
    slope_rows = jnp.concatenate([s_hi, s_mid, s_lo, jnp.zeros((AUG_SEL_ROW0 - 3, nl), F32)], axis=0)
    aug_tail = jnp.zeros((LANE - AUG_SEL_ROW0 - ns, nl), F32)
    aug_win = jnp.concatenate([slope_rows, jnp.zeros((ns, nl), F32), aug_tail], axis=0).astype(BF16)
    k_i = lax.broadcasted_iota(jnp.int32, (kc, nl), 0)
    t_i = lax.broadcasted_iota(jnp.int32, (kc, nl), 1) % tq
    d0 = t_i - k_i
    msk_ref[0] = jnp.where(d0 >= 0, 0.0, NEG_BIG)
    msk_ref[1] = jnp.where(d0 < 0, 0.0, NEG_BIG)
    chunk_bias = slope * float(kc)
    ones = jnp.ones((ONES_ROWS, kc), BF16)

    n_pairs = n_tiles // 2
    n_items = n_sel_items + 2 * (wch + 1)
    n_slots = s_ref.shape[0] // 2

    def tiles_of(p):
        return (p, n_tiles - 1 - p)

    def sel_item(p, w, j):
        tile = tiles_of(p)[w]
        if j == SEL_NEAR:
            return 0, tile >= SEL_NEAR, tile
        return max(tile - j, 0), tile - j >= 0, j

    def win_item(p, w, j):
        chunk = tiles_of(p)[w] - wch + j
        return max(chunk, 0), chunk >= 0, wch - j

    def col_max(r, delta_f):
        r8 = jnp.max(r.reshape(kc // 8, 8, nl), axis=0)
        return r8 - chunk_bias * delta_f

    def scores(chunk, op):
        rows = pl.ds(pl.multiple_of(chunk * kc, kc), kc)
        keys = jnp.concatenate([k_ref[rows, :], ka_ref[rows, :]], axis=1)
        return _dot(keys, qop_ref[op])

    def build(p, slot):
        for w, tile in enumerate(tiles_of(p)):
            qt = qt_ref[tile]
            q_all = jnp.concatenate([qt[h * hd:(h + 1) * hd, :] for h in range(HG)], axis=1)
            zero = jnp.zeros_like(q_all)
            selb = jnp.concatenate([selb_ref[tile]] * HG, axis=1)
            aug_sel = jnp.concatenate([slope_rows, selb, aug_tail], axis=0).astype(BF16)
            qop_ref[4 * slot + 2 * w] = jnp.concatenate([q_all, zero, aug_sel], axis=0)
            qop_ref[4 * slot + 2 * w + 1] = jnp.concatenate([zero, q_all, aug_win], axis=0)
        for x in range(4):
            m_ref[4 * slot + x] = jnp.full((8, nl), NEG_BIG, F32)

    def phase_a(p, slot):
        items = []

        def sel(w, j):
            chunk, _, delta = sel_item(p, w, j)
            r = scores(chunk, 4 * slot + 2 * w)
            if j == 0:
                r = r + msk_ref[0]
            s_ref[slot * n_slots + w * (SEL_NEAR + 1) + j] = r
            x = 4 * slot + 2 * w
            m_ref[x] = jnp.maximum(m_ref[x], col_max(r, float(delta)))

        def win(w, j):
            chunk, _, delta = win_item(p, w, j)
            r = scores(chunk, 4 * slot + 2 * w + 1)
            if j == 0:
                r = r + msk_ref[1]
            elif j == wch:
                r = r + msk_ref[0]
            s_ref[slot * n_slots + n_sel_items + w * (wch + 1) + j] = r
            x = 4 * slot + 2 * w + 1
            m_ref[x] = jnp.maximum(m_ref[x], col_max(r, float(delta)))

        for w in range(2):
            for j in range(SEL_NEAR + 1):
                if sel_item(p, w, j)[1]:
                    items.append(functools.partial(sel, w, j))
        for w in range(2):
            for j in range(wch + 1):
                if win_item(p, w, j)[1]:
                    items.append(functools.partial(win, w, j))
        return items

    def overflow(p, slot, visit):
        count = 0
        for w, tile in enumerate(tiles_of(p)):
            def body(e, n, w=w, tile=tile):
                visit(w, tile, extra_ref[(tile_base + tile) * n_tiles + e], n)
                return n + 1

            count = lax.fori_loop(0, n_extra_ref[tile_base + tile], body, count)

    def overflow_a(p, slot):
        def visit(w, tile, c, n):
            r = scores(c, 4 * slot + 2 * w)
            s_ref[slot * n_slots + n_items + n] = r
            x = 4 * slot + 2 * w
            m_ref[x] = jnp.maximum(m_ref[x], col_max(r, jnp.float32(tile - c)))

        overflow(p, slot, visit)

    def overflow_b(p, slot, m_row):
        def visit(w, tile, c, n):
            row = m_row[2 * w] + chunk_bias * jnp.float32(tile - c)
            pr = jnp.exp2(s_ref[slot * n_slots + n_items + n] - row).astype(BF16)
            v_aug = jnp.concatenate([vs_ref[c], ones], axis=0)
            acc_ref[4 * slot + 2 * w] += _dot(v_aug, pr)

        overflow(p, slot, visit)

    def phase_b(p, slot, m_row):
        for x in range(4):
            acc_ref[4 * slot + x] = jnp.zeros(acc_ref.shape[1:], F32)
        items = []

        def pv(x, v_ref, group):
            prs, vs = [], []
            for s_idx, chunk, delta in group:
                row = m_row[x] + chunk_bias * float(delta)
                prs.append(jnp.exp2((s_ref[slot * n_slots + s_idx] - row).astype(BF16)))
                vs.append(jnp.concatenate([v_ref[chunk], ones], axis=0))
            acc_ref[4 * slot + x] += _dot(jnp.concatenate(vs, axis=1), jnp.concatenate(prs, axis=0))

        for w in range(2):
            sel = [(w * (SEL_NEAR + 1) + j,) + sel_item(p, w, j)[::2]
                   for j in range(SEL_NEAR + 1) if sel_item(p, w, j)[1]]
            win = [(n_sel_items + w * (wch + 1) + j,) + win_item(p, w, j)[::2]
                   for j in range(wch + 1) if win_item(p, w, j)[1]]
            for x, v_ref, group in ((2 * w, vs_ref, sel), (2 * w + 1, vw_ref, win)):
                for i in range(0, len(group), 2):
                    items.append(functools.partial(pv, x, v_ref, group[i:i + 2]))
        return items

    def finalize(p, slot):
        for w, tile in enumerate(tiles_of(p)):
            a_s = acc_ref[4 * slot + 2 * w]
            a_w = acc_ref[4 * slot + 2 * w + 1]
            o_slc = a_s[:hd, :] / a_s[hd:hd + 1, :]
            o_win = a_w[:hd, :] / a_w[hd:hd + 1, :]
            oc = ocmp_ref[tile]
            outs = []
            for h in range(HG):
                lanes = slice(h * tq, (h + 1) * tq)
                row0 = (pl.program_id(1) * HG + h) * 3
                g_cmp, g_slc, g_win = [_sigmoid(gt_ref[tile, pl.ds(row0 + br, 1), :]) for br in range(3)]
                y = (g_cmp * oc[h * hd:(h + 1) * hd, :] + g_slc * o_slc[:, lanes]
                     + g_win * o_win[:, lanes])
                outs.append(y.T)
            rows = pl.ds(pl.multiple_of(tile * tq, tq), tq)
            o_ref[rows, :] = jnp.concatenate(outs, axis=1).astype(o_ref.dtype)

    def step(k, slot, first=False, last=False):
        m_row = [jnp.max(m_ref[4 * slot + x], axis=0, keepdims=True) for x in range(4)]
        if not last:
            build(k + 1, 1 - slot)
        if not first:
            finalize(k - 1, 1 - slot)
        b_items = phase_b(k, slot, m_row)
        a_items = phase_a(k + 1, 1 - slot) if not last else []
        for a, b in itertools.zip_longest(a_items, b_items):
            if a is not None:
                a()
            if b is not None:
                b()
        if not last:
            overflow_a(k + 1, 1 - slot)
        overflow_b(k, slot, m_row)

    build(0, 0)
    for a in phase_a(0, 0):
        a()
    overflow_a(0, 0)
    for k in range(n_pairs):
        step(k, k % 2, first=k == 0, last=k == n_pairs - 1)
    finalize(n_pairs - 1, (n_pairs - 1) % 2)


def _key_features(t):
    kp = np.arange(t)
    f = np.zeros((t, LANE), np.float32)
    f[:, 0:3] = (kp % ATT_TILE)[:, None]
    f[kp, AUG_SEL_ROW0 + kp // SLC_LEN] = 1.0
    return jnp.asarray(f, dtype=BF16)


def _extra_chunks(used, t):
    b = used.shape[0]
    nt = t // ATT_TILE
    need = used.max(axis=-1).reshape(b, G_KV, nt, nt, ATT_TILE // SLC_LEN).max(axis=-1) > 0.5
    chunk = jnp.arange(nt, dtype=jnp.int32)
    extra = need & (chunk >= 1) & (chunk[None, :] <= chunk[:, None] - SEL_NEAR)
    place = jnp.cumsum(extra.astype(jnp.int32), axis=-1) - 1
    hit = extra[..., :, None] & (place[..., :, None] == chunk)
    lst = jnp.sum(jnp.where(hit, chunk[:, None], 0), axis=-2)
    return extra.sum(axis=-1, dtype=jnp.int32).reshape(-1), lst.reshape(-1)


def _sparse(n_extra, extra, qv, gt, proj3, ocmp, selb, slopes, t):
    b = qv.shape[0]
    nchunk = t // LANE
    ns = t // SLC_LEN
    rows = HG * ATT_HD
    ksw_unit0 = RNN_COLS // LANE + G_KV
    vs_blk0 = QT_ROWS // ATT_HD
    vw_blk0 = vs_blk0 + G_KV
    nl = HG * ATT_TILE
    n_slots = 2 * (SEL_NEAR + 1) + 2 * (WINDOW // ATT_TILE + 1) + (nchunk - 1 - SEL_NEAR)
    grid_spec = pltpu.PrefetchScalarGridSpec(
        num_scalar_prefetch=2,
        grid=(b, G_KV),
        in_specs=[pl.BlockSpec((None, nchunk, rows, LANE), lambda i, g, *_: (i, 0, g, 0)),
                  pl.BlockSpec((None, t, LANE), lambda i, g, *_: (i, 0, ksw_unit0 + g)),
                  pl.BlockSpec((t, LANE), lambda i, g, *_: (0, 0)),
                  pl.BlockSpec((None, nchunk, ATT_HD, LANE), lambda i, g, *_: (i, 0, vs_blk0 + g, 0)),
                  pl.BlockSpec((None, nchunk, ATT_HD, LANE), lambda i, g, *_: (i, 0, vw_blk0 + g, 0)),
                  pl.BlockSpec((None, nchunk, GATE_ROWS, LANE), lambda i, g, *_: (i, 0, 0, 0)),
                  pl.BlockSpec((None, nchunk, rows, LANE), lambda i, g, *_: (i, 0, g, 0)),
                  pl.BlockSpec((None, None, nchunk, ns, LANE), lambda i, g, *_: (i, g, 0, 0, 0)),
                  pl.BlockSpec((None, HG, 1, 1), lambda i, g, *_: (g, 0, 0, 0))],
        out_specs=pl.BlockSpec((None, t, rows), lambda i, g, *_: (i, 0, g)),
        scratch_shapes=[pltpu.VMEM((2, ATT_TILE, nl), F32),
                        pltpu.VMEM((2 * 4, 2 * LANE, nl), BF16),
                        pltpu.VMEM((2 * n_slots, ATT_TILE, nl), F32),
                        pltpu.VMEM((2 * 4, 8, nl), F32),
                        pltpu.VMEM((2 * 4, ATT_HD + ONES_ROWS, nl), F32)])
    return pl.pallas_call(
        _sparse_kernel,
        out_shape=jax.ShapeDtypeStruct((b, t, H_ATT * ATT_HD), BF16),
        grid_spec=grid_spec,
        compiler_params=_cparams(("arbitrary", "arbitrary")),
        name="sparse",
    )(n_extra, extra, qv, proj3, _key_features(t), qv, qv, gt, ocmp, selb, slopes)


def _merge_kernel(x_ref, yr_ref, ya_ref, mgr_ref, mga_ref, wr_ref, wa_ref, wo_ref, o_ref):
    pr = _dot(yr_ref[...], wr_ref[...])
    pa = _dot(ya_ref[...], wa_ref[...])
    merged = _sigmoid(mgr_ref[...].astype(F32)) * pr + _sigmoid(mga_ref[...].astype(F32)) * pa
    o_ref[...] = x_ref[...] + _dot(merged.astype(BF16), wo_ref[...])


def _merge(x2, yr, ya, proj, wr, wa, wo):
    m, d = x2.shape
    mg0 = (RNN_COLS + KV_COLS) // d
    tile = lambda col: pl.BlockSpec((TOK_TILE, d), lambda i: (i, col))
    wfull = pl.BlockSpec((d, d), lambda i: (0, 0))
    return pl.pallas_call(
        _merge_kernel,
        out_shape=jax.ShapeDtypeStruct((m, d), F32),
        grid=(m // TOK_TILE,),
        in_specs=[tile(0), tile(0), tile(0), tile(mg0), tile(mg0 + 1), wfull, wfull, wfull],
        out_specs=tile(0),
        compiler_params=_cparams(("arbitrary",)),
        name="merge",
    )(x2, yr, ya, proj, proj, wr, wa, wo)


def _mem_kv_kernel(mem_ref, g_ref, wkt_ref, wv_ref, kt_ref, v_ref):
    a = _rms(mem_ref[...], g_ref[...]).astype(BF16)
    kt_ref[...] = _dot_nt(wkt_ref[...], a).astype(kt_ref.dtype)
    v_ref[...] = _dot(a, wv_ref[...]).astype(v_ref.dtype)


def _mem_kv(mem, g, wkt, wkv):
    b, nm, d = mem.shape
    hw = H_X * X_HD
    return pl.pallas_call(
        _mem_kv_kernel,
        out_shape=(jax.ShapeDtypeStruct((b, hw, nm), BF16),
                   jax.ShapeDtypeStruct((b, nm, hw), BF16)),
        grid=(b,),
        in_specs=[pl.BlockSpec((None, nm, d), lambda i: (i, 0, 0)),
                  pl.BlockSpec((1, d), lambda i: (0, 0)),
                  pl.BlockSpec((hw, d), lambda i: (0, 0)),
                  pl.BlockSpec((d, hw), lambda i: (0, 1))],
        out_specs=(pl.BlockSpec((None, hw, nm), lambda i: (i, 0, 0)),
                   pl.BlockSpec((None, nm, hw), lambda i: (i, 0, 0))),
        compiler_params=_cparams(("arbitrary",)),
        name="mem_kv",
    )(mem, g, wkt, wkv)


def _xattn_kernel(h_ref, g_ref, wq_ref, kt_ref, v_ref, wo_ref, o_ref):
    h = h_ref[...]
    a = _rms(h, g_ref[...]).astype(BF16)
    q = (_dot(a, wq_ref[...]) * (X_HD ** -0.5)).astype(BF16)
    heads = [slice(hh * X_HD, (hh + 1) * X_HD) for hh in range(H_X)]
    scores = [_dot(q[:, cols], kt_ref[cols, :]) for cols in heads]
    probs = []
    for s in scores:
        m = jnp.max(s, axis=-1, keepdims=True)
        p = jnp.exp(s - m)
        probs.append((p / jnp.sum(p, axis=-1, keepdims=True)).astype(BF16))
    outs = [_dot(p, v_ref[:, cols]) for p, cols in zip(probs, heads)]
    o = jnp.concatenate(outs, axis=1).astype(BF16)
    o_ref[...] = h + _dot(o, wo_ref[...])


def _xattn(h3, g, wq, kt, v, wo):
    b, t, d = h3.shape
    hw = H_X * X_HD
    nm = v.shape[1]
    return pl.pallas_call(
        _xattn_kernel,
        out_shape=jax.ShapeDtypeStruct((b, t, d), F32),
        grid=(b, t // TOK_TILE),
        in_specs=[pl.BlockSpec((None, TOK_TILE, d), lambda i, j: (i, j, 0)),
                  pl.BlockSpec((1, d), lambda i, j: (0, 0)),
                  pl.BlockSpec((d, hw), lambda i, j: (0, 0)),
                  pl.BlockSpec((None, hw, nm), lambda i, j: (i, 0, 0)),
                  pl.BlockSpec((None, nm, hw), lambda i, j: (i, 0, 0)),
                  pl.BlockSpec((hw, d), lambda i, j: (0, 0))],
        out_specs=pl.BlockSpec((None, TOK_TILE, d), lambda i, j: (i, j, 0)),
        compiler_params=_cparams(("arbitrary", "arbitrary")),
        name="xattn",
    )(h3, g, wq, kt, v, wo)


def _ffn_kernel(h_ref, g_ref, wg_ref, wu_ref, wd_ref, gf_ref, o_ref, *, fc):
    h = h_ref[...]
    a = _rms(h, g_ref[...]).astype(BF16)
    acc = h
    for f in range(wg_ref.shape[1] // fc):
        cols = slice(f * fc, (f + 1) * fc)
        mid = _silu(_dot(a, wg_ref[:, cols])) * _dot(a, wu_ref[:, cols])
        acc = acc + _dot(mid.astype(BF16), wd_ref[cols, :])
    o_ref[...] = _rms(acc, gf_ref[...])


def _ffn(h2, g, wgu, wd, gf):
    m, d = h2.shape
    ff = wd.shape[0]
    const = lambda shape: pl.BlockSpec(shape, lambda i: (0, 0))
    return pl.pallas_call(
        functools.partial(_ffn_kernel, fc=ff // 2),
        out_shape=jax.ShapeDtypeStruct((m, d), F32),
        grid=(m // TOK_TILE,),
        in_specs=[pl.BlockSpec((TOK_TILE, d), lambda i: (i, 0)),
                  const((1, d)), const((d, ff)), pl.BlockSpec((d, ff), lambda i: (0, 1)),
                  const((ff, d)), const((1, d))],
        out_specs=pl.BlockSpec((TOK_TILE, d), lambda i: (i, 0)),
        compiler_params=_cparams(("arbitrary",)),
        name="ffn",
    )(h2, g, wgu, wgu, wd, gf)


def _overlap_t(t):
    nc = (t - CMP_LEN) // CMP_STRIDE + 1
    ns = t // SLC_LEN
    starts = CMP_STRIDE * np.arange(N_CMP_PAD)
    s_start = SLC_LEN * np.arange(ns)
    ov = ((starts[None, :] + CMP_LEN > s_start[:, None]) & (starts[None, :] < s_start[:, None] + SLC_LEN)
          & (np.arange(N_CMP_PAD)[None, :] < nc))
    return jnp.asarray(ov.astype(np.float32), dtype=BF16)


def kernel(x, mem, g_mix, w_in, lower_bounds, g_rnn_out, pe_ck, w_ck1, w_ck2, pe_cv, w_cv1, w_cv2,
           w_proj_rnn, w_proj_att, w_out, g_xattn, g_mem, w_xq, w_xkv, w_xo, g_ffn, w_gate_up,
           w_down, g_final):
    b, t, d = x.shape
    depth = g_mix.shape[0]
    assert depth == 1, "the final RMSNorm is fused into the layer's FFN kernel"
    lbs = jnp.cumsum(jax.nn.softmax(lower_bounds.astype(F32), axis=0), axis=0)
    slopes = (2.0 ** (-8.0 * jnp.arange(1, H_ATT + 1, dtype=F32) / H_ATT)).reshape(G_KV, HG, 1, 1)
    overlap_t = _overlap_t(t)
    h = x
    for l in range(depth):
        w_n, w_t = _w_prep(w_in[l].T)
        x2 = h.reshape(b * t, d)
        proj = _in_proj(x2, g_mix[l][None, :], w_n)
        proj3 = proj.reshape(b, t, N_COLS)
        qv, gt = _in_proj_t(h, g_mix[l][None, :], w_t)
        y_r = _hgrn(proj3, lbs[l][None, :], g_rnn_out[l][None, :])
        kc, vct = _compress(proj3, pe_ck[l], pe_cv[l], w_ck1[l].astype(BF16), w_cv1[l].astype(BF16),
                            w_ck2[l].astype(BF16), w_cv2[l].T.astype(BF16))
        ocmp, selb, used = _cmp_select(qv, kc, vct, slopes, overlap_t, t)
        n_extra, extra = _extra_chunks(used, t)
        y_a = _sparse(n_extra, extra, qv, gt, proj3, ocmp, selb, slopes, t)
        h1 = _merge(x2, y_r.reshape(b * t, d), y_a.reshape(b * t, d), proj,
                    w_proj_rnn[l].astype(BF16), w_proj_att[l].astype(BF16), w_out[l].astype(BF16))
        w_kv = w_xkv[l].astype(BF16)
        kt, v = _mem_kv(mem, g_mem[l][None, :], w_kv[:, :H_X * X_HD].T, w_kv)
        h2 = _xattn(h1.reshape(b, t, d), g_xattn[l][None, :], w_xq[l].astype(BF16), kt, v,
                    w_xo[l].astype(BF16))
        h = _ffn(h2.reshape(b * t, d), g_ffn[l][None, :], w_gate_up[l].astype(BF16),
                 w_down[l].astype(BF16), g_final[None, :]).reshape(b, t, d)
    return h
```

```python
import functools
import itertools

import jax
import jax.numpy as jnp
import numpy as np
from jax import lax
from jax.experimental import pallas as pl
from jax.experimental.pallas import tpu as pltpu

F32 = jnp.float32
BF16 = jnp.bfloat16

D_MODEL = 1024
N_MEM = 256
H_RNN = 8
RNN_DIM = 128
RNN_CHUNK = 64
H_ATT = 16
ATT_HD = 64
G_KV = 4
HG = H_ATT // G_KV
CMP_LEN = 32
CMP_STRIDE = 16
CMP_HIDDEN = 128
SLC_LEN = 64
SLC_TOPK = 8
WINDOW = 512
FORCE_BONUS = 1.0e4
H_X = 4
X_HD = 128
D_FF = 2816
EPS = 1e-6

LANE = 128
VMEM_LIMIT = 56 * 1024 * 1024
TOK_TILE = 512
ATT_TILE = 128
CMP_TILE = 512
N_CMP_PAD = 128
HGRN_GROUP = 4
ONES_ROWS = 16
NEG_BIG = -1.0e30
LOG2E = 1.4426950408889634
AUG_SEL_ROW0 = 8
SEL_NEAR = 4

QT_ROWS = H_ATT * ATT_HD
VT_ROWS = 2 * G_KV * ATT_HD
GATE_ROWS = 64
F_ROWS = -(-(QT_ROWS + VT_ROWS + 3 * H_ATT) // LANE) * LANE
RNN_COLS = 4 * H_RNN * RNN_DIM
KV_COLS = 4 * G_KV * ATT_HD
MG_COLS = 2 * D_MODEL
N_COLS = RNN_COLS + KV_COLS + MG_COLS


def _cparams(sem):
    return pltpu.CompilerParams(dimension_semantics=sem, vmem_limit_bytes=VMEM_LIMIT)


def _rms(xf, g):
    return xf * lax.rsqrt(jnp.mean(xf * xf, axis=-1, keepdims=True) + EPS) * g


def _sigmoid(x):
    return 1.0 / (1.0 + jnp.exp2(x * -LOG2E))


def _silu(x):
    return x * _sigmoid(x)


def _dot(a, b):
    return jnp.dot(a, b, preferred_element_type=F32)


def _dot_nt(a, b):
    return lax.dot_general(a, b, (((1,), (1,)), ((), ())), preferred_element_type=F32)


def _dot_tn(a, b):
    return lax.dot_general(a, b, (((0,), (0,)), ((), ())), preferred_element_type=F32)


def _w_prep_kernel(w_ref, wn_ref, wf_ref):
    kvw = G_KV * ATT_HD
    kv0 = RNN_COLS + QT_ROWS
    gate0 = kv0 + 6 * kvw
    mg0 = gate0 + 3 * H_ATT

    def pair_by_group(a0, b0):
        parts = []
        for g in range(G_KV):
            parts += [w_ref[a0 + g * ATT_HD:a0 + (g + 1) * ATT_HD, :], w_ref[b0 + g * ATT_HD:b0 + (g + 1) * ATT_HD, :]]
        return parts

    wn = jnp.concatenate([w_ref[0:RNN_COLS, :]] + pair_by_group(kv0, kv0 + kvw)
                         + pair_by_group(kv0 + 2 * kvw, kv0 + 4 * kvw) + [w_ref[mg0:mg0 + MG_COLS, :]], axis=0)
    wn_ref[...] = wn.T.astype(BF16)
    used = QT_ROWS + VT_ROWS + 3 * H_ATT
    wf = jnp.concatenate([w_ref[RNN_COLS:kv0, :] * (ATT_HD ** -0.5 * LOG2E),
                          w_ref[kv0 + 3 * kvw:kv0 + 4 * kvw, :], w_ref[kv0 + 5 * kvw:kv0 + 6 * kvw, :],
                          w_ref[gate0:mg0, :], jnp.zeros((wf_ref.shape[0] - used, w_ref.shape[1]), F32)], axis=0)
    wf_ref[...] = wf.astype(BF16)


def _w_prep(wt):
    n_in, d = wt.shape
    return pl.pallas_call(
        _w_prep_kernel,
        out_shape=(jax.ShapeDtypeStruct((d, N_COLS), BF16), jax.ShapeDtypeStruct((F_ROWS, d), BF16)),
        grid=(d // LANE,),
        in_specs=[pl.BlockSpec((n_in, LANE), lambda i: (0, i))],
        out_specs=(pl.BlockSpec((LANE, N_COLS), lambda i: (i, 0)),
                   pl.BlockSpec((F_ROWS, LANE), lambda i: (0, i))),
        compiler_params=_cparams(("arbitrary",)),
        name="w_prep",
    )(wt)


def _in_proj_kernel(x_ref, g_ref, w_ref, o_ref, *, sub):
    a = _rms(x_ref[...], g_ref[...]).astype(BF16)
    for n in range(o_ref.shape[1] // sub):
        cols = slice(n * sub, (n + 1) * sub)
        o_ref[:, cols] = _dot(a, w_ref[:, cols]).astype(o_ref.dtype)


def _in_proj(x2, g, w):
    m, d = x2.shape
    n = w.shape[1]
    tn = n // 2
    return pl.pallas_call(
        functools.partial(_in_proj_kernel, sub=512),
        out_shape=jax.ShapeDtypeStruct((m, n), BF16),
        grid=(2, m // TOK_TILE),
        in_specs=[pl.BlockSpec((TOK_TILE, d), lambda j, i: (i, 0)),
                  pl.BlockSpec((1, d), lambda j, i: (0, 0)),
                  pl.BlockSpec((d, tn), lambda j, i: (0, j))],
        out_specs=pl.BlockSpec((TOK_TILE, tn), lambda j, i: (i, j)),
        compiler_params=_cparams(("arbitrary", "arbitrary")),
        name="in_proj",
    )(x2, g, w)


def _in_proj_t_kernel(x_ref, g_ref, wt_ref, qv_ref, gt_ref):
    a = _rms(x_ref[...], g_ref[...]).astype(BF16)
    r = _dot_nt(wt_ref[...], a)
    nqv = qv_ref.shape[1]
    ngt = gt_ref.shape[1]
    for c in range(qv_ref.shape[0]):
        qv_ref[c] = r[:nqv, c * LANE:(c + 1) * LANE].astype(qv_ref.dtype)
        gt_ref[c] = r[nqv:nqv + ngt, c * LANE:(c + 1) * LANE]


def _in_proj_t(x, g, wt):
    b, t, d = x.shape
    rows = wt.shape[0]
    nqv = QT_ROWS + VT_ROWS
    ngt = GATE_ROWS
    nc = TOK_TILE // LANE
    return pl.pallas_call(
        _in_proj_t_kernel,
        out_shape=(jax.ShapeDtypeStruct((b, t // LANE, nqv, LANE), BF16),
                   jax.ShapeDtypeStruct((b, t // LANE, ngt, LANE), F32)),
        grid=(b, t // TOK_TILE),
        in_specs=[pl.BlockSpec((None, TOK_TILE, d), lambda i, j: (i, j, 0)),
                  pl.BlockSpec((1, d), lambda i, j: (0, 0)),
                  pl.BlockSpec((rows, d), lambda i, j: (0, 0))],
        out_specs=(pl.BlockSpec((None, nc, nqv, LANE), lambda i, j: (i, j, 0, 0)),
                   pl.BlockSpec((None, nc, ngt, LANE), lambda i, j: (i, j, 0, 0))),
        compiler_params=_cparams(("arbitrary", "arbitrary")),
        name="in_proj_t",
    )(x, g, wt)


def _hgrn_kernel(q_ref, f_ref, i_ref, og_ref, lb_ref, gn_ref, o_ref, qd_ref, oi_ref, ut_ref, dec_ref):
    c = RNN_CHUNK
    kd = RNN_DIM
    n_chunks = q_ref.shape[0] // c
    lb = lb_ref[...]
    gn = gn_ref[...]
    blk = HGRN_GROUP * c
    row = lax.broadcasted_iota(jnp.int32, (blk, blk), 0)
    col = lax.broadcasted_iota(jnp.int32, (blk, blk), 1)
    same_chunk = (row // c) == (col // c)
    causal = same_chunk & (row >= col)
    tril = causal.astype(BF16)

    q = q_ref[...].astype(F32)
    fl = f_ref[...].astype(F32)
    v = i_ref[...]
    f = lb + (1.0 - lb) * _sigmoid(fl)
    k = 1.0 - f
    logf = jnp.log(f)
    hi = logf.astype(BF16)
    lo = (logf - hi.astype(F32)).astype(BF16)
    pieces = jnp.concatenate([hi, lo], axis=1)
    n_blk = q_ref.shape[0] // blk
    cs = [_dot(tril, pieces[g * blk:(g + 1) * blk]) for g in range(n_blk)]
    bcum = jnp.concatenate([x[:, 0:kd] + x[:, kd:2 * kd] for x in cs], axis=0)
    e_neg = jnp.exp2(bcum * -LOG2E)
    dec = jnp.exp(jnp.concatenate([bcum[n * c + c - 1:n * c + c, :] for n in range(n_chunks)], axis=0))
    dec_rows = jnp.concatenate([jnp.broadcast_to(dec[n:n + 1, :], (c, kd)) for n in range(n_chunks)], axis=0)
    q_dec = (_silu(q) * (1.0 / e_neg)).astype(BF16)
    k_neg = k * e_neg
    k_dec = k_neg.astype(BF16)
    k_end = (k_neg * dec_rows).astype(BF16)
    qd_ref[...] = q_dec
    dec_ref[...] = dec
    for g in range(n_blk):
        rows = slice(g * blk, (g + 1) * blk)
        a = jnp.where(causal, _dot_nt(q_dec[rows], k_dec[rows]), 0.0)
        oi_ref[rows, :] = _dot(a.astype(BF16), v[rows])
    for n in range(n_chunks):
        rows = slice(n * c, (n + 1) * c)
        ut_ref[n] = _dot_tn(v[rows], k_end[rows])

    s_t = jnp.zeros((kd, kd), F32)
    for n in range(n_chunks):
        rows = slice(n * c, (n + 1) * c)
        o = oi_ref[rows, :] + _dot_nt(qd_ref[rows, :], s_t.astype(BF16))
        og = og_ref[rows, :].astype(F32)
        o_ref[rows, :] = (_rms(o, gn) * _silu(og)).astype(o_ref.dtype)
        s_t = s_t * dec_ref[n:n + 1, :] + ut_ref[n]


def _hgrn(proj3, lb, gn):
    b, t, _ = proj3.shape
    return pl.pallas_call(
        _hgrn_kernel,
        out_shape=jax.ShapeDtypeStruct((b, t, H_RNN * RNN_DIM), BF16),
        grid=(b, H_RNN),
        in_specs=[pl.BlockSpec((None, t, RNN_DIM), lambda i, h: (i, 0, h)),
                  pl.BlockSpec((None, t, RNN_DIM), lambda i, h: (i, 0, H_RNN + h)),
                  pl.BlockSpec((None, t, RNN_DIM), lambda i, h: (i, 0, 2 * H_RNN + h)),
                  pl.BlockSpec((None, t, RNN_DIM), lambda i, h: (i, 0, 3 * H_RNN + h)),
                  pl.BlockSpec((1, RNN_DIM), lambda i, h: (0, h)),
                  pl.BlockSpec((1, RNN_DIM), lambda i, h: (0, 0))],
        out_specs=pl.BlockSpec((None, t, RNN_DIM), lambda i, h: (i, 0, h)),
        scratch_shapes=[pltpu.VMEM((t, RNN_DIM), BF16),
                        pltpu.VMEM((t, RNN_DIM), F32),
                        pltpu.VMEM((t // RNN_CHUNK, RNN_DIM, RNN_DIM), F32),
                        pltpu.VMEM((t // RNN_CHUNK, RNN_DIM), F32)],
        compiler_params=_cparams(("arbitrary", "arbitrary")),
        name="hgrn",
    )(proj3, proj3, proj3, proj3, lb, gn)


def _compress_kernel(kv_ref, pe_ref, w1_ref, wk2_ref, wv2t_ref, kc_ref, vct_ref, xs_ref, x_ref):
    t = kv_ref.shape[0]
    xs_ref[0:t, :] = kv_ref[...].astype(F32)
    xs_ref[t:, :] = jnp.zeros((xs_ref.shape[0] - t, xs_ref.shape[1]), F32)
    for l in range(CMP_LEN):
        blk = xs_ref[pl.ds(l, N_CMP_PAD, stride=CMP_STRIDE), :]
        x_ref[:, l * LANE:(l + 1) * LANE] = (blk + pe_ref[l:l + 1, :]).astype(BF16)
    hid = _silu(_dot(x_ref[...], w1_ref[...])).astype(BF16)
    kc_ref[...] = _dot(hid[:, :CMP_HIDDEN], wk2_ref[...])
    vct_ref[...] = _dot_nt(wv2t_ref[...], hid[:, CMP_HIDDEN:])


def _compress(proj3, pe, w1, wk2, wv2t):
    b, t, _ = proj3.shape
    kv_unit0 = RNN_COLS // LANE
    full = lambda shape: pl.BlockSpec(shape, lambda i, g: (0,) * len(shape))
    return pl.pallas_call(
        _compress_kernel,
        out_shape=(jax.ShapeDtypeStruct((b, G_KV, N_CMP_PAD, ATT_HD), F32),
                   jax.ShapeDtypeStruct((b, G_KV, ATT_HD, N_CMP_PAD), F32)),
        grid=(b, G_KV),
        in_specs=[pl.BlockSpec((None, t, LANE), lambda i, g: (i, 0, kv_unit0 + g)),
                  full((CMP_LEN, LANE)), full((CMP_LEN * LANE, 2 * CMP_HIDDEN)),
                  full((CMP_HIDDEN, ATT_HD)), full((ATT_HD, CMP_HIDDEN))],
        out_specs=(pl.BlockSpec((None, None, N_CMP_PAD, ATT_HD), lambda i, g: (i, g, 0, 0)),
                   pl.BlockSpec((None, None, ATT_HD, N_CMP_PAD), lambda i, g: (i, g, 0, 0))),
        scratch_shapes=[pltpu.VMEM((t + CMP_STRIDE, LANE), F32),
                        pltpu.VMEM((N_CMP_PAD, CMP_LEN * LANE), BF16)],
        compiler_params=_cparams(("arbitrary", "arbitrary")),
        name="compress",
    )(proj3, pe, w1, wk2, wv2t)


def _compress_weights(pek, pev, wk1, wv1):
    zk = jnp.zeros((CMP_LEN, ATT_HD, CMP_HIDDEN), wk1.dtype)
    top = jnp.concatenate([wk1.reshape(CMP_LEN, ATT_HD, CMP_HIDDEN), zk], axis=2)
    bot = jnp.concatenate([zk, wv1.reshape(CMP_LEN, ATT_HD, CMP_HIDDEN)], axis=2)
    w1 = jnp.concatenate([top, bot], axis=1).reshape(CMP_LEN * LANE, 2 * CMP_HIDDEN)
    return jnp.concatenate([pek, pev], axis=1), w1.astype(BF16)


def _cmp_select_kernel(qt_ref, kc_ref, vct_ref, slope_ref, ov_ref, ocmp_ref, selb_ref, used_ref, *, n_cmp, n_sel):
    nct = CMP_TILE // LANE
    tt = CMP_TILE
    ns = ov_ref.shape[0]
    kc = kc_ref[...].astype(BF16)
    vct = vct_ref[...].astype(BF16)
    ov = ov_ref[...]
    j_i = lax.broadcasted_iota(jnp.int32, (8, tt), 0)
    tq_rel = lax.broadcasted_iota(jnp.int32, (8, tt), 1)
    n_tiles = qt_ref.shape[0] // nct

    def ranges(tile):
        t0 = tile * tt
        n_any = min(N_CMP_PAD, (t0 + tt) // CMP_STRIDE)
        n_all = max(0, (t0 - (CMP_LEN - 1)) // CMP_STRIDE + 1) // 8 * 8
        return t0, n_any, n_all, (t0 + tt) // SLC_LEN

    def pad_rows(x):
        if x.shape[0] == N_CMP_PAD:
            return x
        return jnp.concatenate([x, jnp.zeros((N_CMP_PAD - x.shape[0], x.shape[1]), x.dtype)], axis=0)

    scores = []
    for tile in range(n_tiles):
        _, n_any, _, _ = ranges(tile)
        qt = [jnp.concatenate([qt_ref[tile * nct + c, h * ATT_HD:(h + 1) * ATT_HD, :] for c in range(nct)],
                              axis=1) for h in range(HG)]
        scores.append(_dot(kc[:n_any], jnp.concatenate(qt, axis=1)))

    probs, psums = [], []
    for tile in range(n_tiles):
        t0, n_any, n_all, _ = ranges(tile)
        n_b = lax.broadcasted_iota(jnp.int32, (n_any - n_all, tt), 0) + n_all
        t_b = lax.broadcasted_iota(jnp.int32, (n_any - n_all, tt), 1)
        visible = (CMP_STRIDE * n_b + (CMP_LEN - 1) - t0 <= t_b) & (n_b < n_cmp)
        centre = (lax.broadcasted_iota(jnp.int32, (n_any, tt), 0).astype(F32) * float(CMP_STRIDE)
                  + (CMP_LEN - 1) / 2.0)
        psum = jnp.zeros((n_any, tt), F32)
        ph = []
        for h in range(HG):
            s = scores[tile][:, h * tt:(h + 1) * tt] + slope_ref[h] * LOG2E * centre
            band = jnp.where(visible, s[n_all:], -jnp.inf)
            s = band if n_all == 0 else jnp.concatenate([s[:n_all], band], axis=0)
            m = jnp.max(s, axis=0, keepdims=True)
            m = jnp.where(m == -jnp.inf, 0.0, m)
            p = jnp.exp2(s - m)
            d = jnp.sum(p, axis=0, keepdims=True)
            p = p * (1.0 / jnp.where(d > 0, d, 1.0))
            psum = psum + p
            ph.append(pad_rows(p).astype(BF16))
        probs.append(jnp.concatenate(ph, axis=1))
        psums.append(pad_rows(psum))

    for tile in range(n_tiles):
        o = _dot(vct, probs[tile])
        for h in range(HG):
            for c in range(nct):
                lanes = slice(h * tt + c * LANE, h * tt + (c + 1) * LANE)
                ocmp_ref[tile * nct + c, h * ATT_HD:(h + 1) * ATT_HD, :] = o[:, lanes]

    imps = []
    for tile in range(n_tiles):
        psum = psums[tile]
        hi = psum.astype(BF16)
        rem = psum - hi.astype(F32)
        mid = rem.astype(BF16)
        lo = (rem - mid.astype(F32)).astype(BF16)
        imp3 = _dot(ov, jnp.concatenate([hi, mid, lo], axis=1))
        imps.append(imp3[:, 0:tt] + imp3[:, tt:2 * tt] + imp3[:, 2 * tt:3 * tt])

    for tile in range(n_tiles):
        t0, _, _, nvb = ranges(tile)
        imp = imps[tile]
        score = []
        for rg in range(nvb // 8):
            j = j_i + 8 * rg
            tq = tq_rel + t0
            cur = tq // SLC_LEN
            forced = (j == 0) | (j == cur) | (j == cur - 1)
            sc = imp[8 * rg:8 * rg + 8, :] + jnp.where(forced, FORCE_BONUS, 0.0)
            score.append(jnp.where(j * SLC_LEN <= tq, sc, -jnp.inf))
        rank = [jnp.zeros((8, tt), F32) for _ in score]
        for jp in range(nvb):
            sj = jnp.broadcast_to(score[jp // 8][jp % 8:jp % 8 + 1, :], (8, tt))
            for rg in range(nvb // 8):
                ge = jnp.where(sj >= score[rg], 1.0, 0.0)
                gt = jnp.where(sj > score[rg], 1.0, 0.0)
                if 8 * rg > jp:
                    ahead = ge
                elif 8 * rg + 7 <= jp:
                    ahead = gt
                else:
                    ahead = jnp.where(j_i + 8 * rg > jp, ge, gt)
                rank[rg] = rank[rg] + ahead
        for rg in range(ns // 8):
            for c in range(nct):
                if rg < nvb // 8:
                    picked = rank[rg][:, c * LANE:(c + 1) * LANE] < float(n_sel)
                    selb = jnp.where(picked, 0.0, NEG_BIG)
                    used = jnp.max(jnp.where(picked, 1.0, 0.0), axis=1, keepdims=True)
                else:
                    selb = jnp.full((8, LANE), NEG_BIG, F32)
                    used = jnp.zeros((8, 1), F32)
                selb_ref[tile * nct + c, 8 * rg:8 * rg + 8, :] = selb
                used_ref[tile * nct + c, 8 * rg:8 * rg + 8, :] = jnp.broadcast_to(used, (8, LANE))


def _cmp_select(qv, kc, vct, slopes, overlap_t, t):
    b = qv.shape[0]
    ns = t // SLC_LEN
    nch = t // LANE
    n_cmp = (t - CMP_LEN) // CMP_STRIDE + 1
    n_sel = min(SLC_TOPK, ns)
    rows = HG * ATT_HD
    return pl.pallas_call(
        functools.partial(_cmp_select_kernel, n_cmp=n_cmp, n_sel=n_sel),
        out_shape=(jax.ShapeDtypeStruct((b, nch, H_ATT * ATT_HD, LANE), F32),
                   jax.ShapeDtypeStruct((b, G_KV, nch, ns, LANE), F32),
                   jax.ShapeDtypeStruct((b, G_KV, nch, ns, LANE), F32)),
        grid=(b, G_KV),
        in_specs=[pl.BlockSpec((None, nch, rows, LANE), lambda i, g: (i, 0, g, 0)),
                  pl.BlockSpec((None, None, N_CMP_PAD, ATT_HD), lambda i, g: (i, g, 0, 0)),
                  pl.BlockSpec((None, None, ATT_HD, N_CMP_PAD), lambda i, g: (i, g, 0, 0)),
                  pl.BlockSpec((None, HG, 1, 1), lambda i, g: (g, 0, 0, 0)),
                  pl.BlockSpec((ns, N_CMP_PAD), lambda i, g: (0, 0))],
        out_specs=(pl.BlockSpec((None, nch, rows, LANE), lambda i, g: (i, 0, g, 0)),
                   pl.BlockSpec((None, None, nch, ns, LANE), lambda i, g: (i, g, 0, 0, 0)),
                   pl.BlockSpec((None, None, nch, ns, LANE), lambda i, g: (i, g, 0, 0, 0))),
        compiler_params=_cparams(("arbitrary", "arbitrary")),
        name="cmp_select",
    )(qv, kc, vct, slopes, overlap_t)


def _sparse_kernel(n_extra_ref, extra_ref, qt_ref, k_ref, ka_ref, vs_ref, vw_ref, gt_ref, ocmp_ref, selb_ref,
                   slope_ref, o_ref, msk_ref, qop_ref, s_ref, m_ref, acc_ref):
    tq = ATT_TILE
    kc = ATT_TILE
    hd = ATT_HD
    nl = HG * tq
    n_tiles = qt_ref.shape[0]
    ns = selb_ref.shape[1]
    wch = WINDOW // kc
    n_sel_items = 2 * (SEL_NEAR + 1)
    tile_base = (pl.program_id(0) * pl.num_programs(1) + pl.program_id(1)) * n_tiles
    slope = jnp.concatenate([jnp.broadcast_to(slope_ref[h], (1, tq)) for h in range(HG)], axis=1) * LOG2E
    s_hi = slope.astype(BF16).astype(F32)
    s_mid = (slope - s_hi).astype(BF16).astype(F32)
    s_lo = (slope - s_hi - s_mid).astype(BF16).astype(F32)
    slope_rows = jnp.concatenate([s_hi, s_mid, s_lo, jnp.zeros((AUG_SEL_ROW0 - 3, nl), F32)], axis=0)
    aug_tail = jnp.zeros((LANE - AUG_SEL_ROW0 - ns, nl), F32)
    aug_win = jnp.concatenate([slope_rows, jnp.zeros((ns, nl), F32), aug_tail], axis=0).astype(BF16)
    k_i = lax.broadcasted_iota(jnp.int32, (kc, nl), 0)
    t_i = lax.broadcasted_iota(jnp.int32, (kc, nl), 1) % tq
    d0 = t_i - k_i
    msk_ref[0] = jnp.where(d0 >= 0, 0.0, NEG_BIG)
    msk_ref[1] = jnp.where(d0 < 0, 0.0, NEG_BIG)
    chunk_bias = slope * float(kc)
    ones = jnp.ones((ONES_ROWS, kc), BF16)

    n_pairs = n_tiles // 2
    n_items = n_sel_items + 2 * (wch + 1)
    n_slots = s_ref.shape[0] // 2

    def tiles_of(p):
        return (p, n_tiles - 1 - p)

    def sel_item(p, w, j):
        tile = tiles_of(p)[w]
        if j == SEL_NEAR:
            return 0, tile >= SEL_NEAR, tile
        return max(tile - j, 0), tile - j >= 0, j

    def win_item(p, w, j):
        chunk = tiles_of(p)[w] - wch + j
        return max(chunk, 0), chunk >= 0, wch - j

    def col_max(r, delta_f):
        r8 = jnp.max(r.reshape(kc // 8, 8, nl), axis=0)
        return r8 - chunk_bias * delta_f

    def scores(chunk, op):
        rows = pl.ds(pl.multiple_of(chunk * kc, kc), kc)
        keys = jnp.concatenate([k_ref[rows, :], ka_ref[rows, :]], axis=1)
        return _dot(keys, qop_ref[op])

    def build(p, slot):
        for w, tile in enumerate(tiles_of(p)):
            qt = qt_ref[tile]
            q_all = jnp.concatenate([qt[h * hd:(h + 1) * hd, :] for h in range(HG)], axis=1)
            zero = jnp.zeros_like(q_all)
            selb = jnp.concatenate([selb_ref[tile]] * HG, axis=1)
            aug_sel = jnp.concatenate([slope_rows, selb, aug_tail], axis=0).astype(BF16)
            qop_ref[4 * slot + 2 * w] = jnp.concatenate([q_all, zero, aug_sel], axis=0)
            qop_ref[4 * slot + 2 * w + 1] = jnp.concatenate([zero, q_all, aug_win], axis=0)
        for x in range(4):
            m_ref[4 * slot + x] = jnp.full((8, nl), NEG_BIG, F32)

    def phase_a(p, slot):
        items = []

        def sel(w, j):
            chunk, _, delta = sel_item(p, w, j)
            r = scores(chunk, 4 * slot + 2 * w)
            if j == 0:
                r = r + msk_ref[0]
            s_ref[slot * n_slots + w * (SEL_NEAR + 1) + j] = r
            x = 4 * slot + 2 * w
            m_ref[x] = jnp.maximum(m_ref[x], col_max(r, float(delta)))

        def win(w, j):
            chunk, _, delta = win_item(p, w, j)
            r = scores(chunk, 4 * slot + 2 * w + 1)
            if j == 0:
                r = r + msk_ref[1]
            elif j == wch:
                r = r + msk_ref[0]
            s_ref[slot * n_slots + n_sel_items + w * (wch + 1) + j] = r
            x = 4 * slot + 2 * w + 1
            m_ref[x] = jnp.maximum(m_ref[x], col_max(r, float(delta)))

        for w in range(2):
            for j in range(SEL_NEAR + 1):
                if sel_item(p, w, j)[1]:
                    items.append(functools.partial(sel, w, j))
        for w in range(2):
            for j in range(wch + 1):
                if win_item(p, w, j)[1]:
                    items.append(functools.partial(win, w, j))
        return items

    def overflow(p, slot, visit):
        count = 0
        for w, tile in enumerate(tiles_of(p)):
            def body(e, n, w=w, tile=tile):
                visit(w, tile, extra_ref[(tile_base + tile) * n_tiles + e], n)
                return n + 1

            count = lax.fori_loop(0, n_extra_ref[tile_base + tile], body, count)

    def overflow_a(p, slot):
        def visit(w, tile, c, n):
            r = scores(c, 4 * slot + 2 * w)
            s_ref[slot * n_slots + n_items + n] = r
            x = 4 * slot + 2 * w
            m_ref[x] = jnp.maximum(m_ref[x], col_max(r, jnp.float32(tile - c)))

        overflow(p, slot, visit)

    def overflow_b(p, slot, m_row):
        def visit(w, tile, c, n):
            row = m_row[2 * w] + chunk_bias * jnp.float32(tile - c)
            pr = jnp.exp2(s_ref[slot * n_slots + n_items + n] - row).astype(BF16)
            v_aug = jnp.concatenate([vs_ref[c], ones], axis=0)
            acc_ref[4 * slot + 2 * w] += _dot(v_aug, pr)

        overflow(p, slot, visit)

    def phase_b(p, slot, m_row):
        for x in range(4):
            acc_ref[4 * slot + x] = jnp.zeros(acc_ref.shape[1:], F32)
        items = []

        def pv(x, v_ref, group):
            prs, vs = [], []
            for s_idx, chunk, delta in group:
                row = m_row[x] + chunk_bias * float(delta)
                prs.append(jnp.exp2(s_ref[slot * n_slots + s_idx] - row).astype(BF16))
                vs.append(jnp.concatenate([v_ref[chunk], ones], axis=0))
            acc_ref[4 * slot + x] += _dot(jnp.concatenate(vs, axis=1), jnp.concatenate(prs, axis=0))

        for w in range(2):
            sel = [(w * (SEL_NEAR + 1) + j,) + sel_item(p, w, j)[::2]
                   for j in range(SEL_NEAR + 1) if sel_item(p, w, j)[1]]
            win = [(n_sel_items + w * (wch + 1) + j,) + win_item(p, w, j)[::2]
                   for j in range(wch + 1) if win_item(p, w, j)[1]]
            for x, v_ref, group in ((2 * w, vs_ref, sel), (2 * w + 1, vw_ref, win)):
                for i in range(0, len(group), 2):
                    items.append(functools.partial(pv, x, v_ref, group[i:i + 2]))
        return items

    def finalize(p, slot):
        for w, tile in enumerate(tiles_of(p)):
            a_s = acc_ref[4 * slot + 2 * w]
            a_w = acc_ref[4 * slot + 2 * w + 1]
            o_slc = a_s[:hd, :] / a_s[hd:hd + 1, :]
            o_win = a_w[:hd, :] / a_w[hd:hd + 1, :]
            oc = ocmp_ref[tile]
            outs = []
            for h in range(HG):
                lanes = slice(h * tq, (h + 1) * tq)
                row0 = (pl.program_id(1) * HG + h) * 3
                g_cmp, g_slc, g_win = [_sigmoid(gt_ref[tile, pl.ds(row0 + br, 1), :]) for br in range(3)]
                y = (g_cmp * oc[h * hd:(h + 1) * hd, :] + g_slc * o_slc[:, lanes]
                     + g_win * o_win[:, lanes])
                outs.append(y.T)
            rows = pl.ds(pl.multiple_of(tile * tq, tq), tq)
            o_ref[rows, :] = jnp.concatenate(outs, axis=1).astype(o_ref.dtype)

    def step(k, slot, first=False, last=False):
        m_row = [jnp.max(m_ref[4 * slot + x], axis=0, keepdims=True) for x in range(4)]
        if not last:
            build(k + 1, 1 - slot)
        if not first:
            finalize(k - 1, 1 - slot)
        b_items = phase_b(k, slot, m_row)
        a_items = phase_a(k + 1, 1 - slot) if not last else []
        for a, b in itertools.zip_longest(a_items, b_items):
            if a is not None:
                a()
            if b is not None:
                b()
        if not last:
            overflow_a(k + 1, 1 - slot)
        overflow_b(k, slot, m_row)

    build(0, 0)
    for a in phase_a(0, 0):
        a()
    overflow_a(0, 0)
    for k in range(n_pairs):
        step(k, k % 2, first=k == 0, last=k == n_pairs - 1)
    finalize(n_pairs - 1, (n_pairs - 1) % 2)


def _key_features(t):
    kp = np.arange(t)
    f = np.zeros((t, LANE), np.float32)
    f[:, 0:3] = (kp % ATT_TILE)[:, None]
    f[kp, AUG_SEL_ROW0 + kp // SLC_LEN] = 1.0
    return jnp.asarray(f, dtype=BF16)


def _extra_chunks(used, t):
    b = used.shape[0]
    nt = t // ATT_TILE
    need = used.max(axis=-1).reshape(b, G_KV, nt, nt, ATT_TILE // SLC_LEN).max(axis=-1) > 0.5
    chunk = jnp.arange(nt, dtype=jnp.int32)
    extra = need & (chunk >= 1) & (chunk[None, :] <= chunk[:, None] - SEL_NEAR)
    place = jnp.cumsum(extra.astype(jnp.int32), axis=-1) - 1
    hit = extra[..., :, None] & (place[..., :, None] == chunk)
    lst = jnp.sum(jnp.where(hit, chunk[:, None], 0), axis=-2)
    return extra.sum(axis=-1, dtype=jnp.int32).reshape(-1), lst.reshape(-1)


def _sparse(n_extra, extra, qv, gt, proj3, ocmp, selb, slopes, t):
    b = qv.shape[0]
    nchunk = t // LANE
    ns = t // SLC_LEN
    rows = HG * ATT_HD
    ksw_unit0 = RNN_COLS // LANE + G_KV
    vs_blk0 = QT_ROWS // ATT_HD
    vw_blk0 = vs_blk0 + G_KV
    nl = HG * ATT_TILE
    n_slots = 2 * (SEL_NEAR + 1) + 2 * (WINDOW // ATT_TILE + 1) + (nchunk - 1 - SEL_NEAR)
    grid_spec = pltpu.PrefetchScalarGridSpec(
        num_scalar_prefetch=2,
        grid=(b, G_KV),
        in_specs=[pl.BlockSpec((None, nchunk, rows, LANE), lambda i, g, *_: (i, 0, g, 0)),
                  pl.BlockSpec((None, t, LANE), lambda i, g, *_: (i, 0, ksw_unit0 + g)),
                  pl.BlockSpec((t, LANE), lambda i, g, *_: (0, 0)),
                  pl.BlockSpec((None, nchunk, ATT_HD, LANE), lambda i, g, *_: (i, 0, vs_blk0 + g, 0)),
                  pl.BlockSpec((None, nchunk, ATT_HD, LANE), lambda i, g, *_: (i, 0, vw_blk0 + g, 0)),
                  pl.BlockSpec((None, nchunk, GATE_ROWS, LANE), lambda i, g, *_: (i, 0, 0, 0)),
                  pl.BlockSpec((None, nchunk, rows, LANE), lambda i, g, *_: (i, 0, g, 0)),
                  pl.BlockSpec((None, None, nchunk, ns, LANE), lambda i, g, *_: (i, g, 0, 0, 0)),
                  pl.BlockSpec((None, HG, 1, 1), lambda i, g, *_: (g, 0, 0, 0))],
        out_specs=pl.BlockSpec((None, t, rows), lambda i, g, *_: (i, 0, g)),
        scratch_shapes=[pltpu.VMEM((2, ATT_TILE, nl), F32),
                        pltpu.VMEM((2 * 4, 2 * LANE, nl), BF16),
                        pltpu.VMEM((2 * n_slots, ATT_TILE, nl), F32),
                        pltpu.VMEM((2 * 4, 8, nl), F32),
                        pltpu.VMEM((2 * 4, ATT_HD + ONES_ROWS, nl), F32)])
    return pl.pallas_call(
        _sparse_kernel,
        out_shape=jax.ShapeDtypeStruct((b, t, H_ATT * ATT_HD), BF16),
        grid_spec=grid_spec,
        compiler_params=_cparams(("arbitrary", "arbitrary")),
        name="sparse",
    )(n_extra, extra, qv, proj3, _key_features(t), qv, qv, gt, ocmp, selb, slopes)


def _merge_kernel(x_ref, yr_ref, ya_ref, mgr_ref, mga_ref, wr_ref, wa_ref, wo_ref, o_ref):
    pr = _dot(yr_ref[...], wr_ref[...])
    pa = _dot(ya_ref[...], wa_ref[...])
    merged = _sigmoid(mgr_ref[...].astype(F32)) * pr + _sigmoid(mga_ref[...].astype(F32)) * pa
    o_ref[...] = x_ref[...] + _dot(merged.astype(BF16), wo_ref[...])


def _merge(x2, yr, ya, proj, wr, wa, wo):
    m, d = x2.shape
    mg0 = (RNN_COLS + KV_COLS) // d
    tile = lambda col: pl.BlockSpec((TOK_TILE, d), lambda i: (i, col))
    wfull = pl.BlockSpec((d, d), lambda i: (0, 0))
    return pl.pallas_call(
        _merge_kernel,
        out_shape=jax.ShapeDtypeStruct((m, d), F32),
        grid=(m // TOK_TILE,),
        in_specs=[tile(0), tile(0), tile(0), tile(mg0), tile(mg0 + 1), wfull, wfull, wfull],
        out_specs=tile(0),
        compiler_params=_cparams(("arbitrary",)),
        name="merge",
    )(x2, yr, ya, proj, proj, wr, wa, wo)


def _mem_kv_kernel(mem_ref, g_ref, wkt_ref, wv_ref, kt_ref, v_ref):
    a = _rms(mem_ref[...], g_ref[...]).astype(BF16)
    kt_ref[...] = _dot_nt(wkt_ref[...], a).astype(kt_ref.dtype)
    v_ref[...] = _dot(a, wv_ref[...]).astype(v_ref.dtype)


def _mem_kv(mem, g, wkt, wkv):
    b, nm, d = mem.shape
    hw = H_X * X_HD
    return pl.pallas_call(
        _mem_kv_kernel,
        out_shape=(jax.ShapeDtypeStruct((b, hw, nm), BF16),
                   jax.ShapeDtypeStruct((b, nm, hw), BF16)),
        grid=(b,),
        in_specs=[pl.BlockSpec((None, nm, d), lambda i: (i, 0, 0)),
                  pl.BlockSpec((1, d), lambda i: (0, 0)),
                  pl.BlockSpec((hw, d), lambda i: (0, 0)),
                  pl.BlockSpec((d, hw), lambda i: (0, 1))],
        out_specs=(pl.BlockSpec((None, hw, nm), lambda i: (i, 0, 0)),
                   pl.BlockSpec((None, nm, hw), lambda i: (i, 0, 0))),
        compiler_params=_cparams(("arbitrary",)),
        name="mem_kv",
    )(mem, g, wkt, wkv)


def _xattn_kernel(h_ref, g_ref, wq_ref, kt_ref, v_ref, wo_ref, o_ref):
    h = h_ref[...]
    a = _rms(h, g_ref[...]).astype(BF16)
    q = (_dot(a, wq_ref[...]) * (X_HD ** -0.5)).astype(BF16)
    heads = [slice(hh * X_HD, (hh + 1) * X_HD) for hh in range(H_X)]
    scores = [_dot(q[:, cols], kt_ref[cols, :]) for cols in heads]
    probs = []
    for s in scores:
        m = jnp.max(s, axis=-1, keepdims=True)
        p = jnp.exp(s - m)
        probs.append((p / jnp.sum(p, axis=-1, keepdims=True)).astype(BF16))
    outs = [_dot(p, v_ref[:, cols]) for p, cols in zip(probs, heads)]
    o = jnp.concatenate(outs, axis=1).astype(BF16)
    o_ref[...] = h + _dot(o, wo_ref[...])


def _xattn(h3, g, wq, kt, v, wo):
    b, t, d = h3.shape
    hw = H_X * X_HD
    nm = v.shape[1]
    return pl.pallas_call(
        _xattn_kernel,
        out_shape=jax.ShapeDtypeStruct((b, t, d), F32),
        grid=(b, t // TOK_TILE),
        in_specs=[pl.BlockSpec((None, TOK_TILE, d), lambda i, j: (i, j, 0)),
                  pl.BlockSpec((1, d), lambda i, j: (0, 0)),
                  pl.BlockSpec((d, hw), lambda i, j: (0, 0)),
                  pl.BlockSpec((None, hw, nm), lambda i, j: (i, 0, 0)),
                  pl.BlockSpec((None, nm, hw), lambda i, j: (i, 0, 0)),
                  pl.BlockSpec((hw, d), lambda i, j: (0, 0))],
        out_specs=pl.BlockSpec((None, TOK_TILE, d), lambda i, j: (i, j, 0)),
        compiler_params=_cparams(("arbitrary", "arbitrary")),
        name="xattn",
    )(h3, g, wq, kt, v, wo)


def _ffn_kernel(h_ref, g_ref, wg_ref, wu_ref, wd_ref, gf_ref, o_ref, *, fc):
    h = h_ref[...]
    a = _rms(h, g_ref[...]).astype(BF16)
    acc = h
    for f in range(wg_ref.shape[1] // fc):
        cols = slice(f * fc, (f + 1) * fc)
        mid = _silu(_dot(a, wg_ref[:, cols])) * _dot(a, wu_ref[:, cols])
        acc = acc + _dot(mid.astype(BF16), wd_ref[cols, :])
    o_ref[...] = _rms(acc, gf_ref[...])


def _ffn(h2, g, wgu, wd, gf):
    m, d = h2.shape
    ff = wd.shape[0]
    const = lambda shape: pl.BlockSpec(shape, lambda i: (0, 0))
    return pl.pallas_call(
        functools.partial(_ffn_kernel, fc=ff // 2),
        out_shape=jax.ShapeDtypeStruct((m, d), F32),
        grid=(m // TOK_TILE,),
        in_specs=[pl.BlockSpec((TOK_TILE, d), lambda i: (i, 0)),
                  const((1, d)), const((d, ff)), pl.BlockSpec((d, ff), lambda i: (0, 1)),
                  const((ff, d)), const((1, d))],
        out_specs=pl.BlockSpec((TOK_TILE, d), lambda i: (i, 0)),
        compiler_params=_cparams(("arbitrary",)),
        name="ffn",
    )(h2, g, wgu, wgu, wd, gf)


def _overlap_t(t):
    nc = (t - CMP_LEN) // CMP_STRIDE + 1
    ns = t // SLC_LEN
    starts = CMP_STRIDE * np.arange(N_CMP_PAD)
    s_start = SLC_LEN * np.arange(ns)
    ov = ((starts[None, :] + CMP_LEN > s_start[:, None]) & (starts[None, :] < s_start[:, None] + SLC_LEN)
          & (np.arange(N_CMP_PAD)[None, :] < nc))
    return jnp.asarray(ov.astype(np.float32), dtype=BF16)


def kernel(x, mem, g_mix, w_in, lower_bounds, g_rnn_out, pe_ck, w_ck1, w_ck2, pe_cv, w_cv1, w_cv2,
           w_proj_rnn, w_proj_att, w_out, g_xattn, g_mem, w_xq, w_xkv, w_xo, g_ffn, w_gate_up,
           w_down, g_final):
    b, t, d = x.shape
    depth = g_mix.shape[0]
    assert depth == 1, "the final RMSNorm is fused into the layer's FFN kernel"
    lbs = jnp.cumsum(jax.nn.softmax(lower_bounds.astype(F32), axis=0), axis=0)
    slopes = (2.0 ** (-8.0 * jnp.arange(1, H_ATT + 1, dtype=F32) / H_ATT)).reshape(G_KV, HG, 1, 1)
    overlap_t = _overlap_t(t)
    h = x
    for l in range(depth):
        w_n, w_t = _w_prep(w_in[l].T)
        x2 = h.reshape(b * t, d)
        proj = _in_proj(x2, g_mix[l][None, :], w_n)
        proj3 = proj.reshape(b, t, N_COLS)
        qv, gt = _in_proj_t(h, g_mix[l][None, :], w_t)
        y_r = _hgrn(proj3, lbs[l][None, :], g_rnn_out[l][None, :])
        pe_kv, w_c1 = _compress_weights(pe_ck[l], pe_cv[l], w_ck1[l], w_cv1[l])
        kc, vct = _compress(proj3, pe_kv, w_c1, w_ck2[l].astype(BF16), w_cv2[l].T.astype(BF16))
        ocmp, selb, used = _cmp_select(qv, kc, vct, slopes, overlap_t, t)
        n_extra, extra = _extra_chunks(used, t)
        y_a = _sparse(n_extra, extra, qv, gt, proj3, ocmp, selb, slopes, t)
        h1 = _merge(x2, y_r.reshape(b * t, d), y_a.reshape(b * t, d), proj,
                    w_proj_rnn[l].astype(BF16), w_proj_att[l].astype(BF16), w_out[l].astype(BF16))
        w_kv = w_xkv[l].astype(BF16)
        kt, v = _mem_kv(mem, g_mem[l][None, :], w_kv[:, :H_X * X_HD].T, w_kv)
        h2 = _xattn(h1.reshape(b, t, d), g_xattn[l][None, :], w_xq[l].astype(BF16), kt, v,
                    w_xo[l].astype(BF16))
        h = _ffn(h2.reshape(b * t, d), g_ffn[l][None, :], w_gate_up[l].astype(BF16),
                 w_down[l].astype(BF16), g_final[None, :]).reshape(b, t, d)
    return h
```

```python
import functools
import itertools

import jax
import jax.numpy as jnp
import numpy as np
from jax import lax
from jax.experimental import pallas as pl
from jax.experimental.pallas import tpu as pltpu

F32 = jnp.float32
BF16 = jnp.bfloat16

D_MODEL = 1024
N_MEM = 256
H_RNN = 8
RNN_DIM = 128
RNN_CHUNK = 64
H_ATT = 16
ATT_HD = 64
G_KV = 4
HG = H_ATT // G_KV
CMP_LEN = 32
CMP_STRIDE = 16
CMP_HIDDEN = 128
SLC_LEN = 64
SLC_TOPK = 8
WINDOW = 512
FORCE_BONUS = 1.0e4
H_X = 4
X_HD = 128
D_FF = 2816
EPS = 1e-6

LANE = 128
MXU_TILE = 256
VMEM_LIMIT = 56 * 1024 * 1024
TOK_TILE = 512
ATT_TILE = 128
CMP_TILE = 512
N_CMP_PAD = 128
HGRN_GROUP = 4
ONES_ROWS = 16
NEG_BIG = -1.0e30
LOG2E = 1.4426950408889634
AUG_SEL_ROW0 = 8
SEL_NEAR = 5

QT_ROWS = H_ATT * ATT_HD
VT_ROWS = 2 * G_KV * ATT_HD
GATE_ROWS = 64
F_ROWS = -(-(QT_ROWS + VT_ROWS + 3 * H_ATT) // LANE) * LANE
RNN_COLS = 4 * H_RNN * RNN_DIM
KV_COLS = 4 * G_KV * ATT_HD
MG_COLS = 2 * D_MODEL
N_COLS = RNN_COLS + KV_COLS + MG_COLS


def _cparams(sem):
    return pltpu.CompilerParams(dimension_semantics=sem, vmem_limit_bytes=VMEM_LIMIT)


def _rms(xf, g):
    return xf * lax.rsqrt(jnp.mean(xf * xf, axis=-1, keepdims=True) + EPS) * g


def _sigmoid(x):
    return 1.0 / (1.0 + jnp.exp2(x * -LOG2E))


def _silu(x):
    return x * _sigmoid(x)


def _dot(a, b):
    return jnp.dot(a, b, preferred_element_type=F32)


def _dot_nt(a, b):
    return lax.dot_general(a, b, (((1,), (1,)), ((), ())), preferred_element_type=F32)


def _dot_tn(a, b):
    return lax.dot_general(a, b, (((0,), (0,)), ((), ())), preferred_element_type=F32)


def _w_prep_kernel(w_ref, wn_ref, wf_ref):
    kvw = G_KV * ATT_HD
    kv0 = RNN_COLS + QT_ROWS
    gate0 = kv0 + 6 * kvw
    mg0 = gate0 + 3 * H_ATT

    def pair_by_group(a0, b0):
        parts = []
        for g in range(G_KV):
            parts += [w_ref[a0 + g * ATT_HD:a0 + (g + 1) * ATT_HD, :], w_ref[b0 + g * ATT_HD:b0 + (g + 1) * ATT_HD, :]]
        return parts

    wn = jnp.concatenate([w_ref[0:RNN_COLS, :]] + pair_by_group(kv0, kv0 + kvw)
                         + pair_by_group(kv0 + 2 * kvw, kv0 + 4 * kvw) + [w_ref[mg0:mg0 + MG_COLS, :]], axis=0)
    wn_ref[...] = wn.T.astype(BF16)
    used = QT_ROWS + VT_ROWS + 3 * H_ATT
    wf = jnp.concatenate([w_ref[RNN_COLS:kv0, :] * (ATT_HD ** -0.5 * LOG2E),
                          w_ref[kv0 + 3 * kvw:kv0 + 4 * kvw, :], w_ref[kv0 + 5 * kvw:kv0 + 6 * kvw, :],
                          w_ref[gate0:mg0, :], jnp.zeros((wf_ref.shape[0] - used, w_ref.shape[1]), F32)], axis=0)
    wf_ref[...] = wf.astype(BF16)


def _w_prep(wt):
    n_in, d = wt.shape
    return pl.pallas_call(
        _w_prep_kernel,
        out_shape=(jax.ShapeDtypeStruct((d, N_COLS), BF16), jax.ShapeDtypeStruct((F_ROWS, d), BF16)),
        grid=(d // LANE,),
        in_specs=[pl.BlockSpec((n_in, LANE), lambda i: (0, i))],
        out_specs=(pl.BlockSpec((LANE, N_COLS), lambda i: (i, 0)),
                   pl.BlockSpec((F_ROWS, LANE), lambda i: (0, i))),
        compiler_params=_cparams(("arbitrary",)),
        name="w_prep",
    )(wt)


def _in_proj_kernel(x_ref, g_ref, w_ref, o_ref, *, sub):
    a = _rms(x_ref[...], g_ref[...]).astype(BF16)
    for n in range(o_ref.shape[1] // sub):
        cols = slice(n * sub, (n + 1) * sub)
        o_ref[:, cols] = _dot(a, w_ref[:, cols]).astype(o_ref.dtype)


def _in_proj(x2, g, w):
    m, d = x2.shape
    n = w.shape[1]
    tn = n // 2
    return pl.pallas_call(
        functools.partial(_in_proj_kernel, sub=512),
        out_shape=jax.ShapeDtypeStruct((m, n), BF16),
        grid=(2, m // TOK_TILE),
        in_specs=[pl.BlockSpec((TOK_TILE, d), lambda j, i: (i, 0)),
                  pl.BlockSpec((1, d), lambda j, i: (0, 0)),
                  pl.BlockSpec((d, tn), lambda j, i: (0, j))],
        out_specs=pl.BlockSpec((TOK_TILE, tn), lambda j, i: (i, j)),
        compiler_params=_cparams(("arbitrary", "arbitrary")),
        name="in_proj",
    )(x2, g, w)


def _in_proj_t_kernel(x_ref, g_ref, wt_ref, qv_ref, gt_ref):
    a = _rms(x_ref[...], g_ref[...]).astype(BF16)
    r = _dot_nt(wt_ref[...], a)
    nqv = qv_ref.shape[1]
    ngt = gt_ref.shape[1]
    for c in range(qv_ref.shape[0]):
        qv_ref[c] = r[:nqv, c * LANE:(c + 1) * LANE].astype(qv_ref.dtype)
        gt_ref[c] = r[nqv:nqv + ngt, c * LANE:(c + 1) * LANE]


def _in_proj_t(x, g, wt):
    b, t, d = x.shape
    rows = wt.shape[0]
    nqv = QT_ROWS + VT_ROWS
    ngt = GATE_ROWS
    nc = TOK_TILE // LANE
    return pl.pallas_call(
        _in_proj_t_kernel,
        out_shape=(jax.ShapeDtypeStruct((b, t // LANE, nqv, LANE), BF16),
                   jax.ShapeDtypeStruct((b, t // LANE, ngt, LANE), F32)),
        grid=(b, t // TOK_TILE),
        in_specs=[pl.BlockSpec((None, TOK_TILE, d), lambda i, j: (i, j, 0)),
                  pl.BlockSpec((1, d), lambda i, j: (0, 0)),
                  pl.BlockSpec((rows, d), lambda i, j: (0, 0))],
        out_specs=(pl.BlockSpec((None, nc, nqv, LANE), lambda i, j: (i, j, 0, 0)),
                   pl.BlockSpec((None, nc, ngt, LANE), lambda i, j: (i, j, 0, 0))),
        compiler_params=_cparams(("arbitrary", "arbitrary")),
        name="in_proj_t",
    )(x, g, wt)


def _hgrn_kernel(q_ref, f_ref, i_ref, og_ref, lb_ref, gn_ref, o_ref, qd_ref, oi_ref, ut_ref, dec_ref):
    c = RNN_CHUNK
    kd = RNN_DIM
    n_chunks = q_ref.shape[0] // c
    lb = lb_ref[...]
    gn = gn_ref[...]
    blk = HGRN_GROUP * c
    row = lax.broadcasted_iota(jnp.int32, (blk, blk), 0)
    col = lax.broadcasted_iota(jnp.int32, (blk, blk), 1)
    same_chunk = (row // c) == (col // c)
    causal = same_chunk & (row >= col)
    tril = causal.astype(BF16)

    q = q_ref[...].astype(F32)
    fl = f_ref[...].astype(F32)
    v = i_ref[...]
    f = lb + (1.0 - lb) * _sigmoid(fl)
    k = 1.0 - f
    logf = jnp.log(f)
    hi = logf.astype(BF16)
    lo = (logf - hi.astype(F32)).astype(BF16)
    pieces = jnp.concatenate([hi, lo], axis=1)
    n_blk = q_ref.shape[0] // blk
    cs = [_dot(tril, pieces[g * blk:(g + 1) * blk]) for g in range(n_blk)]
    bcum = jnp.concatenate([x[:, 0:kd] + x[:, kd:2 * kd] for x in cs], axis=0)
    e_neg = jnp.exp2(bcum * -LOG2E)
    dec = jnp.exp(jnp.concatenate([bcum[n * c + c - 1:n * c + c, :] for n in range(n_chunks)], axis=0))
    dec_rows = jnp.concatenate([jnp.broadcast_to(dec[n:n + 1, :], (c, kd)) for n in range(n_chunks)], axis=0)
    q_dec = (_silu(q) * (1.0 / e_neg)).astype(BF16)
    k_neg = k * e_neg
    k_dec = k_neg.astype(BF16)
    k_end = (k_neg * dec_rows).astype(BF16)
    qd_ref[...] = q_dec
    dec_ref[...] = dec
    for g in range(n_blk):
        rows = slice(g * blk, (g + 1) * blk)
        a = jnp.where(causal, _dot_nt(q_dec[rows], k_dec[rows]), 0.0)
        oi_ref[rows, :] = _dot(a.astype(BF16), v[rows])
    for n in range(n_chunks):
        rows = slice(n * c, (n + 1) * c)
        ut_ref[n] = _dot_tn(v[rows], k_end[rows])

    s_t = jnp.zeros((kd, kd), F32)
    for n in range(n_chunks):
        rows = slice(n * c, (n + 1) * c)
        o = oi_ref[rows, :] + _dot_nt(qd_ref[rows, :], s_t.astype(BF16))
        og = og_ref[rows, :].astype(F32)
        o_ref[rows, :] = (_rms(o, gn) * _silu(og)).astype(o_ref.dtype)
        s_t = s_t * dec_ref[n:n + 1, :] + ut_ref[n]


def _hgrn(proj3, lb, gn):
    b, t, _ = proj3.shape
    return pl.pallas_call(
        _hgrn_kernel,
        out_shape=jax.ShapeDtypeStruct((b, t, H_RNN * RNN_DIM), BF16),
        grid=(b, H_RNN),
        in_specs=[pl.BlockSpec((None, t, RNN_DIM), lambda i, h: (i, 0, h)),
                  pl.BlockSpec((None, t, RNN_DIM), lambda i, h: (i, 0, H_RNN + h)),
                  pl.BlockSpec((None, t, RNN_DIM), lambda i, h: (i, 0, 2 * H_RNN + h)),
                  pl.BlockSpec((None, t, RNN_DIM), lambda i, h: (i, 0, 3 * H_RNN + h)),
                  pl.BlockSpec((1, RNN_DIM), lambda i, h: (0, h)),
                  pl.BlockSpec((1, RNN_DIM), lambda i, h: (0, 0))],
        out_specs=pl.BlockSpec((None, t, RNN_DIM), lambda i, h: (i, 0, h)),
        scratch_shapes=[pltpu.VMEM((t, RNN_DIM), BF16),
                        pltpu.VMEM((t, RNN_DIM), F32),
                        pltpu.VMEM((t // RNN_CHUNK, RNN_DIM, RNN_DIM), F32),
                        pltpu.VMEM((t // RNN_CHUNK, RNN_DIM), F32)],
        compiler_params=_cparams(("arbitrary", "arbitrary")),
        name="hgrn",
    )(proj3, proj3, proj3, proj3, lb, gn)


def _compress_kernel(kv_ref, pe_ref, w1_ref, wk2_ref, wv2t_ref, kc_ref, vct_ref, xs_ref, x_ref):
    t = kv_ref.shape[0]
    xs_ref[0:t, :] = kv_ref[...].astype(F32)
    xs_ref[t:, :] = jnp.zeros((xs_ref.shape[0] - t, xs_ref.shape[1]), F32)
    for l in range(CMP_LEN):
        blk = xs_ref[pl.ds(l, N_CMP_PAD, stride=CMP_STRIDE), :]
        x_ref[:, l * LANE:(l + 1) * LANE] = (blk + pe_ref[l:l + 1, :]).astype(BF16)
    hid = _silu(_dot(x_ref[...], w1_ref[...])).astype(BF16)
    kc_ref[...] = _dot(hid[:, :CMP_HIDDEN], wk2_ref[...])
    vct_ref[...] = _dot_nt(wv2t_ref[...], hid[:, CMP_HIDDEN:])


def _compress(proj3, pe, w1, wk2, wv2t):
    b, t, _ = proj3.shape
    kv_unit0 = RNN_COLS // LANE
    full = lambda shape: pl.BlockSpec(shape, lambda i, g: (0,) * len(shape))
    return pl.pallas_call(
        _compress_kernel,
        out_shape=(jax.ShapeDtypeStruct((b, G_KV, N_CMP_PAD, ATT_HD), F32),
                   jax.ShapeDtypeStruct((b, G_KV, ATT_HD, N_CMP_PAD), F32)),
        grid=(b, G_KV),
        in_specs=[pl.BlockSpec((None, t, LANE), lambda i, g: (i, 0, kv_unit0 + g)),
                  full((CMP_LEN, LANE)), full((CMP_LEN * LANE, 2 * CMP_HIDDEN)),
                  full((CMP_HIDDEN, ATT_HD)), full((ATT_HD, CMP_HIDDEN))],
        out_specs=(pl.BlockSpec((None, None, N_CMP_PAD, ATT_HD), lambda i, g: (i, g, 0, 0)),
                   pl.BlockSpec((None, None, ATT_HD, N_CMP_PAD), lambda i, g: (i, g, 0, 0))),
        scratch_shapes=[pltpu.VMEM((t + CMP_STRIDE, LANE), F32),
                        pltpu.VMEM((N_CMP_PAD, CMP_LEN * LANE), BF16)],
        compiler_params=_cparams(("arbitrary", "arbitrary")),
        name="compress",
    )(proj3, pe, w1, wk2, wv2t)


def _compress_weights(pek, pev, wk1, wv1):
    zk = jnp.zeros((CMP_LEN, ATT_HD, CMP_HIDDEN), wk1.dtype)
    top = jnp.concatenate([wk1.reshape(CMP_LEN, ATT_HD, CMP_HIDDEN), zk], axis=2)
    bot = jnp.concatenate([zk, wv1.reshape(CMP_LEN, ATT_HD, CMP_HIDDEN)], axis=2)
    w1 = jnp.concatenate([top, bot], axis=1).reshape(CMP_LEN * LANE, 2 * CMP_HIDDEN)
    return jnp.concatenate([pek, pev], axis=1), w1.astype(BF16)


def _cmp_select_kernel(qt_ref, kc_ref, vct_ref, slope_ref, ov_ref, ocmp_ref, selb_ref, used_ref, *, n_cmp, n_sel):
    nct = CMP_TILE // LANE
    tt = CMP_TILE
    ns = ov_ref.shape[0]
    kc = kc_ref[...].astype(BF16)
    vct = vct_ref[...].astype(BF16)
    ov = ov_ref[...]
    j_i = lax.broadcasted_iota(jnp.int32, (8, tt), 0)
    tq_rel = lax.broadcasted_iota(jnp.int32, (8, tt), 1)
    n_tiles = qt_ref.shape[0] // nct

    def ranges(tile):
        t0 = tile * tt
        n_any = min(N_CMP_PAD, (t0 + tt) // CMP_STRIDE)
        n_all = max(0, (t0 - (CMP_LEN - 1)) // CMP_STRIDE + 1) // 8 * 8
        return t0, n_any, n_all, (t0 + tt) // SLC_LEN

    def pad_rows(x):
        if x.shape[0] == N_CMP_PAD:
            return x
        return jnp.concatenate([x, jnp.zeros((N_CMP_PAD - x.shape[0], x.shape[1]), x.dtype)], axis=0)

    scores = []
    for tile in range(n_tiles):
        _, n_any, _, _ = ranges(tile)
        qt = [jnp.concatenate([qt_ref[tile * nct + c, h * ATT_HD:(h + 1) * ATT_HD, :] for c in range(nct)],
                              axis=1) for h in range(HG)]
        scores.append(_dot(kc[:n_any], jnp.concatenate(qt, axis=1)))

    probs, psums = [], []
    for tile in range(n_tiles):
        t0, n_any, n_all, _ = ranges(tile)
        n_b = lax.broadcasted_iota(jnp.int32, (n_any - n_all, tt), 0) + n_all
        t_b = lax.broadcasted_iota(jnp.int32, (n_any - n_all, tt), 1)
        visible = (CMP_STRIDE * n_b + (CMP_LEN - 1) - t0 <= t_b) & (n_b < n_cmp)
        centre = (lax.broadcasted_iota(jnp.int32, (n_any, tt), 0).astype(F32) * float(CMP_STRIDE)
                  + (CMP_LEN - 1) / 2.0)
        psum = jnp.zeros((n_any, tt), F32)
        ph = []
        for h in range(HG):
            s = scores[tile][:, h * tt:(h + 1) * tt] + slope_ref[h] * LOG2E * centre
            band = jnp.where(visible, s[n_all:], -jnp.inf)
            s = band if n_all == 0 else jnp.concatenate([s[:n_all], band], axis=0)
            m = jnp.max(s, axis=0, keepdims=True)
            m = jnp.where(m == -jnp.inf, 0.0, m)
            p = jnp.exp2(s - m)
            d = jnp.sum(p, axis=0, keepdims=True)
            p = p * (1.0 / jnp.where(d > 0, d, 1.0))
            psum = psum + p
            ph.append(pad_rows(p).astype(BF16))
        probs.append(jnp.concatenate(ph, axis=1))
        psums.append(pad_rows(psum))

    for tile in range(n_tiles):
        o = _dot(vct, probs[tile])
        for h in range(HG):
            for c in range(nct):
                lanes = slice(h * tt + c * LANE, h * tt + (c + 1) * LANE)
                ocmp_ref[tile * nct + c, h * ATT_HD:(h + 1) * ATT_HD, :] = o[:, lanes]

    imps = []
    for tile in range(n_tiles):
        psum = psums[tile]
        hi = psum.astype(BF16)
        rem = psum - hi.astype(F32)
        mid = rem.astype(BF16)
        lo = (rem - mid.astype(F32)).astype(BF16)
        imp3 = _dot(ov, jnp.concatenate([hi, mid, lo], axis=1))
        imps.append(imp3[:, 0:tt] + imp3[:, tt:2 * tt] + imp3[:, 2 * tt:3 * tt])

    for tile in range(n_tiles):
        t0, _, _, nvb = ranges(tile)
        imp = imps[tile]
        score = []
        for rg in range(nvb // 8):
            j = j_i + 8 * rg
            tq = tq_rel + t0
            cur = tq // SLC_LEN
            forced = (j == 0) | (j == cur) | (j == cur - 1)
            sc = imp[8 * rg:8 * rg + 8, :] + jnp.where(forced, FORCE_BONUS, 0.0)
            score.append(jnp.where(j * SLC_LEN <= tq, sc, -jnp.inf))
        rank = [jnp.zeros((8, tt), F32) for _ in score]
        for jp in range(nvb):
            sj = jnp.broadcast_to(score[jp // 8][jp % 8:jp % 8 + 1, :], (8, tt))
            for rg in range(nvb // 8):
                ge = jnp.where(sj >= score[rg], 1.0, 0.0)
                gt = jnp.where(sj > score[rg], 1.0, 0.0)
                if 8 * rg > jp:
                    ahead = ge
                elif 8 * rg + 7 <= jp:
                    ahead = gt
                else:
                    ahead = jnp.where(j_i + 8 * rg > jp, ge, gt)
                rank[rg] = rank[rg] + ahead
        for rg in range(ns // 8):
            for c in range(nct):
                if rg < nvb // 8:
                    picked = rank[rg][:, c * LANE:(c + 1) * LANE] < float(n_sel)
                    selb = jnp.where(picked, 0.0, NEG_BIG)
                    used = jnp.max(jnp.where(picked, 1.0, 0.0), axis=1, keepdims=True)
                else:
                    selb = jnp.full((8, LANE), NEG_BIG, F32)
                    used = jnp.zeros((8, 1), F32)
                selb_ref[tile * nct + c, 8 * rg:8 * rg + 8, :] = selb
                used_ref[tile * nct + c, 8 * rg:8 * rg + 8, :] = jnp.broadcast_to(used, (8, LANE))


def _cmp_select(qv, kc, vct, slopes, overlap_t, t):
    b = qv.shape[0]
    ns = t // SLC_LEN
    nch = t // LANE
    n_cmp = (t - CMP_LEN) // CMP_STRIDE + 1
    n_sel = min(SLC_TOPK, ns)
    rows = HG * ATT_HD
    return pl.pallas_call(
        functools.partial(_cmp_select_kernel, n_cmp=n_cmp, n_sel=n_sel),
        out_shape=(jax.ShapeDtypeStruct((b, nch, H_ATT * ATT_HD, LANE), F32),
                   jax.ShapeDtypeStruct((b, G_KV, nch, ns, LANE), F32),
                   jax.ShapeDtypeStruct((b, G_KV, nch, ns, LANE), F32)),
        grid=(b, G_KV),
        in_specs=[pl.BlockSpec((None, nch, rows, LANE), lambda i, g: (i, 0, g, 0)),
                  pl.BlockSpec((None, None, N_CMP_PAD, ATT_HD), lambda i, g: (i, g, 0, 0)),
                  pl.BlockSpec((None, None, ATT_HD, N_CMP_PAD), lambda i, g: (i, g, 0, 0)),
                  pl.BlockSpec((None, HG, 1, 1), lambda i, g: (g, 0, 0, 0)),
                  pl.BlockSpec((ns, N_CMP_PAD), lambda i, g: (0, 0))],
        out_specs=(pl.BlockSpec((None, nch, rows, LANE), lambda i, g: (i, 0, g, 0)),
                   pl.BlockSpec((None, None, nch, ns, LANE), lambda i, g: (i, g, 0, 0, 0)),
                   pl.BlockSpec((None, None, nch, ns, LANE), lambda i, g: (i, g, 0, 0, 0))),
        compiler_params=_cparams(("arbitrary", "arbitrary")),
        name="cmp_select",
    )(qv, kc, vct, slopes, overlap_t)


def _sparse_kernel(n_extra_ref, extra_ref, qt_ref, k_ref, ka_ref, vs_ref, vw_ref, gt_ref, ocmp_ref, selb_ref,
                   slope_ref, o_ref, msk_ref, qop_ref, s_ref, m_ref, acc_ref):
    tq = ATT_TILE
    kc = ATT_TILE
    hd = ATT_HD
    nl = HG * tq
    n_tiles = qt_ref.shape[0]
    ns = selb_ref.shape[1]
    wch = WINDOW // kc
    n_sel_items = 2 * (SEL_NEAR + 1)
    tile_base = (pl.program_id(0) * pl.num_programs(1) + pl.program_id(1)) * n_tiles
    slope = jnp.concatenate([jnp.broadcast_to(slope_ref[h], (1, tq)) for h in range(HG)], axis=1) * LOG2E
    s_hi = slope.astype(BF16).astype(F32)
    s_mid = (slope - s_hi).astype(BF16).astype(F32)
    s_lo = (slope - s_hi - s_mid).astype(BF16).astype(F32)
    slope_rows = jnp.concatenate([s_hi, s_mid, s_lo, jnp.zeros((AUG_SEL_ROW0 - 3, nl), F32)], axis=0)
    aug_tail = jnp.zeros((LANE - AUG_SEL_ROW0 - ns, nl), F32)
    aug_win = jnp.concatenate([slope_rows, jnp.zeros((ns, nl), F32), aug_tail], axis=0).astype(BF16)
    k_i = lax.broadcasted_iota(jnp.int32, (kc, nl), 0)
    t_i = lax.broadcasted_iota(jnp.int32, (kc, nl), 1) % tq
    d0 = t_i - k_i
    msk_ref[0] = jnp.where(d0 >= 0, 0.0, NEG_BIG)
    msk_ref[1] = jnp.where(d0 < 0, 0.0, NEG_BIG)
    chunk_bias = slope * float(kc)
    ones = jnp.ones((ONES_ROWS, kc), BF16)

    n_pairs = n_tiles // 2
    n_items = n_sel_items + 2 * (wch + 1)
    n_slots = s_ref.shape[0] // 2

    def tiles_of(p):
        return (p, n_tiles - 1 - p)

    def sel_item(p, w, j):
        tile = tiles_of(p)[w]
        if j == SEL_NEAR:
            return 0, tile >= SEL_NEAR, tile
        return max(tile - j, 0), tile - j >= 0, j

    def win_item(p, w, j):
        chunk = tiles_of(p)[w] - wch + j
        return max(chunk, 0), chunk >= 0, wch - j

    def col_max(r, delta_f):
        r8 = jnp.max(r.reshape(kc // 8, 8, nl), axis=0)
        return r8 - chunk_bias * delta_f

    def scores(chunk, op):
        rows = pl.ds(pl.multiple_of(chunk * kc, kc), kc)
        keys = jnp.concatenate([k_ref[rows, :], ka_ref[rows, :]], axis=1)
        return _dot(keys, qop_ref[op])

    def build(p, slot):
        for w, tile in enumerate(tiles_of(p)):
            qt = qt_ref[tile]
            q_all = jnp.concatenate([qt[h * hd:(h + 1) * hd, :] for h in range(HG)], axis=1)
            zero = jnp.zeros_like(q_all)
            selb = jnp.concatenate([selb_ref[tile]] * HG, axis=1)
            aug_sel = jnp.concatenate([slope_rows, selb, aug_tail], axis=0).astype(BF16)
            qop_ref[4 * slot + 2 * w] = jnp.concatenate([q_all, zero, aug_sel], axis=0)
            qop_ref[4 * slot + 2 * w + 1] = jnp.concatenate([zero, q_all, aug_win], axis=0)
        for x in range(4):
            m_ref[4 * slot + x] = jnp.full((8, nl), NEG_BIG, F32)

    def phase_a(p, slot):
        items = []

        def sel(w, j):
            chunk, _, delta = sel_item(p, w, j)
            r = scores(chunk, 4 * slot + 2 * w)
            if j == 0:
                r = r + msk_ref[0]
            s_ref[slot * n_slots + w * (SEL_NEAR + 1) + j] = r
            x = 4 * slot + 2 * w
            m_ref[x] = jnp.maximum(m_ref[x], col_max(r, float(delta)))

        def win(w, j):
            chunk, _, delta = win_item(p, w, j)
            r = scores(chunk, 4 * slot + 2 * w + 1)
            if j == 0:
                r = r + msk_ref[1]
            elif j == wch:
                r = r + msk_ref[0]
            s_ref[slot * n_slots + n_sel_items + w * (wch + 1) + j] = r
            x = 4 * slot + 2 * w + 1
            m_ref[x] = jnp.maximum(m_ref[x], col_max(r, float(delta)))

        for w in range(2):
            for j in range(SEL_NEAR + 1):
                if sel_item(p, w, j)[1]:
                    items.append(functools.partial(sel, w, j))
        for w in range(2):
            for j in range(wch + 1):
                if win_item(p, w, j)[1]:
                    items.append(functools.partial(win, w, j))
        return items

    def overflow(p, slot, visit):
        count = 0
        for w, tile in enumerate(tiles_of(p)):
            def body(e, n, w=w, tile=tile):
                visit(w, tile, extra_ref[(tile_base + tile) * n_tiles + e], n)
                return n + 1

            count = lax.fori_loop(0, n_extra_ref[tile_base + tile], body, count)

    def overflow_a(p, slot):
        def visit(w, tile, c, n):
            r = scores(c, 4 * slot + 2 * w)
            s_ref[slot * n_slots + n_items + n] = r
            x = 4 * slot + 2 * w
            m_ref[x] = jnp.maximum(m_ref[x], col_max(r, jnp.float32(tile - c)))

        overflow(p, slot, visit)

    def overflow_b(p, slot, m_row):
        def visit(w, tile, c, n):
            row = m_row[2 * w] + chunk_bias * jnp.float32(tile - c)
            pr = jnp.exp2(s_ref[slot * n_slots + n_items + n] - row).astype(BF16)
            v_aug = jnp.concatenate([vs_ref[c], ones], axis=0)
            acc_ref[4 * slot + 2 * w] += _dot(v_aug, pr)

        overflow(p, slot, visit)

    def phase_b(p, slot, m_row):
        for x in range(4):
            acc_ref[4 * slot + x] = jnp.zeros(acc_ref.shape[1:], F32)
        items = []

        def pv(x, v_ref, group):
            prs, vs = [], []
            for s_idx, chunk, delta in group:
                row = m_row[x] + chunk_bias * float(delta)
                prs.append(jnp.exp2(s_ref[slot * n_slots + s_idx] - row).astype(BF16))
                vs.append(jnp.concatenate([v_ref[chunk], ones], axis=0))
            acc_ref[4 * slot + x] += _dot(jnp.concatenate(vs, axis=1), jnp.concatenate(prs, axis=0))

        for w in range(2):
            sel = [(w * (SEL_NEAR + 1) + j,) + sel_item(p, w, j)[::2]
                   for j in range(SEL_NEAR + 1) if sel_item(p, w, j)[1]]
            win = [(n_sel_items + w * (wch + 1) + j,) + win_item(p, w, j)[::2]
                   for j in range(wch + 1) if win_item(p, w, j)[1]]
            for x, v_ref, group in ((2 * w, vs_ref, sel), (2 * w + 1, vw_ref, win)):
                for i in range(0, len(group), 2):
                    items.append(functools.partial(pv, x, v_ref, group[i:i + 2]))
        return items

    def finalize(p, slot):
        for w, tile in enumerate(tiles_of(p)):
            a_s = acc_ref[4 * slot + 2 * w]
            a_w = acc_ref[4 * slot + 2 * w + 1]
            o_slc = a_s[:hd, :] / a_s[hd:hd + 1, :]
            o_win = a_w[:hd, :] / a_w[hd:hd + 1, :]
            oc = ocmp_ref[tile]
            outs = []
            for h in range(HG):
                lanes = slice(h * tq, (h + 1) * tq)
                row0 = (pl.program_id(1) * HG + h) * 3
                g_cmp, g_slc, g_win = [_sigmoid(gt_ref[tile, pl.ds(row0 + br, 1), :]) for br in range(3)]
                y = (g_cmp * oc[h * hd:(h + 1) * hd, :] + g_slc * o_slc[:, lanes]
                     + g_win * o_win[:, lanes])
                outs.append(y.T)
            rows = pl.ds(pl.multiple_of(tile * tq, tq), tq)
            o_ref[rows, :] = jnp.concatenate(outs, axis=1).astype(o_ref.dtype)

    def step(k, slot, first=False, last=False):
        m_row = [jnp.max(m_ref[4 * slot + x], axis=0, keepdims=True) for x in range(4)]
        if not last:
            build(k + 1, 1 - slot)
        if not first:
            finalize(k - 1, 1 - slot)
        b_items = phase_b(k, slot, m_row)
        a_items = phase_a(k + 1, 1 - slot) if not last else []
        for a, b in itertools.zip_longest(a_items, b_items):
            if a is not None:
                a()
            if b is not None:
                b()
        if not last:
            overflow_a(k + 1, 1 - slot)
        overflow_b(k, slot, m_row)

    build(0, 0)
    for a in phase_a(0, 0):
        a()
    overflow_a(0, 0)
    for k in range(n_pairs):
        step(k, k % 2, first=k == 0, last=k == n_pairs - 1)
    finalize(n_pairs - 1, (n_pairs - 1) % 2)


def _key_features(t):
    kp = np.arange(t)
    f = np.zeros((t, LANE), np.float32)
    f[:, 0:3] = (kp % ATT_TILE)[:, None]
    f[kp, AUG_SEL_ROW0 + kp // SLC_LEN] = 1.0
    return jnp.asarray(f, dtype=BF16)


def _extra_chunks(used, t):
    b = used.shape[0]
    nt = t // ATT_TILE
    need = used.max(axis=-1).reshape(b, G_KV, nt, nt, ATT_TILE // SLC_LEN).max(axis=-1) > 0.5
    chunk = jnp.arange(nt, dtype=jnp.int32)
    extra = need & (chunk >= 1) & (chunk[None, :] <= chunk[:, None] - SEL_NEAR)
    place = jnp.cumsum(extra.astype(jnp.int32), axis=-1) - 1
    hit = extra[..., :, None] & (place[..., :, None] == chunk)
    lst = jnp.sum(jnp.where(hit, chunk[:, None], 0), axis=-2)
    return extra.sum(axis=-1, dtype=jnp.int32).reshape(-1), lst.reshape(-1)


def _sparse(n_extra, extra, qv, gt, proj3, ocmp, selb, slopes, t):
    b = qv.shape[0]
    nchunk = t // LANE
    ns = t // SLC_LEN
    rows = HG * ATT_HD
    ksw_unit0 = RNN_COLS // LANE + G_KV
    vs_blk0 = QT_ROWS // ATT_HD
    vw_blk0 = vs_blk0 + G_KV
    nl = HG * ATT_TILE
    n_slots = 2 * (SEL_NEAR + 1) + 2 * (WINDOW // ATT_TILE + 1) + (nchunk - 1 - SEL_NEAR)
    grid_spec = pltpu.PrefetchScalarGridSpec(
        num_scalar_prefetch=2,
        grid=(b, G_KV),
        in_specs=[pl.BlockSpec((None, nchunk, rows, LANE), lambda i, g, *_: (i, 0, g, 0)),
                  pl.BlockSpec((None, t, LANE), lambda i, g, *_: (i, 0, ksw_unit0 + g)),
                  pl.BlockSpec((t, LANE), lambda i, g, *_: (0, 0)),
                  pl.BlockSpec((None, nchunk, ATT_HD, LANE), lambda i, g, *_: (i, 0, vs_blk0 + g, 0)),
                  pl.BlockSpec((None, nchunk, ATT_HD, LANE), lambda i, g, *_: (i, 0, vw_blk0 + g, 0)),
                  pl.BlockSpec((None, nchunk, GATE_ROWS, LANE), lambda i, g, *_: (i, 0, 0, 0)),
                  pl.BlockSpec((None, nchunk, rows, LANE), lambda i, g, *_: (i, 0, g, 0)),
                  pl.BlockSpec((None, None, nchunk, ns, LANE), lambda i, g, *_: (i, g, 0, 0, 0)),
                  pl.BlockSpec((None, HG, 1, 1), lambda i, g, *_: (g, 0, 0, 0))],
        out_specs=pl.BlockSpec((None, t, rows), lambda i, g, *_: (i, 0, g)),
        scratch_shapes=[pltpu.VMEM((2, ATT_TILE, nl), F32),
                        pltpu.VMEM((2 * 4, 2 * LANE, nl), BF16),
                        pltpu.VMEM((2 * n_slots, ATT_TILE, nl), F32),
                        pltpu.VMEM((2 * 4, 8, nl), F32),
                        pltpu.VMEM((2 * 4, ATT_HD + ONES_ROWS, nl), F32)])
    return pl.pallas_call(
        _sparse_kernel,
        out_shape=jax.ShapeDtypeStruct((b, t, H_ATT * ATT_HD), BF16),
        grid_spec=grid_spec,
        compiler_params=_cparams(("arbitrary", "arbitrary")),
        name="sparse",
    )(n_extra, extra, qv, proj3, _key_features(t), qv, qv, gt, ocmp, selb, slopes)


def _merge_kernel(x_ref, yr_ref, ya_ref, mgr_ref, mga_ref, wr_ref, wa_ref, wo_ref, o_ref):
    pr = _dot(yr_ref[...], wr_ref[...])
    pa = _dot(ya_ref[...], wa_ref[...])
    merged = _sigmoid(mgr_ref[...].astype(F32)) * pr + _sigmoid(mga_ref[...].astype(F32)) * pa
    o_ref[...] = x_ref[...] + _dot(merged.astype(BF16), wo_ref[...])


def _merge(x2, yr, ya, proj, wr, wa, wo):
    m, d = x2.shape
    mg0 = (RNN_COLS + KV_COLS) // d
    tile = lambda col: pl.BlockSpec((TOK_TILE, d), lambda i: (i, col))
    wfull = pl.BlockSpec((d, d), lambda i: (0, 0))
    return pl.pallas_call(
        _merge_kernel,
        out_shape=jax.ShapeDtypeStruct((m, d), F32),
        grid=(m // TOK_TILE,),
        in_specs=[tile(0), tile(0), tile(0), tile(mg0), tile(mg0 + 1), wfull, wfull, wfull],
        out_specs=tile(0),
        compiler_params=_cparams(("arbitrary",)),
        name="merge",
    )(x2, yr, ya, proj, proj, wr, wa, wo)


def _mem_kv_kernel(mem_ref, g_ref, wkt_ref, wv_ref, kt_ref, v_ref):
    a = _rms(mem_ref[...], g_ref[...]).astype(BF16)
    kt_ref[...] = _dot_nt(wkt_ref[...], a).astype(kt_ref.dtype)
    v_ref[...] = _dot(a, wv_ref[...]).astype(v_ref.dtype)


def _mem_kv(mem, g, wkt, wkv):
    b, nm, d = mem.shape
    hw = H_X * X_HD
    return pl.pallas_call(
        _mem_kv_kernel,
        out_shape=(jax.ShapeDtypeStruct((b, hw, nm), BF16),
                   jax.ShapeDtypeStruct((b, nm, hw), BF16)),
        grid=(b,),
        in_specs=[pl.BlockSpec((None, nm, d), lambda i: (i, 0, 0)),
                  pl.BlockSpec((1, d), lambda i: (0, 0)),
                  pl.BlockSpec((hw, d), lambda i: (0, 0)),
                  pl.BlockSpec((d, hw), lambda i: (0, 1))],
        out_specs=(pl.BlockSpec((None, hw, nm), lambda i: (i, 0, 0)),
                   pl.BlockSpec((None, nm, hw), lambda i: (i, 0, 0))),
        compiler_params=_cparams(("arbitrary",)),
        name="mem_kv",
    )(mem, g, wkt, wkv)


def _xattn_kernel(h_ref, g_ref, wq_ref, kt_ref, v_ref, wo_ref, o_ref):
    h = h_ref[...]
    a = _rms(h, g_ref[...]).astype(BF16)
    q = (_dot(a, wq_ref[...]) * (X_HD ** -0.5)).astype(BF16)
    heads = [slice(hh * X_HD, (hh + 1) * X_HD) for hh in range(H_X)]
    scores = [_dot(q[:, cols], kt_ref[cols, :]) for cols in heads]
    probs = []
    for s in scores:
        m = jnp.max(s, axis=-1, keepdims=True)
        p = jnp.exp(s - m)
        probs.append((p / jnp.sum(p, axis=-1, keepdims=True)).astype(BF16))
    outs = [_dot(p, v_ref[:, cols]) for p, cols in zip(probs, heads)]
    o = jnp.concatenate(outs, axis=1).astype(BF16)
    o_ref[...] = h + _dot(o, wo_ref[...])


def _xattn(h3, g, wq, kt, v, wo):
    b, t, d = h3.shape
    hw = H_X * X_HD
    nm = v.shape[1]
    return pl.pallas_call(
        _xattn_kernel,
        out_shape=jax.ShapeDtypeStruct((b, t, d), F32),
        grid=(b, t // TOK_TILE),
        in_specs=[pl.BlockSpec((None, TOK_TILE, d), lambda i, j: (i, j, 0)),
                  pl.BlockSpec((1, d), lambda i, j: (0, 0)),
                  pl.BlockSpec((d, hw), lambda i, j: (0, 0)),
                  pl.BlockSpec((None, hw, nm), lambda i, j: (i, 0, 0)),
                  pl.BlockSpec((None, nm, hw), lambda i, j: (i, 0, 0)),
                  pl.BlockSpec((hw, d), lambda i, j: (0, 0))],
        out_specs=pl.BlockSpec((None, TOK_TILE, d), lambda i, j: (i, j, 0)),
        compiler_params=_cparams(("arbitrary", "arbitrary")),
        name="xattn",
    )(h3, g, wq, kt, v, wo)


def _ffn_kernel(h_ref, g_ref, wg_ref, wu_ref, wd_ref, gf_ref, o_ref, *, bounds):
    h = h_ref[...]
    a = _rms(h, g_ref[...]).astype(BF16)
    acc = h
    for lo, hi in zip(bounds[:-1], bounds[1:]):
        cols = slice(lo, hi)
        mid = _silu(_dot(a, wg_ref[:, cols])) * _dot(a, wu_ref[:, cols])
        acc = acc + _dot(mid.astype(BF16), wd_ref[cols, :])
    o_ref[...] = _rms(acc, gf_ref[...])


def _ffn(h2, g, wgu, wd, gf):
    m, d = h2.shape
    ff = wd.shape[0]
    const = lambda shape: pl.BlockSpec(shape, lambda i: (0, 0))
    return pl.pallas_call(
        functools.partial(_ffn_kernel, bounds=(0, -(-ff // (2 * MXU_TILE)) * MXU_TILE, ff)),
        out_shape=jax.ShapeDtypeStruct((m, d), F32),
        grid=(m // TOK_TILE,),
        in_specs=[pl.BlockSpec((TOK_TILE, d), lambda i: (i, 0)),
                  const((1, d)), const((d, ff)), pl.BlockSpec((d, ff), lambda i: (0, 1)),
                  const((ff, d)), const((1, d))],
        out_specs=pl.BlockSpec((TOK_TILE, d), lambda i: (i, 0)),
        compiler_params=_cparams(("arbitrary",)),
        name="ffn",
    )(h2, g, wgu, wgu, wd, gf)


def _overlap_t(t):
    nc = (t - CMP_LEN) // CMP_STRIDE + 1
    ns = t // SLC_LEN
    starts = CMP_STRIDE * np.arange(N_CMP_PAD)
    s_start = SLC_LEN * np.arange(ns)
    ov = ((starts[None, :] + CMP_LEN > s_start[:, None]) & (starts[None, :] < s_start[:, None] + SLC_LEN)
          & (np.arange(N_CMP_PAD)[None, :] < nc))
    return jnp.asarray(ov.astype(np.float32), dtype=BF16)


def kernel(x, mem, g_mix, w_in, lower_bounds, g_rnn_out, pe_ck, w_ck1, w_ck2, pe_cv, w_cv1, w_cv2,
           w_proj_rnn, w_proj_att, w_out, g_xattn, g_mem, w_xq, w_xkv, w_xo, g_ffn, w_gate_up,
           w_down, g_final):
    b, t, d = x.shape
    depth = g_mix.shape[0]
    assert depth == 1, "the final RMSNorm is fused into the layer's FFN kernel"
    lbs = jnp.cumsum(jax.nn.softmax(lower_bounds.astype(F32), axis=0), axis=0)
    slopes = (2.0 ** (-8.0 * jnp.arange(1, H_ATT + 1, dtype=F32) / H_ATT)).reshape(G_KV, HG, 1, 1)
    overlap_t = _overlap_t(t)
    h = x
    for l in range(depth):
        w_n, w_t = _w_prep(w_in[l].T)
        x2 = h.reshape(b * t, d)
        proj = _in_proj(x2, g_mix[l][None, :], w_n)
        proj3 = proj.reshape(b, t, N_COLS)
        qv, gt = _in_proj_t(h, g_mix[l][None, :], w_t)
        y_r = _hgrn(proj3, lbs[l][None, :], g_rnn_out[l][None, :])
        pe_kv, w_c1 = _compress_weights(pe_ck[l], pe_cv[l], w_ck1[l], w_cv1[l])
        kc, vct = _compress(proj3, pe_kv, w_c1, w_ck2[l].astype(BF16), w_cv2[l].T.astype(BF16))
        ocmp, selb, used = _cmp_select(qv, kc, vct, slopes, overlap_t, t)
        n_extra, extra = _extra_chunks(used, t)
        y_a = _sparse(n_extra, extra, qv, gt, proj3, ocmp, selb, slopes, t)
        h1 = _merge(x2, y_r.reshape(b * t, d), y_a.reshape(b * t, d), proj,
                    w_proj_rnn[l].astype(BF16), w_proj_att[l].astype(BF16), w_out[l].astype(BF16))
        w_kv = w_xkv[l].astype(BF16)
        kt, v = _mem_kv(mem, g_mem[l][None, :], w_kv[:, :H_X * X_HD].T, w_kv)
        h2 = _xattn(h1.reshape(b, t, d), g_xattn[l][None, :], w_xq[l].astype(BF16), kt, v,
                    w_xo[l].astype(BF16))
        h = _ffn(h2.reshape(b * t, d), g_ffn[l][None, :], w_gate_up[l].astype(BF16),
                 w_down[l].astype(BF16), g_final[None, :]).reshape(b, t, d)
    return h
```

```python
import functools
import itertools

import jax
import jax.numpy as jnp
import numpy as np
from jax import lax
from jax.experimental import pallas as pl
from jax.experimental.pallas import tpu as pltpu

F32 = jnp.float32
BF16 = jnp.bfloat16

D_MODEL = 1024
N_MEM = 256
H_RNN = 8
RNN_DIM = 128
RNN_CHUNK = 64
H_ATT = 16
ATT_HD = 64
G_KV = 4
HG = H_ATT // G_KV
CMP_LEN = 32
CMP_STRIDE = 16
CMP_HIDDEN = 128
SLC_LEN = 64
SLC_TOPK = 8
WINDOW = 512
FORCE_BONUS = 1.0e4
H_X = 4
X_HD = 128
D_FF = 2816
EPS = 1e-6

LANE = 128
MXU_TILE = 256
VMEM_LIMIT = 56 * 1024 * 1024
TOK_TILE = 512
XATTN_TILE = 1024
IN_PROJ_TILE = 1024
ATT_TILE = 128
CMP_TILE = 512
N_CMP_PAD = 128
HGRN_GROUP = 4
ONES_ROWS = 16
NEG_BIG = -1.0e30
LOG2E = 1.4426950408889634
AUG_SEL_ROW0 = 8
SEL_NEAR = 5

QT_ROWS = H_ATT * ATT_HD
VT_ROWS = 2 * G_KV * ATT_HD
GATE_ROWS = 64
F_ROWS = QT_ROWS + VT_ROWS + GATE_ROWS
RNN_COLS = 4 * H_RNN * RNN_DIM
KV_COLS = 4 * G_KV * ATT_HD
MG_COLS = 2 * D_MODEL
N_COLS = RNN_COLS + KV_COLS + MG_COLS


def _cparams(sem):
    return pltpu.CompilerParams(dimension_semantics=sem, vmem_limit_bytes=VMEM_LIMIT)


def _rms(xf, g):
    return xf * lax.rsqrt(jnp.mean(xf * xf, axis=-1, keepdims=True) + EPS) * g


def _sigmoid(x):
    return 1.0 / (1.0 + jnp.exp2(x * -LOG2E))


def _silu(x):
    return x * _sigmoid(x)


def _dot(a, b):
    return jnp.dot(a, b, preferred_element_type=F32)


def _dot_nt(a, b):
    return lax.dot_general(a, b, (((1,), (1,)), ((), ())), preferred_element_type=F32)


def _dot_tn(a, b):
    return lax.dot_general(a, b, (((0,), (0,)), ((), ())), preferred_element_type=F32)


def _w_prep_kernel(w_ref, wn_ref, wf_ref):
    kvw = G_KV * ATT_HD
    kv0 = RNN_COLS + QT_ROWS
    gate0 = kv0 + 6 * kvw
    mg0 = gate0 + 3 * H_ATT

    def pair_by_group(a0, b0):
        parts = []
        for g in range(G_KV):
            parts += [w_ref[a0 + g * ATT_HD:a0 + (g + 1) * ATT_HD, :], w_ref[b0 + g * ATT_HD:b0 + (g + 1) * ATT_HD, :]]
        return parts

    wn = jnp.concatenate([w_ref[0:RNN_COLS, :]] + pair_by_group(kv0, kv0 + kvw)
                         + pair_by_group(kv0 + 2 * kvw, kv0 + 4 * kvw) + [w_ref[mg0:mg0 + MG_COLS, :]], axis=0)
    wn_ref[...] = wn.T.astype(BF16)
    used = QT_ROWS + VT_ROWS + 3 * H_ATT
    wf = jnp.concatenate([w_ref[RNN_COLS:kv0, :] * (ATT_HD ** -0.5 * LOG2E),
                          w_ref[kv0 + 3 * kvw:kv0 + 4 * kvw, :], w_ref[kv0 + 5 * kvw:kv0 + 6 * kvw, :],
                          w_ref[gate0:mg0, :], jnp.zeros((wf_ref.shape[0] - used, w_ref.shape[1]), F32)], axis=0)
    wf_ref[...] = wf.astype(BF16)


def _w_prep(wt):
    n_in, d = wt.shape
    return pl.pallas_call(
        _w_prep_kernel,
        out_shape=(jax.ShapeDtypeStruct((d, N_COLS), BF16), jax.ShapeDtypeStruct((F_ROWS, d), BF16)),
        grid=(d // LANE,),
        in_specs=[pl.BlockSpec((n_in, LANE), lambda i: (0, i))],
        out_specs=(pl.BlockSpec((LANE, N_COLS), lambda i: (i, 0)),
                   pl.BlockSpec((F_ROWS, LANE), lambda i: (0, i))),
        compiler_params=_cparams(("arbitrary",)),
        name="w_prep",
    )(wt)


def _in_proj_kernel(x_ref, g_ref, w_ref, o_ref, *, sub):
    a = _rms(x_ref[...], g_ref[...]).astype(BF16)
    for n in range(o_ref.shape[1] // sub):
        cols = slice(n * sub, (n + 1) * sub)
        o_ref[:, cols] = _dot(a, w_ref[:, cols]).astype(o_ref.dtype)


def _in_proj(x2, g, w):
    m, d = x2.shape
    n = w.shape[1]
    tn = n // 2
    return pl.pallas_call(
        functools.partial(_in_proj_kernel, sub=512),
        out_shape=jax.ShapeDtypeStruct((m, n), BF16),
        grid=(2, m // IN_PROJ_TILE),
        in_specs=[pl.BlockSpec((IN_PROJ_TILE, d), lambda j, i: (i, 0)),
                  pl.BlockSpec((1, d), lambda j, i: (0, 0)),
                  pl.BlockSpec((d, tn), lambda j, i: (0, j))],
        out_specs=pl.BlockSpec((IN_PROJ_TILE, tn), lambda j, i: (i, j)),
        compiler_params=_cparams(("arbitrary", "arbitrary")),
        name="in_proj",
    )(x2, g, w)


def _in_proj_t_kernel(x_ref, g_ref, wt_ref, qv_ref, gt_ref):
    a = _rms(x_ref[...], g_ref[...]).astype(BF16)
    r = _dot_nt(wt_ref[...], a)
    nqv = qv_ref.shape[1]
    ngt = gt_ref.shape[1]
    for c in range(qv_ref.shape[0]):
        qv_ref[c] = r[:nqv, c * LANE:(c + 1) * LANE].astype(qv_ref.dtype)
        gt_ref[c] = r[nqv:nqv + ngt, c * LANE:(c + 1) * LANE]


def _in_proj_t(x, g, wt):
    b, t, d = x.shape
    rows = wt.shape[0]
    nqv = QT_ROWS + VT_ROWS
    ngt = GATE_ROWS
    nc = TOK_TILE // LANE
    return pl.pallas_call(
        _in_proj_t_kernel,
        out_shape=(jax.ShapeDtypeStruct((b, t // LANE, nqv, LANE), BF16),
                   jax.ShapeDtypeStruct((b, t // LANE, ngt, LANE), F32)),
        grid=(b, t // TOK_TILE),
        in_specs=[pl.BlockSpec((None, TOK_TILE, d), lambda i, j: (i, j, 0)),
                  pl.BlockSpec((1, d), lambda i, j: (0, 0)),
                  pl.BlockSpec((rows, d), lambda i, j: (0, 0))],
        out_specs=(pl.BlockSpec((None, nc, nqv, LANE), lambda i, j: (i, j, 0, 0)),
                   pl.BlockSpec((None, nc, ngt, LANE), lambda i, j: (i, j, 0, 0))),
        compiler_params=_cparams(("arbitrary", "arbitrary")),
        name="in_proj_t",
    )(x, g, wt)


def _hgrn_kernel(q_ref, f_ref, i_ref, og_ref, lb_ref, gn_ref, o_ref, qd_ref, oi_ref, ut_ref, dec_ref):
    c = RNN_CHUNK
    kd = RNN_DIM
    n_chunks = q_ref.shape[0] // c
    lb = lb_ref[...]
    gn = gn_ref[...]
    blk = HGRN_GROUP * c
    row = lax.broadcasted_iota(jnp.int32, (blk, blk), 0)
    col = lax.broadcasted_iota(jnp.int32, (blk, blk), 1)
    same_chunk = (row // c) == (col // c)
    causal = same_chunk & (row >= col)
    tril = causal.astype(BF16)

    q = q_ref[...].astype(F32)
    fl = f_ref[...].astype(F32)
    v = i_ref[...]
    f = lb + (1.0 - lb) * _sigmoid(fl)
    k = 1.0 - f
    logf = jnp.log(f)
    hi = logf.astype(BF16)
    lo = (logf - hi.astype(F32)).astype(BF16)
    pieces = jnp.concatenate([hi, lo], axis=1)
    n_blk = q_ref.shape[0] // blk
    cs = [_dot(tril, pieces[g * blk:(g + 1) * blk]) for g in range(n_blk)]
    bcum = jnp.concatenate([x[:, 0:kd] + x[:, kd:2 * kd] for x in cs], axis=0)
    e_neg = jnp.exp2(bcum * -LOG2E)
    dec = jnp.exp(jnp.concatenate([bcum[n * c + c - 1:n * c + c, :] for n in range(n_chunks)], axis=0))
    dec_rows = jnp.concatenate([jnp.broadcast_to(dec[n:n + 1, :], (c, kd)) for n in range(n_chunks)], axis=0)
    q_dec = (_silu(q) * (1.0 / e_neg)).astype(BF16)
    k_neg = k * e_neg
    k_dec = k_neg.astype(BF16)
    k_end = (k_neg * dec_rows).astype(BF16)
    qd_ref[...] = q_dec
    dec_ref[...] = dec
    for g in range(n_blk):
        rows = slice(g * blk, (g + 1) * blk)
        a = jnp.where(causal, _dot_nt(q_dec[rows], k_dec[rows]), 0.0)
        oi_ref[rows, :] = _dot(a.astype(BF16), v[rows])
    for n in range(n_chunks):
        rows = slice(n * c, (n + 1) * c)
        ut_ref[n] = _dot_tn(v[rows], k_end[rows])

    s_t = jnp.zeros((kd, kd), F32)
    for n in range(n_chunks):
        rows = slice(n * c, (n + 1) * c)
        o = oi_ref[rows, :] + _dot_nt(qd_ref[rows, :], s_t.astype(BF16))
        og = og_ref[rows, :].astype(F32)
        o_ref[rows, :] = (_rms(o, gn) * _silu(og)).astype(o_ref.dtype)
        s_t = s_t * dec_ref[n:n + 1, :] + ut_ref[n]


def _hgrn(proj3, lb, gn):
    b, t, _ = proj3.shape
    return pl.pallas_call(
        _hgrn_kernel,
        out_shape=jax.ShapeDtypeStruct((b, t, H_RNN * RNN_DIM), BF16),
        grid=(b, H_RNN),
        in_specs=[pl.BlockSpec((None, t, RNN_DIM), lambda i, h: (i, 0, h)),
                  pl.BlockSpec((None, t, RNN_DIM), lambda i, h: (i, 0, H_RNN + h)),
                  pl.BlockSpec((None, t, RNN_DIM), lambda i, h: (i, 0, 2 * H_RNN + h)),
                  pl.BlockSpec((None, t, RNN_DIM), lambda i, h: (i, 0, 3 * H_RNN + h)),
                  pl.BlockSpec((1, RNN_DIM), lambda i, h: (0, h)),
                  pl.BlockSpec((1, RNN_DIM), lambda i, h: (0, 0))],
        out_specs=pl.BlockSpec((None, t, RNN_DIM), lambda i, h: (i, 0, h)),
        scratch_shapes=[pltpu.VMEM((t, RNN_DIM), BF16),
                        pltpu.VMEM((t, RNN_DIM), F32),
                        pltpu.VMEM((t // RNN_CHUNK, RNN_DIM, RNN_DIM), F32),
                        pltpu.VMEM((t // RNN_CHUNK, RNN_DIM), F32)],
        compiler_params=_cparams(("arbitrary", "arbitrary")),
        name="hgrn",
    )(proj3, proj3, proj3, proj3, lb, gn)


def _compress_kernel(kv_ref, pe_ref, w1_ref, wk2_ref, wv2t_ref, kc_ref, vct_ref, xs_ref, x_ref):
    t = kv_ref.shape[0]
    xs_ref[0:t, :] = kv_ref[...].astype(F32)
    xs_ref[t:, :] = jnp.zeros((xs_ref.shape[0] - t, xs_ref.shape[1]), F32)
    for l in range(CMP_LEN):
        blk = xs_ref[pl.ds(l, N_CMP_PAD, stride=CMP_STRIDE), :]
        x_ref[:, l * LANE:(l + 1) * LANE] = (blk + pe_ref[l:l + 1, :]).astype(BF16)
    hid = _silu(_dot(x_ref[...], w1_ref[...])).astype(BF16)
    kc_ref[...] = _dot(hid[:, :CMP_HIDDEN], wk2_ref[...])
    vct_ref[...] = _dot_nt(wv2t_ref[...], hid[:, CMP_HIDDEN:])


def _compress(proj3, pe, w1, wk2, wv2t):
    b, t, _ = proj3.shape
    kv_unit0 = RNN_COLS // LANE
    full = lambda shape: pl.BlockSpec(shape, lambda i, g: (0,) * len(shape))
    return pl.pallas_call(
        _compress_kernel,
        out_shape=(jax.ShapeDtypeStruct((b, G_KV, N_CMP_PAD, ATT_HD), F32),
                   jax.ShapeDtypeStruct((b, G_KV, ATT_HD, N_CMP_PAD), F32)),
        grid=(b, G_KV),
        in_specs=[pl.BlockSpec((None, t, LANE), lambda i, g: (i, 0, kv_unit0 + g)),
                  full((CMP_LEN, LANE)), full((CMP_LEN * LANE, 2 * CMP_HIDDEN)),
                  full((CMP_HIDDEN, ATT_HD)), full((ATT_HD, CMP_HIDDEN))],
        out_specs=(pl.BlockSpec((None, None, N_CMP_PAD, ATT_HD), lambda i, g: (i, g, 0, 0)),
                   pl.BlockSpec((None, None, ATT_HD, N_CMP_PAD), lambda i, g: (i, g, 0, 0))),
        scratch_shapes=[pltpu.VMEM((t + CMP_STRIDE, LANE), F32),
                        pltpu.VMEM((N_CMP_PAD, CMP_LEN * LANE), BF16)],
        compiler_params=_cparams(("arbitrary", "arbitrary")),
        name="compress",
    )(proj3, pe, w1, wk2, wv2t)


def _compress_weights(pek, pev, wk1, wv1):
    zk = jnp.zeros((CMP_LEN, ATT_HD, CMP_HIDDEN), wk1.dtype)
    top = jnp.concatenate([wk1.reshape(CMP_LEN, ATT_HD, CMP_HIDDEN), zk], axis=2)
    bot = jnp.concatenate([zk, wv1.reshape(CMP_LEN, ATT_HD, CMP_HIDDEN)], axis=2)
    w1 = jnp.concatenate([top, bot], axis=1).reshape(CMP_LEN * LANE, 2 * CMP_HIDDEN)
    return jnp.concatenate([pek, pev], axis=1), w1.astype(BF16)


def _cmp_select_kernel(qt_ref, kc_ref, vct_ref, slope_ref, ov_ref, ocmp_ref, selb_ref, used_ref, *, n_cmp, n_sel):
    nct = CMP_TILE // LANE
    tt = CMP_TILE
    ns = ov_ref.shape[0]
    kc = kc_ref[...].astype(BF16)
    vct = vct_ref[...].astype(BF16)
    ov = ov_ref[...]
    j_i = lax.broadcasted_iota(jnp.int32, (8, tt), 0)
    tq_rel = lax.broadcasted_iota(jnp.int32, (8, tt), 1)
    n_tiles = qt_ref.shape[0] // nct

    def ranges(tile):
        t0 = tile * tt
        n_any = min(N_CMP_PAD, (t0 + tt) // CMP_STRIDE)
        n_all = max(0, (t0 - (CMP_LEN - 1)) // CMP_STRIDE + 1) // 8 * 8
        return t0, n_any, n_all, (t0 + tt) // SLC_LEN

    def pad_rows(x):
        if x.shape[0] == N_CMP_PAD:
            return x
        return jnp.concatenate([x, jnp.zeros((N_CMP_PAD - x.shape[0], x.shape[1]), x.dtype)], axis=0)

    scores = []
    for tile in range(n_tiles):
        _, n_any, _, _ = ranges(tile)
        qt = [jnp.concatenate([qt_ref[tile * nct + c, h * ATT_HD:(h + 1) * ATT_HD, :] for c in range(nct)],
                              axis=1) for h in range(HG)]
        scores.append(_dot(kc[:n_any], jnp.concatenate(qt, axis=1)))

    probs, psums = [], []
    for tile in range(n_tiles):
        t0, n_any, n_all, _ = ranges(tile)
        n_b = lax.broadcasted_iota(jnp.int32, (n_any - n_all, tt), 0) + n_all
        t_b = lax.broadcasted_iota(jnp.int32, (n_any - n_all, tt), 1)
        visible = (CMP_STRIDE * n_b + (CMP_LEN - 1) - t0 <= t_b) & (n_b < n_cmp)
        centre = (lax.broadcasted_iota(jnp.int32, (n_any, tt), 0).astype(F32) * float(CMP_STRIDE)
                  + (CMP_LEN - 1) / 2.0)
        psum = jnp.zeros((n_any, tt), F32)
        ph = []
        for h in range(HG):
            s = scores[tile][:, h * tt:(h + 1) * tt] + slope_ref[h] * LOG2E * centre
            band = jnp.where(visible, s[n_all:], -jnp.inf)
            s = band if n_all == 0 else jnp.concatenate([s[:n_all], band], axis=0)
            m = jnp.max(s, axis=0, keepdims=True)
            m = jnp.where(m == -jnp.inf, 0.0, m)
            p = jnp.exp2(s - m)
            d = jnp.sum(p, axis=0, keepdims=True)
            p = p * (1.0 / jnp.where(d > 0, d, 1.0))
            psum = psum + p
            ph.append(pad_rows(p).astype(BF16))
        probs.append(jnp.concatenate(ph, axis=1))
        psums.append(pad_rows(psum))

    for tile in range(n_tiles):
        o = _dot(vct, probs[tile])
        for h in range(HG):
            for c in range(nct):
                lanes = slice(h * tt + c * LANE, h * tt + (c + 1) * LANE)
                ocmp_ref[tile * nct + c, h * ATT_HD:(h + 1) * ATT_HD, :] = o[:, lanes]

    imps = []
    for tile in range(n_tiles):
        psum = psums[tile]
        hi = psum.astype(BF16)
        rem = psum - hi.astype(F32)
        mid = rem.astype(BF16)
        lo = (rem - mid.astype(F32)).astype(BF16)
        imp3 = _dot(ov, jnp.concatenate([hi, mid, lo], axis=1))
        imps.append(imp3[:, 0:tt] + imp3[:, tt:2 * tt] + imp3[:, 2 * tt:3 * tt])

    for tile in range(n_tiles):
        t0, _, _, nvb = ranges(tile)
        imp = imps[tile]
        score = []
        for rg in range(nvb // 8):
            j = j_i + 8 * rg
            tq = tq_rel + t0
            cur = tq // SLC_LEN
            forced = (j == 0) | (j == cur) | (j == cur - 1)
            sc = imp[8 * rg:8 * rg + 8, :] + jnp.where(forced, FORCE_BONUS, 0.0)
            score.append(jnp.where(j * SLC_LEN <= tq, sc, -jnp.inf))
        rank = [jnp.zeros((8, tt), F32) for _ in score]
        for jp in range(nvb):
            sj = jnp.broadcast_to(score[jp // 8][jp % 8:jp % 8 + 1, :], (8, tt))
            for rg in range(nvb // 8):
                ge = jnp.where(sj >= score[rg], 1.0, 0.0)
                gt = jnp.where(sj > score[rg], 1.0, 0.0)
                if 8 * rg > jp:
                    ahead = ge
                elif 8 * rg + 7 <= jp:
                    ahead = gt
                else:
                    ahead = jnp.where(j_i + 8 * rg > jp, ge, gt)
                rank[rg] = rank[rg] + ahead
        for rg in range(ns // 8):
            for c in range(nct):
                if rg < nvb // 8:
                    picked = rank[rg][:, c * LANE:(c + 1) * LANE] < float(n_sel)
                    selb = jnp.where(picked, 0.0, NEG_BIG)
                    used = jnp.max(jnp.where(picked, 1.0, 0.0), axis=1, keepdims=True)
                else:
                    selb = jnp.full((8, LANE), NEG_BIG, F32)
                    used = jnp.zeros((8, 1), F32)
                selb_ref[tile * nct + c, 8 * rg:8 * rg + 8, :] = selb
                used_ref[tile * nct + c, 8 * rg:8 * rg + 8, :] = jnp.broadcast_to(used, (8, LANE))


def _cmp_select(qv, kc, vct, slopes, overlap_t, t):
    b = qv.shape[0]
    ns = t // SLC_LEN
    nch = t // LANE
    n_cmp = (t - CMP_LEN) // CMP_STRIDE + 1
    n_sel = min(SLC_TOPK, ns)
    rows = HG * ATT_HD
    return pl.pallas_call(
        functools.partial(_cmp_select_kernel, n_cmp=n_cmp, n_sel=n_sel),
        out_shape=(jax.ShapeDtypeStruct((b, nch, H_ATT * ATT_HD, LANE), F32),
                   jax.ShapeDtypeStruct((b, G_KV, nch, ns, LANE), F32),
                   jax.ShapeDtypeStruct((b, G_KV, nch, ns, LANE), F32)),
        grid=(b, G_KV),
        in_specs=[pl.BlockSpec((None, nch, rows, LANE), lambda i, g: (i, 0, g, 0)),
                  pl.BlockSpec((None, None, N_CMP_PAD, ATT_HD), lambda i, g: (i, g, 0, 0)),
                  pl.BlockSpec((None, None, ATT_HD, N_CMP_PAD), lambda i, g: (i, g, 0, 0)),
                  pl.BlockSpec((None, HG, 1, 1), lambda i, g: (g, 0, 0, 0)),
                  pl.BlockSpec((ns, N_CMP_PAD), lambda i, g: (0, 0))],
        out_specs=(pl.BlockSpec((None, nch, rows, LANE), lambda i, g: (i, 0, g, 0)),
                   pl.BlockSpec((None, None, nch, ns, LANE), lambda i, g: (i, g, 0, 0, 0)),
                   pl.BlockSpec((None, None, nch, ns, LANE), lambda i, g: (i, g, 0, 0, 0))),
        compiler_params=_cparams(("arbitrary", "arbitrary")),
        name="cmp_select",
    )(qv, kc, vct, slopes, overlap_t)


def _sparse_kernel(n_extra_ref, extra_ref, qt_ref, k_ref, ka_ref, vs_ref, vw_ref, gt_ref, ocmp_ref, selb_ref,
                   slope_ref, o_ref, msk_ref, qop_ref, s_ref, m_ref, acc_ref):
    tq = ATT_TILE
    kc = ATT_TILE
    hd = ATT_HD
    nl = HG * tq
    n_tiles = qt_ref.shape[0]
    ns = selb_ref.shape[1]
    wch = WINDOW // kc
    n_sel_items = 2 * (SEL_NEAR + 1)
    tile_base = (pl.program_id(0) * pl.num_programs(1) + pl.program_id(1)) * n_tiles
    slope = jnp.concatenate([jnp.broadcast_to(slope_ref[h], (1, tq)) for h in range(HG)], axis=1) * LOG2E
    s_hi = slope.astype(BF16).astype(F32)
    s_mid = (slope - s_hi).astype(BF16).astype(F32)
    s_lo = (slope - s_hi - s_mid).astype(BF16).astype(F32)
    slope_rows = jnp.concatenate([s_hi, s_mid, s_lo, jnp.zeros((AUG_SEL_ROW0 - 3, nl), F32)], axis=0)
    aug_tail = jnp.zeros((LANE - AUG_SEL_ROW0 - ns, nl), F32)
    aug_win = jnp.concatenate([slope_rows, jnp.zeros((ns, nl), F32), aug_tail], axis=0).astype(BF16)
    k_i = lax.broadcasted_iota(jnp.int32, (kc, nl), 0)
    t_i = lax.broadcasted_iota(jnp.int32, (kc, nl), 1) % tq
    d0 = t_i - k_i
    msk_ref[0] = jnp.where(d0 >= 0, 0.0, NEG_BIG)
    msk_ref[1] = jnp.where(d0 < 0, 0.0, NEG_BIG)
    chunk_bias = slope * float(kc)
    ones = jnp.ones((ONES_ROWS, kc), BF16)

    n_pairs = n_tiles // 2
    n_items = n_sel_items + 2 * (wch + 1)
    n_slots = s_ref.shape[0] // 2

    def tiles_of(p):
        return (p, n_tiles - 1 - p)

    def sel_item(p, w, j):
        tile = tiles_of(p)[w]
        if j == SEL_NEAR:
            return 0, tile >= SEL_NEAR, tile
        return max(tile - j, 0), tile - j >= 0, j

    def win_item(p, w, j):
        chunk = tiles_of(p)[w] - wch + j
        return max(chunk, 0), chunk >= 0, wch - j

    def col_max(r, delta_f):
        r8 = jnp.max(r.reshape(kc // 8, 8, nl), axis=0)
        return r8 - chunk_bias * delta_f

    def scores(chunk, op):
        rows = pl.ds(pl.multiple_of(chunk * kc, kc), kc)
        keys = jnp.concatenate([k_ref[rows, :], ka_ref[rows, :]], axis=1)
        return _dot(keys, qop_ref[op])

    def build(p, slot):
        for w, tile in enumerate(tiles_of(p)):
            qt = qt_ref[tile]
            q_all = jnp.concatenate([qt[h * hd:(h + 1) * hd, :] for h in range(HG)], axis=1)
            zero = jnp.zeros_like(q_all)
            selb = jnp.concatenate([selb_ref[tile]] * HG, axis=1)
            aug_sel = jnp.concatenate([slope_rows, selb, aug_tail], axis=0).astype(BF16)
            qop_ref[4 * slot + 2 * w] = jnp.concatenate([q_all, zero, aug_sel], axis=0)
            qop_ref[4 * slot + 2 * w + 1] = jnp.concatenate([zero, q_all, aug_win], axis=0)
        for x in range(4):
            m_ref[4 * slot + x] = jnp.full((8, nl), NEG_BIG, F32)

    def phase_a(p, slot):
        items = []

        def sel(w, j):
            chunk, _, delta = sel_item(p, w, j)
            r = scores(chunk, 4 * slot + 2 * w)
            if j == 0:
                r = r + msk_ref[0]
            s_ref[slot * n_slots + w * (SEL_NEAR + 1) + j] = r
            x = 4 * slot + 2 * w
            m_ref[x] = jnp.maximum(m_ref[x], col_max(r, float(delta)))

        def win(w, j):
            chunk, _, delta = win_item(p, w, j)
            r = scores(chunk, 4 * slot + 2 * w + 1)
            if j == 0:
                r = r + msk_ref[1]
            elif j == wch:
                r = r + msk_ref[0]
            s_ref[slot * n_slots + n_sel_items + w * (wch + 1) + j] = r
            x = 4 * slot + 2 * w + 1
            m_ref[x] = jnp.maximum(m_ref[x], col_max(r, float(delta)))

        for w in range(2):
            for j in range(SEL_NEAR + 1):
                if sel_item(p, w, j)[1]:
                    items.append(functools.partial(sel, w, j))
        for w in range(2):
            for j in range(wch + 1):
                if win_item(p, w, j)[1]:
                    items.append(functools.partial(win, w, j))
        return items

    def overflow(p, slot, visit):
        count = 0
        for w, tile in enumerate(tiles_of(p)):
            def body(e, n, w=w, tile=tile):
                visit(w, tile, extra_ref[(tile_base + tile) * n_tiles + e], n)
                return n + 1

            count = lax.fori_loop(0, n_extra_ref[tile_base + tile], body, count)

    def overflow_a(p, slot):
        def visit(w, tile, c, n):
            r = scores(c, 4 * slot + 2 * w)
            s_ref[slot * n_slots + n_items + n] = r
            x = 4 * slot + 2 * w
            m_ref[x] = jnp.maximum(m_ref[x], col_max(r, jnp.float32(tile - c)))

        overflow(p, slot, visit)

    def overflow_b(p, slot, m_row):
        def visit(w, tile, c, n):
            row = m_row[2 * w] + chunk_bias * jnp.float32(tile - c)
            pr = jnp.exp2(s_ref[slot * n_slots + n_items + n] - row).astype(BF16)
            v_aug = jnp.concatenate([vs_ref[c], ones], axis=0)
            acc_ref[4 * slot + 2 * w] += _dot(v_aug, pr)

        overflow(p, slot, visit)

    def phase_b(p, slot, m_row):
        for x in range(4):
            acc_ref[4 * slot + x] = jnp.zeros(acc_ref.shape[1:], F32)
        items = []

        def pv(x, v_ref, group):
            prs, vs = [], []
            for s_idx, chunk, delta in group:
                row = m_row[x] + chunk_bias * float(delta)
                prs.append(jnp.exp2(s_ref[slot * n_slots + s_idx] - row).astype(BF16))
                vs.append(jnp.concatenate([v_ref[chunk], ones], axis=0))
            acc_ref[4 * slot + x] += _dot(jnp.concatenate(vs, axis=1), jnp.concatenate(prs, axis=0))

        for w in range(2):
            sel = [(w * (SEL_NEAR + 1) + j,) + sel_item(p, w, j)[::2]
                   for j in range(SEL_NEAR + 1) if sel_item(p, w, j)[1]]
            win = [(n_sel_items + w * (wch + 1) + j,) + win_item(p, w, j)[::2]
                   for j in range(wch + 1) if win_item(p, w, j)[1]]
            for x, v_ref, group in ((2 * w, vs_ref, sel), (2 * w + 1, vw_ref, win)):
                for i in range(0, len(group), 2):
                    items.append(functools.partial(pv, x, v_ref, group[i:i + 2]))
        return items

    def finalize(p, slot):
        for w, tile in enumerate(tiles_of(p)):
            a_s = acc_ref[4 * slot + 2 * w]
            a_w = acc_ref[4 * slot + 2 * w + 1]
            o_slc = a_s[:hd, :] / a_s[hd:hd + 1, :]
            o_win = a_w[:hd, :] / a_w[hd:hd + 1, :]
            oc = ocmp_ref[tile]
            outs = []
            for h in range(HG):
                lanes = slice(h * tq, (h + 1) * tq)
                row0 = (pl.program_id(1) * HG + h) * 3
                g_cmp, g_slc, g_win = [_sigmoid(gt_ref[tile, pl.ds(row0 + br, 1), :]) for br in range(3)]
                y = (g_cmp * oc[h * hd:(h + 1) * hd, :] + g_slc * o_slc[:, lanes]
                     + g_win * o_win[:, lanes])
                outs.append(y.T)
            rows = pl.ds(pl.multiple_of(tile * tq, tq), tq)
            o_ref[rows, :] = jnp.concatenate(outs, axis=1).astype(o_ref.dtype)

    def step(k, slot, first=False, last=False):
        m_row = [jnp.max(m_ref[4 * slot + x], axis=0, keepdims=True) for x in range(4)]
        if not last:
            build(k + 1, 1 - slot)
        if not first:
            finalize(k - 1, 1 - slot)
        b_items = phase_b(k, slot, m_row)
        a_items = phase_a(k + 1, 1 - slot) if not last else []
        for a, b in itertools.zip_longest(a_items, b_items):
            if a is not None:
                a()
            if b is not None:
                b()
        if not last:
            overflow_a(k + 1, 1 - slot)
        overflow_b(k, slot, m_row)

    build(0, 0)
    for a in phase_a(0, 0):
        a()
    overflow_a(0, 0)
    for k in range(n_pairs):
        step(k, k % 2, first=k == 0, last=k == n_pairs - 1)
    finalize(n_pairs - 1, (n_pairs - 1) % 2)


def _key_features(t):
    kp = np.arange(t)
    f = np.zeros((t, LANE), np.float32)
    f[:, 0:3] = (kp % ATT_TILE)[:, None]
    f[kp, AUG_SEL_ROW0 + kp // SLC_LEN] = 1.0
    return jnp.asarray(f, dtype=BF16)


def _extra_chunks(used, t):
    b = used.shape[0]
    nt = t // ATT_TILE
    need = used.max(axis=-1).reshape(b, G_KV, nt, nt, ATT_TILE // SLC_LEN).max(axis=-1) > 0.5
    chunk = jnp.arange(nt, dtype=jnp.int32)
    extra = need & (chunk >= 1) & (chunk[None, :] <= chunk[:, None] - SEL_NEAR)
    place = jnp.cumsum(extra.astype(jnp.int32), axis=-1) - 1
    hit = extra[..., :, None] & (place[..., :, None] == chunk)
    lst = jnp.sum(jnp.where(hit, chunk[:, None], 0), axis=-2)
    return extra.sum(axis=-1, dtype=jnp.int32).reshape(-1), lst.reshape(-1)


def _sparse(n_extra, extra, qv, gt, proj3, ocmp, selb, slopes, t):
    b = qv.shape[0]
    nchunk = t // LANE
    ns = t // SLC_LEN
    rows = HG * ATT_HD
    ksw_unit0 = RNN_COLS // LANE + G_KV
    vs_blk0 = QT_ROWS // ATT_HD
    vw_blk0 = vs_blk0 + G_KV
    nl = HG * ATT_TILE
    n_slots = 2 * (SEL_NEAR + 1) + 2 * (WINDOW // ATT_TILE + 1) + (nchunk - 1 - SEL_NEAR)
    grid_spec = pltpu.PrefetchScalarGridSpec(
        num_scalar_prefetch=2,
        grid=(b, G_KV),
        in_specs=[pl.BlockSpec((None, nchunk, rows, LANE), lambda i, g, *_: (i, 0, g, 0)),
                  pl.BlockSpec((None, t, LANE), lambda i, g, *_: (i, 0, ksw_unit0 + g)),
                  pl.BlockSpec((t, LANE), lambda i, g, *_: (0, 0)),
                  pl.BlockSpec((None, nchunk, ATT_HD, LANE), lambda i, g, *_: (i, 0, vs_blk0 + g, 0)),
                  pl.BlockSpec((None, nchunk, ATT_HD, LANE), lambda i, g, *_: (i, 0, vw_blk0 + g, 0)),
                  pl.BlockSpec((None, nchunk, GATE_ROWS, LANE), lambda i, g, *_: (i, 0, 0, 0)),
                  pl.BlockSpec((None, nchunk, rows, LANE), lambda i, g, *_: (i, 0, g, 0)),
                  pl.BlockSpec((None, None, nchunk, ns, LANE), lambda i, g, *_: (i, g, 0, 0, 0)),
                  pl.BlockSpec((None, HG, 1, 1), lambda i, g, *_: (g, 0, 0, 0))],
        out_specs=pl.BlockSpec((None, t, rows), lambda i, g, *_: (i, 0, g)),
        scratch_shapes=[pltpu.VMEM((2, ATT_TILE, nl), F32),
                        pltpu.VMEM((2 * 4, 2 * LANE, nl), BF16),
                        pltpu.VMEM((2 * n_slots, ATT_TILE, nl), F32),
                        pltpu.VMEM((2 * 4, 8, nl), F32),
                        pltpu.VMEM((2 * 4, ATT_HD + ONES_ROWS, nl), F32)])
    return pl.pallas_call(
        _sparse_kernel,
        out_shape=jax.ShapeDtypeStruct((b, t, H_ATT * ATT_HD), BF16),
        grid_spec=grid_spec,
        compiler_params=_cparams(("arbitrary", "arbitrary")),
        name="sparse",
    )(n_extra, extra, qv, proj3, _key_features(t), qv, qv, gt, ocmp, selb, slopes)


def _merge_kernel(x_ref, yr_ref, ya_ref, mgr_ref, mga_ref, wr_ref, wa_ref, wo_ref, o_ref):
    pr = _dot(yr_ref[...], wr_ref[...])
    pa = _dot(ya_ref[...], wa_ref[...])
    merged = _sigmoid(mgr_ref[...].astype(F32)) * pr + _sigmoid(mga_ref[...].astype(F32)) * pa
    o_ref[...] = x_ref[...] + _dot(merged.astype(BF16), wo_ref[...])


def _merge(x2, yr, ya, proj, wr, wa, wo):
    m, d = x2.shape
    mg0 = (RNN_COLS + KV_COLS) // d
    tile = lambda col: pl.BlockSpec((TOK_TILE, d), lambda i: (i, col))
    wfull = pl.BlockSpec((d, d), lambda i: (0, 0))
    return pl.pallas_call(
        _merge_kernel,
        out_shape=jax.ShapeDtypeStruct((m, d), F32),
        grid=(m // TOK_TILE,),
        in_specs=[tile(0), tile(0), tile(0), tile(mg0), tile(mg0 + 1), wfull, wfull, wfull],
        out_specs=tile(0),
        compiler_params=_cparams(("arbitrary",)),
        name="merge",
    )(x2, yr, ya, proj, proj, wr, wa, wo)


def _mem_kv_kernel(mem_ref, g_ref, wkt_ref, wv_ref, kt_ref, v_ref):
    a = _rms(mem_ref[...], g_ref[...]).astype(BF16)
    kt_ref[...] = _dot_nt(wkt_ref[...], a).astype(kt_ref.dtype)
    v_ref[...] = _dot(a, wv_ref[...]).astype(v_ref.dtype)


def _mem_kv(mem, g, wkt, wkv):
    b, nm, d = mem.shape
    hw = H_X * X_HD
    return pl.pallas_call(
        _mem_kv_kernel,
        out_shape=(jax.ShapeDtypeStruct((b, hw, nm), BF16),
                   jax.ShapeDtypeStruct((b, nm, hw), BF16)),
        grid=(b,),
        in_specs=[pl.BlockSpec((None, nm, d), lambda i: (i, 0, 0)),
                  pl.BlockSpec((1, d), lambda i: (0, 0)),
                  pl.BlockSpec((hw, d), lambda i: (0, 0)),
                  pl.BlockSpec((d, hw), lambda i: (0, 1))],
        out_specs=(pl.BlockSpec((None, hw, nm), lambda i: (i, 0, 0)),
                   pl.BlockSpec((None, nm, hw), lambda i: (i, 0, 0))),
        compiler_params=_cparams(("arbitrary",)),
        name="mem_kv",
    )(mem, g, wkt, wkv)


def _xattn_kernel(h_ref, g_ref, wq_ref, kt_ref, v_ref, wo_ref, o_ref):
    h = h_ref[...]
    a = _rms(h, g_ref[...]).astype(BF16)
    q = (_dot(a, wq_ref[...]) * (X_HD ** -0.5)).astype(BF16)
    heads = [slice(hh * X_HD, (hh + 1) * X_HD) for hh in range(H_X)]
    scores = [_dot(q[:, cols], kt_ref[cols, :]) for cols in heads]
    probs = []
    for s in scores:
        m = jnp.max(s, axis=-1, keepdims=True)
        p = jnp.exp(s - m)
        probs.append((p / jnp.sum(p, axis=-1, keepdims=True)).astype(BF16))
    outs = [_dot(p, v_ref[:, cols]) for p, cols in zip(probs, heads)]
    o = jnp.concatenate(outs, axis=1).astype(BF16)
    o_ref[...] = h + _dot(o, wo_ref[...])


def _xattn(h3, g, wq, kt, v, wo):
    b, t, d = h3.shape
    hw = H_X * X_HD
    nm = v.shape[1]
    return pl.pallas_call(
        _xattn_kernel,
        out_shape=jax.ShapeDtypeStruct((b, t, d), F32),
        grid=(b, t // XATTN_TILE),
        in_specs=[pl.BlockSpec((None, XATTN_TILE, d), lambda i, j: (i, j, 0)),
                  pl.BlockSpec((1, d), lambda i, j: (0, 0)),
                  pl.BlockSpec((d, hw), lambda i, j: (0, 0)),
                  pl.BlockSpec((None, hw, nm), lambda i, j: (i, 0, 0)),
                  pl.BlockSpec((None, nm, hw), lambda i, j: (i, 0, 0)),
                  pl.BlockSpec((hw, d), lambda i, j: (0, 0))],
        out_specs=pl.BlockSpec((None, XATTN_TILE, d), lambda i, j: (i, j, 0)),
        compiler_params=_cparams(("arbitrary", "arbitrary")),
        name="xattn",
    )(h3, g, wq, kt, v, wo)


def _ffn_kernel(h_ref, g_ref, wg_ref, wu_ref, wd_ref, gf_ref, o_ref, *, bounds):
    h = h_ref[...]
    a = _rms(h, g_ref[...]).astype(BF16)
    acc = h
    for lo, hi in zip(bounds[:-1], bounds[1:]):
        cols = slice(lo, hi)
        mid = _silu(_dot(a, wg_ref[:, cols])) * _dot(a, wu_ref[:, cols])
        acc = acc + _dot(mid.astype(BF16), wd_ref[cols, :])
    o_ref[...] = _rms(acc, gf_ref[...])


def _ffn(h2, g, wgu, wd, gf):
    m, d = h2.shape
    ff = wd.shape[0]
    const = lambda shape: pl.BlockSpec(shape, lambda i: (0, 0))
    return pl.pallas_call(
        functools.partial(_ffn_kernel, bounds=(0, -(-ff // (2 * MXU_TILE)) * MXU_TILE, ff)),
        out_shape=jax.ShapeDtypeStruct((m, d), F32),
        grid=(m // TOK_TILE,),
        in_specs=[pl.BlockSpec((TOK_TILE, d), lambda i: (i, 0)),
                  const((1, d)), const((d, ff)), pl.BlockSpec((d, ff), lambda i: (0, 1)),
                  const((ff, d)), const((1, d))],
        out_specs=pl.BlockSpec((TOK_TILE, d), lambda i: (i, 0)),
        compiler_params=_cparams(("arbitrary",)),
        name="ffn",
    )(h2, g, wgu, wgu, wd, gf)


def _overlap_t(t):
    nc = (t - CMP_LEN) // CMP_STRIDE + 1
    ns = t // SLC_LEN
    starts = CMP_STRIDE * np.arange(N_CMP_PAD)
    s_start = SLC_LEN * np.arange(ns)
    ov = ((starts[None, :] + CMP_LEN > s_start[:, None]) & (starts[None, :] < s_start[:, None] + SLC_LEN)
          & (np.arange(N_CMP_PAD)[None, :] < nc))
    return jnp.asarray(ov.astype(np.float32), dtype=BF16)


def kernel(x, mem, g_mix, w_in, lower_bounds, g_rnn_out, pe_ck, w_ck1, w_ck2, pe_cv, w_cv1, w_cv2,
           w_proj_rnn, w_proj_att, w_out, g_xattn, g_mem, w_xq, w_xkv, w_xo, g_ffn, w_gate_up,
           w_down, g_final):
    b, t, d = x.shape
    depth = g_mix.shape[0]
    assert depth == 1, "the final RMSNorm is fused into the layer's FFN kernel"
    lbs = jnp.cumsum(jax.nn.softmax(lower_bounds.astype(F32), axis=0), axis=0)
    slopes = (2.0 ** (-8.0 * jnp.arange(1, H_ATT + 1, dtype=F32) / H_ATT)).reshape(G_KV, HG, 1, 1)
    overlap_t = _overlap_t(t)
    h = x
    for l in range(depth):
        w_n, w_t = _w_prep(w_in[l].T)
        x2 = h.reshape(b * t, d)
        proj = _in_proj(x2, g_mix[l][None, :], w_n)
        proj3 = proj.reshape(b, t, N_COLS)
        qv, gt = _in_proj_t(h, g_mix[l][None, :], w_t)
        y_r = _hgrn(proj3, lbs[l][None, :], g_rnn_out[l][None, :])
        pe_kv, w_c1 = _compress_weights(pe_ck[l], pe_cv[l], w_ck1[l], w_cv1[l])
        kc, vct = _compress(proj3, pe_kv, w_c1, w_ck2[l].astype(BF16), w_cv2[l].T.astype(BF16))
        ocmp, selb, used = _cmp_select(qv, kc, vct, slopes, overlap_t, t)
        n_extra, extra = _extra_chunks(used, t)
        y_a = _sparse(n_extra, extra, qv, gt, proj3, ocmp, selb, slopes, t)
        h1 = _merge(x2, y_r.reshape(b * t, d), y_a.reshape(b * t, d), proj,
                    w_proj_rnn[l].astype(BF16), w_proj_att[l].astype(BF16), w_out[l].astype(BF16))
        w_kv = w_xkv[l].astype(BF16)
        kt, v = _mem_kv(mem, g_mem[l][None, :], w_kv[:, :H_X * X_HD].T, w_kv)
        h2 = _xattn(h1.reshape(b, t, d), g_xattn[l][None, :], w_xq[l].astype(BF16), kt, v,
                    w_xo[l].astype(BF16))
        h = _ffn(h2.reshape(b * t, d), g_ffn[l][None, :], w_gate_up[l].astype(BF16),
                 w_down[l].astype(BF16), g_final[None, :]).reshape(b, t, d)
    return h
```

```python
import functools
import itertools

import jax
import jax.numpy as jnp
import numpy as np
from jax import lax
from jax.experimental import pallas as pl
from jax.experimental.pallas import tpu as pltpu

F32 = jnp.float32
BF16 = jnp.bfloat16

D_MODEL = 1024
N_MEM = 256
H_RNN = 8
RNN_DIM = 128
RNN_CHUNK = 64
H_ATT = 16
ATT_HD = 64
G_KV = 4
HG = H_ATT // G_KV
CMP_LEN = 32
CMP_STRIDE = 16
CMP_HIDDEN = 128
SLC_LEN = 64
SLC_TOPK = 8
WINDOW = 512
FORCE_BONUS = 1.0e4
H_X = 4
X_HD = 128
D_FF = 2816
EPS = 1e-6

LANE = 128
MXU_TILE = 256
VMEM_LIMIT = 56 * 1024 * 1024
TOK_TILE = 512
XATTN_TILE = 1024
IN_PROJ_TILE = 1024
FFN_TILE = 1024
MERGE_TILE = 1024
ATT_TILE = 128
CMP_TILE = 512
N_CMP_PAD = 128
HGRN_GROUP = 4
ONES_ROWS = 16
NEG_BIG = -1.0e30
LOG2E = 1.4426950408889634
AUG_SEL_ROW0 = 8
SEL_NEAR = 5

QT_ROWS = H_ATT * ATT_HD
VT_ROWS = 2 * G_KV * ATT_HD
GATE_ROWS = 64
F_ROWS = QT_ROWS + VT_ROWS + GATE_ROWS
RNN_COLS = 4 * H_RNN * RNN_DIM
KV_COLS = 4 * G_KV * ATT_HD
MG_COLS = 2 * D_MODEL
N_COLS = RNN_COLS + KV_COLS + MG_COLS


def _cparams(sem):
    return pltpu.CompilerParams(dimension_semantics=sem, vmem_limit_bytes=VMEM_LIMIT)


def _rms(xf, g):
    return xf * lax.rsqrt(jnp.mean(xf * xf, axis=-1, keepdims=True) + EPS) * g


def _sigmoid(x):
    return 1.0 / (1.0 + jnp.exp2(x * -LOG2E))


def _silu(x):
    return x * _sigmoid(x)


def _dot(a, b):
    return jnp.dot(a, b, preferred_element_type=F32)


def _dot_nt(a, b):
    return lax.dot_general(a, b, (((1,), (1,)), ((), ())), preferred_element_type=F32)


def _dot_tn(a, b):
    return lax.dot_general(a, b, (((0,), (0,)), ((), ())), preferred_element_type=F32)


def _w_prep_kernel(w_ref, wn_ref, wf_ref):
    kvw = G_KV * ATT_HD
    kv0 = RNN_COLS + QT_ROWS
    gate0 = kv0 + 6 * kvw
    mg0 = gate0 + 3 * H_ATT

    def pair_by_group(a0, b0):
        parts = []
        for g in range(G_KV):
            parts += [w_ref[a0 + g * ATT_HD:a0 + (g + 1) * ATT_HD, :], w_ref[b0 + g * ATT_HD:b0 + (g + 1) * ATT_HD, :]]
        return parts

    wn = jnp.concatenate([w_ref[0:RNN_COLS, :]] + pair_by_group(kv0, kv0 + kvw)
                         + pair_by_group(kv0 + 2 * kvw, kv0 + 4 * kvw) + [w_ref[mg0:mg0 + MG_COLS, :]], axis=0)
    wn_ref[...] = wn.T.astype(BF16)
    used = QT_ROWS + VT_ROWS + 3 * H_ATT
    wf = jnp.concatenate([w_ref[RNN_COLS:kv0, :] * (ATT_HD ** -0.5 * LOG2E),
                          w_ref[kv0 + 3 * kvw:kv0 + 4 * kvw, :], w_ref[kv0 + 5 * kvw:kv0 + 6 * kvw, :],
                          w_ref[gate0:mg0, :], jnp.zeros((wf_ref.shape[0] - used, w_ref.shape[1]), F32)], axis=0)
    wf_ref[...] = wf.astype(BF16)


def _w_prep(wt):
    n_in, d = wt.shape
    return pl.pallas_call(
        _w_prep_kernel,
        out_shape=(jax.ShapeDtypeStruct((d, N_COLS), BF16), jax.ShapeDtypeStruct((F_ROWS, d), BF16)),
        grid=(d // LANE,),
        in_specs=[pl.BlockSpec((n_in, LANE), lambda i: (0, i))],
        out_specs=(pl.BlockSpec((LANE, N_COLS), lambda i: (i, 0)),
                   pl.BlockSpec((F_ROWS, LANE), lambda i: (0, i))),
        compiler_params=_cparams(("arbitrary",)),
        name="w_prep",
    )(wt)


def _in_proj_kernel(x_ref, g_ref, w_ref, o_ref, *, sub):
    a = _rms(x_ref[...], g_ref[...]).astype(BF16)
    for n in range(o_ref.shape[1] // sub):
        cols = slice(n * sub, (n + 1) * sub)
        o_ref[:, cols] = _dot(a, w_ref[:, cols]).astype(o_ref.dtype)


def _in_proj(x2, g, w):
    m, d = x2.shape
    n = w.shape[1]
    tn = n // 2
    return pl.pallas_call(
        functools.partial(_in_proj_kernel, sub=512),
        out_shape=jax.ShapeDtypeStruct((m, n), BF16),
        grid=(2, m // IN_PROJ_TILE),
        in_specs=[pl.BlockSpec((IN_PROJ_TILE, d), lambda j, i: (i, 0)),
                  pl.BlockSpec((1, d), lambda j, i: (0, 0)),
                  pl.BlockSpec((d, tn), lambda j, i: (0, j))],
        out_specs=pl.BlockSpec((IN_PROJ_TILE, tn), lambda j, i: (i, j)),
        compiler_params=_cparams(("arbitrary", "arbitrary")),
        name="in_proj",
    )(x2, g, w)


def _in_proj_t_kernel(x_ref, g_ref, wt_ref, qv_ref, gt_ref):
    a = _rms(x_ref[...], g_ref[...]).astype(BF16)
    r = _dot_nt(wt_ref[...], a)
    nqv = qv_ref.shape[1]
    ngt = gt_ref.shape[1]
    for c in range(qv_ref.shape[0]):
        qv_ref[c] = r[:nqv, c * LANE:(c + 1) * LANE].astype(qv_ref.dtype)
        gt_ref[c] = r[nqv:nqv + ngt, c * LANE:(c + 1) * LANE]


def _in_proj_t(x, g, wt):
    b, t, d = x.shape
    rows = wt.shape[0]
    nqv = QT_ROWS + VT_ROWS
    ngt = GATE_ROWS
    nc = IN_PROJ_TILE // LANE
    return pl.pallas_call(
        _in_proj_t_kernel,
        out_shape=(jax.ShapeDtypeStruct((b, t // LANE, nqv, LANE), BF16),
                   jax.ShapeDtypeStruct((b, t // LANE, ngt, LANE), F32)),
        grid=(b, t // IN_PROJ_TILE),
        in_specs=[pl.BlockSpec((None, IN_PROJ_TILE, d), lambda i, j: (i, j, 0)),
                  pl.BlockSpec((1, d), lambda i, j: (0, 0)),
                  pl.BlockSpec((rows, d), lambda i, j: (0, 0))],
        out_specs=(pl.BlockSpec((None, nc, nqv, LANE), lambda i, j: (i, j, 0, 0)),
                   pl.BlockSpec((None, nc, ngt, LANE), lambda i, j: (i, j, 0, 0))),
        compiler_params=_cparams(("arbitrary", "arbitrary")),
        name="in_proj_t",
    )(x, g, wt)


def _hgrn_kernel(q_ref, f_ref, i_ref, og_ref, lb_ref, gn_ref, o_ref, qd_ref, oi_ref, ut_ref, dec_ref):
    c = RNN_CHUNK
    kd = RNN_DIM
    n_chunks = q_ref.shape[0] // c
    lb = lb_ref[...]
    gn = gn_ref[...]
    blk = HGRN_GROUP * c
    row = lax.broadcasted_iota(jnp.int32, (blk, blk), 0)
    col = lax.broadcasted_iota(jnp.int32, (blk, blk), 1)
    same_chunk = (row // c) == (col // c)
    causal = same_chunk & (row >= col)
    tril = causal.astype(BF16)

    q = q_ref[...].astype(F32)
    fl = f_ref[...].astype(F32)
    v = i_ref[...]
    f = lb + (1.0 - lb) * _sigmoid(fl)
    k = 1.0 - f
    logf = jnp.log(f)
    hi = logf.astype(BF16)
    lo = (logf - hi.astype(F32)).astype(BF16)
    pieces = jnp.concatenate([hi, lo], axis=1)
    n_blk = q_ref.shape[0] // blk
    cs = [_dot(tril, pieces[g * blk:(g + 1) * blk]) for g in range(n_blk)]
    bcum = jnp.concatenate([x[:, 0:kd] + x[:, kd:2 * kd] for x in cs], axis=0)
    e_neg = jnp.exp2(bcum * -LOG2E)
    dec = jnp.exp(jnp.concatenate([bcum[n * c + c - 1:n * c + c, :] for n in range(n_chunks)], axis=0))
    dec_rows = jnp.concatenate([jnp.broadcast_to(dec[n:n + 1, :], (c, kd)) for n in range(n_chunks)], axis=0)
    q_dec = (_silu(q) * (1.0 / e_neg)).astype(BF16)
    k_neg = k * e_neg
    k_dec = k_neg.astype(BF16)
    k_end = (k_neg * dec_rows).astype(BF16)
    qd_ref[...] = q_dec
    dec_ref[...] = dec
    for g in range(n_blk):
        rows = slice(g * blk, (g + 1) * blk)
        a = jnp.where(causal, _dot_nt(q_dec[rows], k_dec[rows]), 0.0)
        oi_ref[rows, :] = _dot(a.astype(BF16), v[rows])
    for n in range(n_chunks):
        rows = slice(n * c, (n + 1) * c)
        ut_ref[n] = _dot_tn(v[rows], k_end[rows])

    s_t = jnp.zeros((kd, kd), F32)
    for n in range(n_chunks):
        rows = slice(n * c, (n + 1) * c)
        o = oi_ref[rows, :] + _dot_nt(qd_ref[rows, :], s_t.astype(BF16))
        og = og_ref[rows, :].astype(F32)
        o_ref[rows, :] = (_rms(o, gn) * _silu(og)).astype(o_ref.dtype)
        s_t = s_t * dec_ref[n:n + 1, :] + ut_ref[n]


def _hgrn(proj3, lb, gn):
    b, t, _ = proj3.shape
    return pl.pallas_call(
        _hgrn_kernel,
        out_shape=jax.ShapeDtypeStruct((b, t, H_RNN * RNN_DIM), BF16),
        grid=(b, H_RNN),
        in_specs=[pl.BlockSpec((None, t, RNN_DIM), lambda i, h: (i, 0, h)),
                  pl.BlockSpec((None, t, RNN_DIM), lambda i, h: (i, 0, H_RNN + h)),
                  pl.BlockSpec((None, t, RNN_DIM), lambda i, h: (i, 0, 2 * H_RNN + h)),
                  pl.BlockSpec((None, t, RNN_DIM), lambda i, h: (i, 0, 3 * H_RNN + h)),
                  pl.BlockSpec((1, RNN_DIM), lambda i, h: (0, h)),
                  pl.BlockSpec((1, RNN_DIM), lambda i, h: (0, 0))],
        out_specs=pl.BlockSpec((None, t, RNN_DIM), lambda i, h: (i, 0, h)),
        scratch_shapes=[pltpu.VMEM((t, RNN_DIM), BF16),
                        pltpu.VMEM((t, RNN_DIM), F32),
                        pltpu.VMEM((t // RNN_CHUNK, RNN_DIM, RNN_DIM), F32),
                        pltpu.VMEM((t // RNN_CHUNK, RNN_DIM), F32)],
        compiler_params=_cparams(("arbitrary", "arbitrary")),
        name="hgrn",
    )(proj3, proj3, proj3, proj3, lb, gn)


def _compress_kernel(kv_ref, pe_ref, w1_ref, wk2_ref, wv2t_ref, kc_ref, vct_ref, xs_ref, x_ref):
    t = kv_ref.shape[0]
    xs_ref[0:t, :] = kv_ref[...].astype(F32)
    xs_ref[t:, :] = jnp.zeros((xs_ref.shape[0] - t, xs_ref.shape[1]), F32)
    for l in range(CMP_LEN):
        blk = xs_ref[pl.ds(l, N_CMP_PAD, stride=CMP_STRIDE), :]
        x_ref[:, l * LANE:(l + 1) * LANE] = (blk + pe_ref[l:l + 1, :]).astype(BF16)
    hid = _silu(_dot(x_ref[...], w1_ref[...])).astype(BF16)
    kc_ref[...] = _dot(hid[:, :CMP_HIDDEN], wk2_ref[...])
    vct_ref[...] = _dot_nt(wv2t_ref[...], hid[:, CMP_HIDDEN:])


def _compress(proj3, pe, w1, wk2, wv2t):
    b, t, _ = proj3.shape
    kv_unit0 = RNN_COLS // LANE
    full = lambda shape: pl.BlockSpec(shape, lambda i, g: (0,) * len(shape))
    return pl.pallas_call(
        _compress_kernel,
        out_shape=(jax.ShapeDtypeStruct((b, G_KV, N_CMP_PAD, ATT_HD), F32),
                   jax.ShapeDtypeStruct((b, G_KV, ATT_HD, N_CMP_PAD), F32)),
        grid=(b, G_KV),
        in_specs=[pl.BlockSpec((None, t, LANE), lambda i, g: (i, 0, kv_unit0 + g)),
                  full((CMP_LEN, LANE)), full((CMP_LEN * LANE, 2 * CMP_HIDDEN)),
                  full((CMP_HIDDEN, ATT_HD)), full((ATT_HD, CMP_HIDDEN))],
        out_specs=(pl.BlockSpec((None, None, N_CMP_PAD, ATT_HD), lambda i, g: (i, g, 0, 0)),
                   pl.BlockSpec((None, None, ATT_HD, N_CMP_PAD), lambda i, g: (i, g, 0, 0))),
        scratch_shapes=[pltpu.VMEM((t + CMP_STRIDE, LANE), F32),
                        pltpu.VMEM((N_CMP_PAD, CMP_LEN * LANE), BF16)],
        compiler_params=_cparams(("arbitrary", "arbitrary")),
        name="compress",
    )(proj3, pe, w1, wk2, wv2t)


def _compress_weights(pek, pev, wk1, wv1):
    zk = jnp.zeros((CMP_LEN, ATT_HD, CMP_HIDDEN), wk1.dtype)
    top = jnp.concatenate([wk1.reshape(CMP_LEN, ATT_HD, CMP_HIDDEN), zk], axis=2)
    bot = jnp.concatenate([zk, wv1.reshape(CMP_LEN, ATT_HD, CMP_HIDDEN)], axis=2)
    w1 = jnp.concatenate([top, bot], axis=1).reshape(CMP_LEN * LANE, 2 * CMP_HIDDEN)
    return jnp.concatenate([pek, pev], axis=1), w1.astype(BF16)


def _cmp_select_kernel(qt_ref, kc_ref, vct_ref, slope_ref, ov_ref, ocmp_ref, selb_ref, used_ref, *, n_cmp, n_sel):
    nct = CMP_TILE // LANE
    tt = CMP_TILE
    ns = ov_ref.shape[0]
    kc = kc_ref[...].astype(BF16)
    vct = vct_ref[...].astype(BF16)
    ov = ov_ref[...]
    j_i = lax.broadcasted_iota(jnp.int32, (8, tt), 0)
    tq_rel = lax.broadcasted_iota(jnp.int32, (8, tt), 1)
    n_tiles = qt_ref.shape[0] // nct

    def ranges(tile):
        t0 = tile * tt
        n_any = min(N_CMP_PAD, (t0 + tt) // CMP_STRIDE)
        n_all = max(0, (t0 - (CMP_LEN - 1)) // CMP_STRIDE + 1) // 8 * 8
        return t0, n_any, n_all, (t0 + tt) // SLC_LEN

    def pad_rows(x):
        if x.shape[0] == N_CMP_PAD:
            return x
        return jnp.concatenate([x, jnp.zeros((N_CMP_PAD - x.shape[0], x.shape[1]), x.dtype)], axis=0)

    scores = []
    for tile in range(n_tiles):
        _, n_any, _, _ = ranges(tile)
        qt = [jnp.concatenate([qt_ref[tile * nct + c, h * ATT_HD:(h + 1) * ATT_HD, :] for c in range(nct)],
                              axis=1) for h in range(HG)]
        scores.append(_dot(kc[:n_any], jnp.concatenate(qt, axis=1)))

    probs, psums = [], []
    for tile in range(n_tiles):
        t0, n_any, n_all, _ = ranges(tile)
        n_b = lax.broadcasted_iota(jnp.int32, (n_any - n_all, tt), 0) + n_all
        t_b = lax.broadcasted_iota(jnp.int32, (n_any - n_all, tt), 1)
        visible = (CMP_STRIDE * n_b + (CMP_LEN - 1) - t0 <= t_b) & (n_b < n_cmp)
        centre = (lax.broadcasted_iota(jnp.int32, (n_any, tt), 0).astype(F32) * float(CMP_STRIDE)
                  + (CMP_LEN - 1) / 2.0)
        psum = jnp.zeros((n_any, tt), F32)
        ph = []
        for h in range(HG):
            s = scores[tile][:, h * tt:(h + 1) * tt] + slope_ref[h] * LOG2E * centre
            band = jnp.where(visible, s[n_all:], -jnp.inf)
            s = band if n_all == 0 else jnp.concatenate([s[:n_all], band], axis=0)
            m = jnp.max(s, axis=0, keepdims=True)
            m = jnp.where(m == -jnp.inf, 0.0, m)
            p = jnp.exp2(s - m)
            d = jnp.sum(p, axis=0, keepdims=True)
            p = p * (1.0 / jnp.where(d > 0, d, 1.0))
            psum = psum + p
            ph.append(pad_rows(p).astype(BF16))
        probs.append(jnp.concatenate(ph, axis=1))
        psums.append(pad_rows(psum))

    for tile in range(n_tiles):
        o = _dot(vct, probs[tile])
        for h in range(HG):
            for c in range(nct):
                lanes = slice(h * tt + c * LANE, h * tt + (c + 1) * LANE)
                ocmp_ref[tile * nct + c, h * ATT_HD:(h + 1) * ATT_HD, :] = o[:, lanes]

    imps = []
    for tile in range(n_tiles):
        psum = psums[tile]
        hi = psum.astype(BF16)
        rem = psum - hi.astype(F32)
        mid = rem.astype(BF16)
        lo = (rem - mid.astype(F32)).astype(BF16)
        imp3 = _dot(ov, jnp.concatenate([hi, mid, lo], axis=1))
        imps.append(imp3[:, 0:tt] + imp3[:, tt:2 * tt] + imp3[:, 2 * tt:3 * tt])

    for tile in range(n_tiles):
        t0, _, _, nvb = ranges(tile)
        imp = imps[tile]
        score = []
        for rg in range(nvb // 8):
            j = j_i + 8 * rg
            tq = tq_rel + t0
            cur = tq // SLC_LEN
            forced = (j == 0) | (j == cur) | (j == cur - 1)
            sc = imp[8 * rg:8 * rg + 8, :] + jnp.where(forced, FORCE_BONUS, 0.0)
            score.append(jnp.where(j * SLC_LEN <= tq, sc, -jnp.inf))
        rank = [jnp.zeros((8, tt), F32) for _ in score]
        for jp in range(nvb):
            sj = jnp.broadcast_to(score[jp // 8][jp % 8:jp % 8 + 1, :], (8, tt))
            for rg in range(nvb // 8):
                ge = jnp.where(sj >= score[rg], 1.0, 0.0)
                gt = jnp.where(sj > score[rg], 1.0, 0.0)
                if 8 * rg > jp:
                    ahead = ge
                elif 8 * rg + 7 <= jp:
                    ahead = gt
                else:
                    ahead = jnp.where(j_i + 8 * rg > jp, ge, gt)
                rank[rg] = rank[rg] + ahead
        for rg in range(ns // 8):
            for c in range(nct):
                if rg < nvb // 8:
                    picked = rank[rg][:, c * LANE:(c + 1) * LANE] < float(n_sel)
                    selb = jnp.where(picked, 0.0, NEG_BIG)
                    used = jnp.max(jnp.where(picked, 1.0, 0.0), axis=1, keepdims=True)
                else:
                    selb = jnp.full((8, LANE), NEG_BIG, F32)
                    used = jnp.zeros((8, 1), F32)
                selb_ref[tile * nct + c, 8 * rg:8 * rg + 8, :] = selb
                used_ref[tile * nct + c, 8 * rg:8 * rg + 8, :] = jnp.broadcast_to(used, (8, LANE))


def _cmp_select(qv, kc, vct, slopes, overlap_t, t):
    b = qv.shape[0]
    ns = t // SLC_LEN
    nch = t // LANE
    n_cmp = (t - CMP_LEN) // CMP_STRIDE + 1
    n_sel = min(SLC_TOPK, ns)
    rows = HG * ATT_HD
    return pl.pallas_call(
        functools.partial(_cmp_select_kernel, n_cmp=n_cmp, n_sel=n_sel),
        out_shape=(jax.ShapeDtypeStruct((b, nch, H_ATT * ATT_HD, LANE), F32),
                   jax.ShapeDtypeStruct((b, G_KV, nch, ns, LANE), F32),
                   jax.ShapeDtypeStruct((b, G_KV, nch, ns, LANE), F32)),
        grid=(b, G_KV),
        in_specs=[pl.BlockSpec((None, nch, rows, LANE), lambda i, g: (i, 0, g, 0)),
                  pl.BlockSpec((None, None, N_CMP_PAD, ATT_HD), lambda i, g: (i, g, 0, 0)),
                  pl.BlockSpec((None, None, ATT_HD, N_CMP_PAD), lambda i, g: (i, g, 0, 0)),
                  pl.BlockSpec((None, HG, 1, 1), lambda i, g: (g, 0, 0, 0)),
                  pl.BlockSpec((ns, N_CMP_PAD), lambda i, g: (0, 0))],
        out_specs=(pl.BlockSpec((None, nch, rows, LANE), lambda i, g: (i, 0, g, 0)),
                   pl.BlockSpec((None, None, nch, ns, LANE), lambda i, g: (i, g, 0, 0, 0)),
                   pl.BlockSpec((None, None, nch, ns, LANE), lambda i, g: (i, g, 0, 0, 0))),
        compiler_params=_cparams(("arbitrary", "arbitrary")),
        name="cmp_select",
    )(qv, kc, vct, slopes, overlap_t)


def _sparse_kernel(n_extra_ref, extra_ref, qt_ref, k_ref, ka_ref, vs_ref, vw_ref, gt_ref, ocmp_ref, selb_ref,
                   slope_ref, o_ref, msk_ref, qop_ref, s_ref, m_ref, acc_ref):
    tq = ATT_TILE
    kc = ATT_TILE
    hd = ATT_HD
    nl = HG * tq
    n_tiles = qt_ref.shape[0]
    ns = selb_ref.shape[1]
    wch = WINDOW // kc
    n_sel_items = 2 * (SEL_NEAR + 1)
    tile_base = (pl.program_id(0) * pl.num_programs(1) + pl.program_id(1)) * n_tiles
    slope = jnp.concatenate([jnp.broadcast_to(slope_ref[h], (1, tq)) for h in range(HG)], axis=1) * LOG2E
    s_hi = slope.astype(BF16).astype(F32)
    s_mid = (slope - s_hi).astype(BF16).astype(F32)
    s_lo = (slope - s_hi - s_mid).astype(BF16).astype(F32)
    slope_rows = jnp.concatenate([s_hi, s_mid, s_lo, jnp.zeros((AUG_SEL_ROW0 - 3, nl), F32)], axis=0)
    aug_tail = jnp.zeros((LANE - AUG_SEL_ROW0 - ns, nl), F32)
    aug_win = jnp.concatenate([slope_rows, jnp.zeros((ns, nl), F32), aug_tail], axis=0).astype(BF16)
    k_i = lax.broadcasted_iota(jnp.int32, (kc, nl), 0)
    t_i = lax.broadcasted_iota(jnp.int32, (kc, nl), 1) % tq
    d0 = t_i - k_i
    msk_ref[0] = jnp.where(d0 >= 0, 0.0, NEG_BIG)
    msk_ref[1] = jnp.where(d0 < 0, 0.0, NEG_BIG)
    chunk_bias = slope * float(kc)
    ones = jnp.ones((ONES_ROWS, kc), BF16)

    n_pairs = n_tiles // 2
    n_items = n_sel_items + 2 * (wch + 1)
    n_slots = s_ref.shape[0] // 2

    def tiles_of(p):
        return (p, n_tiles - 1 - p)

    def sel_item(p, w, j):
        tile = tiles_of(p)[w]
        if j == SEL_NEAR:
            return 0, tile >= SEL_NEAR, tile
        return max(tile - j, 0), tile - j >= 0, j

    def win_item(p, w, j):
        chunk = tiles_of(p)[w] - wch + j
        return max(chunk, 0), chunk >= 0, wch - j

    def col_max(r, delta_f):
        r8 = jnp.max(r.reshape(kc // 8, 8, nl), axis=0)
        return r8 - chunk_bias * delta_f

    def scores(chunk, op):
        rows = pl.ds(pl.multiple_of(chunk * kc, kc), kc)
        keys = jnp.concatenate([k_ref[rows, :], ka_ref[rows, :]], axis=1)
        return _dot(keys, qop_ref[op])

    def build(p, slot):
        for w, tile in enumerate(tiles_of(p)):
            qt = qt_ref[tile]
            q_all = jnp.concatenate([qt[h * hd:(h + 1) * hd, :] for h in range(HG)], axis=1)
            zero = jnp.zeros_like(q_all)
            selb = jnp.concatenate([selb_ref[tile]] * HG, axis=1)
            aug_sel = jnp.concatenate([slope_rows, selb, aug_tail], axis=0).astype(BF16)
            qop_ref[4 * slot + 2 * w] = jnp.concatenate([q_all, zero, aug_sel], axis=0)
            qop_ref[4 * slot + 2 * w + 1] = jnp.concatenate([zero, q_all, aug_win], axis=0)
        for x in range(4):
            m_ref[4 * slot + x] = jnp.full((8, nl), NEG_BIG, F32)

    def phase_a(p, slot):
        items = []

        def sel(w, j):
            chunk, _, delta = sel_item(p, w, j)
            r = scores(chunk, 4 * slot + 2 * w)
            if j == 0:
                r = r + msk_ref[0]
            s_ref[slot * n_slots + w * (SEL_NEAR + 1) + j] = r
            x = 4 * slot + 2 * w
            m_ref[x] = jnp.maximum(m_ref[x], col_max(r, float(delta)))

        def win(w, j):
            chunk, _, delta = win_item(p, w, j)
            r = scores(chunk, 4 * slot + 2 * w + 1)
            if j == 0:
                r = r + msk_ref[1]
            elif j == wch:
                r = r + msk_ref[0]
            s_ref[slot * n_slots + n_sel_items + w * (wch + 1) + j] = r
            x = 4 * slot + 2 * w + 1
            m_ref[x] = jnp.maximum(m_ref[x], col_max(r, float(delta)))

        for w in range(2):
            for j in range(SEL_NEAR + 1):
                if sel_item(p, w, j)[1]:
                    items.append(functools.partial(sel, w, j))
        for w in range(2):
            for j in range(wch + 1):
                if win_item(p, w, j)[1]:
                    items.append(functools.partial(win, w, j))
        return items

    def overflow(p, slot, visit):
        count = 0
        for w, tile in enumerate(tiles_of(p)):
            def body(e, n, w=w, tile=tile):
                visit(w, tile, extra_ref[(tile_base + tile) * n_tiles + e], n)
                return n + 1

            count = lax.fori_loop(0, n_extra_ref[tile_base + tile], body, count)

    def overflow_a(p, slot):
        def visit(w, tile, c, n):
            r = scores(c, 4 * slot + 2 * w)
            s_ref[slot * n_slots + n_items + n] = r
            x = 4 * slot + 2 * w
            m_ref[x] = jnp.maximum(m_ref[x], col_max(r, jnp.float32(tile - c)))

        overflow(p, slot, visit)

    def overflow_b(p, slot, m_row):
        def visit(w, tile, c, n):
            row = m_row[2 * w] + chunk_bias * jnp.float32(tile - c)
            pr = jnp.exp2(s_ref[slot * n_slots + n_items + n] - row).astype(BF16)
            v_aug = jnp.concatenate([vs_ref[c], ones], axis=0)
            acc_ref[4 * slot + 2 * w] += _dot(v_aug, pr)

        overflow(p, slot, visit)

    def phase_b(p, slot, m_row):
        for x in range(4):
            acc_ref[4 * slot + x] = jnp.zeros(acc_ref.shape[1:], F32)
        items = []

        def pv(x, v_ref, group):
            prs, vs = [], []
            for s_idx, chunk, delta in group:
                row = m_row[x] + chunk_bias * float(delta)
                prs.append(jnp.exp2(s_ref[slot * n_slots + s_idx] - row).astype(BF16))
                vs.append(jnp.concatenate([v_ref[chunk], ones], axis=0))
            acc_ref[4 * slot + x] += _dot(jnp.concatenate(vs, axis=1), jnp.concatenate(prs, axis=0))

        for w in range(2):
            sel = [(w * (SEL_NEAR + 1) + j,) + sel_item(p, w, j)[::2]
                   for j in range(SEL_NEAR + 1) if sel_item(p, w, j)[1]]
            win = [(n_sel_items + w * (wch + 1) + j,) + win_item(p, w, j)[::2]
                   for j in range(wch + 1) if win_item(p, w, j)[1]]
            for x, v_ref, group in ((2 * w, vs_ref, sel), (2 * w + 1, vw_ref, win)):
                for i in range(0, len(group), 2):
                    items.append(functools.partial(pv, x, v_ref, group[i:i + 2]))
        return items

    def finalize(p, slot):
        for w, tile in enumerate(tiles_of(p)):
            a_s = acc_ref[4 * slot + 2 * w]
            a_w = acc_ref[4 * slot + 2 * w + 1]
            o_slc = a_s[:hd, :] / a_s[hd:hd + 1, :]
            o_win = a_w[:hd, :] / a_w[hd:hd + 1, :]
            oc = ocmp_ref[tile]
            outs = []
            for h in range(HG):
                lanes = slice(h * tq, (h + 1) * tq)
                row0 = (pl.program_id(1) * HG + h) * 3
                g_cmp, g_slc, g_win = [_sigmoid(gt_ref[tile, pl.ds(row0 + br, 1), :]) for br in range(3)]
                y = (g_cmp * oc[h * hd:(h + 1) * hd, :] + g_slc * o_slc[:, lanes]
                     + g_win * o_win[:, lanes])
                outs.append(y.T)
            rows = pl.ds(pl.multiple_of(tile * tq, tq), tq)
            o_ref[rows, :] = jnp.concatenate(outs, axis=1).astype(o_ref.dtype)

    def step(k, slot, first=False, last=False):
        m_row = [jnp.max(m_ref[4 * slot + x], axis=0, keepdims=True) for x in range(4)]
        if not last:
            build(k + 1, 1 - slot)
        if not first:
            finalize(k - 1, 1 - slot)
        b_items = phase_b(k, slot, m_row)
        a_items = phase_a(k + 1, 1 - slot) if not last else []
        for a, b in itertools.zip_longest(a_items, b_items):
            if a is not None:
                a()
            if b is not None:
                b()
        if not last:
            overflow_a(k + 1, 1 - slot)
        overflow_b(k, slot, m_row)

    build(0, 0)
    for a in phase_a(0, 0):
        a()
    overflow_a(0, 0)
    for k in range(n_pairs):
        step(k, k % 2, first=k == 0, last=k == n_pairs - 1)
    finalize(n_pairs - 1, (n_pairs - 1) % 2)


def _key_features(t):
    kp = np.arange(t)
    f = np.zeros((t, LANE), np.float32)
    f[:, 0:3] = (kp % ATT_TILE)[:, None]
    f[kp, AUG_SEL_ROW0 + kp // SLC_LEN] = 1.0
    return jnp.asarray(f, dtype=BF16)


def _extra_chunks(used, t):
    b = used.shape[0]
    nt = t // ATT_TILE
    need = used.max(axis=-1).reshape(b, G_KV, nt, nt, ATT_TILE // SLC_LEN).max(axis=-1) > 0.5
    chunk = jnp.arange(nt, dtype=jnp.int32)
    extra = need & (chunk >= 1) & (chunk[None, :] <= chunk[:, None] - SEL_NEAR)
    place = jnp.cumsum(extra.astype(jnp.int32), axis=-1) - 1
    hit = extra[..., :, None] & (place[..., :, None] == chunk)
    lst = jnp.sum(jnp.where(hit, chunk[:, None], 0), axis=-2)
    return extra.sum(axis=-1, dtype=jnp.int32).reshape(-1), lst.reshape(-1)


def _sparse(n_extra, extra, qv, gt, proj3, ocmp, selb, slopes, t):
    b = qv.shape[0]
    nchunk = t // LANE
    ns = t // SLC_LEN
    rows = HG * ATT_HD
    ksw_unit0 = RNN_COLS // LANE + G_KV
    vs_blk0 = QT_ROWS // ATT_HD
    vw_blk0 = vs_blk0 + G_KV
    nl = HG * ATT_TILE
    n_slots = 2 * (SEL_NEAR + 1) + 2 * (WINDOW // ATT_TILE + 1) + (nchunk - 1 - SEL_NEAR)
    grid_spec = pltpu.PrefetchScalarGridSpec(
        num_scalar_prefetch=2,
        grid=(b, G_KV),
        in_specs=[pl.BlockSpec((None, nchunk, rows, LANE), lambda i, g, *_: (i, 0, g, 0)),
                  pl.BlockSpec((None, t, LANE), lambda i, g, *_: (i, 0, ksw_unit0 + g)),
                  pl.BlockSpec((t, LANE), lambda i, g, *_: (0, 0)),
                  pl.BlockSpec((None, nchunk, ATT_HD, LANE), lambda i, g, *_: (i, 0, vs_blk0 + g, 0)),
                  pl.BlockSpec((None, nchunk, ATT_HD, LANE), lambda i, g, *_: (i, 0, vw_blk0 + g, 0)),
                  pl.BlockSpec((None, nchunk, GATE_ROWS, LANE), lambda i, g, *_: (i, 0, 0, 0)),
                  pl.BlockSpec((None, nchunk, rows, LANE), lambda i, g, *_: (i, 0, g, 0)),
                  pl.BlockSpec((None, None, nchunk, ns, LANE), lambda i, g, *_: (i, g, 0, 0, 0)),
                  pl.BlockSpec((None, HG, 1, 1), lambda i, g, *_: (g, 0, 0, 0))],
        out_specs=pl.BlockSpec((None, t, rows), lambda i, g, *_: (i, 0, g)),
        scratch_shapes=[pltpu.VMEM((2, ATT_TILE, nl), F32),
                        pltpu.VMEM((2 * 4, 2 * LANE, nl), BF16),
                        pltpu.VMEM((2 * n_slots, ATT_TILE, nl), F32),
                        pltpu.VMEM((2 * 4, 8, nl), F32),
                        pltpu.VMEM((2 * 4, ATT_HD + ONES_ROWS, nl), F32)])
    return pl.pallas_call(
        _sparse_kernel,
        out_shape=jax.ShapeDtypeStruct((b, t, H_ATT * ATT_HD), BF16),
        grid_spec=grid_spec,
        compiler_params=_cparams(("arbitrary", "arbitrary")),
        name="sparse",
    )(n_extra, extra, qv, proj3, _key_features(t), qv, qv, gt, ocmp, selb, slopes)


def _merge_kernel(x_ref, yr_ref, ya_ref, mgr_ref, mga_ref, wr_ref, wa_ref, wo_ref, o_ref):
    pr = _dot(yr_ref[...], wr_ref[...])
    pa = _dot(ya_ref[...], wa_ref[...])
    merged = _sigmoid(mgr_ref[...].astype(F32)) * pr + _sigmoid(mga_ref[...].astype(F32)) * pa
    o_ref[...] = x_ref[...] + _dot(merged.astype(BF16), wo_ref[...])


def _merge(x2, yr, ya, proj, wr, wa, wo):
    m, d = x2.shape
    mg0 = (RNN_COLS + KV_COLS) // d
    tile = lambda col: pl.BlockSpec((MERGE_TILE, d), lambda i: (i, col))
    wfull = pl.BlockSpec((d, d), lambda i: (0, 0), pipeline_mode=pl.Buffered(1))
    return pl.pallas_call(
        _merge_kernel,
        out_shape=jax.ShapeDtypeStruct((m, d), F32),
        grid=(m // MERGE_TILE,),
        in_specs=[tile(0), tile(0), tile(0), tile(mg0), tile(mg0 + 1), wfull, wfull, wfull],
        out_specs=tile(0),
        compiler_params=_cparams(("arbitrary",)),
        name="merge",
    )(x2, yr, ya, proj, proj, wr, wa, wo)


def _mem_kv_kernel(mem_ref, g_ref, wkt_ref, wv_ref, kt_ref, v_ref):
    a = _rms(mem_ref[...], g_ref[...]).astype(BF16)
    kt_ref[...] = _dot_nt(wkt_ref[...], a).astype(kt_ref.dtype)
    v_ref[...] = _dot(a, wv_ref[...]).astype(v_ref.dtype)


def _mem_kv(mem, g, wkt, wkv):
    b, nm, d = mem.shape
    hw = H_X * X_HD
    return pl.pallas_call(
        _mem_kv_kernel,
        out_shape=(jax.ShapeDtypeStruct((b, hw, nm), BF16),
                   jax.ShapeDtypeStruct((b, nm, hw), BF16)),
        grid=(b,),
        in_specs=[pl.BlockSpec((None, nm, d), lambda i: (i, 0, 0)),
                  pl.BlockSpec((1, d), lambda i: (0, 0)),
                  pl.BlockSpec((hw, d), lambda i: (0, 0)),
                  pl.BlockSpec((d, hw), lambda i: (0, 1))],
        out_specs=(pl.BlockSpec((None, hw, nm), lambda i: (i, 0, 0)),
                   pl.BlockSpec((None, nm, hw), lambda i: (i, 0, 0))),
        compiler_params=_cparams(("arbitrary",)),
        name="mem_kv",
    )(mem, g, wkt, wkv)


def _xattn_kernel(h_ref, g_ref, wq_ref, kt_ref, v_ref, wo_ref, o_ref):
    h = h_ref[...]
    a = _rms(h, g_ref[...]).astype(BF16)
    q = (_dot(a, wq_ref[...]) * (X_HD ** -0.5)).astype(BF16)
    heads = [slice(hh * X_HD, (hh + 1) * X_HD) for hh in range(H_X)]
    scores = [_dot(q[:, cols], kt_ref[cols, :]) for cols in heads]
    probs = []
    for s in scores:
        m = jnp.max(s, axis=-1, keepdims=True)
        p = jnp.exp(s - m)
        probs.append((p / jnp.sum(p, axis=-1, keepdims=True)).astype(BF16))
    outs = [_dot(p, v_ref[:, cols]) for p, cols in zip(probs, heads)]
    o = jnp.concatenate(outs, axis=1).astype(BF16)
    o_ref[...] = h + _dot(o, wo_ref[...])


def _xattn(h3, g, wq, kt, v, wo):
    b, t, d = h3.shape
    hw = H_X * X_HD
    nm = v.shape[1]
    return pl.pallas_call(
        _xattn_kernel,
        out_shape=jax.ShapeDtypeStruct((b, t, d), F32),
        grid=(b, t // XATTN_TILE),
        in_specs=[pl.BlockSpec((None, XATTN_TILE, d), lambda i, j: (i, j, 0)),
                  pl.BlockSpec((1, d), lambda i, j: (0, 0)),
                  pl.BlockSpec((d, hw), lambda i, j: (0, 0)),
                  pl.BlockSpec((None, hw, nm), lambda i, j: (i, 0, 0)),
                  pl.BlockSpec((None, nm, hw), lambda i, j: (i, 0, 0)),
                  pl.BlockSpec((hw, d), lambda i, j: (0, 0))],
        out_specs=pl.BlockSpec((None, XATTN_TILE, d), lambda i, j: (i, j, 0)),
        compiler_params=_cparams(("arbitrary", "arbitrary")),
        name="xattn",
    )(h3, g, wq, kt, v, wo)


def _ffn_kernel(h_ref, g_ref, wg_ref, wu_ref, wd_ref, gf_ref, o_ref, *, bounds):
    h = h_ref[...]
    a = _rms(h, g_ref[...]).astype(BF16)
    acc = h
    for lo, hi in zip(bounds[:-1], bounds[1:]):
        cols = slice(lo, hi)
        mid = _silu(_dot(a, wg_ref[:, cols])) * _dot(a, wu_ref[:, cols])
        acc = acc + _dot(mid.astype(BF16), wd_ref[cols, :])
    o_ref[...] = _rms(acc, gf_ref[...])


def _ffn(h2, g, wgu, wd, gf):
    m, d = h2.shape
    ff = wd.shape[0]
    const = lambda shape, col=0: pl.BlockSpec(shape, lambda i: (0, col), pipeline_mode=pl.Buffered(1))
    return pl.pallas_call(
        functools.partial(_ffn_kernel, bounds=(0, -(-ff // (2 * MXU_TILE)) * MXU_TILE, ff)),
        out_shape=jax.ShapeDtypeStruct((m, d), F32),
        grid=(m // FFN_TILE,),
        in_specs=[pl.BlockSpec((FFN_TILE, d), lambda i: (i, 0)),
                  const((1, d)), const((d, ff)), const((d, ff), 1), const((ff, d)), const((1, d))],
        out_specs=pl.BlockSpec((FFN_TILE, d), lambda i: (i, 0)),
        compiler_params=_cparams(("arbitrary",)),
        name="ffn",
    )(h2, g, wgu, wgu, wd, gf)


def _overlap_t(t):
    nc = (t - CMP_LEN) // CMP_STRIDE + 1
    ns = t // SLC_LEN
    starts = CMP_STRIDE * np.arange(N_CMP_PAD)
    s_start = SLC_LEN * np.arange(ns)
    ov = ((starts[None, :] + CMP_LEN > s_start[:, None]) & (starts[None, :] < s_start[:, None] + SLC_LEN)
          & (np.arange(N_CMP_PAD)[None, :] < nc))
    return jnp.asarray(ov.astype(np.float32), dtype=BF16)


def kernel(x, mem, g_mix, w_in, lower_bounds, g_rnn_out, pe_ck, w_ck1, w_ck2, pe_cv, w_cv1, w_cv2,
           w_proj_rnn, w_proj_att, w_out, g_xattn, g_mem, w_xq, w_xkv, w_xo, g_ffn, w_gate_up,
           w_down, g_final):
    b, t, d = x.shape
    depth = g_mix.shape[0]
    assert depth == 1, "the final RMSNorm is fused into the layer's FFN kernel"
    lbs = jnp.cumsum(jax.nn.softmax(lower_bounds.astype(F32), axis=0), axis=0)
    slopes = (2.0 ** (-8.0 * jnp.arange(1, H_ATT + 1, dtype=F32) / H_ATT)).reshape(G_KV, HG, 1, 1)
    overlap_t = _overlap_t(t)
    h = x
    for l in range(depth):
        w_n, w_t = _w_prep(w_in[l].T)
        x2 = h.reshape(b * t, d)
        proj = _in_proj(x2, g_mix[l][None, :], w_n)
        proj3 = proj.reshape(b, t, N_COLS)
        qv, gt = _in_proj_t(h, g_mix[l][None, :], w_t)
        y_r = _hgrn(proj3, lbs[l][None, :], g_rnn_out[l][None, :])
        pe_kv, w_c1 = _compress_weights(pe_ck[l], pe_cv[l], w_ck1[l], w_cv1[l])
        kc, vct = _compress(proj3, pe_kv, w_c1, w_ck2[l].astype(BF16), w_cv2[l].T.astype(BF16))
        ocmp, selb, used = _cmp_select(qv, kc, vct, slopes, overlap_t, t)
        n_extra, extra = _extra_chunks(used, t)
        y_a = _sparse(n_extra, extra, qv, gt, proj3, ocmp, selb, slopes, t)
        h1 = _merge(x2, y_r.reshape(b * t, d), y_a.reshape(b * t, d), proj,
                    w_proj_rnn[l].astype(BF16), w_proj_att[l].astype(BF16), w_out[l].astype(BF16))
        w_kv = w_xkv[l].astype(BF16)
        kt, v = _mem_kv(mem, g_mem[l][None, :], w_kv[:, :H_X * X_HD].T, w_kv)
        h2 = _xattn(h1.reshape(b, t, d), g_xattn[l][None, :], w_xq[l].astype(BF16), kt, v,
                    w_xo[l].astype(BF16))
        h = _ffn(h2.reshape(b * t, d), g_ffn[l][None, :], w_gate_up[l].astype(BF16),
                 w_down[l].astype(BF16), g_final[None, :]).reshape(b, t, d)
    return h
```
